```python
import math
import numpy as np
import jax
import jax.numpy as jnp
from jax import lax

D_MODEL = 1024
BATCH = 2
SEQ = 8192
DEPTH = 2
DEC_BATCH = 32
DEC_SEQ = 4
PAST_LEN = 16384
PAGE_SIZE = 128

N_EVEN = (DEPTH + 1) // 2
N_ODD = DEPTH // 2
DIL_CFG = ((128, 1), (512, 4), (2048, 16))
N_DIL = 3
H_A = 4
DH_A = 64
BLK_A = 128
H_B = 4
DH_B = 128
MLSTM_CHUNK = 128
H_C = 16
KVH_C = 2
G_C = H_C // KVH_C
DH_C = 64
CMP_STRIDE = 16
CMP_BLK = 2 * CMP_STRIDE
CMP_HID = 64
SLC_BLK = 64
N_SEL = 16
WIN_C = 512
QBLK_C = 128
D_FF = -(-8 * D_MODEL // (3 * 256)) * 256
ROPE_THETA = 10000.0
EPS = 1e-6
A_QW = N_DIL * H_A * DH_A
B_W = H_B * DH_B
IN_AB = 3 * A_QW + 3 * B_W + 2 * H_B + B_W
MIX_AB = H_A * DH_A + B_W
C_Q = H_C * DH_C
C_KV = 3 * 2 * KVH_C * DH_C
IN_C = C_Q + C_KV + 3 * H_C
F32 = jnp.float32

kernel_name = 'hybrid_dilated_mlstm_nsa_decoder_step'


def ein(spec, *ops):
    return jnp.einsum(spec, *ops, preferred_element_type=F32)


def rms_norm(x, g):
    xf = x.astype(F32)
    y = xf * lax.rsqrt(jnp.mean(xf * xf, axis=-1, keepdims=True) + EPS)
    return (y * g.astype(F32)).astype(x.dtype)


def rope(x, pos):
    half = x.shape[-1] // 2
    inv = 1.0 / (ROPE_THETA ** (jnp.arange(half, dtype=F32) / half))
    ang = pos.astype(F32)[:, None] * inv[None, :]
    ang = ang.reshape((ang.shape[0],) + (1,) * (x.ndim - 3) + (half,))
    cos, sin = jnp.cos(ang), jnp.sin(ang)
    xf = x.astype(F32)
    x1, x2 = xf[..., :half], xf[..., half:]
    return jnp.concatenate([x1 * cos - x2 * sin, x2 * cos + x1 * sin], axis=-1).astype(x.dtype)


def masked_softmax(s, mask):
    s = jnp.where(mask, s, -jnp.inf)
    m = jnp.max(s, axis=-1, keepdims=True)
    m = jnp.where(jnp.isfinite(m), m, 0.0)
    p = jnp.exp(s - m)
    return p / jnp.maximum(jnp.sum(p, axis=-1, keepdims=True), 1e-30)


def swiglu(x, wg, wu, wd):
    return (jax.nn.silu(x @ wg) * (x @ wu)) @ wd


def dilated_prompt(q, k, v, window, dil):
    B, S, H, dh = q.shape
    nback = window // dil
    span = dil * BLK_A
    Sp = -(-S // span) * span
    M = Sp // dil
    NB = M // BLK_A

    def sub(x):
        x = jnp.pad(x, ((0, 0), (0, Sp - S), (0, 0), (0, 0)))
        x = x.reshape(B, M, dil, H, dh).transpose(0, 2, 1, 3, 4)
        return x.reshape(B, dil, NB, BLK_A, H, dh)

    def band(x):
        prev = jnp.concatenate([jnp.zeros_like(x[:, :, :1]), x[:, :, :-1]], axis=2)
        return jnp.concatenate([prev, x], axis=3)

    qb = sub(q)
    kk, vv = band(sub(k)), band(sub(v))
    s = ein('bgnqhd,bgnkhd->bgnhqk', qb, kk) * (dh ** -0.5)
    qi = BLK_A + jnp.arange(BLK_A)
    ki = jnp.arange(2 * BLK_A)
    rel = qi[:, None] - ki[None, :]
    band_ok = (rel >= 0) & (rel <= nback)
    first_ok = (jnp.arange(NB)[:, None, None] > 0) | (ki[None, None, :] >= BLK_A)
    mask = (band_ok[None] & first_ok)[None, None, :, None]
    s = jnp.where(mask, s, -jnp.inf)
    m = jnp.max(s, axis=-1, keepdims=True)
    p = jnp.exp(s - m)
    l = jnp.sum(p, axis=-1, keepdims=True)
    o = ein('bgnhqk,bgnkhd->bgnhqd', p, vv) / l
    lse = (m + jnp.log(l))[..., 0]
    o = o.transpose(0, 2, 4, 1, 3, 5).reshape(B, Sp, H, dh)[:, :S]
    lse = lse.transpose(0, 2, 4, 1, 3).reshape(B, Sp, H)[:, :S]
    return o, lse


def dilated_sample(q, kx, vx, window, dil, n_buf):
    T = q.shape[1]
    nback = window // dil
    idx = n_buf + np.arange(T)[:, None] - dil * np.arange(nback + 1)[None, :]
    valid = jnp.asarray(idx >= 0)[None, :, None, :]
    idc = np.clip(idx, 0, None)
    kg, vg = kx[:, idc], vx[:, idc]
    s = ein('bthd,btjhd->bthj', q, kg) * (q.shape[-1] ** -0.5)
    s = jnp.where(valid, s, -jnp.inf)
    m = jnp.max(s, axis=-1, keepdims=True)
    p = jnp.exp(s - m)
    l = jnp.sum(p, axis=-1, keepdims=True)
    o = ein('bthj,btjhd->bthd', p, vg) / l
    return o, (m + jnp.log(l))[..., 0]


def combine_groups(outs, lses):
    w = jax.nn.softmax(jnp.stack(lses, axis=0), axis=0)
    return jnp.sum(w[..., None] * jnp.stack(outs, axis=0), axis=0)


def mlstm(q, k, v, ig, lf, C0, n0, m0, chunk):
    B, S, H, dh = q.shape
    NC = S // chunk
    L = chunk

    def chunks(x):
        x = x.astype(F32).reshape((B, NC, L) + x.shape[2:])
        return jnp.swapaxes(jnp.moveaxis(x, 1, 0), 2, 3)

    tri = jnp.tril(jnp.ones((L, L), dtype=bool))

    def step(carry, xs):
        C, n, m = carry
        qc, kc, vc, ic, fc = xs
        b = jnp.cumsum(fc, axis=-1)
        a = b + m[..., None]
        D = b[..., :, None] - b[..., None, :] + ic[..., None, :]
        D = jnp.where(tri, D, -jnp.inf)
        mt = jnp.maximum(a, jnp.max(D, axis=-1))
        Dw = jnp.exp(D - mt[..., None])
        iw = jnp.exp(a - mt)
        sc = jnp.einsum('bhtd,bhsd->bhts', qc, kc) * Dw
        num = iw[..., None] * jnp.einsum('bhed,bhtd->bhte', C, qc) + jnp.einsum('bhts,bhse->bhte', sc, vc)
        den = iw * jnp.einsum('bhd,bhtd->bht', n, qc) + jnp.sum(sc, axis=-1)
        h = num / jnp.maximum(jnp.abs(den), jnp.exp(-mt))[..., None]
        mL = mt[..., -1]
        wL = jnp.exp(b[..., -1:] - b + ic - mL[..., None])
        dec = jnp.exp(a[..., -1] - mL)
        C = dec[..., None, None] * C + jnp.einsum('bhs,bhse,bhsd->bhed', wL, vc, kc)
        n = dec[..., None] * n + jnp.einsum('bhs,bhsd->bhd', wL, kc)
        return (C, n, mL), h

    carry0 = (C0.astype(F32), n0.astype(F32), m0.astype(F32))
    (C, n, m), hs = lax.scan(step, carry0, (chunks(q), chunks(k), chunks(v), chunks(ig), chunks(lf)))
    hs = jnp.moveaxis(jnp.swapaxes(hs, 2, 3), 0, 1).reshape(B, S, H, dh)
    return hs, (C, n, m)


def ab_project(hn, w_in, b_if, pos):
    B, S, _ = hn.shape
    z = hn @ w_in
    cuts = np.cumsum([A_QW, A_QW, A_QW, B_W, B_W, B_W, H_B, H_B])
    qa, ka, va, qb, kb, vb, ig, fg, og = jnp.split(z, cuts, axis=-1)
    grp = lambda t: t.reshape(B, S, N_DIL, H_A, DH_A)
    hd = lambda t: t.reshape(B, S, H_B, DH_B)
    ig = ig.astype(F32) + b_if[0]
    lf = jax.nn.log_sigmoid(fg.astype(F32) + b_if[1])
    return (rope(grp(qa), pos), rope(grp(ka), pos), grp(va),
            hd(qb), hd(kb) * (DH_B ** -0.5), hd(vb), ig, lf, og)


def ab_merge(o_a, h_b, og, g_mn, w_out, dtype):
    B, S = o_a.shape[:2]
    hb = h_b * lax.rsqrt(jnp.mean(h_b * h_b, axis=-1, keepdims=True) + EPS)
    hb = (hb * g_mn.astype(F32).reshape(H_B, DH_B)).reshape(B, S, B_W) * jax.nn.sigmoid(og.astype(F32))
    cat = jnp.concatenate([o_a.reshape(B, S, H_A * DH_A), hb], axis=-1).astype(dtype)
    return cat @ w_out


def ab_prompt(hn, w_in, b_if, g_mn, w_out):
    B, S, _ = hn.shape
    pos = jnp.arange(S)
    qa, ka, va, qb, kb, vb, ig, lf, og = ab_project(hn, w_in, b_if, pos)
    outs, lses, bufs = [], [], []
    for gi, (win, dil) in enumerate(DIL_CFG):
        o, lse = dilated_prompt(qa[:, :, gi], ka[:, :, gi], va[:, :, gi], win, dil)
        outs.append(o)
        lses.append(lse)
        nb = min(win, S)
        bufs.append(jnp.stack([ka[:, S - nb:, gi], va[:, S - nb:, gi]], axis=2))
    o_a = combine_groups(outs, lses)
    C0 = jnp.zeros((B, H_B, DH_B, DH_B), F32)
    n0 = jnp.zeros((B, H_B, DH_B), F32)
    m0 = jnp.zeros((B, H_B), F32)
    h_b, st = mlstm(qb, kb, vb, ig, lf, C0, n0, m0, MLSTM_CHUNK)
    return ab_merge(o_a, h_b, og, g_mn, w_out, hn.dtype), bufs, st


def ab_sample(hn, bufs_in, C0, n0, m0, w_in, b_if, g_mn, w_out):
    B, T, _ = hn.shape
    pos = PAST_LEN + jnp.arange(T)
    qa, ka, va, qb, kb, vb, ig, lf, og = ab_project(hn, w_in, b_if, pos)
    outs, lses, bufs = [], [], []
    for gi, (win, dil) in enumerate(DIL_CFG):
        buf = bufs_in[gi]
        n_buf = buf.shape[1]
        new = jnp.stack([ka[:, :, gi], va[:, :, gi]], axis=2)
        ext = jnp.concatenate([buf.astype(new.dtype), new], axis=1)
        o, lse = dilated_sample(qa[:, :, gi], ext[:, :, 0], ext[:, :, 1], win, dil, n_buf)
        outs.append(o)
        lses.append(lse)
        bufs.append(ext[:, T:])
    o_a = combine_groups(outs, lses)
    h_b, st = mlstm(qb, kb, vb, ig, lf, C0, n0, m0, T)
    return ab_merge(o_a, h_b, og, g_mn, w_out, hn.dtype), bufs, st


def c_project(hn, w_in, pos):
    B, S, _ = hn.shape
    z = hn @ w_in
    q = z[..., :C_Q].reshape(B, S, KVH_C, G_C, DH_C)
    kv = z[..., C_Q:C_Q + C_KV].reshape(B, S, 3, 2, KVH_C, DH_C)
    gates = jax.nn.sigmoid(z[..., C_Q + C_KV:].astype(F32)).reshape(B, S, KVH_C, G_C, 3)
    kv_cmp = kv[:, :, 0]
    kv_sel = jnp.stack([rope(kv[:, :, 1, 0], pos), kv[:, :, 1, 1]], axis=2)
    kv_win = jnp.stack([rope(kv[:, :, 2, 0], pos), kv[:, :, 2, 1]], axis=2)
    return q, rope(q, pos), kv_cmp, kv_sel, kv_win, gates


def compress(x, w1, w2, pe):
    B, L = x.shape[:2]
    n_cmp = (L - CMP_BLK) // CMP_STRIDE + 1
    ch = x[:, :(n_cmp + 1) * CMP_STRIDE].reshape(B, n_cmp + 1, CMP_STRIDE, KVH_C, DH_C)
    hid = (ein('bcrhd,rde->bche', ch[:, :-1], w1[:CMP_STRIDE])
           + ein('bcrhd,rde->bche', ch[:, 1:], w1[CMP_STRIDE:])
           + ein('rd,rde->e', pe, w1))
    return ein('bche,ed->bchd', jax.nn.gelu(hid), w2)


def slc_importance(p_grp, n_slc):
    n_cmp = p_grp.shape[-1]
    ratio = SLC_BLK // CMP_STRIDE
    offs = np.arange(1 - CMP_BLK // CMP_STRIDE, ratio)
    idx = ratio * np.arange(n_slc)[:, None] + offs[None, :]
    ok = jnp.asarray((idx >= 0) & (idx < n_cmp))
    g = p_grp[..., np.clip(idx, 0, n_cmp - 1)]
    return jnp.sum(jnp.where(ok, g, 0.0), axis=-1)


def nsa_attend(q, q_rot, gates, t, kc, vc, cmp_end, gather_sel, kw, vw, kw_pos, n_slc):
    scale = DH_C ** -0.5
    B, Q = q.shape[:2]
    mc = (cmp_end[None, :] <= t[:, None])[None, :, None, None, :]
    p_cmp = masked_softmax(ein('bqhgd,bnhd->bqhgn', q, kc) * scale, mc)
    o_cmp = ein('bqhgn,bnhd->bqhgd', p_cmp, vc)
    imp = slc_importance(jnp.sum(p_cmp, axis=3), n_slc)
    j = jnp.arange(n_slc)[None, :]
    cur = (t // SLC_BLK)[:, None]
    forced = ((j == 0) | (j == cur) | (j == cur - 1))[None, :, None, :]
    causal = (j <= cur)[None, :, None, :]
    imp = jnp.where(forced, jnp.inf, jnp.where(causal, imp, -jnp.inf))
    val, idx = lax.top_k(imp, min(N_SEL, n_slc))
    ks, vs = gather_sel(idx)
    kpos = idx[..., None] * SLC_BLK + jnp.arange(SLC_BLK)
    ms = (val > -jnp.inf)[..., None] & (kpos <= t[None, :, None, None, None])
    nk = ks.shape[3] * SLC_BLK
    ks = ks.reshape(B, Q, KVH_C, nk, DH_C)
    vs = vs.reshape(B, Q, KVH_C, nk, DH_C)
    ms = ms.reshape(B, Q, KVH_C, 1, nk)
    o_sel = ein('bqhgk,bqhkd->bqhgd', masked_softmax(ein('bqhgd,bqhkd->bqhgk', q_rot, ks) * scale, ms), vs)
    dlt = t[:, None] - kw_pos[None, :]
    mw = ((kw_pos[None, :] >= 0) & (dlt >= 0) & (dlt < WIN_C))[None, :, None, None, :]
    o_win = ein('bqhgn,bnhd->bqhgd', masked_softmax(ein('bqhgd,bnhd->bqhgn', q_rot, kw) * scale, mw), vw)
    o = gates[..., 0:1] * o_cmp + gates[..., 1:2] * o_sel + gates[..., 2:3] * o_win
    return o.reshape(B, Q, H_C * DH_C)


def c_prompt(hn, w_in, cmp_w1, cmp_w2, cmp_pe, w_out):
    B, S, _ = hn.shape
    pos = jnp.arange(S)
    q, q_rot, kv_cmp, kv_sel, kv_win, gates = c_project(hn, w_in, pos)
    kc = compress(kv_cmp[:, :, 0], cmp_w1[0], cmp_w2[0], cmp_pe[0])
    vc = compress(kv_cmp[:, :, 1], cmp_w1[1], cmp_w2[1], cmp_pe[1])
    cmp_end = jnp.arange(kc.shape[1]) * CMP_STRIDE + CMP_BLK - 1
    n_slc = S // SLC_BLK
    ksb = kv_sel[:, :, 0].reshape(B, n_slc, SLC_BLK, KVH_C, DH_C).transpose(0, 3, 1, 2, 4)
    vsb = kv_sel[:, :, 1].reshape(B, n_slc, SLC_BLK, KVH_C, DH_C).transpose(0, 3, 1, 2, 4)
    bi = jnp.arange(B)[:, None, None, None]
    hi = jnp.arange(KVH_C)[None, None, :, None]

    def gather_sel(idx):
        return ksb[bi, hi, idx], vsb[bi, hi, idx]

    kwp = jnp.pad(kv_win, ((0, 0), (WIN_C, 0), (0, 0), (0, 0), (0, 0)))

    def block(i):
        s0 = i * QBLK_C
        sl = lambda a: lax.dynamic_slice_in_dim(a, s0, QBLK_C, axis=1)
        t = s0 + jnp.arange(QBLK_C)
        kwb = lax.dynamic_slice_in_dim(kwp, s0, WIN_C + QBLK_C, axis=1)
        kw_pos = s0 - WIN_C + jnp.arange(WIN_C + QBLK_C)
        return nsa_attend(sl(q), sl(q_rot), sl(gates), t, kc, vc, cmp_end, gather_sel,
                          kwb[:, :, 0], kwb[:, :, 1], kw_pos, n_slc)

    o = lax.map(block, jnp.arange(S // QBLK_C))
    o = o.transpose(1, 0, 2, 3).reshape(B, S, H_C * DH_C)
    nw = min(WIN_C, S)
    return o.astype(hn.dtype) @ w_out, kv_cmp, kv_sel, kv_win[:, S - nw:]


def c_sample(hn, cmp_pool, sel_pool, win_buf, page_table, w_in, cmp_w1, cmp_w2, cmp_pe, w_out):
    DB, T, _ = hn.shape
    pos = PAST_LEN + jnp.arange(T)
    q, q_rot, kv_cmp, kv_sel, kv_win, gates = c_project(hn, w_in, pos)
    n_pages = page_table.shape[1]
    past = n_pages * PAGE_SIZE
    past_cmp = cmp_pool[page_table].reshape(DB, past, 2, KVH_C, DH_C).astype(kv_cmp.dtype)
    full = jnp.concatenate([past_cmp, kv_cmp], axis=1)
    kc = compress(full[:, :, 0], cmp_w1[0], cmp_w2[0], cmp_pe[0])
    vc = compress(full[:, :, 1], cmp_w1[1], cmp_w2[1], cmp_pe[1])
    cmp_end = jnp.arange(kc.shape[1]) * CMP_STRIDE + CMP_BLK - 1
    n_past_blk = past // SLC_BLK
    n_new_blk = -(-T // SLC_BLK)
    new_blocks = jnp.pad(kv_sel, ((0, 0), (0, n_new_blk * SLC_BLK - T), (0, 0), (0, 0), (0, 0)))
    new_blocks = new_blocks.reshape(DB, n_new_blk, SLC_BLK, 2, KVH_C, DH_C).transpose(0, 4, 1, 2, 3, 5)
    bi = jnp.arange(DB)[:, None, None, None]
    hi = jnp.arange(KVH_C)[None, None, :, None]

    def gather_sel(idx):
        row0 = idx * SLC_BLK
        phys = page_table[bi, jnp.clip(row0 // PAGE_SIZE, 0, n_pages - 1)]
        rows = (row0 % PAGE_SIZE)[..., None] + jnp.arange(SLC_BLK)
        g_past = sel_pool[phys[..., None], rows, :, hi[..., None]]
        g_new = new_blocks[bi, hi, jnp.clip(idx - n_past_blk, 0, n_new_blk - 1)]
        g = jnp.where((idx >= n_past_blk)[..., None, None, None], g_new, g_past)
        return g[..., 0, :], g[..., 1, :]

    n_buf = win_buf.shape[1]
    ext = jnp.concatenate([win_buf.astype(kv_win.dtype), kv_win], axis=1)
    kw_pos = PAST_LEN - n_buf + jnp.arange(n_buf + T)
    o = nsa_attend(q, q_rot, gates, pos, kc, vc, cmp_end, gather_sel,
                   ext[:, :, 0], ext[:, :, 1], kw_pos, n_past_blk + n_new_blk)
    return o.astype(hn.dtype) @ w_out, kv_cmp, kv_sel, ext[:, T:]


def setup_inputs(seed: int = 0) -> dict:
    key = jax.random.key(seed)
    ks = jax.random.split(key, 32)
    nrm = lambda i, shape, scale=1.0: scale * jax.random.normal(ks[i], shape, F32)
    n_pages = PAST_LEN // PAGE_SIZE
    n_phys = (5 * DEC_BATCH * n_pages + 3) // 4
    inp = {}
    inp['x_prompt'] = nrm(0, (BATCH, SEQ, D_MODEL))
    inp['x_sample'] = nrm(1, (DEC_BATCH, DEC_SEQ, D_MODEL))
    for g, (win, _) in enumerate(DIL_CFG):
        inp['cache_a%d_kv' % g] = nrm(2 + g, (N_EVEN, DEC_BATCH, min(win, PAST_LEN), 2, H_A, DH_A))
    inp['state_b_C'] = nrm(5, (N_EVEN, DEC_BATCH, H_B, DH_B, DH_B), DH_B ** -0.5)
    inp['state_b_n'] = nrm(6, (N_EVEN, DEC_BATCH, H_B, DH_B), DH_B ** -0.5)
    inp['state_b_m'] = jax.random.uniform(ks[7], (N_EVEN, DEC_BATCH, H_B), F32, 0.0, 2.0)
    inp['cache_c_cmp_kv'] = nrm(8, (N_ODD, n_phys, PAGE_SIZE, 2, KVH_C, DH_C))
    inp['cache_c_sel_kv'] = nrm(9, (N_ODD, n_phys, PAGE_SIZE, 2, KVH_C, DH_C))
    inp['cache_c_win_kv'] = nrm(10, (N_ODD, DEC_BATCH, min(WIN_C, PAST_LEN), 2, KVH_C, DH_C))
    perm = jax.random.permutation(ks[11], n_phys)
    inp['page_table'] = perm[:DEC_BATCH * n_pages].reshape(DEC_BATCH, n_pages).astype(jnp.int32)
    inp['norm_g'] = 1.0 + nrm(12, (DEPTH, 2, D_MODEL), 0.02)
    inp['w_in_ab'] = nrm(13, (N_EVEN, D_MODEL, IN_AB), D_MODEL ** -0.5)
    inp['b_if'] = jnp.stack([nrm(14, (N_EVEN, H_B), 0.1),
                             3.0 + 3.0 * jax.random.uniform(ks[15], (N_EVEN, H_B), F32)], axis=1)
    inp['g_mlstm'] = 1.0 + nrm(16, (N_EVEN, B_W), 0.02)
    inp['w_out_ab'] = nrm(17, (N_EVEN, MIX_AB, D_MODEL), MIX_AB ** -0.5)
    inp['w_in_c'] = nrm(18, (N_ODD, D_MODEL, IN_C), D_MODEL ** -0.5)
    inp['cmp_w1'] = nrm(19, (N_ODD, 2, CMP_BLK, DH_C, CMP_HID), (CMP_BLK * DH_C) ** -0.5)
    inp['cmp_w2'] = nrm(20, (N_ODD, 2, CMP_HID, DH_C), CMP_HID ** -0.5)
    inp['cmp_pe'] = nrm(21, (N_ODD, 2, CMP_BLK, DH_C), 0.5)
    inp['w_out_c'] = nrm(22, (N_ODD, C_Q, D_MODEL), C_Q ** -0.5)
    inp['w_ffn_gate'] = nrm(23, (DEPTH, D_MODEL, D_FF), D_MODEL ** -0.5)
    inp['w_ffn_up'] = nrm(24, (DEPTH, D_MODEL, D_FF), D_MODEL ** -0.5)
    inp['w_ffn_down'] = nrm(25, (DEPTH, D_FF, D_MODEL), D_FF ** -0.5)
    inp['norm_final'] = 1.0 + nrm(26, (D_MODEL,), 0.02)
    return inp


def reference(x_prompt, x_sample, cache_a0_kv, cache_a1_kv, cache_a2_kv, state_b_C, state_b_n, state_b_m,
              cache_c_cmp_kv, cache_c_sel_kv, cache_c_win_kv, page_table, norm_g, w_in_ab, b_if, g_mlstm,
              w_out_ab, w_in_c, cmp_w1, cmp_w2, cmp_pe, w_out_c, w_ffn_gate, w_ffn_up, w_ffn_down, norm_final):
    hp, hs = x_prompt, x_sample
    a_p, a_s = [[], [], []], [[], [], []]
    bC_p, bC_s, bn_p, bn_s, bm_p, bm_s = [], [], [], [], [], []
    cc_p, cc_s, csl_p, csl_s, cw_p, cw_s = [], [], [], [], [], []
    for layer in range(DEPTH):
        hn_p = rms_norm(hp, norm_g[layer, 0])
        hn_s = rms_norm(hs, norm_g[layer, 0])
        if layer % 2 == 0:
            e = layer // 2
            yp, bufp, (Cp, n_p, m_p) = ab_prompt(hn_p, w_in_ab[e], b_if[e], g_mlstm[e], w_out_ab[e])
            ys, bufs, (Cs, n_s, m_s) = ab_sample(hn_s, (cache_a0_kv[e], cache_a1_kv[e], cache_a2_kv[e]),
                                                 state_b_C[e], state_b_n[e], state_b_m[e],
                                                 w_in_ab[e], b_if[e], g_mlstm[e], w_out_ab[e])
            for gi in range(N_DIL):
                a_p[gi].append(bufp[gi])
                a_s[gi].append(bufs[gi])
            bC_p.append(Cp); bC_s.append(Cs)
            bn_p.append(n_p); bn_s.append(n_s)
            bm_p.append(m_p); bm_s.append(m_s)
        else:
            o = layer // 2
            yp, ccp, cslp, cwp = c_prompt(hn_p, w_in_c[o], cmp_w1[o], cmp_w2[o], cmp_pe[o], w_out_c[o])
            ys, ccs, csls, cws = c_sample(hn_s, cache_c_cmp_kv[o], cache_c_sel_kv[o], cache_c_win_kv[o],
                                          page_table, w_in_c[o], cmp_w1[o], cmp_w2[o], cmp_pe[o], w_out_c[o])
            cc_p.append(ccp); cc_s.append(ccs)
            csl_p.append(cslp); csl_s.append(csls)
            cw_p.append(cwp); cw_s.append(cws)
        hp = hp + yp
        hs = hs + ys
        hp = hp + swiglu(rms_norm(hp, norm_g[layer, 1]), w_ffn_gate[layer], w_ffn_up[layer], w_ffn_down[layer])
        hs = hs + swiglu(rms_norm(hs, norm_g[layer, 1]), w_ffn_gate[layer], w_ffn_up[layer], w_ffn_down[layer])
    y_prompt = rms_norm(hp, norm_final)
    y_sample = rms_norm(hs, norm_final)
    st = lambda xs: jnp.stack(xs, axis=0)
    return (y_prompt, y_sample,
            st(a_p[0]), st(a_s[0]), st(a_p[1]), st(a_s[1]), st(a_p[2]), st(a_s[2]),
            st(bC_p), st(bC_s), st(bn_p), st(bn_s), st(bm_p), st(bm_s),
            st(cc_p), st(cc_s), st(csl_p), st(csl_s), st(cw_p), st(cw_s))
```

```python
import functools
import math

import numpy as np
import jax
import jax.numpy as jnp
from jax import lax
from jax.experimental import pallas as pl
from jax.experimental.pallas import tpu as pltpu

F32 = jnp.float32
BF16 = jnp.bfloat16

PAGE_SIZE = 128
DIL_CFG = ((128, 1), (512, 4), (2048, 16))
N_DIL = 3
H_A = 4
DH_A = 64
BLK_A = 128
H_B = 4
DH_B = 128
MLSTM_CHUNK = 128
H_C = 16
KVH_C = 2
G_C = H_C // KVH_C
DH_C = 64
CMP_STRIDE = 16
CMP_BLK = 2 * CMP_STRIDE
CMP_HID = 64
SLC_BLK = 64
N_SEL = 16
WIN_C = 512
QBLK_C = 128
ROPE_THETA = 10000.0
EPS = 1e-6
A_QW = N_DIL * H_A * DH_A
B_W = H_B * DH_B
AW = H_A * DH_A
C_Q = H_C * DH_C
C_KV = 3 * 2 * KVH_C * DH_C
KVW = 2 * KVH_C * DH_C

LANE = 128
SUBLANE = 8
VMEM_LIMIT = 48 * 1024 * 1024

NEG_BIG = -1e30
T_PAD = SUBLANE

AB_QB, AB_KB, AB_VB, AB_OG = 0, B_W, 2 * B_W, 3 * B_W
AB_QA = 4 * B_W
AB_KA = AB_QA + A_QW
AB_VA = AB_KA + A_QW
AB_IG = AB_VA + A_QW
AB_FG = AB_IG + LANE
AB_N = AB_FG + LANE

C_CMP = C_Q
C_SEL = C_Q + KVW
C_WIN = C_Q + 2 * KVW
C_GATE = C_Q + 3 * KVW
C_N = 2048


def _cparams(sem, vmem=VMEM_LIMIT):
    return pltpu.CompilerParams(dimension_semantics=sem, vmem_limit_bytes=vmem)


def _pick_tile(m, pref):
    t = min(m, pref)
    while m % t:
        t //= 2
    return t


def _rope_lanes(x, cos, sin):
    lane = lax.broadcasted_iota(jnp.int32, x.shape, 1)
    first = (lane % DH_C) < (DH_C // 2)
    partner = jnp.where(first, pltpu.roll(x, LANE - DH_C // 2, 1), pltpu.roll(x, DH_C // 2, 1))
    return x * cos + partner * sin


def _split3(x):
    hi = x.astype(BF16)
    r1 = x - hi.astype(F32)
    mid = r1.astype(BF16)
    lo = (r1 - mid.astype(F32)).astype(BF16)
    return hi, mid, lo


def _dot_exact_rhs01(x, m01):
    hi, mid, lo = _split3(x)
    d = lambda a: jnp.dot(a, m01, preferred_element_type=F32)
    return d(hi) + d(mid) + d(lo)


def _dot_exact_lhs01(m01, x):
    hi, mid, lo = _split3(x)
    d = lambda a: jnp.dot(m01, a, preferred_element_type=F32)
    return d(hi) + d(mid) + d(lo)


def _dot_nt(a, b):
    return lax.dot_general(a, b, (((1,), (1,)), ((), ())), preferred_element_type=F32)


def _dot_tn(a, b):
    return lax.dot_general(a, b, (((0,), (0,)), ((), ())), preferred_element_type=F32)


def _log_sigmoid(x):
    return jnp.minimum(x, 0.0) - jnp.log1p(jnp.exp(-jnp.abs(x)))


def _gelu_tanh(x):
    return 0.5 * x * (1.0 + jnp.tanh(math.sqrt(2.0 / math.pi) * (x + 0.044715 * (x * x * x))))


def _rms_rows(x, g):
    ms = jnp.mean(x * x, axis=-1, keepdims=True)
    return x * lax.rsqrt(ms + EPS) * g


def _norm_proj_kernel(flags_ref, x_ref, g_ref, w_ref, cos_ref, sin_ref, o_ref, xn_ref, *, tn):
    j = pl.program_id(1)

    @pl.when(j == 0)
    def _():
        xn_ref[...] = _rms_rows(x_ref[...], g_ref[...]).astype(BF16)

    acc = jnp.dot(xn_ref[...], w_ref[...], preferred_element_type=F32)
    nchunk = tn // LANE
    for c in range(nchunk):
        a = acc[:, c * LANE:(c + 1) * LANE]
        flag = flags_ref[j * nchunk + c]

        @pl.when(flag == 1)
        def _():
            o_ref[:, c * LANE:(c + 1) * LANE] = _rope_lanes(a, cos_ref[...], sin_ref[...])

        @pl.when(flag == 0)
        def _():
            o_ref[:, c * LANE:(c + 1) * LANE] = a


def norm_proj(x, g, w_bf16, rope_flags, cos_t, sin_t, *, tm_pref=512, tn=256):
    m, d = x.shape
    n = w_bf16.shape[1]
    tm = _pick_tile(m, tm_pref)
    grid = (m // tm, n // tn)
    gs = pltpu.PrefetchScalarGridSpec(
        num_scalar_prefetch=1,
        grid=grid,
        in_specs=[
            pl.BlockSpec((tm, d), lambda i, j, f: (i, 0)),
            pl.BlockSpec((1, d), lambda i, j, f: (0, 0)),
            pl.BlockSpec((d, tn), lambda i, j, f: (0, j)),
            pl.BlockSpec((tm, LANE), lambda i, j, f: (i, 0)),
            pl.BlockSpec((tm, LANE), lambda i, j, f: (i, 0)),
        ],
        out_specs=pl.BlockSpec((tm, tn), lambda i, j, f: (i, j)),
        scratch_shapes=[pltpu.VMEM((tm, d), BF16)],
    )
    return pl.pallas_call(
        functools.partial(_norm_proj_kernel, tn=tn),
        grid_spec=gs,
        out_shape=jax.ShapeDtypeStruct((m, n), F32),
        compiler_params=_cparams(("parallel", "arbitrary")),
        name="norm_proj",
    )(rope_flags, x, g.reshape(1, d), w_bf16, cos_t, sin_t)


def _ffn_kernel(x_ref, g_ref, wg_ref, wu_ref, wd_ref, gf_ref, o_ref, xn_ref, *, final_norm):
    f = pl.program_id(1)

    @pl.when(f == 0)
    def _():
        x = x_ref[...]
        xn_ref[...] = _rms_rows(x, g_ref[...]).astype(BF16)
        o_ref[...] = x

    xn = xn_ref[...]
    a = jnp.dot(xn, wg_ref[...], preferred_element_type=F32)
    u = jnp.dot(xn, wu_ref[...], preferred_element_type=F32)
    act = (a * jax.nn.sigmoid(a)) * u
    o_ref[...] += jnp.dot(act.astype(BF16), wd_ref[...], preferred_element_type=F32)

    if final_norm:
        @pl.when(f == pl.num_programs(1) - 1)
        def _():
            o_ref[...] = _rms_rows(o_ref[...], gf_ref[...])


def ffn(x, g, wg, wu, wd, g_final, *, final_norm, tm_pref=512):
    m, d = x.shape
    dff = wg.shape[1]
    tf = dff // 2 if (dff // 2) % LANE == 0 else dff
    tm = _pick_tile(m, tm_pref)
    grid = (m // tm, dff // tf)
    return pl.pallas_call(
        functools.partial(_ffn_kernel, final_norm=final_norm),
        grid=grid,
        in_specs=[
            pl.BlockSpec((tm, d), lambda i, f: (i, 0)),
            pl.BlockSpec((1, d), lambda i, f: (0, 0)),
            pl.BlockSpec((d, tf), lambda i, f: (0, f)),
            pl.BlockSpec((d, tf), lambda i, f: (0, f)),
            pl.BlockSpec((tf, d), lambda i, f: (f, 0)),
            pl.BlockSpec((1, d), lambda i, f: (0, 0)),
        ],
        out_specs=pl.BlockSpec((tm, d), lambda i, f: (i, 0)),
        out_shape=jax.ShapeDtypeStruct((m, d), F32),
        scratch_shapes=[pltpu.VMEM((tm, d), BF16)],
        compiler_params=_cparams(("parallel", "arbitrary")),
        name="ffn",
    )(x, g.reshape(1, d), wg, wu, wd, g_final.reshape(1, d))


def _out_proj_kernel(x_ref, w_ref, r_ref, o_ref):
    o_ref[...] = r_ref[...] + jnp.dot(x_ref[...].astype(BF16), w_ref[...], preferred_element_type=F32)


def out_proj(x, w_bf16, resid, *, tm_pref=512):
    m, k = x.shape
    n = w_bf16.shape[1]
    tm = _pick_tile(m, tm_pref)
    return pl.pallas_call(
        _out_proj_kernel,
        grid=(m // tm,),
        in_specs=[
            pl.BlockSpec((tm, k), lambda i: (i, 0)),
            pl.BlockSpec((k, n), lambda i: (0, 0)),
            pl.BlockSpec((tm, n), lambda i: (i, 0)),
        ],
        out_specs=pl.BlockSpec((tm, n), lambda i: (i, 0)),
        out_shape=jax.ShapeDtypeStruct((m, n), F32),
        compiler_params=_cparams(("parallel",)),
        name="out_proj",
    )(x, w_bf16, resid)


def _dil_prompt_kernel(q_ref, kp_ref, kc_ref, vp_ref, vc_ref, o_ref, l_ref, *, nback):
    n = pl.program_id(2)
    q = q_ref[0] * (DH_A ** -0.5)
    kk = jnp.concatenate([kp_ref[0], kc_ref[0]], axis=0)
    vv = jnp.concatenate([vp_ref[0], vc_ref[0]], axis=0)
    qi = BLK_A + lax.broadcasted_iota(jnp.int32, (BLK_A, 2 * BLK_A), 0)
    ki = lax.broadcasted_iota(jnp.int32, (BLK_A, 2 * BLK_A), 1)
    rel = qi - ki
    mask = (rel >= 0) & (rel <= nback) & ((n > 0) | (ki >= BLK_A))
    outs, lses = [], []
    for h in range(H_A):
        hs = slice(h * DH_A, (h + 1) * DH_A)
        s = _dot_nt(q[:, hs].astype(BF16), kk[:, hs].astype(BF16))
        s = jnp.where(mask, s, -jnp.inf)
        m = jnp.max(s, axis=-1, keepdims=True)
        p = jnp.exp(s - m)
        l = jnp.sum(p, axis=-1, keepdims=True)
        o = jnp.dot(p.astype(BF16), vv[:, hs].astype(BF16), preferred_element_type=F32) / l
        outs.append(o)
        lses.append(jnp.broadcast_to(m + jnp.log(l), (BLK_A, DH_A)))
    o_ref[0] = jnp.concatenate(outs, axis=-1)
    l_ref[0] = jnp.concatenate(lses, axis=-1)


def dil_prompt(z3, gi, window, dil):
    b, s, n = z3.shape
    nback = window // dil
    assert s % (dil * BLK_A) == 0
    m = s // dil
    nb = m // BLK_A
    zv = z3.reshape(b, m, dil * n)
    cpb = n // AW
    qo, ko, vo = AB_QA // AW + gi, AB_KA // AW + gi, AB_VA // AW + gi
    blk = (1, BLK_A, AW)
    in_specs = [
        pl.BlockSpec(blk, lambda bb, r, i: (bb, i, r * cpb + qo)),
        pl.BlockSpec(blk, lambda bb, r, i: (bb, jnp.maximum(i - 1, 0), r * cpb + ko)),
        pl.BlockSpec(blk, lambda bb, r, i: (bb, i, r * cpb + ko)),
        pl.BlockSpec(blk, lambda bb, r, i: (bb, jnp.maximum(i - 1, 0), r * cpb + vo)),
        pl.BlockSpec(blk, lambda bb, r, i: (bb, i, r * cpb + vo)),
    ]
    out_spec = pl.BlockSpec(blk, lambda bb, r, i: (bb, i, r))
    o, l = pl.pallas_call(
        functools.partial(_dil_prompt_kernel, nback=nback),
        grid=(b, dil, nb),
        in_specs=in_specs,
        out_specs=[out_spec, out_spec],
        out_shape=[jax.ShapeDtypeStruct((b, m, dil * AW), F32)] * 2,
        compiler_params=_cparams(("parallel", "parallel", "arbitrary")),
        name="dil_prompt_%d" % gi,
    )(zv, zv, zv, zv, zv)
    return o.reshape(b * s, AW), l.reshape(b * s, AW)


def _dil_sample_kernel(c_ref, q_ref, kn_ref, vn_ref, o_ref, l_ref, co_ref, *, window, dil, n_buf, t_new):
    rows = H_A * T_PAD
    q = q_ref[0] * (DH_A ** -0.5)
    q4 = jnp.concatenate([q] * H_A, axis=0)
    rr = lax.broadcasted_iota(jnp.int32, (rows, AW), 0)
    ll = lax.broadcasted_iota(jnp.int32, (rows, AW), 1)
    head_sel = (rr // T_PAD) == (ll // DH_A)
    qbd = jnp.where(head_sel, q4, 0.0).astype(BF16)
    kb = c_ref[0, :, 0:AW].astype(BF16)
    vb = c_ref[0, :, AW:2 * AW].astype(BF16)
    kn = kn_ref[0]
    vn = vn_ref[0]
    s_buf = _dot_nt(qbd, kb)
    s_new = _dot_nt(qbd, kn.astype(BF16))
    t_b = lax.broadcasted_iota(jnp.int32, (rows, n_buf), 0) % T_PAD
    c_b = lax.broadcasted_iota(jnp.int32, (rows, n_buf), 1)
    d_b = n_buf + t_b - c_b
    m_b = ((d_b % dil) == 0) & (d_b <= window)
    t_n = lax.broadcasted_iota(jnp.int32, (rows, T_PAD), 0) % T_PAD
    u_n = lax.broadcasted_iota(jnp.int32, (rows, T_PAD), 1)
    d_n = t_n - u_n
    m_n = (d_n >= 0) & ((d_n % dil) == 0) & (d_n <= window) & (u_n < t_new)
    s_buf = jnp.where(m_b, s_buf, NEG_BIG)
    s_new = jnp.where(m_n, s_new, NEG_BIG)
    mx = jnp.maximum(jnp.max(s_buf, axis=-1, keepdims=True), jnp.max(s_new, axis=-1, keepdims=True))
    p_b = jnp.where(m_b, jnp.exp(s_buf - mx), 0.0)
    p_n = jnp.where(m_n, jnp.exp(s_new - mx), 0.0)
    l = jnp.sum(p_b, axis=-1, keepdims=True) + jnp.sum(p_n, axis=-1, keepdims=True)
    l = jnp.maximum(l, 1e-30)
    acc = (jnp.dot(p_b.astype(BF16), vb, preferred_element_type=F32)
           + jnp.dot(p_n.astype(BF16), vn.astype(BF16), preferred_element_type=F32))
    res = jnp.where(head_sel, acc / l, 0.0)
    lse = jnp.where(head_sel, mx + jnp.log(l), 0.0)
    o = res[0:T_PAD]
    ls = lse[0:T_PAD]
    for h in range(1, H_A):
        o = o + res[h * T_PAD:(h + 1) * T_PAD]
        ls = ls + lse[h * T_PAD:(h + 1) * T_PAD]
    o_ref[0] = o
    l_ref[0] = ls
    co_ref[0, 0:n_buf - t_new, :] = c_ref[0, t_new:n_buf, :]
    co_ref[0, n_buf - t_new:n_buf, 0:AW] = kn[0:t_new]
    co_ref[0, n_buf - t_new:n_buf, AW:2 * AW] = vn[0:t_new]


def dil_sample(cache, z3, gi, window, dil, t_new):
    db, n_buf, _ = cache.shape
    qo, ko, vo = AB_QA // AW + gi, AB_KA // AW + gi, AB_VA // AW + gi
    blk = (1, T_PAD, AW)
    o, l, co = pl.pallas_call(
        functools.partial(_dil_sample_kernel, window=window, dil=dil, n_buf=n_buf, t_new=t_new),
        grid=(db,),
        in_specs=[
            pl.BlockSpec((1, n_buf, 2 * AW), lambda b: (b, 0, 0)),
            pl.BlockSpec(blk, lambda b: (b, 0, qo)),
            pl.BlockSpec(blk, lambda b: (b, 0, ko)),
            pl.BlockSpec(blk, lambda b: (b, 0, vo)),
        ],
        out_specs=[
            pl.BlockSpec(blk, lambda b: (b, 0, 0)),
            pl.BlockSpec(blk, lambda b: (b, 0, 0)),
            pl.BlockSpec((1, n_buf, 2 * AW), lambda b: (b, 0, 0)),
        ],
        out_shape=[
            jax.ShapeDtypeStruct((db, T_PAD, AW), F32),
            jax.ShapeDtypeStruct((db, T_PAD, AW), F32),
            jax.ShapeDtypeStruct((db, n_buf, 2 * AW), F32),
        ],
        compiler_params=_cparams(("parallel",)),
        name="dil_sample_%d" % gi,
    )(cache, z3, z3, z3)
    return o.reshape(db * T_PAD, AW), l.reshape(db * T_PAD, AW), co


def _mlstm_kernel(q_ref, k_ref, v_ref, gi_ref, gf_ref, gt_ref, bi_ref, bf_ref, b8_ref,
                  c0_ref, n0_ref, m0_ref, h_ref, co_ref, no_ref, mo_ref,
                  c_s, n_s, m_s, *, bb, L, n_valid):
    c = pl.program_id(1)

    @pl.when(c == 0)
    def _():
        c_s[...] = c0_ref[...]
        n_s[...] = n0_ref[...]
        m_s[...] = m0_ref[...]

    row = lax.broadcasted_iota(jnp.int32, (L, L), 0)
    col = lax.broadcasted_iota(jnp.int32, (L, L), 1)
    tri = row >= col
    tri_l = jnp.where(tri, 1.0, 0.0).astype(BF16)
    tri_u = jnp.where(row <= col, 1.0, 0.0).astype(BF16)
    lane = lax.broadcasted_iota(jnp.int32, (1, LANE), 1)
    for b in range(bb):
        ig_col = gi_ref[b] + bi_ref[...]
        lf_col = _log_sigmoid(gf_ref[b] + bf_ref[...])
        gt = gt_ref[b, 0]
        ig_row = gt[0:H_B] + b8_ref[0:H_B]
        lf_row = _log_sigmoid(gt[H_B:2 * H_B] + b8_ref[H_B:2 * H_B])
        if n_valid < L:
            rv = lax.broadcasted_iota(jnp.int32, (L, LANE), 0) < n_valid
            ig_col = jnp.where(rv, ig_col, -jnp.inf)
            lf_col = jnp.where(rv, lf_col, 0.0)
            cv = lax.broadcasted_iota(jnp.int32, (H_B, L), 1) < n_valid
            ig_row = jnp.where(cv, ig_row, -jnp.inf)
            lf_row = jnp.where(cv, lf_row, 0.0)
        b_col = _dot_exact_lhs01(tri_l, lf_col)
        b_row = _dot_exact_rhs01(lf_row, tri_u)
        m_row = m_s[b]
        a_col = b_col + m_row
        m_new = m_row
        hs_out = []
        for h in range(H_B):
            hsl = slice(h * DH_B, (h + 1) * DH_B)
            bc = b_col[:, h:h + 1]
            ac = a_col[:, h:h + 1]
            icol = ig_col[:, h:h + 1]
            D = bc - b_row[h:h + 1, :] + ig_row[h:h + 1, :]
            D = jnp.where(tri, D, -jnp.inf)
            mt = jnp.maximum(ac, jnp.max(D, axis=-1, keepdims=True))
            Dw = jnp.exp(D - mt)
            iw = jnp.exp(ac - mt)
            qf = q_ref[b, :, hsl]
            kf = k_ref[b, :, hsl] * (DH_B ** -0.5)
            vf = v_ref[b, :, hsl]
            qb, kb, vb = qf.astype(BF16), kf.astype(BF16), vf.astype(BF16)
            Cm = c_s[b * H_B + h]
            nv = n_s[b * H_B + h]
            sc = _dot_nt(qb, kb) * Dw
            num = iw * _dot_nt(qb, Cm.astype(BF16)) + jnp.dot(sc.astype(BF16), vb, preferred_element_type=F32)
            den = iw * jnp.sum(qf * nv, axis=-1, keepdims=True) + jnp.sum(sc, axis=-1, keepdims=True)
            hs_out.append(num / jnp.maximum(jnp.abs(den), jnp.exp(-mt)))
            mL = mt[L - 1:L, :]
            wL = jnp.exp(bc[L - 1:L, :] - bc + icol - mL)
            dec = jnp.exp(ac[L - 1:L, :] - mL)
            c_s[b * H_B + h] = dec * Cm + _dot_tn((vf * wL).astype(BF16), kb)
            n_s[b * H_B + h] = dec * nv + jnp.sum(wL * kf, axis=0, keepdims=True)
            m_new = jnp.where(lane == h, mL, m_new)
        m_s[b] = m_new
        h_ref[b] = jnp.concatenate(hs_out, axis=-1)

    @pl.when(c == pl.num_programs(1) - 1)
    def _():
        co_ref[...] = c_s[...]
        no_ref[...] = n_s[...]
        mo_ref[...] = m_s[...]


def mlstm(z3, gt, b_if, c0, n0, m0, *, L, n_valid, bb):
    b, s, _ = z3.shape
    nc = s // L
    bi_row = jnp.zeros((1, LANE), F32).at[0, :H_B].set(b_if[0])
    bf_row = jnp.zeros((1, LANE), F32).at[0, :H_B].set(b_if[1])
    b8 = b_if.reshape(2 * H_B, 1)
    c0r = c0.reshape(b * H_B, DH_B, DH_B)
    n0r = n0.reshape(b * H_B, 1, DH_B)
    m0r = jnp.zeros((b, 1, LANE), F32).at[:, 0, :H_B].set(m0)
    cw = B_W // LANE
    h, co, no, mo = pl.pallas_call(
        functools.partial(_mlstm_kernel, bb=bb, L=L, n_valid=n_valid),
        grid=(b // bb, nc),
        in_specs=[
            pl.BlockSpec((bb, L, B_W), lambda g, c: (g, c, AB_QB // B_W)),
            pl.BlockSpec((bb, L, B_W), lambda g, c: (g, c, AB_KB // B_W)),
            pl.BlockSpec((bb, L, B_W), lambda g, c: (g, c, AB_VB // B_W)),
            pl.BlockSpec((bb, L, LANE), lambda g, c: (g, c, AB_IG // LANE)),
            pl.BlockSpec((bb, L, LANE), lambda g, c: (g, c, AB_FG // LANE)),
            pl.BlockSpec((bb, 1, 2 * H_B, L), lambda g, c: (g, c, 0, 0)),
            pl.BlockSpec((1, LANE), lambda g, c: (0, 0)),
            pl.BlockSpec((1, LANE), lambda g, c: (0, 0)),
            pl.BlockSpec((2 * H_B, 1), lambda g, c: (0, 0)),
            pl.BlockSpec((bb * H_B, DH_B, DH_B), lambda g, c: (g, 0, 0)),
            pl.BlockSpec((bb * H_B, 1, DH_B), lambda g, c: (g, 0, 0)),
            pl.BlockSpec((bb, 1, LANE), lambda g, c: (g, 0, 0)),
        ],
        out_specs=[
            pl.BlockSpec((bb, L, B_W), lambda g, c: (g, c, 0)),
            pl.BlockSpec((bb * H_B, DH_B, DH_B), lambda g, c: (g, 0, 0)),
            pl.BlockSpec((bb * H_B, 1, DH_B), lambda g, c: (g, 0, 0)),
            pl.BlockSpec((bb, 1, LANE), lambda g, c: (g, 0, 0)),
        ],
        out_shape=[
            jax.ShapeDtypeStruct((b, s, B_W), F32),
            jax.ShapeDtypeStruct((b * H_B, DH_B, DH_B), F32),
            jax.ShapeDtypeStruct((b * H_B, 1, DH_B), F32),
            jax.ShapeDtypeStruct((b, 1, LANE), F32),
        ],
        scratch_shapes=[
            pltpu.VMEM((bb * H_B, DH_B, DH_B), F32),
            pltpu.VMEM((bb * H_B, 1, DH_B), F32),
            pltpu.VMEM((bb, 1, LANE), F32),
        ],
        compiler_params=_cparams(("parallel", "arbitrary")),
        name="mlstm_L%d" % L,
    )(z3, z3, z3, z3, z3, gt, bi_row, bf_row, b8, c0r, n0r, m0r)
    del cw
    return (h.reshape(b * s, B_W), co.reshape(b, H_B, DH_B, DH_B), no.reshape(b, H_B, DH_B),
            mo[:, 0, :H_B])


def _ab_merge_kernel(o0, o1, o2, l0, l1, l2, hb_ref, og_ref, g_ref, w_ref, r_ref, out_ref):
    a0, a1, a2 = l0[...], l1[...], l2[...]
    mx = jnp.maximum(jnp.maximum(a0, a1), a2)
    e0, e1, e2 = jnp.exp(a0 - mx), jnp.exp(a1 - mx), jnp.exp(a2 - mx)
    o_a = (e0 * o0[...] + e1 * o1[...] + e2 * o2[...]) / (e0 + e1 + e2)
    hb = hb_ref[...]
    parts = []
    for h in range(H_B):
        hs = slice(h * DH_B, (h + 1) * DH_B)
        x = hb[:, hs]
        parts.append(x * lax.rsqrt(jnp.mean(x * x, axis=-1, keepdims=True) + EPS))
    hbn = jnp.concatenate(parts, axis=-1) * g_ref[...] * jax.nn.sigmoid(og_ref[...])
    y = (jnp.dot(o_a.astype(BF16), w_ref[0:AW, :], preferred_element_type=F32)
         + jnp.dot(hbn.astype(BF16), w_ref[AW:AW + B_W, :], preferred_element_type=F32))
    out_ref[...] = r_ref[...] + y


def ab_merge(os_, ls_, hb, z2, g_mn, w_bf16, resid, *, tm_pref=512):
    m, d = resid.shape
    tm = _pick_tile(m, tm_pref)
    a_spec = pl.BlockSpec((tm, AW), lambda i: (i, 0))
    return pl.pallas_call(
        _ab_merge_kernel,
        grid=(m // tm,),
        in_specs=[a_spec] * 6 + [
            pl.BlockSpec((tm, B_W), lambda i: (i, 0)),
            pl.BlockSpec((tm, B_W), lambda i: (i, AB_OG // B_W)),
            pl.BlockSpec((1, B_W), lambda i: (0, 0)),
            pl.BlockSpec((AW + B_W, d), lambda i: (0, 0)),
            pl.BlockSpec((tm, d), lambda i: (i, 0)),
        ],
        out_specs=pl.BlockSpec((tm, d), lambda i: (i, 0)),
        out_shape=jax.ShapeDtypeStruct((m, d), F32),
        compiler_params=_cparams(("parallel",)),
        name="ab_merge",
    )(*os_, *ls_, hb, z2, g_mn.reshape(1, B_W), w_bf16, resid)


def _compress_rows(x, wa, wb, pa, pb, w2):
    xb = x.astype(BF16)
    a = jnp.dot(xb, wa, preferred_element_type=F32)
    b = jnp.dot(xb, wb, preferred_element_type=F32)
    bias = (jnp.dot(pa.astype(BF16), wa, preferred_element_type=F32)
            + jnp.dot(pb.astype(BF16), wb, preferred_element_type=F32))
    del w2
    return a, b, bias


def _compress_prompt_kernel(x_ref, wa_ref, wb_ref, pa_ref, pb_ref, w2_ref, o_ref):
    x = x_ref[0]
    n = x.shape[0]
    a, b, bias = _compress_rows(x, wa_ref[...], wb_ref[...], pa_ref[...], pb_ref[...], None)
    hid = a + pltpu.roll(b, n - 1, 0) + bias
    o_ref[0] = jnp.dot(_gelu_tanh(hid).astype(BF16), w2_ref[...], preferred_element_type=F32)


def compress_prompt(xc, cw):
    b, n, w = xc.shape
    wa, wb, pa, pb, w2 = cw
    full = lambda a: pl.BlockSpec(a.shape, lambda i: (0,) * a.ndim)
    return pl.pallas_call(
        _compress_prompt_kernel,
        grid=(b,),
        in_specs=[pl.BlockSpec((1, n, w), lambda i: (i, 0, 0)), full(wa), full(wb), full(pa), full(pb), full(w2)],
        out_specs=pl.BlockSpec((1, n, KVW), lambda i: (i, 0, 0)),
        out_shape=jax.ShapeDtypeStruct((b, n, KVW), F32),
        compiler_params=_cparams(("parallel",)),
        name="compress_prompt",
    )(xc, wa, wb, pa, pb, w2)


def _compress_paged_kernel(pt_ref, *refs, pg):
    del pt_ref
    page_refs = refs[:pg + 1]
    wa_ref, wb_ref, pa_ref, pb_ref, w2_ref, o_ref = refs[pg + 1:]
    x = jnp.concatenate([r[0] for r in page_refs], axis=0)
    n = pg * (PAGE_SIZE // CMP_STRIDE)
    a, b, bias = _compress_rows(x, wa_ref[...], wb_ref[...], pa_ref[...], pb_ref[...], None)
    hid = a[0:n] + b[1:n + 1] + bias
    o_ref[0] = jnp.dot(_gelu_tanh(hid).astype(BF16), w2_ref[...], preferred_element_type=F32)


def compress_paged(pool_v, page_table, cw, *, pg=32):
    db, n_pages = page_table.shape
    rpp = PAGE_SIZE // CMP_STRIDE
    w = pool_v.shape[-1]
    pg = min(pg, n_pages)
    assert n_pages % pg == 0
    wa, wb, pa, pb, w2 = cw
    full = lambda a: pl.BlockSpec(a.shape, lambda b, g, pt: (0,) * a.ndim)

    def page_spec(j):
        return pl.BlockSpec((1, rpp, w), lambda b, g, pt: (pt[b, jnp.minimum(g * pg + j, n_pages - 1)], 0, 0))

    gs = pltpu.PrefetchScalarGridSpec(
        num_scalar_prefetch=1,
        grid=(db, n_pages // pg),
        in_specs=[page_spec(j) for j in range(pg + 1)] + [full(wa), full(wb), full(pa), full(pb), full(w2)],
        out_specs=pl.BlockSpec((1, pg * rpp, KVW), lambda b, g, pt: (b, g, 0)),
    )
    return pl.pallas_call(
        functools.partial(_compress_paged_kernel, pg=pg),
        grid_spec=gs,
        out_shape=jax.ShapeDtypeStruct((db, n_pages * rpp, KVW), F32),
        compiler_params=_cparams(("parallel", "arbitrary")),
        name="compress_paged",
    )(page_table, *([pool_v] * (pg + 1)), wa, wb, pa, pb, w2)


def _stack_heads(x, t):
    del t
    return jnp.concatenate([x[:, g * DH_C:(g + 1) * DH_C] for g in range(G_C)], axis=0)


def _mask_rows(s, mask, fill):
    t, n = mask.shape
    return jnp.where(mask[None], s.reshape(G_C, t, n), fill).reshape(G_C * t, n)


def _cmp_branch(q2, kc, vc, t_pos, tq):
    n = kc.shape[0]
    s = _dot_nt(q2, kc)
    cend = lax.broadcasted_iota(jnp.int32, (tq, n), 1) * CMP_STRIDE + (CMP_BLK - 1)
    s = _mask_rows(s, cend <= t_pos, -jnp.inf)
    m = jnp.max(s, axis=-1, keepdims=True)
    m = jnp.where(m > -jnp.inf, m, 0.0)
    p = jnp.exp(s - m)
    p = p / jnp.maximum(jnp.sum(p, axis=-1, keepdims=True), 1e-30)
    o = jnp.dot(p.astype(BF16), vc, preferred_element_type=F32)
    pg = p[0:tq]
    for g in range(1, G_C):
        pg = pg + p[g * tq:(g + 1) * tq]
    return o, pg


def _select_blocks(imp, t_pos):
    tq, nsp = imp.shape
    j = lax.broadcasted_iota(jnp.int32, (tq, nsp), 1)
    jf = j.astype(F32)
    cur = t_pos // SLC_BLK
    forced = (j == 0) | (j == cur) | (j == cur - 1)
    work = jnp.where(forced, jnp.inf, jnp.where(j <= cur, imp, -jnp.inf))
    sel = jnp.zeros((tq, nsp), F32)
    for _ in range(N_SEL):
        mx = jnp.max(work, axis=-1, keepdims=True)
        first = jnp.min(jnp.where(work == mx, jf, float(nsp)), axis=-1, keepdims=True)
        pick = jf == first
        sel = jnp.where(pick, jnp.where(mx > -jnp.inf, 1.0, sel), sel)
        work = jnp.where(pick, -jnp.inf, work)
    return sel


def _flash_update(s, mask, v, m_ref, l_ref, acc_ref):
    s = _mask_rows(s, mask, NEG_BIG)
    m_old = m_ref[...]
    m_new = jnp.maximum(m_old, jnp.max(s, axis=-1, keepdims=True))
    alpha = jnp.exp(m_old - m_new)
    p = jnp.exp(s - m_new)
    l_ref[...] = alpha * l_ref[...] + jnp.sum(p, axis=-1, keepdims=True)
    acc_ref[...] = alpha * acc_ref[...] + jnp.dot(p.astype(BF16), v, preferred_element_type=F32)
    m_ref[...] = m_new


def _flash_finish(m_ref, l_ref, acc_ref):
    return jnp.where(m_ref[...] > 0.5 * NEG_BIG, acc_ref[...] / jnp.maximum(l_ref[...], 1e-30), 0.0)


def _softmax_av(s, mask, v):
    s = _mask_rows(s, mask, -jnp.inf)
    m = jnp.max(s, axis=-1, keepdims=True)
    m = jnp.where(m > -jnp.inf, m, 0.0)
    p = jnp.exp(s - m)
    p = p / jnp.maximum(jnp.sum(p, axis=-1, keepdims=True), 1e-30)
    return jnp.dot(p.astype(BF16), v, preferred_element_type=F32)


def _gate_mix(gates, h, o_cmp, o_sel, o_win, tq):
    outs = []
    for g in range(G_C):
        base = (h * G_C + g) * 3
        rs = slice(g * tq, (g + 1) * tq)
        outs.append(gates[:, base:base + 1] * o_cmp[rs] + gates[:, base + 1:base + 2] * o_sel[rs]
                    + gates[:, base + 2:base + 3] * o_win[rs])
    return jnp.concatenate(outs, axis=-1)


def _rope_q(qh, cos, sin):
    return jnp.concatenate(
        [_rope_lanes(qh[:, c * LANE:(c + 1) * LANE], cos, sin) for c in range(G_C * DH_C // LANE)], axis=-1)


SEL_TK = 512


def _nsa_prompt_kernel(q_ref, gz_ref, cos_ref, sin_ref, kc_ref, kv_ref, mimp_ref, o_ref,
                       m_s, l_s, acc_s, *, n_slc):
    i = pl.program_id(1)
    tq = QBLK_C
    s0 = i * tq
    t_pos = s0 + lax.broadcasted_iota(jnp.int32, (tq, 1), 0)
    gates = jax.nn.sigmoid(gz_ref[0])
    cos, sin = cos_ref[...], sin_ref[...]
    scale = DH_C ** -0.5
    hw = G_C * DH_C
    for h in range(KVH_C):
        qh = q_ref[0, :, h * hw:(h + 1) * hw] * scale
        q2 = _stack_heads(qh, tq).astype(BF16)
        q2r = _stack_heads(_rope_q(qh, cos, sin), tq).astype(BF16)
        kc = kc_ref[0, :, h * DH_C:(h + 1) * DH_C].astype(BF16)
        vc = kc_ref[0, :, KVH_C * DH_C + h * DH_C:KVH_C * DH_C + (h + 1) * DH_C].astype(BF16)
        o_cmp, pgrp = _cmp_branch(q2, kc, vc, t_pos, tq)
        imp = _dot_exact_rhs01(pgrp, mimp_ref[...])
        sel = _select_blocks(imp, t_pos).astype(BF16)
        m_s[...] = jnp.full(m_s.shape, NEG_BIG, F32)
        l_s[...] = jnp.zeros(l_s.shape, F32)
        acc_s[...] = jnp.zeros(acc_s.shape, F32)
        kcol = h * DH_C
        vcol = KVH_C * DH_C + h * DH_C

        def tile(kt, carry):
            k0 = pl.multiple_of(kt * SEL_TK, SEL_TK)
            ks = kv_ref[0, pl.ds(k0, SEL_TK), kcol:kcol + DH_C]
            vs = kv_ref[0, pl.ds(k0, SEL_TK), vcol:vcol + DH_C]
            s = _dot_nt(q2r, ks)
            blk = lax.broadcasted_iota(jnp.int32, (n_slc, SEL_TK), 0)
            kcl = lax.broadcasted_iota(jnp.int32, (n_slc, SEL_TK), 1)
            expand = jnp.where(blk == kt * (SEL_TK // SLC_BLK) + kcl // SLC_BLK, 1.0, 0.0).astype(BF16)
            mk = jnp.dot(sel, expand, preferred_element_type=F32) > 0.5
            kpos = k0 + lax.broadcasted_iota(jnp.int32, (tq, SEL_TK), 1)
            mk = mk & (kpos <= t_pos)
            _flash_update(s, mk, vs, m_s, l_s, acc_s)
            return carry

        lax.fori_loop(0, (s0 + tq + SEL_TK - 1) // SEL_TK, tile, 0)
        o_sel = _flash_finish(m_s, l_s, acc_s)
        nw = WIN_C + tq
        w0 = pl.multiple_of(jnp.maximum(s0 - WIN_C, 0), tq)
        kw = kv_ref[0, pl.ds(w0, nw), KVW + kcol:KVW + kcol + DH_C]
        vw = kv_ref[0, pl.ds(w0, nw), KVW + vcol:KVW + vcol + DH_C]
        dlt = t_pos - (w0 + lax.broadcasted_iota(jnp.int32, (tq, nw), 1))
        mw = (dlt >= 0) & (dlt < WIN_C)
        o_win = _softmax_av(_dot_nt(q2r, kw), mw, vw)
        o_ref[0, :, h * hw:(h + 1) * hw] = _gate_mix(gates, h, o_cmp, o_sel, o_win, tq)


def nsa_prompt(z3, kc, kvb, cos_t, sin_t, mimp):
    b, s, _ = z3.shape
    n_slc = s // SLC_BLK
    nq = s // QBLK_C
    ncmp = kc.shape[1]
    rows = G_C * QBLK_C
    return pl.pallas_call(
        functools.partial(_nsa_prompt_kernel, n_slc=n_slc),
        grid=(b, nq),
        in_specs=[
            pl.BlockSpec((1, QBLK_C, C_Q), lambda bb, i: (bb, i, 0)),
            pl.BlockSpec((1, QBLK_C, LANE), lambda bb, i: (bb, i, C_GATE // LANE)),
            pl.BlockSpec((QBLK_C, LANE), lambda bb, i: (i, 0)),
            pl.BlockSpec((QBLK_C, LANE), lambda bb, i: (i, 0)),
            pl.BlockSpec((1, ncmp, KVW), lambda bb, i: (bb, 0, 0)),
            pl.BlockSpec((1, s, 2 * KVW), lambda bb, i: (bb, 0, 0)),
            pl.BlockSpec(mimp.shape, lambda bb, i: (0, 0)),
        ],
        out_specs=pl.BlockSpec((1, QBLK_C, C_Q), lambda bb, i: (bb, i, 0)),
        out_shape=jax.ShapeDtypeStruct((b, s, C_Q), F32),
        scratch_shapes=[pltpu.VMEM((rows, 1), F32), pltpu.VMEM((rows, 1), F32), pltpu.VMEM((rows, DH_C), F32)],
        compiler_params=_cparams(("parallel", "arbitrary")),
        name="nsa_prompt",
    )(z3, z3, cos_t, sin_t, kc, kvb, mimp)


def _nsa_sample_kernel(pt_ref, *refs, pg, nsp, past, t_new):
    del pt_ref
    page_refs = refs[:pg]
    (q_ref, gz_ref, cos_ref, sin_ref, kc_ref, mimp_ref, seln_ref, winb_ref, winn_ref,
     o_ref, wo_ref, q2r_s, sel_s, ocmp_s, m_s, l_s, acc_s) = refs[pg:]
    g = pl.program_id(1)
    ng = pl.num_programs(1)
    tq = T_PAD
    rows = G_C * tq
    t_pos = past + lax.broadcasted_iota(jnp.int32, (tq, 1), 0)
    scale = DH_C ** -0.5
    hw = G_C * DH_C

    @pl.when(g == 0)
    def _():
        cos, sin = cos_ref[...], sin_ref[...]
        for h in range(KVH_C):
            qh = q_ref[0, :, h * hw:(h + 1) * hw] * scale
            q2 = _stack_heads(qh, tq).astype(BF16)
            q2r_s[h] = _stack_heads(_rope_q(qh, cos, sin), tq).astype(BF16)
            kc = kc_ref[0, :, h * DH_C:(h + 1) * DH_C].astype(BF16)
            vc = kc_ref[0, :, KVH_C * DH_C + h * DH_C:KVH_C * DH_C + (h + 1) * DH_C].astype(BF16)
            o_cmp, pgrp = _cmp_branch(q2, kc, vc, t_pos, tq)
            ocmp_s[h] = o_cmp
            imp = _dot_exact_rhs01(pgrp, mimp_ref[...])
            sel_s[h] = _select_blocks(imp, t_pos)
        m_s[...] = jnp.full(m_s.shape, NEG_BIG, F32)
        l_s[...] = jnp.zeros(l_s.shape, F32)
        acc_s[...] = jnp.zeros(acc_s.shape, F32)

    nk = pg * PAGE_SIZE
    kv = jnp.concatenate([r[0] for r in page_refs], axis=0).astype(BF16)
    blk = lax.broadcasted_iota(jnp.int32, (nsp, nk), 0)
    kcl = lax.broadcasted_iota(jnp.int32, (nsp, nk), 1)
    expand = jnp.where(blk == g * (nk // SLC_BLK) + kcl // SLC_BLK, 1.0, 0.0).astype(BF16)
    for h in range(KVH_C):
        ks = kv[:, h * DH_C:(h + 1) * DH_C]
        vs = kv[:, KVH_C * DH_C + h * DH_C:KVH_C * DH_C + (h + 1) * DH_C]
        s = _dot_nt(q2r_s[h], ks)
        mk = jnp.dot(sel_s[h].astype(BF16), expand, preferred_element_type=F32) > 0.5
        _flash_update(s, mk, vs, m_s.at[h], l_s.at[h], acc_s.at[h])

    @pl.when(g == ng - 1)
    def _():
        gates = jax.nn.sigmoid(gz_ref[0])
        n_buf = winb_ref.shape[1]
        lane_n = lax.broadcasted_iota(jnp.int32, (tq, nsp), 1)
        u = lax.broadcasted_iota(jnp.int32, (tq, tq), 1)
        seln = seln_ref[0]
        winn = winn_ref[0]
        for h in range(KVH_C):
            kcol = h * DH_C
            vcol = KVH_C * DH_C + h * DH_C
            q2r = q2r_s[h]
            new_sel = jnp.max(jnp.where(lane_n == past // SLC_BLK, sel_s[h], 0.0), axis=-1, keepdims=True) > 0.5
            mk = new_sel & (past + u <= t_pos) & (u < t_new)
            _flash_update(_dot_nt(q2r, seln[:, kcol:kcol + DH_C].astype(BF16)), mk,
                          seln[:, vcol:vcol + DH_C].astype(BF16), m_s.at[h], l_s.at[h], acc_s.at[h])
            o_sel = _flash_finish(m_s.at[h], l_s.at[h], acc_s.at[h])
            kb = winb_ref[0, :, kcol:kcol + DH_C].astype(BF16)
            vb = winb_ref[0, :, vcol:vcol + DH_C].astype(BF16)
            pos_b = past - n_buf + lax.broadcasted_iota(jnp.int32, (tq, n_buf), 1)
            d_b = t_pos - pos_b
            m_b = (pos_b >= 0) & (d_b >= 0) & (d_b < WIN_C)
            d_n = t_pos - (past + u)
            m_n = (d_n >= 0) & (d_n < WIN_C) & (u < t_new)
            s_b = _mask_rows(_dot_nt(q2r, kb), m_b, -jnp.inf)
            s_n = _mask_rows(_dot_nt(q2r, winn[:, kcol:kcol + DH_C].astype(BF16)), m_n, -jnp.inf)
            mx = jnp.maximum(jnp.max(s_b, axis=-1, keepdims=True), jnp.max(s_n, axis=-1, keepdims=True))
            mx = jnp.where(mx > -jnp.inf, mx, 0.0)
            p_b, p_n = jnp.exp(s_b - mx), jnp.exp(s_n - mx)
            den = jnp.maximum(jnp.sum(p_b, axis=-1, keepdims=True) + jnp.sum(p_n, axis=-1, keepdims=True), 1e-30)
            o_win = (jnp.dot((p_b / den).astype(BF16), vb, preferred_element_type=F32)
                     + jnp.dot((p_n / den).astype(BF16), winn[:, vcol:vcol + DH_C].astype(BF16),
                               preferred_element_type=F32))
            o_ref[0, :, h * hw:(h + 1) * hw] = _gate_mix(gates, h, ocmp_s[h], o_sel, o_win, tq)
        wo_ref[0, 0:n_buf - t_new, :] = winb_ref[0, t_new:n_buf, :]
        wo_ref[0, n_buf - t_new:n_buf, :] = winn[0:t_new]


def nsa_sample(z3, kc, sel_pool, win_buf, page_table, cos_t, sin_t, mimp, *, t_new, pg=16):
    db, n_pages = page_table.shape
    past = n_pages * PAGE_SIZE
    ncmp = kc.shape[1]
    nsp = mimp.shape[1]
    n_buf = win_buf.shape[1]
    pg = min(pg, n_pages)
    assert n_pages % pg == 0
    rows = G_C * T_PAD

    def page_spec(j):
        return pl.BlockSpec((1, PAGE_SIZE, KVW), lambda b, g, pt: (pt[b, g * pg + j], 0, 0))

    gs = pltpu.PrefetchScalarGridSpec(
        num_scalar_prefetch=1,
        grid=(db, n_pages // pg),
        in_specs=[page_spec(j) for j in range(pg)] + [
            pl.BlockSpec((1, T_PAD, C_Q), lambda b, g, pt: (b, 0, 0)),
            pl.BlockSpec((1, T_PAD, LANE), lambda b, g, pt: (b, 0, C_GATE // LANE)),
            pl.BlockSpec((T_PAD, LANE), lambda b, g, pt: (0, 0)),
            pl.BlockSpec((T_PAD, LANE), lambda b, g, pt: (0, 0)),
            pl.BlockSpec((1, ncmp, KVW), lambda b, g, pt: (b, 0, 0)),
            pl.BlockSpec(mimp.shape, lambda b, g, pt: (0, 0)),
            pl.BlockSpec((1, T_PAD, KVW), lambda b, g, pt: (b, 0, C_SEL // KVW)),
            pl.BlockSpec((1, n_buf, KVW), lambda b, g, pt: (b, 0, 0)),
            pl.BlockSpec((1, T_PAD, KVW), lambda b, g, pt: (b, 0, C_WIN // KVW)),
        ],
        out_specs=[
            pl.BlockSpec((1, T_PAD, C_Q), lambda b, g, pt: (b, 0, 0)),
            pl.BlockSpec((1, n_buf, KVW), lambda b, g, pt: (b, 0, 0)),
        ],
        scratch_shapes=[
            pltpu.VMEM((KVH_C, rows, DH_C), BF16),
            pltpu.VMEM((KVH_C, T_PAD, nsp), F32),
            pltpu.VMEM((KVH_C, rows, DH_C), F32),
            pltpu.VMEM((KVH_C, rows, 1), F32),
            pltpu.VMEM((KVH_C, rows, 1), F32),
            pltpu.VMEM((KVH_C, rows, DH_C), F32),
        ],
    )
    return pl.pallas_call(
        functools.partial(_nsa_sample_kernel, pg=pg, nsp=nsp, past=past, t_new=t_new),
        grid_spec=gs,
        out_shape=[jax.ShapeDtypeStruct((db, T_PAD, C_Q), F32), jax.ShapeDtypeStruct((db, n_buf, KVW), F32)],
        compiler_params=_cparams(("parallel", "arbitrary")),
        name="nsa_sample",
    )(page_table, *([sel_pool] * pg), z3, z3, cos_t, sin_t, kc, mimp, z3, win_buf, z3)


def _rope_tables(pos):
    half = DH_C // 2
    inv = 1.0 / (ROPE_THETA ** (jnp.arange(half, dtype=F32) / half))
    ang = pos.astype(F32)[:, None] * inv[None, :]
    cos, sin = jnp.cos(ang), jnp.sin(ang)
    return jnp.tile(cos, (1, 4)), jnp.concatenate([-sin, sin, -sin, sin], axis=-1)


def _prep_w_in_ab(w):
    cuts = np.cumsum([A_QW, A_QW, A_QW, B_W, B_W, B_W, H_B, H_B])
    qa, ka, va, qb, kb, vb, ig, fg, og = jnp.split(w, cuts, axis=-1)
    padg = lambda t: jnp.pad(t, ((0, 0), (0, LANE - H_B)))
    return jnp.concatenate([qb, kb, vb, og, qa, ka, va, padg(ig), padg(fg)], axis=-1).astype(BF16)


def _ab_rope_flags():
    f = np.zeros((AB_N // LANE,), np.int32)
    f[AB_QA // LANE:AB_VA // LANE] = 1
    return jnp.asarray(f)


def _c_rope_flags():
    f = np.zeros((C_N // LANE,), np.int32)
    f[C_SEL // LANE] = 1
    f[C_WIN // LANE] = 1
    return jnp.asarray(f)


def _prep_compress(w1, w2, pe):
    e2 = jnp.eye(2, dtype=F32)

    def half(w1h):
        return jnp.einsum('krde,kK,hH->rkhdKHe', w1h, e2, e2).reshape(CMP_STRIDE * KVW, KVW).astype(BF16)

    wa, wb = half(w1[:, :CMP_STRIDE]), half(w1[:, CMP_STRIDE:])
    w2b = jnp.einsum('ked,kK,hH->kheKHd', w2, e2, e2).reshape(KVW, KVW).astype(BF16)

    def pe_half(p):
        return jnp.broadcast_to(p.transpose(1, 0, 2)[:, :, None, :],
                                (CMP_STRIDE, 2, KVH_C, DH_C)).reshape(1, CMP_STRIDE * KVW)

    return wa, wb, pe_half(pe[:, :CMP_STRIDE]), pe_half(pe[:, CMP_STRIDE:]), w2b


def _importance_matrix(n_rows, n_cmp, n_slc, n_cols):
    ratio = SLC_BLK // CMP_STRIDE
    m = np.zeros((n_rows, n_cols), np.float32)
    for jblk in range(n_slc):
        for off in range(1 - CMP_BLK // CMP_STRIDE, ratio):
            i = ratio * jblk + off
            if 0 <= i < n_cmp:
                m[i, jblk] = 1.0
    return jnp.asarray(m, dtype=BF16)


def _gates_t(z3, L):
    b, s, _ = z3.shape
    g = jnp.concatenate([z3[..., AB_IG:AB_IG + H_B], z3[..., AB_FG:AB_FG + H_B]], axis=-1)
    return g.reshape(b, s // L, L, 2 * H_B).transpose(0, 1, 3, 2)


def kernel(x_prompt, x_sample, cache_a0_kv, cache_a1_kv, cache_a2_kv, state_b_C, state_b_n, state_b_m,
           cache_c_cmp_kv, cache_c_sel_kv, cache_c_win_kv, page_table, norm_g, w_in_ab, b_if, g_mlstm,
           w_out_ab, w_in_c, cmp_w1, cmp_w2, cmp_pe, w_out_c, w_ffn_gate, w_ffn_up, w_ffn_down, norm_final):
    B, S, D = x_prompt.shape
    DB, T, _ = x_sample.shape
    depth = norm_g.shape[0]
    n_pages = page_table.shape[1]
    past = n_pages * PAGE_SIZE
    caches_a = (cache_a0_kv, cache_a1_kv, cache_a2_kv)
    assert T <= T_PAD and S % (DIL_CFG[-1][1] * BLK_A) == 0 and S >= WIN_C + QBLK_C

    hp = x_prompt.reshape(B * S, D)
    hs = jnp.pad(x_sample, ((0, 0), (0, T_PAD - T), (0, 0))).reshape(DB * T_PAD, D)

    pos_p = jnp.arange(S)
    pos_s = past + jnp.arange(T_PAD)
    cos_p1, sin_p1 = _rope_tables(pos_p)
    cos_s1, sin_s1 = _rope_tables(pos_s)
    cos_p, sin_p = jnp.tile(cos_p1, (B, 1)), jnp.tile(sin_p1, (B, 1))
    cos_s, sin_s = jnp.tile(cos_s1, (DB, 1)), jnp.tile(sin_s1, (DB, 1))

    a_p, a_s = [[], [], []], [[], [], []]
    bC_p, bC_s, bn_p, bn_s, bm_p, bm_s = [], [], [], [], [], []
    cc_p, cc_s, csl_p, csl_s, cw_p, cw_s = [], [], [], [], [], []

    for layer in range(depth):
        if layer % 2 == 0:
            e = layer // 2
            w_in = _prep_w_in_ab(w_in_ab[e])
            flags = _ab_rope_flags()
            w_out = w_out_ab[e].astype(BF16)
            z = norm_proj(hp, norm_g[layer, 0], w_in, flags, cos_p, sin_p)
            z3 = z.reshape(B, S, AB_N)
            os_, ls_ = [], []
            for gi, (win, dil) in enumerate(DIL_CFG):
                o, l = dil_prompt(z3, gi, win, dil)
                os_.append(o)
                ls_.append(l)
                nb = min(win, S)
                kk = z3[:, S - nb:, AB_KA + gi * AW:AB_KA + (gi + 1) * AW].reshape(B, nb, H_A, DH_A)
                vv = z3[:, S - nb:, AB_VA + gi * AW:AB_VA + (gi + 1) * AW].reshape(B, nb, H_A, DH_A)
                a_p[gi].append(jnp.stack([kk, vv], axis=2))
            zc = jnp.zeros
            hb, Cp, n_p, m_p = mlstm(z3, _gates_t(z3, MLSTM_CHUNK), b_if[e],
                                     zc((B, H_B, DH_B, DH_B), F32), zc((B, H_B, DH_B), F32), zc((B, H_B), F32),
                                     L=MLSTM_CHUNK, n_valid=MLSTM_CHUNK, bb=B if B <= 2 else 1)
            hp = ab_merge(os_, ls_, hb, z, g_mlstm[e], w_out, hp)
            bC_p.append(Cp); bn_p.append(n_p); bm_p.append(m_p)
            z = norm_proj(hs, norm_g[layer, 0], w_in, flags, cos_s, sin_s)
            z3 = z.reshape(DB, T_PAD, AB_N)
            os_, ls_ = [], []
            for gi, (win, dil) in enumerate(DIL_CFG):
                cache = caches_a[gi][e]
                n_buf = cache.shape[1]
                o, l, co = dil_sample(cache.reshape(DB, n_buf, 2 * AW), z3, gi, win, dil, T)
                os_.append(o)
                ls_.append(l)
                a_s[gi].append(co.reshape(DB, n_buf, 2, H_A, DH_A))
            bbs = 4 if DB % 4 == 0 else 1
            hb, Cs, n_s, m_s = mlstm(z3, _gates_t(z3, T_PAD), b_if[e], state_b_C[e], state_b_n[e], state_b_m[e],
                                     L=T_PAD, n_valid=T, bb=bbs)
            hs = ab_merge(os_, ls_, hb, z, g_mlstm[e], w_out, hs)
            bC_s.append(Cs); bn_s.append(n_s); bm_s.append(m_s)
        else:
            o_i = layer // 2
            w_in = jnp.pad(w_in_c[o_i], ((0, 0), (0, C_N - w_in_c.shape[-1]))).astype(BF16)
            flags = _c_rope_flags()
            w_out = w_out_c[o_i].astype(BF16)
            cw = _prep_compress(cmp_w1[o_i], cmp_w2[o_i], cmp_pe[o_i])
            z = norm_proj(hp, norm_g[layer, 0], w_in, flags, cos_p, sin_p)
            z3 = z.reshape(B, S, C_N)
            kv_cmp = z3[:, :, C_CMP:C_CMP + KVW]
            kv_sel = z3[:, :, C_SEL:C_SEL + KVW]
            kv_win = z3[:, :, C_WIN:C_WIN + KVW]
            kc = compress_prompt(kv_cmp.reshape(B, S // CMP_STRIDE, CMP_STRIDE * KVW), cw)
            n_cmp = (S - CMP_BLK) // CMP_STRIDE + 1
            n_slc = S // SLC_BLK
            mimp = _importance_matrix(S // CMP_STRIDE, n_cmp, n_slc, n_slc)
            kvb = z3[:, :, C_SEL:C_SEL + 2 * KVW].astype(BF16)
            o = nsa_prompt(z3, kc, kvb, cos_p1, sin_p1, mimp)
            hp = out_proj(o.reshape(B * S, C_Q), w_out, hp)
            nw = min(WIN_C, S)
            sh = lambda t: t.reshape(t.shape[0], t.shape[1], 2, KVH_C, DH_C)
            cc_p.append(sh(kv_cmp)); csl_p.append(sh(kv_sel)); cw_p.append(sh(kv_win[:, S - nw:]))
            z = norm_proj(hs, norm_g[layer, 0], w_in, flags, cos_s, sin_s)
            z3 = z.reshape(DB, T_PAD, C_N)
            n_phys = cache_c_cmp_kv.shape[1]
            pool_c = cache_c_cmp_kv[o_i].reshape(n_phys, PAGE_SIZE // CMP_STRIDE, CMP_STRIDE * KVW)
            kc = compress_paged(pool_c, page_table, cw)
            full_len = past + T
            n_cmp = (full_len - CMP_BLK) // CMP_STRIDE + 1
            assert (n_cmp + 1) * CMP_STRIDE <= past
            n_slc = past // SLC_BLK + -(-T // SLC_BLK)
            nsp = -(-n_slc // LANE) * LANE
            mimp = _importance_matrix(past // CMP_STRIDE, n_cmp, n_slc, nsp)
            win_buf = cache_c_win_kv[o_i]
            n_buf = win_buf.shape[1]
            o, wo = nsa_sample(z3, kc, cache_c_sel_kv[o_i].reshape(n_phys, PAGE_SIZE, KVW),
                               win_buf.reshape(DB, n_buf, KVW), page_table, cos_s1, sin_s1, mimp, t_new=T)
            hs = out_proj(o.reshape(DB * T_PAD, C_Q), w_out, hs)
            cc_s.append(sh(z3[:, :T, C_CMP:C_CMP + KVW])); csl_s.append(sh(z3[:, :T, C_SEL:C_SEL + KVW]))
            cw_s.append(sh(wo))
        last = layer == depth - 1
        wg, wu, wd = (w_ffn_gate[layer].astype(BF16), w_ffn_up[layer].astype(BF16), w_ffn_down[layer].astype(BF16))
        hp = ffn(hp, norm_g[layer, 1], wg, wu, wd, norm_final, final_norm=last)
        hs = ffn(hs, norm_g[layer, 1], wg, wu, wd, norm_final, final_norm=last)

    y_prompt = hp.reshape(B, S, D)
    y_sample = hs.reshape(DB, T_PAD, D)[:, :T]
    st = lambda xs: jnp.stack(xs, axis=0)
    return (y_prompt, y_sample,
            st(a_p[0]), st(a_s[0]), st(a_p[1]), st(a_s[1]), st(a_p[2]), st(a_s[2]),
            st(bC_p), st(bC_s), st(bn_p), st(bn_s), st(bm_p), st(bm_s),
            st(cc_p), st(cc_s), st(csl_p), st(csl_s), st(cw_p), st(cw_s))
```

```python
import functools
import math

import numpy as np
import jax
import jax.numpy as jnp
from jax import lax
from jax.experimental import pallas as pl
from jax.experimental.pallas import tpu as pltpu

F32 = jnp.float32
BF16 = jnp.bfloat16

PAGE_SIZE = 128
DIL_CFG = ((128, 1), (512, 4), (2048, 16))
N_DIL = 3
H_A = 4
DH_A = 64
BLK_A = 128
H_B = 4
DH_B = 128
MLSTM_CHUNK = 128
H_C = 16
KVH_C = 2
G_C = H_C // KVH_C
DH_C = 64
CMP_STRIDE = 16
CMP_BLK = 2 * CMP_STRIDE
CMP_HID = 64
SLC_BLK = 64
N_SEL = 16
WIN_C = 512
QBLK_C = 128
ROPE_THETA = 10000.0
EPS = 1e-6
A_QW = N_DIL * H_A * DH_A
B_W = H_B * DH_B
AW = H_A * DH_A
C_Q = H_C * DH_C
C_KV = 3 * 2 * KVH_C * DH_C
KVW = 2 * KVH_C * DH_C

LANE = 128
SUBLANE = 8
VMEM_LIMIT = 48 * 1024 * 1024

NEG_BIG = -1e30
T_PAD = SUBLANE

AB_QB, AB_KB, AB_VB, AB_OG = 0, B_W, 2 * B_W, 3 * B_W
AB_QA = 4 * B_W
AB_KA = AB_QA + A_QW
AB_VA = AB_KA + A_QW
AB_IG = AB_VA + A_QW
AB_FG = AB_IG + LANE
AB_N = AB_FG + LANE

C_CMP = C_Q
C_SEL = C_Q + KVW
C_WIN = C_Q + 2 * KVW
C_GATE = C_Q + 3 * KVW
C_N = 2048


def _cparams(sem, vmem=VMEM_LIMIT):
    return pltpu.CompilerParams(dimension_semantics=sem, vmem_limit_bytes=vmem)


def _pick_tile(m, pref):
    t = min(m, pref)
    while m % t:
        t //= 2
    return t


def _rope_lanes(x, cos, sin):
    lane = lax.broadcasted_iota(jnp.int32, x.shape, 1)
    first = (lane % DH_C) < (DH_C // 2)
    partner = jnp.where(first, pltpu.roll(x, LANE - DH_C // 2, 1), pltpu.roll(x, DH_C // 2, 1))
    return x * cos + partner * sin


def _split3(x):
    hi = x.astype(BF16)
    r1 = x - hi.astype(F32)
    mid = r1.astype(BF16)
    lo = (r1 - mid.astype(F32)).astype(BF16)
    return hi, mid, lo


def _dot_exact_rhs01(x, m01):
    hi, mid, lo = _split3(x)
    d = lambda a: jnp.dot(a, m01, preferred_element_type=F32)
    return d(hi) + d(mid) + d(lo)


def _dot_exact_lhs01(m01, x):
    hi, mid, lo = _split3(x)
    d = lambda a: jnp.dot(m01, a, preferred_element_type=F32)
    return d(hi) + d(mid) + d(lo)


def _dot_nt(a, b):
    return lax.dot_general(a, b, (((1,), (1,)), ((), ())), preferred_element_type=F32)


def _dot_tn(a, b):
    return lax.dot_general(a, b, (((0,), (0,)), ((), ())), preferred_element_type=F32)


def _log_sigmoid(x):
    return jnp.minimum(x, 0.0) - jnp.log1p(jnp.exp(-jnp.abs(x)))


def _gelu_tanh(x):
    return 0.5 * x * (1.0 + jnp.tanh(math.sqrt(2.0 / math.pi) * (x + 0.044715 * (x * x * x))))


def _rms_rows(x, g):
    ms = jnp.mean(x * x, axis=-1, keepdims=True)
    return x * lax.rsqrt(ms + EPS) * g


def _norm_proj_kernel(flags_ref, x_ref, g_ref, w_ref, cos_ref, sin_ref, o_ref, xn_ref, *, tn):
    j = pl.program_id(1)

    @pl.when(j == 0)
    def _():
        xn_ref[...] = _rms_rows(x_ref[...], g_ref[...]).astype(BF16)

    acc = jnp.dot(xn_ref[...], w_ref[...], preferred_element_type=F32)
    nchunk = tn // LANE
    for c in range(nchunk):
        a = acc[:, c * LANE:(c + 1) * LANE]
        flag = flags_ref[j * nchunk + c]

        @pl.when(flag == 1)
        def _():
            o_ref[:, c * LANE:(c + 1) * LANE] = _rope_lanes(a, cos_ref[...], sin_ref[...])

        @pl.when(flag == 0)
        def _():
            o_ref[:, c * LANE:(c + 1) * LANE] = a


def norm_proj(x, g, w_bf16, rope_flags, cos_t, sin_t, *, tm_pref=512, tn=256):
    m, d = x.shape
    n = w_bf16.shape[1]
    tm = _pick_tile(m, tm_pref)
    grid = (m // tm, n // tn)
    gs = pltpu.PrefetchScalarGridSpec(
        num_scalar_prefetch=1,
        grid=grid,
        in_specs=[
            pl.BlockSpec((tm, d), lambda i, j, f: (i, 0)),
            pl.BlockSpec((1, d), lambda i, j, f: (0, 0)),
            pl.BlockSpec((d, tn), lambda i, j, f: (0, j)),
            pl.BlockSpec((tm, LANE), lambda i, j, f: (i, 0)),
            pl.BlockSpec((tm, LANE), lambda i, j, f: (i, 0)),
        ],
        out_specs=pl.BlockSpec((tm, tn), lambda i, j, f: (i, j)),
        scratch_shapes=[pltpu.VMEM((tm, d), BF16)],
    )
    return pl.pallas_call(
        functools.partial(_norm_proj_kernel, tn=tn),
        grid_spec=gs,
        out_shape=jax.ShapeDtypeStruct((m, n), F32),
        compiler_params=_cparams(("parallel", "arbitrary")),
        name="norm_proj",
    )(rope_flags, x, g.reshape(1, d), w_bf16, cos_t, sin_t)


def _ffn_kernel(x_ref, g_ref, wg_ref, wu_ref, wd_ref, gf_ref, o_ref, xn_ref, *, final_norm):
    f = pl.program_id(1)

    @pl.when(f == 0)
    def _():
        x = x_ref[...]
        xn_ref[...] = _rms_rows(x, g_ref[...]).astype(BF16)
        o_ref[...] = x

    xn = xn_ref[...]
    a = jnp.dot(xn, wg_ref[...], preferred_element_type=F32)
    u = jnp.dot(xn, wu_ref[...], preferred_element_type=F32)
    act = (a * jax.nn.sigmoid(a)) * u
    o_ref[...] += jnp.dot(act.astype(BF16), wd_ref[...], preferred_element_type=F32)

    if final_norm:
        @pl.when(f == pl.num_programs(1) - 1)
        def _():
            o_ref[...] = _rms_rows(o_ref[...], gf_ref[...])


def ffn(x, g, wg, wu, wd, g_final, *, final_norm, tm_pref=512):
    m, d = x.shape
    dff = wg.shape[1]
    tf = dff // 2 if (dff // 2) % LANE == 0 else dff
    tm = _pick_tile(m, tm_pref)
    grid = (m // tm, dff // tf)
    return pl.pallas_call(
        functools.partial(_ffn_kernel, final_norm=final_norm),
        grid=grid,
        in_specs=[
            pl.BlockSpec((tm, d), lambda i, f: (i, 0)),
            pl.BlockSpec((1, d), lambda i, f: (0, 0)),
            pl.BlockSpec((d, tf), lambda i, f: (0, f)),
            pl.BlockSpec((d, tf), lambda i, f: (0, f)),
            pl.BlockSpec((tf, d), lambda i, f: (f, 0)),
            pl.BlockSpec((1, d), lambda i, f: (0, 0)),
        ],
        out_specs=pl.BlockSpec((tm, d), lambda i, f: (i, 0)),
        out_shape=jax.ShapeDtypeStruct((m, d), F32),
        scratch_shapes=[pltpu.VMEM((tm, d), BF16)],
        compiler_params=_cparams(("parallel", "arbitrary")),
        name="ffn",
    )(x, g.reshape(1, d), wg, wu, wd, g_final.reshape(1, d))


def _out_proj_kernel(x_ref, w_ref, r_ref, o_ref):
    o_ref[...] = r_ref[...] + jnp.dot(x_ref[...].astype(BF16), w_ref[...], preferred_element_type=F32)


def out_proj(x, w_bf16, resid, *, tm_pref=512):
    m, k = x.shape
    n = w_bf16.shape[1]
    tm = _pick_tile(m, tm_pref)
    return pl.pallas_call(
        _out_proj_kernel,
        grid=(m // tm,),
        in_specs=[
            pl.BlockSpec((tm, k), lambda i: (i, 0)),
            pl.BlockSpec((k, n), lambda i: (0, 0)),
            pl.BlockSpec((tm, n), lambda i: (i, 0)),
        ],
        out_specs=pl.BlockSpec((tm, n), lambda i: (i, 0)),
        out_shape=jax.ShapeDtypeStruct((m, n), F32),
        compiler_params=_cparams(("parallel",)),
        name="out_proj",
    )(x, w_bf16, resid)


def _dil_prompt_kernel(q_ref, kp_ref, kc_ref, vp_ref, vc_ref, o_ref, l_ref, *, nback, dil):
    n = pl.program_id(2)
    qi = BLK_A + lax.broadcasted_iota(jnp.int32, (BLK_A, 2 * BLK_A), 0)
    ki = lax.broadcasted_iota(jnp.int32, (BLK_A, 2 * BLK_A), 1)
    rel = qi - ki
    mask = (rel >= 0) & (rel <= nback) & ((n > 0) | (ki >= BLK_A))

    def residue(r, carry):
        rows = pl.ds(r, BLK_A, stride=dil) if dil > 1 else pl.ds(0, BLK_A)
        q = q_ref[0, rows, :] * (DH_A ** -0.5)
        kk = jnp.concatenate([kp_ref[0, rows, :], kc_ref[0, rows, :]], axis=0)
        vv = jnp.concatenate([vp_ref[0, rows, :], vc_ref[0, rows, :]], axis=0)
        outs, lses = [], []
        for h in range(LANE // DH_A):
            hs = slice(h * DH_A, (h + 1) * DH_A)
            s = _dot_nt(q[:, hs].astype(BF16), kk[:, hs].astype(BF16))
            s = jnp.where(mask, s, -jnp.inf)
            m = jnp.max(s, axis=-1, keepdims=True)
            p = jnp.exp(s - m)
            l = jnp.sum(p, axis=-1, keepdims=True)
            o = jnp.dot(p.astype(BF16), vv[:, hs].astype(BF16), preferred_element_type=F32) / l
            outs.append(o)
            lses.append(jnp.broadcast_to(m + jnp.log(l), (BLK_A, DH_A)))
        o_ref[0, rows, :] = jnp.concatenate(outs, axis=-1)
        l_ref[0, rows, :] = jnp.concatenate(lses, axis=-1)
        return carry

    if dil > 1:
        lax.fori_loop(0, dil, residue, 0)
    else:
        residue(0, 0)


def dil_prompt(z3, gi, window, dil):
    b, s, _ = z3.shape
    nback = window // dil
    span = dil * BLK_A
    assert s % span == 0
    nh = AW // LANE
    qo, ko, vo = (AB_QA + gi * AW) // LANE, (AB_KA + gi * AW) // LANE, (AB_VA + gi * AW) // LANE
    blk = (1, span, LANE)
    in_specs = [
        pl.BlockSpec(blk, lambda bb, hp, i: (bb, i, qo + hp)),
        pl.BlockSpec(blk, lambda bb, hp, i: (bb, jnp.maximum(i - 1, 0), ko + hp)),
        pl.BlockSpec(blk, lambda bb, hp, i: (bb, i, ko + hp)),
        pl.BlockSpec(blk, lambda bb, hp, i: (bb, jnp.maximum(i - 1, 0), vo + hp)),
        pl.BlockSpec(blk, lambda bb, hp, i: (bb, i, vo + hp)),
    ]
    out_spec = pl.BlockSpec(blk, lambda bb, hp, i: (bb, i, hp))
    o, l = pl.pallas_call(
        functools.partial(_dil_prompt_kernel, nback=nback, dil=dil),
        grid=(b, nh, s // span),
        in_specs=in_specs,
        out_specs=[out_spec, out_spec],
        out_shape=[jax.ShapeDtypeStruct((b, s, AW), F32)] * 2,
        compiler_params=_cparams(("parallel", "parallel", "arbitrary")),
        name="dil_prompt_%d" % gi,
    )(z3, z3, z3, z3, z3)
    return o.reshape(b * s, AW), l.reshape(b * s, AW)


def _dil_sample_kernel(c_ref, q_ref, kn_ref, vn_ref, nt_ref, o_ref, l_ref, co_ref, *, window, dil, n_buf, t_new):
    rows = H_A * T_PAD
    q = q_ref[0] * (DH_A ** -0.5)
    q4 = jnp.concatenate([q] * H_A, axis=0)
    rr = lax.broadcasted_iota(jnp.int32, (rows, AW), 0)
    ll = lax.broadcasted_iota(jnp.int32, (rows, AW), 1)
    head_sel = (rr // T_PAD) == (ll // DH_A)
    qbd = jnp.where(head_sel, q4, 0.0).astype(BF16)
    kbt = c_ref[0, 0:AW, :].astype(BF16)
    vbt = c_ref[0, AW:2 * AW, :].astype(BF16)
    kn = kn_ref[0]
    vn = vn_ref[0]
    s_buf = jnp.dot(qbd, kbt, preferred_element_type=F32)
    s_new = _dot_nt(qbd, kn.astype(BF16))
    t_b = lax.broadcasted_iota(jnp.int32, (rows, n_buf), 0) % T_PAD
    c_b = lax.broadcasted_iota(jnp.int32, (rows, n_buf), 1)
    d_b = n_buf + t_b - c_b
    m_b = ((d_b % dil) == 0) & (d_b <= window)
    t_n = lax.broadcasted_iota(jnp.int32, (rows, T_PAD), 0) % T_PAD
    u_n = lax.broadcasted_iota(jnp.int32, (rows, T_PAD), 1)
    d_n = t_n - u_n
    m_n = (d_n >= 0) & ((d_n % dil) == 0) & (d_n <= window) & (u_n < t_new)
    s_buf = jnp.where(m_b, s_buf, NEG_BIG)
    s_new = jnp.where(m_n, s_new, NEG_BIG)
    mx = jnp.maximum(jnp.max(s_buf, axis=-1, keepdims=True), jnp.max(s_new, axis=-1, keepdims=True))
    p_b = jnp.where(m_b, jnp.exp(s_buf - mx), 0.0)
    p_n = jnp.where(m_n, jnp.exp(s_new - mx), 0.0)
    l = jnp.sum(p_b, axis=-1, keepdims=True) + jnp.sum(p_n, axis=-1, keepdims=True)
    l = jnp.maximum(l, 1e-30)
    acc = (_dot_nt(p_b.astype(BF16), vbt)
           + jnp.dot(p_n.astype(BF16), vn.astype(BF16), preferred_element_type=F32))
    res = jnp.where(head_sel, acc / l, 0.0)
    lse = jnp.where(head_sel, mx + jnp.log(l), 0.0)
    o = res[0:T_PAD]
    ls = lse[0:T_PAD]
    for h in range(1, H_A):
        o = o + res[h * T_PAD:(h + 1) * T_PAD]
        ls = ls + lse[h * T_PAD:(h + 1) * T_PAD]
    o_ref[0] = o
    l_ref[0] = ls
    co_ref[0, :, 0:n_buf - t_new] = c_ref[0, :, t_new:n_buf]
    co_ref[0, :, n_buf - t_new:n_buf] = nt_ref[0, :, 0:t_new]


def dil_sample(cache_t, z3, new_t, gi, window, dil, t_new):
    db, _, n_buf = cache_t.shape
    qo, ko, vo = AB_QA // AW + gi, AB_KA // AW + gi, AB_VA // AW + gi
    blk = (1, T_PAD, AW)
    o, l, co = pl.pallas_call(
        functools.partial(_dil_sample_kernel, window=window, dil=dil, n_buf=n_buf, t_new=t_new),
        grid=(db,),
        in_specs=[
            pl.BlockSpec((1, 2 * AW, n_buf), lambda b: (b, 0, 0)),
            pl.BlockSpec(blk, lambda b: (b, 0, qo)),
            pl.BlockSpec(blk, lambda b: (b, 0, ko)),
            pl.BlockSpec(blk, lambda b: (b, 0, vo)),
            pl.BlockSpec((1, 2 * AW, T_PAD), lambda b: (b, 0, 0)),
        ],
        out_specs=[
            pl.BlockSpec(blk, lambda b: (b, 0, 0)),
            pl.BlockSpec(blk, lambda b: (b, 0, 0)),
            pl.BlockSpec((1, 2 * AW, n_buf), lambda b: (b, 0, 0)),
        ],
        out_shape=[
            jax.ShapeDtypeStruct((db, T_PAD, AW), F32),
            jax.ShapeDtypeStruct((db, T_PAD, AW), F32),
            jax.ShapeDtypeStruct((db, 2 * AW, n_buf), F32),
        ],
        compiler_params=_cparams(("parallel",)),
        name="dil_sample_%d" % gi,
    )(cache_t, z3, z3, z3, new_t)
    return o.reshape(db * T_PAD, AW), l.reshape(db * T_PAD, AW), co


def _mlstm_kernel(q_ref, k_ref, v_ref, gi_ref, gf_ref, gt_ref, bi_ref, bf_ref, b8_ref,
                  c0_ref, n0_ref, m0_ref, h_ref, co_ref, no_ref, mo_ref,
                  c_s, n_s, m_s, *, bb, L, n_valid):
    c = pl.program_id(1)

    @pl.when(c == 0)
    def _():
        c_s[...] = c0_ref[...]
        n_s[...] = n0_ref[...]
        m_s[...] = m0_ref[...]

    row = lax.broadcasted_iota(jnp.int32, (L, L), 0)
    col = lax.broadcasted_iota(jnp.int32, (L, L), 1)
    tri = row >= col
    tri_l = jnp.where(tri, 1.0, 0.0).astype(BF16)
    tri_u = jnp.where(row <= col, 1.0, 0.0).astype(BF16)
    lane = lax.broadcasted_iota(jnp.int32, (1, LANE), 1)
    for b in range(bb):
        ig_col = gi_ref[b] + bi_ref[...]
        lf_col = _log_sigmoid(gf_ref[b] + bf_ref[...])
        gt = gt_ref[b, 0]
        ig_row = gt[0:H_B] + b8_ref[0:H_B]
        lf_row = _log_sigmoid(gt[H_B:2 * H_B] + b8_ref[H_B:2 * H_B])
        if n_valid < L:
            rv = lax.broadcasted_iota(jnp.int32, (L, LANE), 0) < n_valid
            ig_col = jnp.where(rv, ig_col, -jnp.inf)
            lf_col = jnp.where(rv, lf_col, 0.0)
            cv = lax.broadcasted_iota(jnp.int32, (H_B, L), 1) < n_valid
            ig_row = jnp.where(cv, ig_row, -jnp.inf)
            lf_row = jnp.where(cv, lf_row, 0.0)
        b_col = _dot_exact_lhs01(tri_l, lf_col)
        b_row = _dot_exact_rhs01(lf_row, tri_u)
        m_row = m_s[b]
        a_col = b_col + m_row
        m_new = m_row
        hs_out = []
        for h in range(H_B):
            hsl = slice(h * DH_B, (h + 1) * DH_B)
            bc = b_col[:, h:h + 1]
            ac = a_col[:, h:h + 1]
            icol = ig_col[:, h:h + 1]
            D = bc - b_row[h:h + 1, :] + ig_row[h:h + 1, :]
            D = jnp.where(tri, D, -jnp.inf)
            mt = jnp.maximum(ac, jnp.max(D, axis=-1, keepdims=True))
            Dw = jnp.exp(D - mt)
            iw = jnp.exp(ac - mt)
            qf = q_ref[b, :, hsl]
            kf = k_ref[b, :, hsl] * (DH_B ** -0.5)
            vf = v_ref[b, :, hsl]
            qb, kb, vb = qf.astype(BF16), kf.astype(BF16), vf.astype(BF16)
            Cm = c_s[b * H_B + h]
            nv = n_s[b * H_B + h]
            sc = _dot_nt(qb, kb) * Dw
            num = iw * _dot_nt(qb, Cm.astype(BF16)) + jnp.dot(sc.astype(BF16), vb, preferred_element_type=F32)
            den = iw * jnp.sum(qf * nv, axis=-1, keepdims=True) + jnp.sum(sc, axis=-1, keepdims=True)
            hs_out.append(num / jnp.maximum(jnp.abs(den), jnp.exp(-mt)))
            mL = mt[L - 1:L, :]
            wL = jnp.exp(bc[L - 1:L, :] - bc + icol - mL)
            dec = jnp.exp(ac[L - 1:L, :] - mL)
            c_s[b * H_B + h] = dec * Cm + _dot_tn((vf * wL).astype(BF16), kb)
            n_s[b * H_B + h] = dec * nv + jnp.sum(wL * kf, axis=0, keepdims=True)
            m_new = jnp.where(lane == h, mL, m_new)
        m_s[b] = m_new
        h_ref[b] = jnp.concatenate(hs_out, axis=-1)

    @pl.when(c == pl.num_programs(1) - 1)
    def _():
        co_ref[...] = c_s[...]
        no_ref[...] = n_s[...]
        mo_ref[...] = m_s[...]


def mlstm(z3, gt, b_if, c0, n0, m0, *, L, n_valid, bb):
    b, s, _ = z3.shape
    nc = s // L
    bi_row = jnp.zeros((1, LANE), F32).at[0, :H_B].set(b_if[0])
    bf_row = jnp.zeros((1, LANE), F32).at[0, :H_B].set(b_if[1])
    b8 = b_if.reshape(2 * H_B, 1)
    c0r = c0.reshape(b * H_B, DH_B, DH_B)
    n0r = n0.reshape(b * H_B, 1, DH_B)
    m0r = jnp.zeros((b, 1, LANE), F32).at[:, 0, :H_B].set(m0)
    cw = B_W // LANE
    h, co, no, mo = pl.pallas_call(
        functools.partial(_mlstm_kernel, bb=bb, L=L, n_valid=n_valid),
        grid=(b // bb, nc),
        in_specs=[
            pl.BlockSpec((bb, L, B_W), lambda g, c: (g, c, AB_QB // B_W)),
            pl.BlockSpec((bb, L, B_W), lambda g, c: (g, c, AB_KB // B_W)),
            pl.BlockSpec((bb, L, B_W), lambda g, c: (g, c, AB_VB // B_W)),
            pl.BlockSpec((bb, L, LANE), lambda g, c: (g, c, AB_IG // LANE)),
            pl.BlockSpec((bb, L, LANE), lambda g, c: (g, c, AB_FG // LANE)),
            pl.BlockSpec((bb, 1, 2 * H_B, L), lambda g, c: (g, c, 0, 0)),
            pl.BlockSpec((1, LANE), lambda g, c: (0, 0)),
            pl.BlockSpec((1, LANE), lambda g, c: (0, 0)),
            pl.BlockSpec((2 * H_B, 1), lambda g, c: (0, 0)),
            pl.BlockSpec((bb * H_B, DH_B, DH_B), lambda g, c: (g, 0, 0)),
            pl.BlockSpec((bb * H_B, 1, DH_B), lambda g, c: (g, 0, 0)),
            pl.BlockSpec((bb, 1, LANE), lambda g, c: (g, 0, 0)),
        ],
        out_specs=[
            pl.BlockSpec((bb, L, B_W), lambda g, c: (g, c, 0)),
            pl.BlockSpec((bb * H_B, DH_B, DH_B), lambda g, c: (g, 0, 0)),
            pl.BlockSpec((bb * H_B, 1, DH_B), lambda g, c: (g, 0, 0)),
            pl.BlockSpec((bb, 1, LANE), lambda g, c: (g, 0, 0)),
        ],
        out_shape=[
            jax.ShapeDtypeStruct((b, s, B_W), F32),
            jax.ShapeDtypeStruct((b * H_B, DH_B, DH_B), F32),
            jax.ShapeDtypeStruct((b * H_B, 1, DH_B), F32),
            jax.ShapeDtypeStruct((b, 1, LANE), F32),
        ],
        scratch_shapes=[
            pltpu.VMEM((bb * H_B, DH_B, DH_B), F32),
            pltpu.VMEM((bb * H_B, 1, DH_B), F32),
            pltpu.VMEM((bb, 1, LANE), F32),
        ],
        compiler_params=_cparams(("parallel", "arbitrary")),
        name="mlstm_L%d" % L,
    )(z3, z3, z3, z3, z3, gt, bi_row, bf_row, b8, c0r, n0r, m0r)
    del cw
    return (h.reshape(b * s, B_W), co.reshape(b, H_B, DH_B, DH_B), no.reshape(b, H_B, DH_B),
            mo[:, 0, :H_B])


def _ab_merge_kernel(o0, o1, o2, l0, l1, l2, hb_ref, og_ref, g_ref, w_ref, r_ref, out_ref):
    a0, a1, a2 = l0[...], l1[...], l2[...]
    mx = jnp.maximum(jnp.maximum(a0, a1), a2)
    e0, e1, e2 = jnp.exp(a0 - mx), jnp.exp(a1 - mx), jnp.exp(a2 - mx)
    o_a = (e0 * o0[...] + e1 * o1[...] + e2 * o2[...]) / (e0 + e1 + e2)
    hb = hb_ref[...]
    parts = []
    for h in range(H_B):
        hs = slice(h * DH_B, (h + 1) * DH_B)
        x = hb[:, hs]
        parts.append(x * lax.rsqrt(jnp.mean(x * x, axis=-1, keepdims=True) + EPS))
    hbn = jnp.concatenate(parts, axis=-1) * g_ref[...] * jax.nn.sigmoid(og_ref[...])
    y = (jnp.dot(o_a.astype(BF16), w_ref[0:AW, :], preferred_element_type=F32)
         + jnp.dot(hbn.astype(BF16), w_ref[AW:AW + B_W, :], preferred_element_type=F32))
    out_ref[...] = r_ref[...] + y


def ab_merge(os_, ls_, hb, z2, g_mn, w_bf16, resid, *, tm_pref=512):
    m, d = resid.shape
    tm = _pick_tile(m, tm_pref)
    a_spec = pl.BlockSpec((tm, AW), lambda i: (i, 0))
    return pl.pallas_call(
        _ab_merge_kernel,
        grid=(m // tm,),
        in_specs=[a_spec] * 6 + [
            pl.BlockSpec((tm, B_W), lambda i: (i, 0)),
            pl.BlockSpec((tm, B_W), lambda i: (i, AB_OG // B_W)),
            pl.BlockSpec((1, B_W), lambda i: (0, 0)),
            pl.BlockSpec((AW + B_W, d), lambda i: (0, 0)),
            pl.BlockSpec((tm, d), lambda i: (i, 0)),
        ],
        out_specs=pl.BlockSpec((tm, d), lambda i: (i, 0)),
        out_shape=jax.ShapeDtypeStruct((m, d), F32),
        compiler_params=_cparams(("parallel",)),
        name="ab_merge",
    )(*os_, *ls_, hb, z2, g_mn.reshape(1, B_W), w_bf16, resid)


HKV = KVH_C * DH_C


def _compress_bias_kernel(pa_ref, pb_ref, w_ref, o_ref):
    w = w_ref[0]
    o_ref[0] = (jnp.dot(pa_ref[0].astype(BF16), w, preferred_element_type=F32)[:, 0:HKV]
                + jnp.dot(pb_ref[0].astype(BF16), w, preferred_element_type=F32)[:, HKV:2 * HKV])


def compress_bias(pa, pb, wab2):
    spec = lambda a: pl.BlockSpec((1,) + a.shape[1:], lambda i: (i,) + (0,) * (a.ndim - 1))
    return pl.pallas_call(
        _compress_bias_kernel,
        grid=(2,),
        in_specs=[spec(pa), spec(pb), spec(wab2)],
        out_specs=pl.BlockSpec((1, 1, HKV), lambda i: (i, 0, 0)),
        out_shape=jax.ShapeDtypeStruct((2, 1, HKV), F32),
        compiler_params=_cparams(("arbitrary",)),
        name="compress_bias",
    )(pa, pb, wab2)


def _compress_halves(load_rows, n, w_of_r):
    acc = jnp.zeros((n, 2 * HKV), F32)
    for r in range(CMP_STRIDE):
        acc = acc + jnp.dot(load_rows(r, n).astype(BF16), w_of_r(r), preferred_element_type=F32)
    return acc


def _compress_prompt_kernel(x_ref, wab_ref, bias_ref, w2_ref, o_ref):
    n = x_ref.shape[1] // CMP_STRIDE
    acc = _compress_halves(lambda r, m: x_ref[0, pl.ds(r, m, stride=CMP_STRIDE), :], n, lambda r: wab_ref[0, r])
    hid = acc[:, 0:HKV] + pltpu.roll(acc[:, HKV:2 * HKV], n - 1, 0) + bias_ref[0]
    o_ref[0] = jnp.dot(_gelu_tanh(hid).astype(BF16), w2_ref[0], preferred_element_type=F32)


def compress_prompt(z3, cw, bias):
    b, s, _ = z3.shape
    wab, w2 = cw
    n = s // CMP_STRIDE
    return pl.pallas_call(
        _compress_prompt_kernel,
        grid=(b, 2),
        in_specs=[
            pl.BlockSpec((1, s, HKV), lambda i, kv: (i, 0, C_CMP // HKV + kv)),
            pl.BlockSpec((1,) + wab.shape[1:], lambda i, kv: (kv, 0, 0, 0)),
            pl.BlockSpec((1, 1, HKV), lambda i, kv: (kv, 0, 0)),
            pl.BlockSpec((1, HKV, HKV), lambda i, kv: (kv, 0, 0)),
        ],
        out_specs=pl.BlockSpec((1, n, HKV), lambda i, kv: (i, 0, kv)),
        out_shape=jax.ShapeDtypeStruct((b, n, KVW), F32),
        compiler_params=_cparams(("parallel", "arbitrary")),
        name="compress_prompt",
    )(z3, wab, bias, w2)


def _compress_paged_kernel(pt_ref, *refs, pg):
    del pt_ref
    page_refs = refs[:pg + 1]
    wab_ref, bias_ref, w2_ref, o_ref, xs_ref = refs[pg + 1:]
    for j in range(pg + 1):
        t = page_refs[j][0].T
        xs_ref[0, j * PAGE_SIZE:(j + 1) * PAGE_SIZE, :] = t[:, 0:HKV]
        xs_ref[1, j * PAGE_SIZE:(j + 1) * PAGE_SIZE, :] = t[:, HKV:2 * HKV]
    cpp = PAGE_SIZE // CMP_STRIDE
    n = pg * cpp
    for kv in range(2):
        acc = _compress_halves(lambda r, m: xs_ref[kv, pl.ds(r, m, stride=CMP_STRIDE), :], n + cpp,
                               lambda r: wab_ref[kv, r])
        hid = acc[0:n, 0:HKV] + acc[1:n + 1, HKV:2 * HKV] + bias_ref[kv]
        o_ref[0, :, kv * HKV:(kv + 1) * HKV] = jnp.dot(_gelu_tanh(hid).astype(BF16), w2_ref[kv],
                                                        preferred_element_type=F32)


def compress_paged(pool_t, page_table, cw, bias, *, pg=32):
    db, n_pages = page_table.shape
    cpp = PAGE_SIZE // CMP_STRIDE
    pg = min(pg, n_pages)
    assert n_pages % pg == 0
    wab, w2 = cw
    full = lambda a: pl.BlockSpec(a.shape, lambda b, g, pt: (0,) * a.ndim)

    def page_spec(j):
        return pl.BlockSpec((1, KVW, PAGE_SIZE),
                            lambda b, g, pt: (pt[b, jnp.minimum(g * pg + j, n_pages - 1)], 0, 0))

    gs = pltpu.PrefetchScalarGridSpec(
        num_scalar_prefetch=1,
        grid=(db, n_pages // pg),
        in_specs=[page_spec(j) for j in range(pg + 1)] + [full(wab), full(bias), full(w2)],
        out_specs=pl.BlockSpec((1, pg * cpp, KVW), lambda b, g, pt: (b, g, 0)),
        scratch_shapes=[pltpu.VMEM((2, (pg + 1) * PAGE_SIZE, HKV), F32)],
    )
    return pl.pallas_call(
        functools.partial(_compress_paged_kernel, pg=pg),
        grid_spec=gs,
        out_shape=jax.ShapeDtypeStruct((db, n_pages * cpp, KVW), F32),
        compiler_params=_cparams(("parallel", "arbitrary")),
        name="compress_paged",
    )(page_table, *([pool_t] * (pg + 1)), wab, bias, w2)


def _stack_heads(x, t):
    del t
    return jnp.concatenate([x[:, g * DH_C:(g + 1) * DH_C] for g in range(G_C)], axis=0)


def _mask_rows(s, mask, fill):
    t, n = mask.shape
    return jnp.where(mask[None], s.reshape(G_C, t, n), fill).reshape(G_C * t, n)


def _cmp_branch(q2, kc, vc, t_pos, tq):
    n = kc.shape[0]
    s = _dot_nt(q2, kc)
    cend = lax.broadcasted_iota(jnp.int32, (tq, n), 1) * CMP_STRIDE + (CMP_BLK - 1)
    s = _mask_rows(s, cend <= t_pos, -jnp.inf)
    m = jnp.max(s, axis=-1, keepdims=True)
    m = jnp.where(m > -jnp.inf, m, 0.0)
    p = jnp.exp(s - m)
    p = p / jnp.maximum(jnp.sum(p, axis=-1, keepdims=True), 1e-30)
    o = jnp.dot(p.astype(BF16), vc, preferred_element_type=F32)
    pg = p[0:tq]
    for g in range(1, G_C):
        pg = pg + p[g * tq:(g + 1) * tq]
    return o, pg


def _select_blocks(imp, t_pos):
    tq, nsp = imp.shape
    j = lax.broadcasted_iota(jnp.int32, (tq, nsp), 1)
    jf = j.astype(F32)
    cur = t_pos // SLC_BLK
    forced = (j == 0) | (j == cur) | (j == cur - 1)
    work = jnp.where(forced, jnp.inf, jnp.where(j <= cur, imp, -jnp.inf))
    sel = jnp.zeros((tq, nsp), F32)
    for _ in range(N_SEL):
        mx = jnp.max(work, axis=-1, keepdims=True)
        first = jnp.min(jnp.where(work == mx, jf, float(nsp)), axis=-1, keepdims=True)
        pick = jf == first
        sel = jnp.where(pick, jnp.where(mx > -jnp.inf, 1.0, sel), sel)
        work = jnp.where(pick, -jnp.inf, work)
    return sel


def _flash_update(s, mask, v, m_ref, l_ref, acc_ref, v_channel_major=False):
    s = _mask_rows(s, mask, NEG_BIG)
    m_old = m_ref[...]
    m_new = jnp.maximum(m_old, jnp.max(s, axis=-1, keepdims=True))
    alpha = jnp.exp(m_old - m_new)
    p = jnp.exp(s - m_new)
    l_ref[...] = alpha * l_ref[...] + jnp.sum(p, axis=-1, keepdims=True)
    pv = _dot_nt(p.astype(BF16), v) if v_channel_major else jnp.dot(p.astype(BF16), v, preferred_element_type=F32)
    acc_ref[...] = alpha * acc_ref[...] + pv
    m_ref[...] = m_new


def _flash_finish(m_ref, l_ref, acc_ref):
    return jnp.where(m_ref[...] > 0.5 * NEG_BIG, acc_ref[...] / jnp.maximum(l_ref[...], 1e-30), 0.0)


def _softmax_av(s, mask, v):
    s = _mask_rows(s, mask, -jnp.inf)
    m = jnp.max(s, axis=-1, keepdims=True)
    m = jnp.where(m > -jnp.inf, m, 0.0)
    p = jnp.exp(s - m)
    p = p / jnp.maximum(jnp.sum(p, axis=-1, keepdims=True), 1e-30)
    return jnp.dot(p.astype(BF16), v, preferred_element_type=F32)


def _gate_mix(gates, h, o_cmp, o_sel, o_win, tq):
    outs = []
    for g in range(G_C):
        base = (h * G_C + g) * 3
        rs = slice(g * tq, (g + 1) * tq)
        outs.append(gates[:, base:base + 1] * o_cmp[rs] + gates[:, base + 1:base + 2] * o_sel[rs]
                    + gates[:, base + 2:base + 3] * o_win[rs])
    return jnp.concatenate(outs, axis=-1)


def _rope_q(qh, cos, sin):
    return jnp.concatenate(
        [_rope_lanes(qh[:, c * LANE:(c + 1) * LANE], cos, sin) for c in range(G_C * DH_C // LANE)], axis=-1)


SEL_TK = 512


def _nsa_prompt_kernel(q_ref, gz_ref, cos_ref, sin_ref, kc_ref, kv_ref, mimp_ref, o_ref,
                       m_s, l_s, acc_s, *, n_slc):
    i = pl.program_id(1)
    tq = QBLK_C
    s0 = i * tq
    t_pos = s0 + lax.broadcasted_iota(jnp.int32, (tq, 1), 0)
    gates = jax.nn.sigmoid(gz_ref[0])
    cos, sin = cos_ref[...], sin_ref[...]
    scale = DH_C ** -0.5
    hw = G_C * DH_C
    for h in range(KVH_C):
        qh = q_ref[0, :, h * hw:(h + 1) * hw] * scale
        q2 = _stack_heads(qh, tq).astype(BF16)
        q2r = _stack_heads(_rope_q(qh, cos, sin), tq).astype(BF16)
        kc = kc_ref[0, :, h * DH_C:(h + 1) * DH_C].astype(BF16)
        vc = kc_ref[0, :, KVH_C * DH_C + h * DH_C:KVH_C * DH_C + (h + 1) * DH_C].astype(BF16)
        o_cmp, pgrp = _cmp_branch(q2, kc, vc, t_pos, tq)
        imp = _dot_exact_rhs01(pgrp, mimp_ref[...])
        sel = _select_blocks(imp, t_pos).astype(BF16)
        m_s[...] = jnp.full(m_s.shape, NEG_BIG, F32)
        l_s[...] = jnp.zeros(l_s.shape, F32)
        acc_s[...] = jnp.zeros(acc_s.shape, F32)
        kcol = h * DH_C
        vcol = KVH_C * DH_C + h * DH_C

        def tile(kt, carry):
            k0 = pl.multiple_of(kt * SEL_TK, SEL_TK)
            ks = kv_ref[0, pl.ds(k0, SEL_TK), kcol:kcol + DH_C]
            vs = kv_ref[0, pl.ds(k0, SEL_TK), vcol:vcol + DH_C]
            s = _dot_nt(q2r, ks)
            blk = lax.broadcasted_iota(jnp.int32, (n_slc, SEL_TK), 0)
            kcl = lax.broadcasted_iota(jnp.int32, (n_slc, SEL_TK), 1)
            expand = jnp.where(blk == kt * (SEL_TK // SLC_BLK) + kcl // SLC_BLK, 1.0, 0.0).astype(BF16)
            mk = jnp.dot(sel, expand, preferred_element_type=F32) > 0.5
            kpos = k0 + lax.broadcasted_iota(jnp.int32, (tq, SEL_TK), 1)
            mk = mk & (kpos <= t_pos)
            _flash_update(s, mk, vs, m_s, l_s, acc_s)
            return carry

        lax.fori_loop(0, (s0 + tq + SEL_TK - 1) // SEL_TK, tile, 0)
        o_sel = _flash_finish(m_s, l_s, acc_s)
        nw = WIN_C + tq
        w0 = pl.multiple_of(jnp.maximum(s0 - WIN_C, 0), tq)
        kw = kv_ref[0, pl.ds(w0, nw), KVW + kcol:KVW + kcol + DH_C]
        vw = kv_ref[0, pl.ds(w0, nw), KVW + vcol:KVW + vcol + DH_C]
        dlt = t_pos - (w0 + lax.broadcasted_iota(jnp.int32, (tq, nw), 1))
        mw = (dlt >= 0) & (dlt < WIN_C)
        o_win = _softmax_av(_dot_nt(q2r, kw), mw, vw)
        o_ref[0, :, h * hw:(h + 1) * hw] = _gate_mix(gates, h, o_cmp, o_sel, o_win, tq)


def nsa_prompt(z3, kc, kvb, cos_t, sin_t, mimp):
    b, s, _ = z3.shape
    n_slc = s // SLC_BLK
    nq = s // QBLK_C
    ncmp = kc.shape[1]
    rows = G_C * QBLK_C
    return pl.pallas_call(
        functools.partial(_nsa_prompt_kernel, n_slc=n_slc),
        grid=(b, nq),
        in_specs=[
            pl.BlockSpec((1, QBLK_C, C_Q), lambda bb, i: (bb, i, 0)),
            pl.BlockSpec((1, QBLK_C, LANE), lambda bb, i: (bb, i, C_GATE // LANE)),
            pl.BlockSpec((QBLK_C, LANE), lambda bb, i: (i, 0)),
            pl.BlockSpec((QBLK_C, LANE), lambda bb, i: (i, 0)),
            pl.BlockSpec((1, ncmp, KVW), lambda bb, i: (bb, 0, 0)),
            pl.BlockSpec((1, s, 2 * KVW), lambda bb, i: (bb, 0, 0)),
            pl.BlockSpec(mimp.shape, lambda bb, i: (0, 0)),
        ],
        out_specs=pl.BlockSpec((1, QBLK_C, C_Q), lambda bb, i: (bb, i, 0)),
        out_shape=jax.ShapeDtypeStruct((b, s, C_Q), F32),
        scratch_shapes=[pltpu.VMEM((rows, 1), F32), pltpu.VMEM((rows, 1), F32), pltpu.VMEM((rows, DH_C), F32)],
        compiler_params=_cparams(("parallel", "arbitrary")),
        name="nsa_prompt",
    )(z3, z3, cos_t, sin_t, kc, kvb, mimp)


def _nsa_sample_kernel(pt_ref, *refs, pg, nsp, past, t_new):
    del pt_ref
    page_refs = refs[:pg]
    (q_ref, gz_ref, cos_ref, sin_ref, kc_ref, mimp_ref, seln_ref, winb_ref, winn_ref, winnt_ref,
     o_ref, wo_ref, q2r_s, sel_s, ocmp_s, m_s, l_s, acc_s) = refs[pg:]
    g = pl.program_id(1)
    ng = pl.num_programs(1)
    tq = T_PAD
    rows = G_C * tq
    t_pos = past + lax.broadcasted_iota(jnp.int32, (tq, 1), 0)
    scale = DH_C ** -0.5
    hw = G_C * DH_C

    @pl.when(g == 0)
    def _():
        cos, sin = cos_ref[...], sin_ref[...]
        for h in range(KVH_C):
            qh = q_ref[0, :, h * hw:(h + 1) * hw] * scale
            q2 = _stack_heads(qh, tq).astype(BF16)
            q2r_s[h] = _stack_heads(_rope_q(qh, cos, sin), tq).astype(BF16)
            kc = kc_ref[0, :, h * DH_C:(h + 1) * DH_C].astype(BF16)
            vc = kc_ref[0, :, KVH_C * DH_C + h * DH_C:KVH_C * DH_C + (h + 1) * DH_C].astype(BF16)
            o_cmp, pgrp = _cmp_branch(q2, kc, vc, t_pos, tq)
            ocmp_s[h] = o_cmp
            imp = _dot_exact_rhs01(pgrp, mimp_ref[...])
            sel_s[h] = _select_blocks(imp, t_pos)
        m_s[...] = jnp.full(m_s.shape, NEG_BIG, F32)
        l_s[...] = jnp.zeros(l_s.shape, F32)
        acc_s[...] = jnp.zeros(acc_s.shape, F32)

    nk = pg * PAGE_SIZE
    kvt = jnp.concatenate([r[0] for r in page_refs], axis=1).astype(BF16)
    blk = lax.broadcasted_iota(jnp.int32, (nsp, nk), 0)
    kcl = lax.broadcasted_iota(jnp.int32, (nsp, nk), 1)
    expand = jnp.where(blk == g * (nk // SLC_BLK) + kcl // SLC_BLK, 1.0, 0.0).astype(BF16)
    for h in range(KVH_C):
        kst = kvt[h * DH_C:(h + 1) * DH_C, :]
        vst = kvt[KVH_C * DH_C + h * DH_C:KVH_C * DH_C + (h + 1) * DH_C, :]
        s = jnp.dot(q2r_s[h], kst, preferred_element_type=F32)
        mk = jnp.dot(sel_s[h].astype(BF16), expand, preferred_element_type=F32) > 0.5
        _flash_update(s, mk, vst, m_s.at[h], l_s.at[h], acc_s.at[h], v_channel_major=True)

    @pl.when(g == ng - 1)
    def _():
        gates = jax.nn.sigmoid(gz_ref[0])
        n_buf = winb_ref.shape[2]
        lane_n = lax.broadcasted_iota(jnp.int32, (tq, nsp), 1)
        u = lax.broadcasted_iota(jnp.int32, (tq, tq), 1)
        seln = seln_ref[0]
        winn = winn_ref[0]
        for h in range(KVH_C):
            kcol = h * DH_C
            vcol = KVH_C * DH_C + h * DH_C
            q2r = q2r_s[h]
            new_sel = jnp.max(jnp.where(lane_n == past // SLC_BLK, sel_s[h], 0.0), axis=-1, keepdims=True) > 0.5
            mk = new_sel & (past + u <= t_pos) & (u < t_new)
            _flash_update(_dot_nt(q2r, seln[:, kcol:kcol + DH_C].astype(BF16)), mk,
                          seln[:, vcol:vcol + DH_C].astype(BF16), m_s.at[h], l_s.at[h], acc_s.at[h])
            o_sel = _flash_finish(m_s.at[h], l_s.at[h], acc_s.at[h])
            kbt = winb_ref[0, kcol:kcol + DH_C, :].astype(BF16)
            vbt = winb_ref[0, vcol:vcol + DH_C, :].astype(BF16)
            pos_b = past - n_buf + lax.broadcasted_iota(jnp.int32, (tq, n_buf), 1)
            d_b = t_pos - pos_b
            m_b = (pos_b >= 0) & (d_b >= 0) & (d_b < WIN_C)
            d_n = t_pos - (past + u)
            m_n = (d_n >= 0) & (d_n < WIN_C) & (u < t_new)
            s_b = _mask_rows(jnp.dot(q2r, kbt, preferred_element_type=F32), m_b, -jnp.inf)
            s_n = _mask_rows(_dot_nt(q2r, winn[:, kcol:kcol + DH_C].astype(BF16)), m_n, -jnp.inf)
            mx = jnp.maximum(jnp.max(s_b, axis=-1, keepdims=True), jnp.max(s_n, axis=-1, keepdims=True))
            mx = jnp.where(mx > -jnp.inf, mx, 0.0)
            p_b, p_n = jnp.exp(s_b - mx), jnp.exp(s_n - mx)
            den = jnp.maximum(jnp.sum(p_b, axis=-1, keepdims=True) + jnp.sum(p_n, axis=-1, keepdims=True), 1e-30)
            o_win = (_dot_nt((p_b / den).astype(BF16), vbt)
                     + jnp.dot((p_n / den).astype(BF16), winn[:, vcol:vcol + DH_C].astype(BF16),
                               preferred_element_type=F32))
            o_ref[0, :, h * hw:(h + 1) * hw] = _gate_mix(gates, h, ocmp_s[h], o_sel, o_win, tq)
        wo_ref[0, :, 0:n_buf - t_new] = winb_ref[0, :, t_new:n_buf]
        wo_ref[0, :, n_buf - t_new:n_buf] = winnt_ref[0, :, 0:t_new]


def nsa_sample(z3, kc, sel_pool_t, win_buf_t, win_new_t, page_table, cos_t, sin_t, mimp, *, t_new, pg=16):
    db, n_pages = page_table.shape
    past = n_pages * PAGE_SIZE
    ncmp = kc.shape[1]
    nsp = mimp.shape[1]
    n_buf = win_buf_t.shape[2]
    pg = min(pg, n_pages)
    assert n_pages % pg == 0
    rows = G_C * T_PAD

    def page_spec(j):
        return pl.BlockSpec((1, KVW, PAGE_SIZE), lambda b, g, pt: (pt[b, g * pg + j], 0, 0))

    gs = pltpu.PrefetchScalarGridSpec(
        num_scalar_prefetch=1,
        grid=(db, n_pages // pg),
        in_specs=[page_spec(j) for j in range(pg)] + [
            pl.BlockSpec((1, T_PAD, C_Q), lambda b, g, pt: (b, 0, 0)),
            pl.BlockSpec((1, T_PAD, LANE), lambda b, g, pt: (b, 0, C_GATE // LANE)),
            pl.BlockSpec((T_PAD, LANE), lambda b, g, pt: (0, 0)),
            pl.BlockSpec((T_PAD, LANE), lambda b, g, pt: (0, 0)),
            pl.BlockSpec((1, ncmp, KVW), lambda b, g, pt: (b, 0, 0)),
            pl.BlockSpec(mimp.shape, lambda b, g, pt: (0, 0)),
            pl.BlockSpec((1, T_PAD, KVW), lambda b, g, pt: (b, 0, C_SEL // KVW)),
            pl.BlockSpec((1, KVW, n_buf), lambda b, g, pt: (b, 0, 0)),
            pl.BlockSpec((1, T_PAD, KVW), lambda b, g, pt: (b, 0, C_WIN // KVW)),
            pl.BlockSpec((1, KVW, T_PAD), lambda b, g, pt: (b, 0, 0)),
        ],
        out_specs=[
            pl.BlockSpec((1, T_PAD, C_Q), lambda b, g, pt: (b, 0, 0)),
            pl.BlockSpec((1, KVW, n_buf), lambda b, g, pt: (b, 0, 0)),
        ],
        scratch_shapes=[
            pltpu.VMEM((KVH_C, rows, DH_C), BF16),
            pltpu.VMEM((KVH_C, T_PAD, nsp), F32),
            pltpu.VMEM((KVH_C, rows, DH_C), F32),
            pltpu.VMEM((KVH_C, rows, 1), F32),
            pltpu.VMEM((KVH_C, rows, 1), F32),
            pltpu.VMEM((KVH_C, rows, DH_C), F32),
        ],
    )
    return pl.pallas_call(
        functools.partial(_nsa_sample_kernel, pg=pg, nsp=nsp, past=past, t_new=t_new),
        grid_spec=gs,
        out_shape=[jax.ShapeDtypeStruct((db, T_PAD, C_Q), F32), jax.ShapeDtypeStruct((db, KVW, n_buf), F32)],
        compiler_params=_cparams(("parallel", "arbitrary")),
        name="nsa_sample",
    )(page_table, *([sel_pool_t] * pg), z3, z3, cos_t, sin_t, kc, mimp, z3, win_buf_t, z3, win_new_t)


def _rope_tables(pos):
    half = DH_C // 2
    inv = 1.0 / (ROPE_THETA ** (jnp.arange(half, dtype=F32) / half))
    ang = pos.astype(F32)[:, None] * inv[None, :]
    cos, sin = jnp.cos(ang), jnp.sin(ang)
    return jnp.tile(cos, (1, 4)), jnp.concatenate([-sin, sin, -sin, sin], axis=-1)


def _prep_w_in_ab(w):
    cuts = np.cumsum([A_QW, A_QW, A_QW, B_W, B_W, B_W, H_B, H_B])
    qa, ka, va, qb, kb, vb, ig, fg, og = jnp.split(w, cuts, axis=-1)
    padg = lambda t: jnp.pad(t, ((0, 0), (0, LANE - H_B)))
    return jnp.concatenate([qb, kb, vb, og, qa, ka, va, padg(ig), padg(fg)], axis=-1).astype(BF16)


def _ab_rope_flags():
    f = np.zeros((AB_N // LANE,), np.int32)
    f[AB_QA // LANE:AB_VA // LANE] = 1
    return jnp.asarray(f)


def _c_rope_flags():
    f = np.zeros((C_N // LANE,), np.int32)
    f[C_SEL // LANE] = 1
    f[C_WIN // LANE] = 1
    return jnp.asarray(f)


def _prep_compress(w1, w2, pe):
    e2 = jnp.eye(KVH_C, dtype=F32)

    def half(w1h):
        return jnp.einsum('krde,hH->krhdHe', w1h, e2).reshape(2, CMP_STRIDE, HKV, HKV).astype(BF16)

    wab = jnp.concatenate([half(w1[:, :CMP_STRIDE]), half(w1[:, CMP_STRIDE:])], axis=-1)
    w2b = jnp.einsum('ked,hH->kheHd', w2, e2).reshape(2, HKV, HKV).astype(BF16)

    def pe_half(p):
        return jnp.broadcast_to(p[:, :, None, :], (2, CMP_STRIDE, KVH_C, DH_C)).reshape(2, 1, CMP_STRIDE * HKV)

    bias = compress_bias(pe_half(pe[:, :CMP_STRIDE]), pe_half(pe[:, CMP_STRIDE:]),
                         wab.reshape(2, CMP_STRIDE * HKV, 2 * HKV))
    return (wab, w2b), bias


def _channel_major(x, lead):
    perm = tuple(range(lead)) + (lead + 1, lead + 2, lead + 3, lead)
    xt = jnp.transpose(x, perm)
    return xt.reshape(x.shape[:lead] + (x.shape[lead + 1] * x.shape[lead + 2] * x.shape[lead + 3], x.shape[lead]))


def _row_major(xt, c0, c1, c2):
    lead, _, rows = xt.shape
    return jnp.transpose(xt.reshape(lead, c0, c1, c2, rows), (0, 4, 1, 2, 3))


def _importance_matrix(n_rows, n_cmp, n_slc, n_cols):
    ratio = SLC_BLK // CMP_STRIDE
    m = np.zeros((n_rows, n_cols), np.float32)
    for jblk in range(n_slc):
        for off in range(1 - CMP_BLK // CMP_STRIDE, ratio):
            i = ratio * jblk + off
            if 0 <= i < n_cmp:
                m[i, jblk] = 1.0
    return jnp.asarray(m, dtype=BF16)


def _gates_t(z3, L):
    b, s, _ = z3.shape
    g = jnp.concatenate([z3[..., AB_IG:AB_IG + H_B], z3[..., AB_FG:AB_FG + H_B]], axis=-1)
    return g.reshape(b, s // L, L, 2 * H_B).transpose(0, 1, 3, 2)


def kernel(x_prompt, x_sample, cache_a0_kv, cache_a1_kv, cache_a2_kv, state_b_C, state_b_n, state_b_m,
           cache_c_cmp_kv, cache_c_sel_kv, cache_c_win_kv, page_table, norm_g, w_in_ab, b_if, g_mlstm,
           w_out_ab, w_in_c, cmp_w1, cmp_w2, cmp_pe, w_out_c, w_ffn_gate, w_ffn_up, w_ffn_down, norm_final):
    B, S, D = x_prompt.shape
    DB, T, _ = x_sample.shape
    depth = norm_g.shape[0]
    n_pages = page_table.shape[1]
    past = n_pages * PAGE_SIZE
    caches_a = (cache_a0_kv, cache_a1_kv, cache_a2_kv)
    assert T <= T_PAD and S % (DIL_CFG[-1][1] * BLK_A) == 0 and S >= WIN_C + QBLK_C

    hp = x_prompt.reshape(B * S, D)
    hs = jnp.pad(x_sample, ((0, 0), (0, T_PAD - T), (0, 0))).reshape(DB * T_PAD, D)

    pos_p = jnp.arange(S)
    pos_s = past + jnp.arange(T_PAD)
    cos_p1, sin_p1 = _rope_tables(pos_p)
    cos_s1, sin_s1 = _rope_tables(pos_s)
    cos_p, sin_p = jnp.tile(cos_p1, (B, 1)), jnp.tile(sin_p1, (B, 1))
    cos_s, sin_s = jnp.tile(cos_s1, (DB, 1)), jnp.tile(sin_s1, (DB, 1))

    a_p, a_s = [[], [], []], [[], [], []]
    bC_p, bC_s, bn_p, bn_s, bm_p, bm_s = [], [], [], [], [], []
    cc_p, cc_s, csl_p, csl_s, cw_p, cw_s = [], [], [], [], [], []

    for layer in range(depth):
        if layer % 2 == 0:
            e = layer // 2
            w_in = _prep_w_in_ab(w_in_ab[e])
            flags = _ab_rope_flags()
            w_out = w_out_ab[e].astype(BF16)
            z = norm_proj(hp, norm_g[layer, 0], w_in, flags, cos_p, sin_p)
            z3 = z.reshape(B, S, AB_N)
            os_, ls_ = [], []
            for gi, (win, dil) in enumerate(DIL_CFG):
                o, l = dil_prompt(z3, gi, win, dil)
                os_.append(o)
                ls_.append(l)
                nb = min(win, S)
                kk = z3[:, S - nb:, AB_KA + gi * AW:AB_KA + (gi + 1) * AW].reshape(B, nb, H_A, DH_A)
                vv = z3[:, S - nb:, AB_VA + gi * AW:AB_VA + (gi + 1) * AW].reshape(B, nb, H_A, DH_A)
                a_p[gi].append(jnp.stack([kk, vv], axis=2))
            zc = jnp.zeros
            hb, Cp, n_p, m_p = mlstm(z3, _gates_t(z3, MLSTM_CHUNK), b_if[e],
                                     zc((B, H_B, DH_B, DH_B), F32), zc((B, H_B, DH_B), F32), zc((B, H_B), F32),
                                     L=MLSTM_CHUNK, n_valid=MLSTM_CHUNK, bb=B if B <= 2 else 1)
            hp = ab_merge(os_, ls_, hb, z, g_mlstm[e], w_out, hp)
            bC_p.append(Cp); bn_p.append(n_p); bm_p.append(m_p)
            z = norm_proj(hs, norm_g[layer, 0], w_in, flags, cos_s, sin_s)
            z3 = z.reshape(DB, T_PAD, AB_N)
            os_, ls_ = [], []
            for gi, (win, dil) in enumerate(DIL_CFG):
                new_t = jnp.concatenate(
                    [jnp.swapaxes(z3[:, :, AB_KA + gi * AW:AB_KA + (gi + 1) * AW], 1, 2),
                     jnp.swapaxes(z3[:, :, AB_VA + gi * AW:AB_VA + (gi + 1) * AW], 1, 2)], axis=1)
                o, l, co = dil_sample(_channel_major(caches_a[gi][e], 1), z3, new_t, gi, win, dil, T)
                os_.append(o)
                ls_.append(l)
                a_s[gi].append(_row_major(co, 2, H_A, DH_A))
            bbs = 4 if DB % 4 == 0 else 1
            hb, Cs, n_s, m_s = mlstm(z3, _gates_t(z3, T_PAD), b_if[e], state_b_C[e], state_b_n[e], state_b_m[e],
                                     L=T_PAD, n_valid=T, bb=bbs)
            hs = ab_merge(os_, ls_, hb, z, g_mlstm[e], w_out, hs)
            bC_s.append(Cs); bn_s.append(n_s); bm_s.append(m_s)
        else:
            o_i = layer // 2
            w_in = jnp.pad(w_in_c[o_i], ((0, 0), (0, C_N - w_in_c.shape[-1]))).astype(BF16)
            flags = _c_rope_flags()
            w_out = w_out_c[o_i].astype(BF16)
            cw, cbias = _prep_compress(cmp_w1[o_i], cmp_w2[o_i], cmp_pe[o_i])
            z = norm_proj(hp, norm_g[layer, 0], w_in, flags, cos_p, sin_p)
            z3 = z.reshape(B, S, C_N)
            kv_cmp = z3[:, :, C_CMP:C_CMP + KVW]
            kv_sel = z3[:, :, C_SEL:C_SEL + KVW]
            kv_win = z3[:, :, C_WIN:C_WIN + KVW]
            kc = compress_prompt(z3, cw, cbias)
            n_cmp = (S - CMP_BLK) // CMP_STRIDE + 1
            n_slc = S // SLC_BLK
            mimp = _importance_matrix(S // CMP_STRIDE, n_cmp, n_slc, n_slc)
            kvb = z3[:, :, C_SEL:C_SEL + 2 * KVW].astype(BF16)
            o = nsa_prompt(z3, kc, kvb, cos_p1, sin_p1, mimp)
            hp = out_proj(o.reshape(B * S, C_Q), w_out, hp)
            nw = min(WIN_C, S)
            sh = lambda t: t.reshape(t.shape[0], t.shape[1], 2, KVH_C, DH_C)
            cc_p.append(sh(kv_cmp)); csl_p.append(sh(kv_sel)); cw_p.append(sh(kv_win[:, S - nw:]))
            z = norm_proj(hs, norm_g[layer, 0], w_in, flags, cos_s, sin_s)
            z3 = z.reshape(DB, T_PAD, C_N)
            kc = compress_paged(_channel_major(cache_c_cmp_kv[o_i], 1), page_table, cw, cbias)
            full_len = past + T
            n_cmp = (full_len - CMP_BLK) // CMP_STRIDE + 1
            assert (n_cmp + 1) * CMP_STRIDE <= past
            n_slc = past // SLC_BLK + -(-T // SLC_BLK)
            nsp = -(-n_slc // LANE) * LANE
            mimp = _importance_matrix(past // CMP_STRIDE, n_cmp, n_slc, nsp)
            o, wo = nsa_sample(z3, kc, _channel_major(cache_c_sel_kv[o_i], 1),
                               _channel_major(cache_c_win_kv[o_i], 1),
                               jnp.swapaxes(z3[:, :, C_WIN:C_WIN + KVW], 1, 2),
                               page_table, cos_s1, sin_s1, mimp, t_new=T)
            hs = out_proj(o.reshape(DB * T_PAD, C_Q), w_out, hs)
            cc_s.append(sh(z3[:, :T, C_CMP:C_CMP + KVW])); csl_s.append(sh(z3[:, :T, C_SEL:C_SEL + KVW]))
            cw_s.append(_row_major(wo, 2, KVH_C, DH_C))
        last = layer == depth - 1
        wg, wu, wd = (w_ffn_gate[layer].astype(BF16), w_ffn_up[layer].astype(BF16), w_ffn_down[layer].astype(BF16))
        hp = ffn(hp, norm_g[layer, 1], wg, wu, wd, norm_final, final_norm=last)
        hs = ffn(hs, norm_g[layer, 1], wg, wu, wd, norm_final, final_norm=last)

    y_prompt = hp.reshape(B, S, D)
    y_sample = hs.reshape(DB, T_PAD, D)[:, :T]
    st = lambda xs: jnp.stack(xs, axis=0)
    return (y_prompt, y_sample,
            st(a_p[0]), st(a_s[0]), st(a_p[1]), st(a_s[1]), st(a_p[2]), st(a_s[2]),
            st(bC_p), st(bC_s), st(bn_p), st(bn_s), st(bm_p), st(bm_s),
            st(cc_p), st(cc_s), st(csl_p), st(csl_s), st(cw_p), st(cw_s))
```

```python
import functools
import math

import numpy as np
import jax
import jax.numpy as jnp
from jax import lax
from jax.experimental import pallas as pl
from jax.experimental.pallas import tpu as pltpu

F32 = jnp.float32
BF16 = jnp.bfloat16

PAGE_SIZE = 128
DIL_CFG = ((128, 1), (512, 4), (2048, 16))
N_DIL = 3
H_A = 4
DH_A = 64
BLK_A = 128
H_B = 4
DH_B = 128
MLSTM_CHUNK = 128
H_C = 16
KVH_C = 2
G_C = H_C // KVH_C
DH_C = 64
CMP_STRIDE = 16
CMP_BLK = 2 * CMP_STRIDE
CMP_HID = 64
SLC_BLK = 64
N_SEL = 16
WIN_C = 512
QBLK_C = 128
ROPE_THETA = 10000.0
EPS = 1e-6
A_QW = N_DIL * H_A * DH_A
B_W = H_B * DH_B
AW = H_A * DH_A
C_Q = H_C * DH_C
C_KV = 3 * 2 * KVH_C * DH_C
KVW = 2 * KVH_C * DH_C

LANE = 128
SUBLANE = 8
VMEM_LIMIT = 48 * 1024 * 1024

NEG_BIG = -1e30
T_PAD = SUBLANE

AB_QB, AB_KB, AB_VB, AB_OG = 0, B_W, 2 * B_W, 3 * B_W
AB_QA = 4 * B_W
AB_KA = AB_QA + A_QW
AB_VA = AB_KA + A_QW
AB_IG = AB_VA + A_QW
AB_FG = AB_IG + LANE
AB_N = AB_FG + LANE

C_CMP = C_Q
C_SEL = C_Q + KVW
C_WIN = C_Q + 2 * KVW
C_GATE = C_Q + 3 * KVW
C_N = 2048


def _cparams(sem, vmem=VMEM_LIMIT):
    return pltpu.CompilerParams(dimension_semantics=sem, vmem_limit_bytes=vmem)


def _pick_tile(m, pref):
    t = min(m, pref)
    while m % t:
        t //= 2
    return t


def _rope_lanes(x, cos, sin):
    lane = lax.broadcasted_iota(jnp.int32, x.shape, 1)
    first = (lane % DH_C) < (DH_C // 2)
    partner = jnp.where(first, pltpu.roll(x, LANE - DH_C // 2, 1), pltpu.roll(x, DH_C // 2, 1))
    return x * cos + partner * sin


def _split3(x):
    hi = x.astype(BF16)
    r1 = x - hi.astype(F32)
    mid = r1.astype(BF16)
    lo = (r1 - mid.astype(F32)).astype(BF16)
    return hi, mid, lo


def _dot_exact_rhs01(x, m01):
    hi, mid, lo = _split3(x)
    d = lambda a: jnp.dot(a, m01, preferred_element_type=F32)
    return d(hi) + d(mid) + d(lo)


def _dot_exact_lhs01(m01, x):
    hi, mid, lo = _split3(x)
    d = lambda a: jnp.dot(m01, a, preferred_element_type=F32)
    return d(hi) + d(mid) + d(lo)


def _dot_nt(a, b):
    return lax.dot_general(a, b, (((1,), (1,)), ((), ())), preferred_element_type=F32)


def _dot_tn(a, b):
    return lax.dot_general(a, b, (((0,), (0,)), ((), ())), preferred_element_type=F32)


def _log_sigmoid(x):
    return jnp.minimum(x, 0.0) - jnp.log1p(jnp.exp(-jnp.abs(x)))


def _gelu_tanh(x):
    return 0.5 * x * (1.0 + jnp.tanh(math.sqrt(2.0 / math.pi) * (x + 0.044715 * (x * x * x))))


def _rms_rows(x, g):
    ms = jnp.mean(x * x, axis=-1, keepdims=True)
    return x * lax.rsqrt(ms + EPS) * g


def _norm_proj_kernel(flags_ref, x_ref, g_ref, w_ref, cos_ref, sin_ref, o_ref, xn_ref, *, tn):
    j = pl.program_id(1)

    @pl.when(j == 0)
    def _():
        xn_ref[...] = _rms_rows(x_ref[...], g_ref[...]).astype(BF16)

    acc = jnp.dot(xn_ref[...], w_ref[...], preferred_element_type=F32)
    nchunk = tn // LANE
    for c in range(nchunk):
        a = acc[:, c * LANE:(c + 1) * LANE]
        flag = flags_ref[j * nchunk + c]

        @pl.when(flag == 1)
        def _():
            o_ref[:, c * LANE:(c + 1) * LANE] = _rope_lanes(a, cos_ref[...], sin_ref[...])

        @pl.when(flag == 0)
        def _():
            o_ref[:, c * LANE:(c + 1) * LANE] = a


def norm_proj(x, g, w_bf16, rope_flags, cos_t, sin_t, *, tm_pref=1024, tn=512):
    m, d = x.shape
    n = w_bf16.shape[1]
    tm = _pick_tile(m, tm_pref)
    grid = (m // tm, n // tn)
    gs = pltpu.PrefetchScalarGridSpec(
        num_scalar_prefetch=1,
        grid=grid,
        in_specs=[
            pl.BlockSpec((tm, d), lambda i, j, f: (i, 0)),
            pl.BlockSpec((1, d), lambda i, j, f: (0, 0)),
            pl.BlockSpec((d, tn), lambda i, j, f: (0, j)),
            pl.BlockSpec((tm, LANE), lambda i, j, f: (i, 0)),
            pl.BlockSpec((tm, LANE), lambda i, j, f: (i, 0)),
        ],
        out_specs=pl.BlockSpec((tm, tn), lambda i, j, f: (i, j)),
        scratch_shapes=[pltpu.VMEM((tm, d), BF16)],
    )
    return pl.pallas_call(
        functools.partial(_norm_proj_kernel, tn=tn),
        grid_spec=gs,
        out_shape=jax.ShapeDtypeStruct((m, n), F32),
        compiler_params=_cparams(("parallel", "arbitrary")),
        name="norm_proj",
    )(rope_flags, x, g.reshape(1, d), w_bf16, cos_t, sin_t)


def _ffn_kernel(x_ref, g_ref, wg_ref, wu_ref, wd_ref, gf_ref, o_ref, xn_ref, *, final_norm):
    f = pl.program_id(1)

    @pl.when(f == 0)
    def _():
        x = x_ref[...]
        xn_ref[...] = _rms_rows(x, g_ref[...]).astype(BF16)
        o_ref[...] = x

    xn = xn_ref[...]
    a = jnp.dot(xn, wg_ref[...], preferred_element_type=F32)
    u = jnp.dot(xn, wu_ref[...], preferred_element_type=F32)
    act = (a * jax.nn.sigmoid(a)) * u
    o_ref[...] += jnp.dot(act.astype(BF16), wd_ref[...], preferred_element_type=F32)

    if final_norm:
        @pl.when(f == pl.num_programs(1) - 1)
        def _():
            o_ref[...] = _rms_rows(o_ref[...], gf_ref[...])


def ffn(x, g, wg, wu, wd, g_final, *, final_norm, tm_pref=512):
    m, d = x.shape
    dff = wg.shape[1]
    tf = dff // 2 if (dff // 2) % LANE == 0 else dff
    tm = _pick_tile(m, tm_pref)
    grid = (m // tm, dff // tf)
    return pl.pallas_call(
        functools.partial(_ffn_kernel, final_norm=final_norm),
        grid=grid,
        in_specs=[
            pl.BlockSpec((tm, d), lambda i, f: (i, 0)),
            pl.BlockSpec((1, d), lambda i, f: (0, 0)),
            pl.BlockSpec((d, tf), lambda i, f: (0, f)),
            pl.BlockSpec((d, tf), lambda i, f: (0, f)),
            pl.BlockSpec((tf, d), lambda i, f: (f, 0)),
            pl.BlockSpec((1, d), lambda i, f: (0, 0)),
        ],
        out_specs=pl.BlockSpec((tm, d), lambda i, f: (i, 0)),
        out_shape=jax.ShapeDtypeStruct((m, d), F32),
        scratch_shapes=[pltpu.VMEM((tm, d), BF16)],
        compiler_params=_cparams(("parallel", "arbitrary")),
        name="ffn",
    )(x, g.reshape(1, d), wg, wu, wd, g_final.reshape(1, d))


def _out_proj_kernel(x_ref, w_ref, r_ref, o_ref):
    o_ref[...] = r_ref[...] + jnp.dot(x_ref[...].astype(BF16), w_ref[...], preferred_element_type=F32)


def out_proj(x, w_bf16, resid, *, tm_pref=512):
    m, k = x.shape
    n = w_bf16.shape[1]
    tm = _pick_tile(m, tm_pref)
    return pl.pallas_call(
        _out_proj_kernel,
        grid=(m // tm,),
        in_specs=[
            pl.BlockSpec((tm, k), lambda i: (i, 0)),
            pl.BlockSpec((k, n), lambda i: (0, 0)),
            pl.BlockSpec((tm, n), lambda i: (i, 0)),
        ],
        out_specs=pl.BlockSpec((tm, n), lambda i: (i, 0)),
        out_shape=jax.ShapeDtypeStruct((m, n), F32),
        compiler_params=_cparams(("parallel",)),
        name="out_proj",
    )(x, w_bf16, resid)


DIL_STEP_ROWS = 2048


def _dil_prompt_kernel(q_ref, kp_ref, kc_ref, vp_ref, vc_ref, o_ref, l_ref, *, nback, dil, nblk):
    n = pl.program_id(2)
    span = dil * BLK_A
    qi = BLK_A + lax.broadcasted_iota(jnp.int32, (BLK_A, 2 * BLK_A), 0)
    ki = lax.broadcasted_iota(jnp.int32, (BLK_A, 2 * BLK_A), 1)
    rel = qi - ki
    band = (rel >= 0) & (rel <= nback)
    own = ki >= BLK_A

    def one(it, carry):
        j = it // dil
        r = it % dil
        start = j * span + r
        if dil > 1:
            take = lambda st: pl.ds(st, BLK_A, stride=dil)
        else:
            take = lambda st: pl.ds(pl.multiple_of(st, BLK_A), BLK_A)
        rows = take(start)
        prev_rows = take(jnp.maximum(start - span, r))
        first_rows = take(r)
        q = q_ref[0, rows, :] * (DH_A ** -0.5)
        kprev = jnp.where(j > 0, kc_ref[0, prev_rows, :], kp_ref[0, first_rows, :])
        vprev = jnp.where(j > 0, vc_ref[0, prev_rows, :], vp_ref[0, first_rows, :])
        kk = jnp.concatenate([kprev, kc_ref[0, rows, :]], axis=0)
        vv = jnp.concatenate([vprev, vc_ref[0, rows, :]], axis=0)
        mask = band & ((n * nblk + j > 0) | own)
        outs, lses = [], []
        for h in range(LANE // DH_A):
            hs = slice(h * DH_A, (h + 1) * DH_A)
            s = _dot_nt(q[:, hs].astype(BF16), kk[:, hs].astype(BF16))
            s = jnp.where(mask, s, -jnp.inf)
            m = jnp.max(s, axis=-1, keepdims=True)
            p = jnp.exp(s - m)
            l = jnp.sum(p, axis=-1, keepdims=True)
            o = jnp.dot(p.astype(BF16), vv[:, hs].astype(BF16), preferred_element_type=F32) / l
            outs.append(o)
            lses.append(jnp.broadcast_to(m + jnp.log(l), (BLK_A, DH_A)))
        o_ref[0, rows, :] = jnp.concatenate(outs, axis=-1)
        l_ref[0, rows, :] = jnp.concatenate(lses, axis=-1)
        return carry

    lax.fori_loop(0, nblk * dil, one, 0, unroll=2)


def dil_prompt(z3, gi, window, dil):
    b, s, _ = z3.shape
    nback = window // dil
    span = dil * BLK_A
    nblk = max(1, DIL_STEP_ROWS // span)
    rows = nblk * span
    assert s % rows == 0 and (nblk * dil) % 2 == 0
    nh = AW // LANE
    qo, ko, vo = (AB_QA + gi * AW) // LANE, (AB_KA + gi * AW) // LANE, (AB_VA + gi * AW) // LANE
    blk = (1, rows, LANE)
    pblk = (1, span, LANE)
    in_specs = [
        pl.BlockSpec(blk, lambda bb, hp, i: (bb, i, qo + hp)),
        pl.BlockSpec(pblk, lambda bb, hp, i: (bb, jnp.maximum(i * nblk - 1, 0), ko + hp)),
        pl.BlockSpec(blk, lambda bb, hp, i: (bb, i, ko + hp)),
        pl.BlockSpec(pblk, lambda bb, hp, i: (bb, jnp.maximum(i * nblk - 1, 0), vo + hp)),
        pl.BlockSpec(blk, lambda bb, hp, i: (bb, i, vo + hp)),
    ]
    out_spec = pl.BlockSpec(blk, lambda bb, hp, i: (bb, i, hp))
    o, l = pl.pallas_call(
        functools.partial(_dil_prompt_kernel, nback=nback, dil=dil, nblk=nblk),
        grid=(b, nh, s // rows),
        in_specs=in_specs,
        out_specs=[out_spec, out_spec],
        out_shape=[jax.ShapeDtypeStruct((b, s, AW), F32)] * 2,
        compiler_params=_cparams(("parallel", "parallel", "arbitrary")),
        name="dil_prompt_%d" % gi,
    )(z3, z3, z3, z3, z3)
    return o.reshape(b * s, AW), l.reshape(b * s, AW)


def _dil_sample_kernel(c_ref, q_ref, kn_ref, vn_ref, nt_ref, o_ref, l_ref, co_ref, *, window, dil, n_buf, t_new):
    rows = H_A * T_PAD
    q = q_ref[0] * (DH_A ** -0.5)
    q4 = jnp.concatenate([q] * H_A, axis=0)
    rr = lax.broadcasted_iota(jnp.int32, (rows, AW), 0)
    ll = lax.broadcasted_iota(jnp.int32, (rows, AW), 1)
    head_sel = (rr // T_PAD) == (ll // DH_A)
    qbd = jnp.where(head_sel, q4, 0.0).astype(BF16)
    kbt = c_ref[0, 0:AW, :].astype(BF16)
    vbt = c_ref[0, AW:2 * AW, :].astype(BF16)
    kn = kn_ref[0]
    vn = vn_ref[0]
    s_buf = jnp.dot(qbd, kbt, preferred_element_type=F32)
    s_new = _dot_nt(qbd, kn.astype(BF16))
    t_b = lax.broadcasted_iota(jnp.int32, (rows, n_buf), 0) % T_PAD
    c_b = lax.broadcasted_iota(jnp.int32, (rows, n_buf), 1)
    d_b = n_buf + t_b - c_b
    m_b = ((d_b % dil) == 0) & (d_b <= window)
    t_n = lax.broadcasted_iota(jnp.int32, (rows, T_PAD), 0) % T_PAD
    u_n = lax.broadcasted_iota(jnp.int32, (rows, T_PAD), 1)
    d_n = t_n - u_n
    m_n = (d_n >= 0) & ((d_n % dil) == 0) & (d_n <= window) & (u_n < t_new)
    s_buf = jnp.where(m_b, s_buf, NEG_BIG)
    s_new = jnp.where(m_n, s_new, NEG_BIG)
    mx = jnp.maximum(jnp.max(s_buf, axis=-1, keepdims=True), jnp.max(s_new, axis=-1, keepdims=True))
    p_b = jnp.where(m_b, jnp.exp(s_buf - mx), 0.0)
    p_n = jnp.where(m_n, jnp.exp(s_new - mx), 0.0)
    l = jnp.sum(p_b, axis=-1, keepdims=True) + jnp.sum(p_n, axis=-1, keepdims=True)
    l = jnp.maximum(l, 1e-30)
    acc = (_dot_nt(p_b.astype(BF16), vbt)
           + jnp.dot(p_n.astype(BF16), vn.astype(BF16), preferred_element_type=F32))
    res = jnp.where(head_sel, acc / l, 0.0)
    lse = jnp.where(head_sel, mx + jnp.log(l), 0.0)
    o = res[0:T_PAD]
    ls = lse[0:T_PAD]
    for h in range(1, H_A):
        o = o + res[h * T_PAD:(h + 1) * T_PAD]
        ls = ls + lse[h * T_PAD:(h + 1) * T_PAD]
    o_ref[0] = o
    l_ref[0] = ls
    co_ref[0, :, 0:n_buf - t_new] = c_ref[0, :, t_new:n_buf]
    co_ref[0, :, n_buf - t_new:n_buf] = nt_ref[0, :, 0:t_new]


def dil_sample(cache_t, z3, new_t, gi, window, dil, t_new):
    db, _, n_buf = cache_t.shape
    qo, ko, vo = AB_QA // AW + gi, AB_KA // AW + gi, AB_VA // AW + gi
    blk = (1, T_PAD, AW)
    o, l, co = pl.pallas_call(
        functools.partial(_dil_sample_kernel, window=window, dil=dil, n_buf=n_buf, t_new=t_new),
        grid=(db,),
        in_specs=[
            pl.BlockSpec((1, 2 * AW, n_buf), lambda b: (b, 0, 0)),
            pl.BlockSpec(blk, lambda b: (b, 0, qo)),
            pl.BlockSpec(blk, lambda b: (b, 0, ko)),
            pl.BlockSpec(blk, lambda b: (b, 0, vo)),
            pl.BlockSpec((1, 2 * AW, T_PAD), lambda b: (b, 0, 0)),
        ],
        out_specs=[
            pl.BlockSpec(blk, lambda b: (b, 0, 0)),
            pl.BlockSpec(blk, lambda b: (b, 0, 0)),
            pl.BlockSpec((1, 2 * AW, n_buf), lambda b: (b, 0, 0)),
        ],
        out_shape=[
            jax.ShapeDtypeStruct((db, T_PAD, AW), F32),
            jax.ShapeDtypeStruct((db, T_PAD, AW), F32),
            jax.ShapeDtypeStruct((db, 2 * AW, n_buf), F32),
        ],
        compiler_params=_cparams(("parallel",)),
        name="dil_sample_%d" % gi,
    )(cache_t, z3, z3, z3, new_t)
    return o.reshape(db * T_PAD, AW), l.reshape(db * T_PAD, AW), co


def _mlstm_kernel(q_ref, k_ref, v_ref, gi_ref, gf_ref, gt_ref, bi_ref, bf_ref, b8_ref,
                  c0_ref, n0_ref, m0_ref, h_ref, co_ref, no_ref, mo_ref,
                  c_s, n_s, m_s, *, bb, L, n_valid):
    c = pl.program_id(1)

    @pl.when(c == 0)
    def _():
        c_s[...] = c0_ref[...]
        n_s[...] = n0_ref[...]
        m_s[...] = m0_ref[...]

    row = lax.broadcasted_iota(jnp.int32, (L, L), 0)
    col = lax.broadcasted_iota(jnp.int32, (L, L), 1)
    tri = row >= col
    tri_l = jnp.where(tri, 1.0, 0.0).astype(BF16)
    tri_u = jnp.where(row <= col, 1.0, 0.0).astype(BF16)
    lane = lax.broadcasted_iota(jnp.int32, (1, LANE), 1)
    for b in range(bb):
        ig_col = gi_ref[b] + bi_ref[...]
        lf_col = _log_sigmoid(gf_ref[b] + bf_ref[...])
        gt = gt_ref[b, 0]
        ig_row = gt[0:H_B] + b8_ref[0:H_B]
        lf_row = _log_sigmoid(gt[H_B:2 * H_B] + b8_ref[H_B:2 * H_B])
        if n_valid < L:
            rv = lax.broadcasted_iota(jnp.int32, (L, LANE), 0) < n_valid
            ig_col = jnp.where(rv, ig_col, -jnp.inf)
            lf_col = jnp.where(rv, lf_col, 0.0)
            cv = lax.broadcasted_iota(jnp.int32, (H_B, L), 1) < n_valid
            ig_row = jnp.where(cv, ig_row, -jnp.inf)
            lf_row = jnp.where(cv, lf_row, 0.0)
        b_col = _dot_exact_lhs01(tri_l, lf_col)
        b_row = _dot_exact_rhs01(lf_row, tri_u)
        m_row = m_s[b]
        a_col = b_col + m_row
        m_new = m_row
        hs_out = []
        for h in range(H_B):
            hsl = slice(h * DH_B, (h + 1) * DH_B)
            bc = b_col[:, h:h + 1]
            ac = a_col[:, h:h + 1]
            icol = ig_col[:, h:h + 1]
            D = bc - b_row[h:h + 1, :] + ig_row[h:h + 1, :]
            D = jnp.where(tri, D, -jnp.inf)
            mt = jnp.maximum(ac, jnp.max(D, axis=-1, keepdims=True))
            Dw = jnp.exp(D - mt)
            iw = jnp.exp(ac - mt)
            qf = q_ref[b, :, hsl]
            kf = k_ref[b, :, hsl] * (DH_B ** -0.5)
            vf = v_ref[b, :, hsl]
            qb, kb, vb = qf.astype(BF16), kf.astype(BF16), vf.astype(BF16)
            Cm = c_s[b * H_B + h]
            nv = n_s[b * H_B + h]
            sc = _dot_nt(qb, kb) * Dw
            num = iw * _dot_nt(qb, Cm.astype(BF16)) + jnp.dot(sc.astype(BF16), vb, preferred_element_type=F32)
            den = iw * jnp.sum(qf * nv, axis=-1, keepdims=True) + jnp.sum(sc, axis=-1, keepdims=True)
            hs_out.append(num / jnp.maximum(jnp.abs(den), jnp.exp(-mt)))
            mL = mt[L - 1:L, :]
            wL = jnp.exp(bc[L - 1:L, :] - bc + icol - mL)
            dec = jnp.exp(ac[L - 1:L, :] - mL)
            c_s[b * H_B + h] = dec * Cm + _dot_tn((vf * wL).astype(BF16), kb)
            n_s[b * H_B + h] = dec * nv + jnp.sum(wL * kf, axis=0, keepdims=True)
            m_new = jnp.where(lane == h, mL, m_new)
        m_s[b] = m_new
        h_ref[b] = jnp.concatenate(hs_out, axis=-1)

    @pl.when(c == pl.num_programs(1) - 1)
    def _():
        co_ref[...] = c_s[...]
        no_ref[...] = n_s[...]
        mo_ref[...] = m_s[...]


def mlstm(z3, gt, b_if, c0, n0, m0, *, L, n_valid, bb):
    b, s, _ = z3.shape
    nc = s // L
    bi_row = jnp.zeros((1, LANE), F32).at[0, :H_B].set(b_if[0])
    bf_row = jnp.zeros((1, LANE), F32).at[0, :H_B].set(b_if[1])
    b8 = b_if.reshape(2 * H_B, 1)
    c0r = c0.reshape(b * H_B, DH_B, DH_B)
    n0r = n0.reshape(b * H_B, 1, DH_B)
    m0r = jnp.zeros((b, 1, LANE), F32).at[:, 0, :H_B].set(m0)
    cw = B_W // LANE
    h, co, no, mo = pl.pallas_call(
        functools.partial(_mlstm_kernel, bb=bb, L=L, n_valid=n_valid),
        grid=(b // bb, nc),
        in_specs=[
            pl.BlockSpec((bb, L, B_W), lambda g, c: (g, c, AB_QB // B_W)),
            pl.BlockSpec((bb, L, B_W), lambda g, c: (g, c, AB_KB // B_W)),
            pl.BlockSpec((bb, L, B_W), lambda g, c: (g, c, AB_VB // B_W)),
            pl.BlockSpec((bb, L, LANE), lambda g, c: (g, c, AB_IG // LANE)),
            pl.BlockSpec((bb, L, LANE), lambda g, c: (g, c, AB_FG // LANE)),
            pl.BlockSpec((bb, 1, 2 * H_B, L), lambda g, c: (g, c, 0, 0)),
            pl.BlockSpec((1, LANE), lambda g, c: (0, 0)),
            pl.BlockSpec((1, LANE), lambda g, c: (0, 0)),
            pl.BlockSpec((2 * H_B, 1), lambda g, c: (0, 0)),
            pl.BlockSpec((bb * H_B, DH_B, DH_B), lambda g, c: (g, 0, 0)),
            pl.BlockSpec((bb * H_B, 1, DH_B), lambda g, c: (g, 0, 0)),
            pl.BlockSpec((bb, 1, LANE), lambda g, c: (g, 0, 0)),
        ],
        out_specs=[
            pl.BlockSpec((bb, L, B_W), lambda g, c: (g, c, 0)),
            pl.BlockSpec((bb * H_B, DH_B, DH_B), lambda g, c: (g, 0, 0)),
            pl.BlockSpec((bb * H_B, 1, DH_B), lambda g, c: (g, 0, 0)),
            pl.BlockSpec((bb, 1, LANE), lambda g, c: (g, 0, 0)),
        ],
        out_shape=[
            jax.ShapeDtypeStruct((b, s, B_W), F32),
            jax.ShapeDtypeStruct((b * H_B, DH_B, DH_B), F32),
            jax.ShapeDtypeStruct((b * H_B, 1, DH_B), F32),
            jax.ShapeDtypeStruct((b, 1, LANE), F32),
        ],
        scratch_shapes=[
            pltpu.VMEM((bb * H_B, DH_B, DH_B), F32),
            pltpu.VMEM((bb * H_B, 1, DH_B), F32),
            pltpu.VMEM((bb, 1, LANE), F32),
        ],
        compiler_params=_cparams(("parallel", "arbitrary")),
        name="mlstm_L%d" % L,
    )(z3, z3, z3, z3, z3, gt, bi_row, bf_row, b8, c0r, n0r, m0r)
    del cw
    return (h.reshape(b * s, B_W), co.reshape(b, H_B, DH_B, DH_B), no.reshape(b, H_B, DH_B),
            mo[:, 0, :H_B])


def _ab_merge_kernel(o0, o1, o2, l0, l1, l2, hb_ref, og_ref, g_ref, w_ref, r_ref, out_ref):
    a0, a1, a2 = l0[...], l1[...], l2[...]
    mx = jnp.maximum(jnp.maximum(a0, a1), a2)
    e0, e1, e2 = jnp.exp(a0 - mx), jnp.exp(a1 - mx), jnp.exp(a2 - mx)
    o_a = (e0 * o0[...] + e1 * o1[...] + e2 * o2[...]) / (e0 + e1 + e2)
    hb = hb_ref[...]
    parts = []
    for h in range(H_B):
        hs = slice(h * DH_B, (h + 1) * DH_B)
        x = hb[:, hs]
        parts.append(x * lax.rsqrt(jnp.mean(x * x, axis=-1, keepdims=True) + EPS))
    hbn = jnp.concatenate(parts, axis=-1) * g_ref[...] * jax.nn.sigmoid(og_ref[...])
    y = (jnp.dot(o_a.astype(BF16), w_ref[0:AW, :], preferred_element_type=F32)
         + jnp.dot(hbn.astype(BF16), w_ref[AW:AW + B_W, :], preferred_element_type=F32))
    out_ref[...] = r_ref[...] + y


def ab_merge(os_, ls_, hb, z2, g_mn, w_bf16, resid, *, tm_pref=512):
    m, d = resid.shape
    tm = _pick_tile(m, tm_pref)
    a_spec = pl.BlockSpec((tm, AW), lambda i: (i, 0))
    return pl.pallas_call(
        _ab_merge_kernel,
        grid=(m // tm,),
        in_specs=[a_spec] * 6 + [
            pl.BlockSpec((tm, B_W), lambda i: (i, 0)),
            pl.BlockSpec((tm, B_W), lambda i: (i, AB_OG // B_W)),
            pl.BlockSpec((1, B_W), lambda i: (0, 0)),
            pl.BlockSpec((AW + B_W, d), lambda i: (0, 0)),
            pl.BlockSpec((tm, d), lambda i: (i, 0)),
        ],
        out_specs=pl.BlockSpec((tm, d), lambda i: (i, 0)),
        out_shape=jax.ShapeDtypeStruct((m, d), F32),
        compiler_params=_cparams(("parallel",)),
        name="ab_merge",
    )(*os_, *ls_, hb, z2, g_mn.reshape(1, B_W), w_bf16, resid)


HKV = KVH_C * DH_C


def _compress_bias_kernel(pa_ref, pb_ref, w_ref, o_ref):
    w = w_ref[0]
    o_ref[0] = (jnp.dot(pa_ref[0].astype(BF16), w, preferred_element_type=F32)[:, 0:HKV]
                + jnp.dot(pb_ref[0].astype(BF16), w, preferred_element_type=F32)[:, HKV:2 * HKV])


def compress_bias(pa, pb, wab2):
    spec = lambda a: pl.BlockSpec((1,) + a.shape[1:], lambda i: (i,) + (0,) * (a.ndim - 1))
    return pl.pallas_call(
        _compress_bias_kernel,
        grid=(2,),
        in_specs=[spec(pa), spec(pb), spec(wab2)],
        out_specs=pl.BlockSpec((1, 1, HKV), lambda i: (i, 0, 0)),
        out_shape=jax.ShapeDtypeStruct((2, 1, HKV), F32),
        compiler_params=_cparams(("arbitrary",)),
        name="compress_bias",
    )(pa, pb, wab2)


def _compress_halves(load_rows, n, w_of_r):
    acc = jnp.zeros((n, 2 * HKV), F32)
    for r in range(CMP_STRIDE):
        acc = acc + jnp.dot(load_rows(r, n).astype(BF16), w_of_r(r), preferred_element_type=F32)
    return acc


def _compress_prompt_kernel(x_ref, wab_ref, bias_ref, w2_ref, o_ref):
    n = x_ref.shape[1] // CMP_STRIDE
    acc = _compress_halves(lambda r, m: x_ref[0, pl.ds(r, m, stride=CMP_STRIDE), :], n, lambda r: wab_ref[0, r])
    hid = acc[:, 0:HKV] + pltpu.roll(acc[:, HKV:2 * HKV], n - 1, 0) + bias_ref[0]
    o_ref[0] = jnp.dot(_gelu_tanh(hid).astype(BF16), w2_ref[0], preferred_element_type=F32)


def compress_prompt(z3, cw, bias):
    b, s, _ = z3.shape
    wab, w2 = cw
    n = s // CMP_STRIDE
    return pl.pallas_call(
        _compress_prompt_kernel,
        grid=(b, 2),
        in_specs=[
            pl.BlockSpec((1, s, HKV), lambda i, kv: (i, 0, C_CMP // HKV + kv)),
            pl.BlockSpec((1,) + wab.shape[1:], lambda i, kv: (kv, 0, 0, 0)),
            pl.BlockSpec((1, 1, HKV), lambda i, kv: (kv, 0, 0)),
            pl.BlockSpec((1, HKV, HKV), lambda i, kv: (kv, 0, 0)),
        ],
        out_specs=pl.BlockSpec((1, n, HKV), lambda i, kv: (i, 0, kv)),
        out_shape=jax.ShapeDtypeStruct((b, n, KVW), F32),
        compiler_params=_cparams(("parallel", "arbitrary")),
        name="compress_prompt",
    )(z3, wab, bias, w2)


def _compress_paged_kernel(pt_ref, *refs, pg):
    del pt_ref
    page_refs = refs[:pg + 1]
    wab_ref, bias_ref, w2_ref, o_ref, xs_ref = refs[pg + 1:]
    for j in range(pg + 1):
        t = page_refs[j][0].T
        xs_ref[0, j * PAGE_SIZE:(j + 1) * PAGE_SIZE, :] = t[:, 0:HKV]
        xs_ref[1, j * PAGE_SIZE:(j + 1) * PAGE_SIZE, :] = t[:, HKV:2 * HKV]
    cpp = PAGE_SIZE // CMP_STRIDE
    n = pg * cpp
    for kv in range(2):
        acc = _compress_halves(lambda r, m: xs_ref[kv, pl.ds(r, m, stride=CMP_STRIDE), :], n + cpp,
                               lambda r: wab_ref[kv, r])
        hid = acc[0:n, 0:HKV] + acc[1:n + 1, HKV:2 * HKV] + bias_ref[kv]
        o_ref[0, :, kv * HKV:(kv + 1) * HKV] = jnp.dot(_gelu_tanh(hid).astype(BF16), w2_ref[kv],
                                                        preferred_element_type=F32)


def compress_paged(pool_t, page_table, cw, bias, *, pg=32):
    db, n_pages = page_table.shape
    cpp = PAGE_SIZE // CMP_STRIDE
    pg = min(pg, n_pages)
    assert n_pages % pg == 0
    wab, w2 = cw
    full = lambda a: pl.BlockSpec(a.shape, lambda b, g, pt: (0,) * a.ndim)

    def page_spec(j):
        return pl.BlockSpec((1, KVW, PAGE_SIZE),
                            lambda b, g, pt: (pt[b, jnp.minimum(g * pg + j, n_pages - 1)], 0, 0))

    gs = pltpu.PrefetchScalarGridSpec(
        num_scalar_prefetch=1,
        grid=(db, n_pages // pg),
        in_specs=[page_spec(j) for j in range(pg + 1)] + [full(wab), full(bias), full(w2)],
        out_specs=pl.BlockSpec((1, pg * cpp, KVW), lambda b, g, pt: (b, g, 0)),
        scratch_shapes=[pltpu.VMEM((2, (pg + 1) * PAGE_SIZE, HKV), F32)],
    )
    return pl.pallas_call(
        functools.partial(_compress_paged_kernel, pg=pg),
        grid_spec=gs,
        out_shape=jax.ShapeDtypeStruct((db, n_pages * cpp, KVW), F32),
        compiler_params=_cparams(("parallel", "arbitrary")),
        name="compress_paged",
    )(page_table, *([pool_t] * (pg + 1)), wab, bias, w2)


def _stack_heads(x, t):
    del t
    return jnp.concatenate([x[:, g * DH_C:(g + 1) * DH_C] for g in range(G_C)], axis=0)


def _mask_rows(s, mask, fill):
    t, n = mask.shape
    return jnp.where(mask[None], s.reshape(G_C, t, n), fill).reshape(G_C * t, n)


def _cmp_branch(q2, kc, vc, t_pos, tq):
    n = kc.shape[0]
    s = _dot_nt(q2, kc)
    cend = lax.broadcasted_iota(jnp.int32, (tq, n), 1) * CMP_STRIDE + (CMP_BLK - 1)
    s = _mask_rows(s, cend <= t_pos, -jnp.inf)
    m = jnp.max(s, axis=-1, keepdims=True)
    m = jnp.where(m > -jnp.inf, m, 0.0)
    p = jnp.exp(s - m)
    p = p / jnp.maximum(jnp.sum(p, axis=-1, keepdims=True), 1e-30)
    o = jnp.dot(p.astype(BF16), vc, preferred_element_type=F32)
    pg = p[0:tq]
    for g in range(1, G_C):
        pg = pg + p[g * tq:(g + 1) * tq]
    return o, pg


def _select_blocks(imp, t_pos):
    tq, nsp = imp.shape
    j = lax.broadcasted_iota(jnp.int32, (tq, nsp), 1)
    jf = j.astype(F32)
    cur = t_pos // SLC_BLK
    forced = (j == 0) | (j == cur) | (j == cur - 1)
    work = jnp.where(forced, jnp.inf, jnp.where(j <= cur, imp, -jnp.inf))
    sel = jnp.zeros((tq, nsp), F32)
    for _ in range(N_SEL):
        mx = jnp.max(work, axis=-1, keepdims=True)
        first = jnp.min(jnp.where(work == mx, jf, float(nsp)), axis=-1, keepdims=True)
        pick = jf == first
        sel = jnp.where(pick, jnp.where(mx > -jnp.inf, 1.0, sel), sel)
        work = jnp.where(pick, -jnp.inf, work)
    return sel


def _flash_update(s, mask, v, m_ref, l_ref, acc_ref, v_channel_major=False):
    s = _mask_rows(s, mask, NEG_BIG)
    m_old = m_ref[...]
    m_new = jnp.maximum(m_old, jnp.max(s, axis=-1, keepdims=True))
    alpha = jnp.exp(m_old - m_new)
    p = jnp.exp(s - m_new)
    l_ref[...] = alpha * l_ref[...] + jnp.sum(p, axis=-1, keepdims=True)
    pv = _dot_nt(p.astype(BF16), v) if v_channel_major else jnp.dot(p.astype(BF16), v, preferred_element_type=F32)
    acc_ref[...] = alpha * acc_ref[...] + pv
    m_ref[...] = m_new


def _flash_finish(m_ref, l_ref, acc_ref):
    return jnp.where(m_ref[...] > 0.5 * NEG_BIG, acc_ref[...] / jnp.maximum(l_ref[...], 1e-30), 0.0)


def _softmax_av(s, mask, v):
    s = _mask_rows(s, mask, -jnp.inf)
    m = jnp.max(s, axis=-1, keepdims=True)
    m = jnp.where(m > -jnp.inf, m, 0.0)
    p = jnp.exp(s - m)
    p = p / jnp.maximum(jnp.sum(p, axis=-1, keepdims=True), 1e-30)
    return jnp.dot(p.astype(BF16), v, preferred_element_type=F32)


def _gate_mix(gates, h, o_cmp, o_sel, o_win, tq):
    outs = []
    for g in range(G_C):
        base = (h * G_C + g) * 3
        rs = slice(g * tq, (g + 1) * tq)
        outs.append(gates[:, base:base + 1] * o_cmp[rs] + gates[:, base + 1:base + 2] * o_sel[rs]
                    + gates[:, base + 2:base + 3] * o_win[rs])
    return jnp.concatenate(outs, axis=-1)


def _rope_q(qh, cos, sin):
    return jnp.concatenate(
        [_rope_lanes(qh[:, c * LANE:(c + 1) * LANE], cos, sin) for c in range(G_C * DH_C // LANE)], axis=-1)


SEL_TK = 512


SEL_PHASE = DH_C


def _nsa_prompt_kernel(q_ref, gz_ref, cos_ref, sin_ref, kc_ref, kv_ref, ka_ref, va_ref, mimp_ref, o_ref,
                       qa_s, m_s, acc_s, *, nsp):
    i = pl.program_id(1)
    tq = QBLK_C
    rows = G_C * tq
    s0 = i * tq
    t_pos = s0 + lax.broadcasted_iota(jnp.int32, (tq, 1), 0)
    gates = jax.nn.sigmoid(gz_ref[0])
    cos, sin = cos_ref[...], sin_ref[...]
    scale = DH_C ** -0.5
    hw = G_C * DH_C
    tiles_per_phase = SEL_PHASE * SLC_BLK // SEL_TK
    last = (s0 + tq - 1) // SEL_TK
    q_rot, o_cmps, imps = [], [], []
    for h in range(KVH_C):
        qh = q_ref[0, :, h * hw:(h + 1) * hw] * scale
        q_rot.append(_rope_q(qh, cos, sin))
        kc = kc_ref[0, :, h * DH_C:(h + 1) * DH_C].astype(BF16)
        vc = kc_ref[0, :, KVH_C * DH_C + h * DH_C:KVH_C * DH_C + (h + 1) * DH_C].astype(BF16)
        o_cmp, pgrp = _cmp_branch(_stack_heads(qh, tq).astype(BF16), kc, vc, t_pos, tq)
        o_cmps.append(o_cmp)
        imps.append(_dot_exact_rhs01(pgrp, mimp_ref[...]))
    sel = _select_blocks(jnp.concatenate(imps, axis=0), jnp.concatenate([t_pos] * KVH_C, axis=0))
    selb_all = jnp.where(sel > 0.5, 0.0, NEG_BIG)

    for h in range(KVH_C):
        qrh = q_rot[h]
        o_cmp = o_cmps[h]
        q2r = _stack_heads(qrh, tq).astype(BF16)
        selb = selb_all[h * tq:(h + 1) * tq]
        for ph in range(nsp // SEL_PHASE):
            sb = selb[:, ph * SEL_PHASE:(ph + 1) * SEL_PHASE]
            qa_s[ph] = jnp.concatenate(
                [jnp.concatenate([x, sb] if h == 0 else [sb, x], axis=1)
                 for x in (qrh[:, g * DH_C:(g + 1) * DH_C] for g in range(G_C))], axis=0).astype(BF16)
        m_s[...] = jnp.full(m_s.shape, NEG_BIG, F32)
        acc_s[...] = jnp.zeros(acc_s.shape, F32)

        def tile(kt, causal):
            k0 = pl.multiple_of(kt * SEL_TK, SEL_TK)
            ph = kt // tiles_per_phase
            st = _dot_nt(ka_ref[0, h, pl.ds(k0, SEL_TK), :], qa_s[ph])
            if causal:
                kpos = k0 + lax.broadcasted_iota(jnp.int32, (SEL_TK, 1), 0)
                t_row = s0 + lax.broadcasted_iota(jnp.int32, (1, rows), 1) % tq
                st = jnp.where(kpos <= t_row, st, NEG_BIG)
            m_old = m_s[...]
            m_new = jnp.maximum(m_old, jnp.max(st, axis=0, keepdims=True))
            pt = jnp.exp(st - m_new).astype(BF16)
            acc_s[...] = (jnp.exp(m_old - m_new) * acc_s[...]
                          + jnp.dot(va_ref[0, h, :, pl.ds(k0, SEL_TK)], pt, preferred_element_type=F32))
            m_s[...] = m_new

        def earlier_tile(kt, carry):
            tile(kt, False)
            return carry

        lax.fori_loop(0, last, earlier_tile, 0)
        tile(last, True)
        acc = acc_s[...]
        sums = acc[(1 - h) * DH_C:(1 - h) * DH_C + 1, :]
        o_sel = (acc[h * DH_C:(h + 1) * DH_C, :] / jnp.maximum(sums, 1e-30)).T
        kcol = h * DH_C
        vcol = KVH_C * DH_C + h * DH_C
        nw = WIN_C + tq
        w0 = pl.multiple_of(jnp.maximum(s0 - WIN_C, 0), tq)
        kw = kv_ref[0, pl.ds(w0, nw), kcol:kcol + DH_C]
        vw = kv_ref[0, pl.ds(w0, nw), vcol:vcol + DH_C]
        dlt = t_pos - (w0 + lax.broadcasted_iota(jnp.int32, (tq, nw), 1))
        mw = (dlt >= 0) & (dlt < WIN_C)
        o_win = _softmax_av(_dot_nt(q2r, kw), mw, vw)
        o_ref[0, :, h * hw:(h + 1) * hw] = _gate_mix(gates, h, o_cmp, o_sel, o_win, tq)


def _sel_operands(z3):
    s = z3.shape[1]
    lane = jnp.arange(LANE)
    hot = ((jnp.arange(s)[:, None] // SLC_BLK) % SEL_PHASE == lane[None, :] % SEL_PHASE).astype(BF16)
    ksel = z3[:, :, C_SEL:C_SEL + HKV].astype(BF16)
    vsel_t = jnp.swapaxes(z3[:, :, C_SEL + HKV:C_SEL + 2 * HKV].astype(BF16), 1, 2)
    ka = jnp.stack([jnp.where((lane // DH_C == h)[None, None, :], ksel, hot[None]) for h in range(KVH_C)], axis=1)
    va = jnp.stack([jnp.where((lane // DH_C == h)[None, :, None], vsel_t, jnp.ones_like(vsel_t))
                    for h in range(KVH_C)], axis=1)
    return ka, va


def nsa_prompt(z3, kc, kvw, ka, va, cos_t, sin_t, mimp):
    b, s, _ = z3.shape
    nsp = mimp.shape[1]
    nq = s // QBLK_C
    ncmp = kc.shape[1]
    rows = G_C * QBLK_C
    vm = pltpu.VMEM
    scratch = [
        vm((nsp // SEL_PHASE, rows, LANE), BF16),
        vm((1, rows), F32),
        vm((LANE, rows), F32),
    ]
    return pl.pallas_call(
        functools.partial(_nsa_prompt_kernel, nsp=nsp),
        grid=(b, nq),
        in_specs=[
            pl.BlockSpec((1, QBLK_C, C_Q), lambda bb, i: (bb, i, 0)),
            pl.BlockSpec((1, QBLK_C, LANE), lambda bb, i: (bb, i, C_GATE // LANE)),
            pl.BlockSpec((QBLK_C, LANE), lambda bb, i: (i, 0)),
            pl.BlockSpec((QBLK_C, LANE), lambda bb, i: (i, 0)),
            pl.BlockSpec((1, ncmp, KVW), lambda bb, i: (bb, 0, 0)),
            pl.BlockSpec((1, s, KVW), lambda bb, i: (bb, 0, 0)),
            pl.BlockSpec((1, KVH_C, s, LANE), lambda bb, i: (bb, 0, 0, 0)),
            pl.BlockSpec((1, KVH_C, LANE, s), lambda bb, i: (bb, 0, 0, 0)),
            pl.BlockSpec(mimp.shape, lambda bb, i: (0, 0)),
        ],
        out_specs=pl.BlockSpec((1, QBLK_C, C_Q), lambda bb, i: (bb, i, 0)),
        out_shape=jax.ShapeDtypeStruct((b, s, C_Q), F32),
        scratch_shapes=scratch,
        compiler_params=_cparams(("parallel", "arbitrary")),
        name="nsa_prompt",
    )(z3, z3, cos_t, sin_t, kc, kvw, ka, va, mimp)


def _nsa_sample_kernel(pt_ref, *refs, pg, nsp, past, t_new):
    del pt_ref
    page_refs = refs[:pg]
    (q_ref, gz_ref, cos_ref, sin_ref, kc_ref, mimp_ref, seln_ref, winb_ref, winn_ref, winnt_ref,
     o_ref, wo_ref, q2r_s, sel_s, ocmp_s, m_s, l_s, acc_s) = refs[pg:]
    g = pl.program_id(1)
    ng = pl.num_programs(1)
    tq = T_PAD
    rows = G_C * tq
    t_pos = past + lax.broadcasted_iota(jnp.int32, (tq, 1), 0)
    scale = DH_C ** -0.5
    hw = G_C * DH_C

    @pl.when(g == 0)
    def _():
        cos, sin = cos_ref[...], sin_ref[...]
        for h in range(KVH_C):
            qh = q_ref[0, :, h * hw:(h + 1) * hw] * scale
            q2 = _stack_heads(qh, tq).astype(BF16)
            q2r_s[h] = _stack_heads(_rope_q(qh, cos, sin), tq).astype(BF16)
            kc = kc_ref[0, :, h * DH_C:(h + 1) * DH_C].astype(BF16)
            vc = kc_ref[0, :, KVH_C * DH_C + h * DH_C:KVH_C * DH_C + (h + 1) * DH_C].astype(BF16)
            o_cmp, pgrp = _cmp_branch(q2, kc, vc, t_pos, tq)
            ocmp_s[h] = o_cmp
            imp = _dot_exact_rhs01(pgrp, mimp_ref[...])
            sel_s[h] = _select_blocks(imp, t_pos)
        m_s[...] = jnp.full(m_s.shape, NEG_BIG, F32)
        l_s[...] = jnp.zeros(l_s.shape, F32)
        acc_s[...] = jnp.zeros(acc_s.shape, F32)

    nk = pg * PAGE_SIZE
    kvt = jnp.concatenate([r[0] for r in page_refs], axis=1).astype(BF16)
    blk = lax.broadcasted_iota(jnp.int32, (nsp, nk), 0)
    kcl = lax.broadcasted_iota(jnp.int32, (nsp, nk), 1)
    expand = jnp.where(blk == g * (nk // SLC_BLK) + kcl // SLC_BLK, 1.0, 0.0).astype(BF16)
    for h in range(KVH_C):
        kst = kvt[h * DH_C:(h + 1) * DH_C, :]
        vst = kvt[KVH_C * DH_C + h * DH_C:KVH_C * DH_C + (h + 1) * DH_C, :]
        s = jnp.dot(q2r_s[h], kst, preferred_element_type=F32)
        mk = jnp.dot(sel_s[h].astype(BF16), expand, preferred_element_type=F32) > 0.5
        _flash_update(s, mk, vst, m_s.at[h], l_s.at[h], acc_s.at[h], v_channel_major=True)

    @pl.when(g == ng - 1)
    def _():
        gates = jax.nn.sigmoid(gz_ref[0])
        n_buf = winb_ref.shape[2]
        lane_n = lax.broadcasted_iota(jnp.int32, (tq, nsp), 1)
        u = lax.broadcasted_iota(jnp.int32, (tq, tq), 1)
        seln = seln_ref[0]
        winn = winn_ref[0]
        for h in range(KVH_C):
            kcol = h * DH_C
            vcol = KVH_C * DH_C + h * DH_C
            q2r = q2r_s[h]
            new_sel = jnp.max(jnp.where(lane_n == past // SLC_BLK, sel_s[h], 0.0), axis=-1, keepdims=True) > 0.5
            mk = new_sel & (past + u <= t_pos) & (u < t_new)
            _flash_update(_dot_nt(q2r, seln[:, kcol:kcol + DH_C].astype(BF16)), mk,
                          seln[:, vcol:vcol + DH_C].astype(BF16), m_s.at[h], l_s.at[h], acc_s.at[h])
            o_sel = _flash_finish(m_s.at[h], l_s.at[h], acc_s.at[h])
            kbt = winb_ref[0, kcol:kcol + DH_C, :].astype(BF16)
            vbt = winb_ref[0, vcol:vcol + DH_C, :].astype(BF16)
            pos_b = past - n_buf + lax.broadcasted_iota(jnp.int32, (tq, n_buf), 1)
            d_b = t_pos - pos_b
            m_b = (pos_b >= 0) & (d_b >= 0) & (d_b < WIN_C)
            d_n = t_pos - (past + u)
            m_n = (d_n >= 0) & (d_n < WIN_C) & (u < t_new)
            s_b = _mask_rows(jnp.dot(q2r, kbt, preferred_element_type=F32), m_b, -jnp.inf)
            s_n = _mask_rows(_dot_nt(q2r, winn[:, kcol:kcol + DH_C].astype(BF16)), m_n, -jnp.inf)
            mx = jnp.maximum(jnp.max(s_b, axis=-1, keepdims=True), jnp.max(s_n, axis=-1, keepdims=True))
            mx = jnp.where(mx > -jnp.inf, mx, 0.0)
            p_b, p_n = jnp.exp(s_b - mx), jnp.exp(s_n - mx)
            den = jnp.maximum(jnp.sum(p_b, axis=-1, keepdims=True) + jnp.sum(p_n, axis=-1, keepdims=True), 1e-30)
            o_win = (_dot_nt((p_b / den).astype(BF16), vbt)
                     + jnp.dot((p_n / den).astype(BF16), winn[:, vcol:vcol + DH_C].astype(BF16),
                               preferred_element_type=F32))
            o_ref[0, :, h * hw:(h + 1) * hw] = _gate_mix(gates, h, ocmp_s[h], o_sel, o_win, tq)
        wo_ref[0, :, 0:n_buf - t_new] = winb_ref[0, :, t_new:n_buf]
        wo_ref[0, :, n_buf - t_new:n_buf] = winnt_ref[0, :, 0:t_new]


def nsa_sample(z3, kc, sel_pool_t, win_buf_t, win_new_t, page_table, cos_t, sin_t, mimp, *, t_new, pg=16):
    db, n_pages = page_table.shape
    past = n_pages * PAGE_SIZE
    ncmp = kc.shape[1]
    nsp = mimp.shape[1]
    n_buf = win_buf_t.shape[2]
    pg = min(pg, n_pages)
    assert n_pages % pg == 0
    rows = G_C * T_PAD

    def page_spec(j):
        return pl.BlockSpec((1, KVW, PAGE_SIZE), lambda b, g, pt: (pt[b, g * pg + j], 0, 0))

    gs = pltpu.PrefetchScalarGridSpec(
        num_scalar_prefetch=1,
        grid=(db, n_pages // pg),
        in_specs=[page_spec(j) for j in range(pg)] + [
            pl.BlockSpec((1, T_PAD, C_Q), lambda b, g, pt: (b, 0, 0)),
            pl.BlockSpec((1, T_PAD, LANE), lambda b, g, pt: (b, 0, C_GATE // LANE)),
            pl.BlockSpec((T_PAD, LANE), lambda b, g, pt: (0, 0)),
            pl.BlockSpec((T_PAD, LANE), lambda b, g, pt: (0, 0)),
            pl.BlockSpec((1, ncmp, KVW), lambda b, g, pt: (b, 0, 0)),
            pl.BlockSpec(mimp.shape, lambda b, g, pt: (0, 0)),
            pl.BlockSpec((1, T_PAD, KVW), lambda b, g, pt: (b, 0, C_SEL // KVW)),
            pl.BlockSpec((1, KVW, n_buf), lambda b, g, pt: (b, 0, 0)),
            pl.BlockSpec((1, T_PAD, KVW), lambda b, g, pt: (b, 0, C_WIN // KVW)),
            pl.BlockSpec((1, KVW, T_PAD), lambda b, g, pt: (b, 0, 0)),
        ],
        out_specs=[
            pl.BlockSpec((1, T_PAD, C_Q), lambda b, g, pt: (b, 0, 0)),
            pl.BlockSpec((1, KVW, n_buf), lambda b, g, pt: (b, 0, 0)),
        ],
        scratch_shapes=[
            pltpu.VMEM((KVH_C, rows, DH_C), BF16),
            pltpu.VMEM((KVH_C, T_PAD, nsp), F32),
            pltpu.VMEM((KVH_C, rows, DH_C), F32),
            pltpu.VMEM((KVH_C, rows, 1), F32),
            pltpu.VMEM((KVH_C, rows, 1), F32),
            pltpu.VMEM((KVH_C, rows, DH_C), F32),
        ],
    )
    return pl.pallas_call(
        functools.partial(_nsa_sample_kernel, pg=pg, nsp=nsp, past=past, t_new=t_new),
        grid_spec=gs,
        out_shape=[jax.ShapeDtypeStruct((db, T_PAD, C_Q), F32), jax.ShapeDtypeStruct((db, KVW, n_buf), F32)],
        compiler_params=_cparams(("parallel", "arbitrary")),
        name="nsa_sample",
    )(page_table, *([sel_pool_t] * pg), z3, z3, cos_t, sin_t, kc, mimp, z3, win_buf_t, z3, win_new_t)


def _rope_tables(pos):
    half = DH_C // 2
    inv = 1.0 / (ROPE_THETA ** (jnp.arange(half, dtype=F32) / half))
    ang = pos.astype(F32)[:, None] * inv[None, :]
    cos, sin = jnp.cos(ang), jnp.sin(ang)
    return jnp.tile(cos, (1, 4)), jnp.concatenate([-sin, sin, -sin, sin], axis=-1)


def _prep_w_in_ab(w):
    cuts = np.cumsum([A_QW, A_QW, A_QW, B_W, B_W, B_W, H_B, H_B])
    qa, ka, va, qb, kb, vb, ig, fg, og = jnp.split(w, cuts, axis=-1)
    padg = lambda t: jnp.pad(t, ((0, 0), (0, LANE - H_B)))
    return jnp.concatenate([qb, kb, vb, og, qa, ka, va, padg(ig), padg(fg)], axis=-1).astype(BF16)


def _ab_rope_flags():
    f = np.zeros((AB_N // LANE,), np.int32)
    f[AB_QA // LANE:AB_VA // LANE] = 1
    return jnp.asarray(f)


def _c_rope_flags():
    f = np.zeros((C_N // LANE,), np.int32)
    f[C_SEL // LANE] = 1
    f[C_WIN // LANE] = 1
    return jnp.asarray(f)


def _prep_compress(w1, w2, pe):
    e2 = jnp.eye(KVH_C, dtype=F32)

    def half(w1h):
        return jnp.einsum('krde,hH->krhdHe', w1h, e2).reshape(2, CMP_STRIDE, HKV, HKV).astype(BF16)

    wab = jnp.concatenate([half(w1[:, :CMP_STRIDE]), half(w1[:, CMP_STRIDE:])], axis=-1)
    w2b = jnp.einsum('ked,hH->kheHd', w2, e2).reshape(2, HKV, HKV).astype(BF16)

    def pe_half(p):
        return jnp.broadcast_to(p[:, :, None, :], (2, CMP_STRIDE, KVH_C, DH_C)).reshape(2, 1, CMP_STRIDE * HKV)

    bias = compress_bias(pe_half(pe[:, :CMP_STRIDE]), pe_half(pe[:, CMP_STRIDE:]),
                         wab.reshape(2, CMP_STRIDE * HKV, 2 * HKV))
    return (wab, w2b), bias


def _channel_major(x, lead):
    perm = tuple(range(lead)) + (lead + 1, lead + 2, lead + 3, lead)
    xt = jnp.transpose(x, perm)
    return xt.reshape(x.shape[:lead] + (x.shape[lead + 1] * x.shape[lead + 2] * x.shape[lead + 3], x.shape[lead]))


def _row_major(xt, c0, c1, c2):
    lead, _, rows = xt.shape
    return jnp.transpose(xt.reshape(lead, c0, c1, c2, rows), (0, 4, 1, 2, 3))


def _importance_matrix(n_rows, n_cmp, n_slc, n_cols):
    ratio = SLC_BLK // CMP_STRIDE
    m = np.zeros((n_rows, n_cols), np.float32)
    for jblk in range(n_slc):
        for off in range(1 - CMP_BLK // CMP_STRIDE, ratio):
            i = ratio * jblk + off
            if 0 <= i < n_cmp:
                m[i, jblk] = 1.0
    return jnp.asarray(m, dtype=BF16)


def _gates_t(z3, L):
    b, s, _ = z3.shape
    g = jnp.concatenate([z3[..., AB_IG:AB_IG + H_B], z3[..., AB_FG:AB_FG + H_B]], axis=-1)
    return g.reshape(b, s // L, L, 2 * H_B).transpose(0, 1, 3, 2)


def kernel(x_prompt, x_sample, cache_a0_kv, cache_a1_kv, cache_a2_kv, state_b_C, state_b_n, state_b_m,
           cache_c_cmp_kv, cache_c_sel_kv, cache_c_win_kv, page_table, norm_g, w_in_ab, b_if, g_mlstm,
           w_out_ab, w_in_c, cmp_w1, cmp_w2, cmp_pe, w_out_c, w_ffn_gate, w_ffn_up, w_ffn_down, norm_final):
    B, S, D = x_prompt.shape
    DB, T, _ = x_sample.shape
    depth = norm_g.shape[0]
    n_pages = page_table.shape[1]
    past = n_pages * PAGE_SIZE
    caches_a = (cache_a0_kv, cache_a1_kv, cache_a2_kv)
    assert T <= T_PAD and S % (DIL_CFG[-1][1] * BLK_A) == 0 and S >= WIN_C + QBLK_C

    hp = x_prompt.reshape(B * S, D)
    hs = jnp.pad(x_sample, ((0, 0), (0, T_PAD - T), (0, 0))).reshape(DB * T_PAD, D)

    pos_p = jnp.arange(S)
    pos_s = past + jnp.arange(T_PAD)
    cos_p1, sin_p1 = _rope_tables(pos_p)
    cos_s1, sin_s1 = _rope_tables(pos_s)
    cos_p, sin_p = jnp.tile(cos_p1, (B, 1)), jnp.tile(sin_p1, (B, 1))
    cos_s, sin_s = jnp.tile(cos_s1, (DB, 1)), jnp.tile(sin_s1, (DB, 1))

    a_p, a_s = [[], [], []], [[], [], []]
    bC_p, bC_s, bn_p, bn_s, bm_p, bm_s = [], [], [], [], [], []
    cc_p, cc_s, csl_p, csl_s, cw_p, cw_s = [], [], [], [], [], []

    for layer in range(depth):
        if layer % 2 == 0:
            e = layer // 2
            w_in = _prep_w_in_ab(w_in_ab[e])
            flags = _ab_rope_flags()
            w_out = w_out_ab[e].astype(BF16)
            z = norm_proj(hp, norm_g[layer, 0], w_in, flags, cos_p, sin_p)
            z3 = z.reshape(B, S, AB_N)
            os_, ls_ = [], []
            for gi, (win, dil) in enumerate(DIL_CFG):
                o, l = dil_prompt(z3, gi, win, dil)
                os_.append(o)
                ls_.append(l)
                nb = min(win, S)
                kk = z3[:, S - nb:, AB_KA + gi * AW:AB_KA + (gi + 1) * AW].reshape(B, nb, H_A, DH_A)
                vv = z3[:, S - nb:, AB_VA + gi * AW:AB_VA + (gi + 1) * AW].reshape(B, nb, H_A, DH_A)
                a_p[gi].append(jnp.stack([kk, vv], axis=2))
            zc = jnp.zeros
            hb, Cp, n_p, m_p = mlstm(z3, _gates_t(z3, MLSTM_CHUNK), b_if[e],
                                     zc((B, H_B, DH_B, DH_B), F32), zc((B, H_B, DH_B), F32), zc((B, H_B), F32),
                                     L=MLSTM_CHUNK, n_valid=MLSTM_CHUNK, bb=B if B <= 2 else 1)
            hp = ab_merge(os_, ls_, hb, z, g_mlstm[e], w_out, hp)
            bC_p.append(Cp); bn_p.append(n_p); bm_p.append(m_p)
            z = norm_proj(hs, norm_g[layer, 0], w_in, flags, cos_s, sin_s)
            z3 = z.reshape(DB, T_PAD, AB_N)
            os_, ls_ = [], []
            for gi, (win, dil) in enumerate(DIL_CFG):
                new_t = jnp.concatenate(
                    [jnp.swapaxes(z3[:, :, AB_KA + gi * AW:AB_KA + (gi + 1) * AW], 1, 2),
                     jnp.swapaxes(z3[:, :, AB_VA + gi * AW:AB_VA + (gi + 1) * AW], 1, 2)], axis=1)
                o, l, co = dil_sample(_channel_major(caches_a[gi][e], 1), z3, new_t, gi, win, dil, T)
                os_.append(o)
                ls_.append(l)
                a_s[gi].append(_row_major(co, 2, H_A, DH_A))
            bbs = 4 if DB % 4 == 0 else 1
            hb, Cs, n_s, m_s = mlstm(z3, _gates_t(z3, T_PAD), b_if[e], state_b_C[e], state_b_n[e], state_b_m[e],
                                     L=T_PAD, n_valid=T, bb=bbs)
            hs = ab_merge(os_, ls_, hb, z, g_mlstm[e], w_out, hs)
            bC_s.append(Cs); bn_s.append(n_s); bm_s.append(m_s)
        else:
            o_i = layer // 2
            w_in = jnp.pad(w_in_c[o_i], ((0, 0), (0, C_N - w_in_c.shape[-1]))).astype(BF16)
            flags = _c_rope_flags()
            w_out = w_out_c[o_i].astype(BF16)
            cw, cbias = _prep_compress(cmp_w1[o_i], cmp_w2[o_i], cmp_pe[o_i])
            z = norm_proj(hp, norm_g[layer, 0], w_in, flags, cos_p, sin_p)
            z3 = z.reshape(B, S, C_N)
            kv_cmp = z3[:, :, C_CMP:C_CMP + KVW]
            kv_sel = z3[:, :, C_SEL:C_SEL + KVW]
            kv_win = z3[:, :, C_WIN:C_WIN + KVW]
            kc = compress_prompt(z3, cw, cbias)
            n_cmp = (S - CMP_BLK) // CMP_STRIDE + 1
            n_slc = S // SLC_BLK
            mimp = _importance_matrix(S // CMP_STRIDE, n_cmp, n_slc, -(-n_slc // LANE) * LANE)
            ka, va = _sel_operands(z3)
            o = nsa_prompt(z3, kc, kv_win.astype(BF16), ka, va, cos_p1, sin_p1, mimp)
            hp = out_proj(o.reshape(B * S, C_Q), w_out, hp)
            nw = min(WIN_C, S)
            sh = lambda t: t.reshape(t.shape[0], t.shape[1], 2, KVH_C, DH_C)
            cc_p.append(sh(kv_cmp)); csl_p.append(sh(kv_sel)); cw_p.append(sh(kv_win[:, S - nw:]))
            z = norm_proj(hs, norm_g[layer, 0], w_in, flags, cos_s, sin_s)
            z3 = z.reshape(DB, T_PAD, C_N)
            kc = compress_paged(_channel_major(cache_c_cmp_kv[o_i], 1), page_table, cw, cbias)
            full_len = past + T
            n_cmp = (full_len - CMP_BLK) // CMP_STRIDE + 1
            assert (n_cmp + 1) * CMP_STRIDE <= past
            n_slc = past // SLC_BLK + -(-T // SLC_BLK)
            nsp = -(-n_slc // LANE) * LANE
            mimp = _importance_matrix(past // CMP_STRIDE, n_cmp, n_slc, nsp)
            o, wo = nsa_sample(z3, kc, _channel_major(cache_c_sel_kv[o_i], 1),
                               _channel_major(cache_c_win_kv[o_i], 1),
                               jnp.swapaxes(z3[:, :, C_WIN:C_WIN + KVW], 1, 2),
                               page_table, cos_s1, sin_s1, mimp, t_new=T)
            hs = out_proj(o.reshape(DB * T_PAD, C_Q), w_out, hs)
            cc_s.append(sh(z3[:, :T, C_CMP:C_CMP + KVW])); csl_s.append(sh(z3[:, :T, C_SEL:C_SEL + KVW]))
            cw_s.append(_row_major(wo, 2, KVH_C, DH_C))
        last = layer == depth - 1
        wg, wu, wd = (w_ffn_gate[layer].astype(BF16), w_ffn_up[layer].astype(BF16), w_ffn_down[layer].astype(BF16))
        hp = ffn(hp, norm_g[layer, 1], wg, wu, wd, norm_final, final_norm=last)
        hs = ffn(hs, norm_g[layer, 1], wg, wu, wd, norm_final, final_norm=last)

    y_prompt = hp.reshape(B, S, D)
    y_sample = hs.reshape(DB, T_PAD, D)[:, :T]
    st = lambda xs: jnp.stack(xs, axis=0)
    return (y_prompt, y_sample,
            st(a_p[0]), st(a_s[0]), st(a_p[1]), st(a_s[1]), st(a_p[2]), st(a_s[2]),
            st(bC_p), st(bC_s), st(bn_p), st(bn_s), st(bm_p), st(bm_s),
            st(cc_p), st(cc_s), st(csl_p), st(csl_s), st(cw_p), st(cw_s))
```

```python
import functools
import math

import numpy as np
import jax
import jax.numpy as jnp
from jax import lax
from jax.experimental import pallas as pl
from jax.experimental.pallas import tpu as pltpu

F32 = jnp.float32
BF16 = jnp.bfloat16

PAGE_SIZE = 128
DIL_CFG = ((128, 1), (512, 4), (2048, 16))
N_DIL = 3
H_A = 4
DH_A = 64
BLK_A = 128
H_B = 4
DH_B = 128
MLSTM_CHUNK = 128
H_C = 16
KVH_C = 2
G_C = H_C // KVH_C
DH_C = 64
CMP_STRIDE = 16
CMP_BLK = 2 * CMP_STRIDE
CMP_HID = 64
SLC_BLK = 64
N_SEL = 16
WIN_C = 512
QBLK_C = 128
ROPE_THETA = 10000.0
EPS = 1e-6
A_QW = N_DIL * H_A * DH_A
B_W = H_B * DH_B
AW = H_A * DH_A
C_Q = H_C * DH_C
C_KV = 3 * 2 * KVH_C * DH_C
KVW = 2 * KVH_C * DH_C

LANE = 128
SUBLANE = 8
VMEM_LIMIT = 48 * 1024 * 1024

NEG_BIG = -1e30
T_PAD = SUBLANE

AB_QB, AB_KB, AB_VB, AB_OG = 0, B_W, 2 * B_W, 3 * B_W
AB_QA = 4 * B_W
AB_KA = AB_QA + A_QW
AB_VA = AB_KA + A_QW
AB_IG = AB_VA + A_QW
AB_FG = AB_IG + LANE
AB_N = AB_FG + LANE

C_CMP = C_Q
C_SEL = C_Q + KVW
C_WIN = C_Q + 2 * KVW
C_GATE = C_Q + 3 * KVW
C_N = 2048


def _cparams(sem, vmem=VMEM_LIMIT):
    return pltpu.CompilerParams(dimension_semantics=sem, vmem_limit_bytes=vmem)


def _pick_tile(m, pref):
    t = min(m, pref)
    while m % t:
        t //= 2
    return t


def _rope_lanes(x, cos, sin):
    lane = lax.broadcasted_iota(jnp.int32, x.shape, 1)
    first = (lane % DH_C) < (DH_C // 2)
    partner = jnp.where(first, pltpu.roll(x, LANE - DH_C // 2, 1), pltpu.roll(x, DH_C // 2, 1))
    return x * cos + partner * sin


def _split3(x):
    hi = x.astype(BF16)
    r1 = x - hi.astype(F32)
    mid = r1.astype(BF16)
    lo = (r1 - mid.astype(F32)).astype(BF16)
    return hi, mid, lo


def _dot_exact_rhs01(x, m01):
    hi, mid, lo = _split3(x)
    d = lambda a: jnp.dot(a, m01, preferred_element_type=F32)
    return d(hi) + d(mid) + d(lo)


def _dot_exact_lhs01(m01, x):
    hi, mid, lo = _split3(x)
    d = lambda a: jnp.dot(m01, a, preferred_element_type=F32)
    return d(hi) + d(mid) + d(lo)


def _dot_nt(a, b):
    return lax.dot_general(a, b, (((1,), (1,)), ((), ())), preferred_element_type=F32)


def _dot_tn(a, b):
    return lax.dot_general(a, b, (((0,), (0,)), ((), ())), preferred_element_type=F32)


def _log_sigmoid(x):
    return jnp.minimum(x, 0.0) - jnp.log1p(jnp.exp(-jnp.abs(x)))


def _gelu_tanh(x):
    return 0.5 * x * (1.0 + jnp.tanh(math.sqrt(2.0 / math.pi) * (x + 0.044715 * (x * x * x))))


def _rms_rows(x, g):
    ms = jnp.mean(x * x, axis=-1, keepdims=True)
    return x * lax.rsqrt(ms + EPS) * g


def _norm_proj_kernel(flags_ref, x_ref, g_ref, w_ref, cos_ref, sin_ref, o_ref, xn_ref, *, tn):
    j = pl.program_id(1)

    @pl.when(j == 0)
    def _():
        xn_ref[...] = _rms_rows(x_ref[...], g_ref[...]).astype(BF16)

    acc = jnp.dot(xn_ref[...], w_ref[...], preferred_element_type=F32)
    nchunk = tn // LANE
    for c in range(nchunk):
        a = acc[:, c * LANE:(c + 1) * LANE]
        flag = flags_ref[j * nchunk + c]

        @pl.when(flag == 1)
        def _():
            o_ref[:, c * LANE:(c + 1) * LANE] = _rope_lanes(a, cos_ref[...], sin_ref[...])

        @pl.when(flag == 0)
        def _():
            o_ref[:, c * LANE:(c + 1) * LANE] = a


def norm_proj(x, g, w_bf16, rope_flags, cos_t, sin_t, *, tm_pref=1024, tn=512):
    m, d = x.shape
    n = w_bf16.shape[1]
    tm = _pick_tile(m, tm_pref)
    grid = (m // tm, n // tn)
    gs = pltpu.PrefetchScalarGridSpec(
        num_scalar_prefetch=1,
        grid=grid,
        in_specs=[
            pl.BlockSpec((tm, d), lambda i, j, f: (i, 0)),
            pl.BlockSpec((1, d), lambda i, j, f: (0, 0)),
            pl.BlockSpec((d, tn), lambda i, j, f: (0, j)),
            pl.BlockSpec((tm, LANE), lambda i, j, f: (i, 0)),
            pl.BlockSpec((tm, LANE), lambda i, j, f: (i, 0)),
        ],
        out_specs=pl.BlockSpec((tm, tn), lambda i, j, f: (i, j)),
        scratch_shapes=[pltpu.VMEM((tm, d), BF16)],
    )
    return pl.pallas_call(
        functools.partial(_norm_proj_kernel, tn=tn),
        grid_spec=gs,
        out_shape=jax.ShapeDtypeStruct((m, n), F32),
        compiler_params=_cparams(("parallel", "arbitrary")),
        name="norm_proj",
    )(rope_flags, x, g.reshape(1, d), w_bf16, cos_t, sin_t)


def _ffn_kernel(x_ref, g_ref, wg_ref, wu_ref, wd_ref, gf_ref, o_ref, xn_ref, *, final_norm):
    f = pl.program_id(1)

    @pl.when(f == 0)
    def _():
        x = x_ref[...]
        xn_ref[...] = _rms_rows(x, g_ref[...]).astype(BF16)
        o_ref[...] = x

    xn = xn_ref[...]
    a = jnp.dot(xn, wg_ref[...], preferred_element_type=F32)
    u = jnp.dot(xn, wu_ref[...], preferred_element_type=F32)
    act = (a * jax.nn.sigmoid(a)) * u
    o_ref[...] += jnp.dot(act.astype(BF16), wd_ref[...], preferred_element_type=F32)

    if final_norm:
        @pl.when(f == pl.num_programs(1) - 1)
        def _():
            o_ref[...] = _rms_rows(o_ref[...], gf_ref[...])


def ffn(x, g, wg, wu, wd, g_final, *, final_norm, tm_pref=512):
    m, d = x.shape
    dff = wg.shape[1]
    tf = dff // 2 if (dff // 2) % LANE == 0 else dff
    tm = _pick_tile(m, tm_pref)
    grid = (m // tm, dff // tf)
    return pl.pallas_call(
        functools.partial(_ffn_kernel, final_norm=final_norm),
        grid=grid,
        in_specs=[
            pl.BlockSpec((tm, d), lambda i, f: (i, 0)),
            pl.BlockSpec((1, d), lambda i, f: (0, 0)),
            pl.BlockSpec((d, tf), lambda i, f: (0, f)),
            pl.BlockSpec((d, tf), lambda i, f: (0, f)),
            pl.BlockSpec((tf, d), lambda i, f: (f, 0)),
            pl.BlockSpec((1, d), lambda i, f: (0, 0)),
        ],
        out_specs=pl.BlockSpec((tm, d), lambda i, f: (i, 0)),
        out_shape=jax.ShapeDtypeStruct((m, d), F32),
        scratch_shapes=[pltpu.VMEM((tm, d), BF16)],
        compiler_params=_cparams(("parallel", "arbitrary")),
        name="ffn",
    )(x, g.reshape(1, d), wg, wu, wd, g_final.reshape(1, d))


def _out_proj_kernel(x_ref, w_ref, r_ref, o_ref):
    o_ref[...] = r_ref[...] + jnp.dot(x_ref[...].astype(BF16), w_ref[...], preferred_element_type=F32)


def out_proj(x, w_bf16, resid, *, tm_pref=512):
    m, k = x.shape
    n = w_bf16.shape[1]
    tm = _pick_tile(m, tm_pref)
    return pl.pallas_call(
        _out_proj_kernel,
        grid=(m // tm,),
        in_specs=[
            pl.BlockSpec((tm, k), lambda i: (i, 0)),
            pl.BlockSpec((k, n), lambda i: (0, 0)),
            pl.BlockSpec((tm, n), lambda i: (i, 0)),
        ],
        out_specs=pl.BlockSpec((tm, n), lambda i: (i, 0)),
        out_shape=jax.ShapeDtypeStruct((m, n), F32),
        compiler_params=_cparams(("parallel",)),
        name="out_proj",
    )(x, w_bf16, resid)


DIL_STEP_ROWS = 2048


def _dil_prompt_kernel(q_ref, kp_ref, kc_ref, vp_ref, vc_ref, o_ref, l_ref, *, nback, dil, nblk):
    n = pl.program_id(2)
    span = dil * BLK_A
    qi = BLK_A + lax.broadcasted_iota(jnp.int32, (BLK_A, 2 * BLK_A), 0)
    ki = lax.broadcasted_iota(jnp.int32, (BLK_A, 2 * BLK_A), 1)
    rel = qi - ki
    band = (rel >= 0) & (rel <= nback)
    own = ki >= BLK_A

    def one(it, carry):
        j = it // dil
        r = it % dil
        start = j * span + r
        if dil > 1:
            take = lambda st: pl.ds(st, BLK_A, stride=dil)
        else:
            take = lambda st: pl.ds(pl.multiple_of(st, BLK_A), BLK_A)
        rows = take(start)
        prev_rows = take(jnp.maximum(start - span, r))
        first_rows = take(r)
        q = q_ref[0, rows, :] * (DH_A ** -0.5)
        kprev = jnp.where(j > 0, kc_ref[0, prev_rows, :], kp_ref[0, first_rows, :])
        vprev = jnp.where(j > 0, vc_ref[0, prev_rows, :], vp_ref[0, first_rows, :])
        kk = jnp.concatenate([kprev, kc_ref[0, rows, :]], axis=0)
        vv = jnp.concatenate([vprev, vc_ref[0, rows, :]], axis=0)
        mask = band & ((n * nblk + j > 0) | own)
        outs, lses = [], []
        for h in range(LANE // DH_A):
            hs = slice(h * DH_A, (h + 1) * DH_A)
            s = _dot_nt(q[:, hs].astype(BF16), kk[:, hs].astype(BF16))
            s = jnp.where(mask, s, -jnp.inf)
            m = jnp.max(s, axis=-1, keepdims=True)
            p = jnp.exp(s - m)
            l = jnp.sum(p, axis=-1, keepdims=True)
            o = jnp.dot(p.astype(BF16), vv[:, hs].astype(BF16), preferred_element_type=F32) / l
            outs.append(o)
            lses.append(jnp.broadcast_to(m + jnp.log(l), (BLK_A, DH_A)))
        o_ref[0, rows, :] = jnp.concatenate(outs, axis=-1)
        l_ref[0, rows, :] = jnp.concatenate(lses, axis=-1)
        return carry

    lax.fori_loop(0, nblk * dil, one, 0, unroll=2)


def dil_prompt(z3, gi, window, dil):
    b, s, _ = z3.shape
    nback = window // dil
    span = dil * BLK_A
    nblk = max(1, DIL_STEP_ROWS // span)
    rows = nblk * span
    assert s % rows == 0 and (nblk * dil) % 2 == 0
    nh = AW // LANE
    qo, ko, vo = (AB_QA + gi * AW) // LANE, (AB_KA + gi * AW) // LANE, (AB_VA + gi * AW) // LANE
    blk = (1, rows, LANE)
    pblk = (1, span, LANE)
    in_specs = [
        pl.BlockSpec(blk, lambda bb, hp, i: (bb, i, qo + hp)),
        pl.BlockSpec(pblk, lambda bb, hp, i: (bb, jnp.maximum(i * nblk - 1, 0), ko + hp)),
        pl.BlockSpec(blk, lambda bb, hp, i: (bb, i, ko + hp)),
        pl.BlockSpec(pblk, lambda bb, hp, i: (bb, jnp.maximum(i * nblk - 1, 0), vo + hp)),
        pl.BlockSpec(blk, lambda bb, hp, i: (bb, i, vo + hp)),
    ]
    out_spec = pl.BlockSpec(blk, lambda bb, hp, i: (bb, i, hp))
    o, l = pl.pallas_call(
        functools.partial(_dil_prompt_kernel, nback=nback, dil=dil, nblk=nblk),
        grid=(b, nh, s // rows),
        in_specs=in_specs,
        out_specs=[out_spec, out_spec],
        out_shape=[jax.ShapeDtypeStruct((b, s, AW), F32)] * 2,
        compiler_params=_cparams(("parallel", "parallel", "arbitrary")),
        name="dil_prompt_%d" % gi,
    )(z3, z3, z3, z3, z3)
    return o.reshape(b * s, AW), l.reshape(b * s, AW)


def _dil_sample_kernel(c_ref, q_ref, kn_ref, vn_ref, nt_ref, o_ref, l_ref, co_ref, *, window, dil, n_buf, t_new):
    rows = H_A * T_PAD
    q = q_ref[0] * (DH_A ** -0.5)
    q4 = jnp.concatenate([q] * H_A, axis=0)
    rr = lax.broadcasted_iota(jnp.int32, (rows, AW), 0)
    ll = lax.broadcasted_iota(jnp.int32, (rows, AW), 1)
    head_sel = (rr // T_PAD) == (ll // DH_A)
    qbd = jnp.where(head_sel, q4, 0.0).astype(BF16)
    kbt = c_ref[0, 0:AW, :].astype(BF16)
    vbt = c_ref[0, AW:2 * AW, :].astype(BF16)
    kn = kn_ref[0]
    vn = vn_ref[0]
    s_buf = jnp.dot(qbd, kbt, preferred_element_type=F32)
    s_new = _dot_nt(qbd, kn.astype(BF16))
    t_b = lax.broadcasted_iota(jnp.int32, (rows, n_buf), 0) % T_PAD
    c_b = lax.broadcasted_iota(jnp.int32, (rows, n_buf), 1)
    d_b = n_buf + t_b - c_b
    m_b = ((d_b % dil) == 0) & (d_b <= window)
    t_n = lax.broadcasted_iota(jnp.int32, (rows, T_PAD), 0) % T_PAD
    u_n = lax.broadcasted_iota(jnp.int32, (rows, T_PAD), 1)
    d_n = t_n - u_n
    m_n = (d_n >= 0) & ((d_n % dil) == 0) & (d_n <= window) & (u_n < t_new)
    s_buf = jnp.where(m_b, s_buf, NEG_BIG)
    s_new = jnp.where(m_n, s_new, NEG_BIG)
    mx = jnp.maximum(jnp.max(s_buf, axis=-1, keepdims=True), jnp.max(s_new, axis=-1, keepdims=True))
    p_b = jnp.where(m_b, jnp.exp(s_buf - mx), 0.0)
    p_n = jnp.where(m_n, jnp.exp(s_new - mx), 0.0)
    l = jnp.sum(p_b, axis=-1, keepdims=True) + jnp.sum(p_n, axis=-1, keepdims=True)
    l = jnp.maximum(l, 1e-30)
    acc = (_dot_nt(p_b.astype(BF16), vbt)
           + jnp.dot(p_n.astype(BF16), vn.astype(BF16), preferred_element_type=F32))
    res = jnp.where(head_sel, acc / l, 0.0)
    lse = jnp.where(head_sel, mx + jnp.log(l), 0.0)
    o = res[0:T_PAD]
    ls = lse[0:T_PAD]
    for h in range(1, H_A):
        o = o + res[h * T_PAD:(h + 1) * T_PAD]
        ls = ls + lse[h * T_PAD:(h + 1) * T_PAD]
    o_ref[0] = o
    l_ref[0] = ls
    co_ref[0, :, 0:n_buf - t_new] = c_ref[0, :, t_new:n_buf]
    co_ref[0, :, n_buf - t_new:n_buf] = nt_ref[0, :, 0:t_new]


def dil_sample(cache_t, z3, new_t, gi, window, dil, t_new):
    db, _, n_buf = cache_t.shape
    qo, ko, vo = AB_QA // AW + gi, AB_KA // AW + gi, AB_VA // AW + gi
    blk = (1, T_PAD, AW)
    o, l, co = pl.pallas_call(
        functools.partial(_dil_sample_kernel, window=window, dil=dil, n_buf=n_buf, t_new=t_new),
        grid=(db,),
        in_specs=[
            pl.BlockSpec((1, 2 * AW, n_buf), lambda b: (b, 0, 0)),
            pl.BlockSpec(blk, lambda b: (b, 0, qo)),
            pl.BlockSpec(blk, lambda b: (b, 0, ko)),
            pl.BlockSpec(blk, lambda b: (b, 0, vo)),
            pl.BlockSpec((1, 2 * AW, T_PAD), lambda b: (b, 0, 0)),
        ],
        out_specs=[
            pl.BlockSpec(blk, lambda b: (b, 0, 0)),
            pl.BlockSpec(blk, lambda b: (b, 0, 0)),
            pl.BlockSpec((1, 2 * AW, n_buf), lambda b: (b, 0, 0)),
        ],
        out_shape=[
            jax.ShapeDtypeStruct((db, T_PAD, AW), F32),
            jax.ShapeDtypeStruct((db, T_PAD, AW), F32),
            jax.ShapeDtypeStruct((db, 2 * AW, n_buf), F32),
        ],
        compiler_params=_cparams(("parallel",)),
        name="dil_sample_%d" % gi,
    )(cache_t, z3, z3, z3, new_t)
    return o.reshape(db * T_PAD, AW), l.reshape(db * T_PAD, AW), co


def _mlstm_kernel(q_ref, k_ref, v_ref, gi_ref, gf_ref, gt_ref, bi_ref, bf_ref, b8_ref,
                  c0_ref, n0_ref, m0_ref, h_ref, co_ref, no_ref, mo_ref,
                  c_s, n_s, m_s, *, bb, L, n_valid):
    c = pl.program_id(1)

    @pl.when(c == 0)
    def _():
        c_s[...] = c0_ref[...]
        n_s[...] = n0_ref[...]
        m_s[...] = m0_ref[...]

    row = lax.broadcasted_iota(jnp.int32, (L, L), 0)
    col = lax.broadcasted_iota(jnp.int32, (L, L), 1)
    tri = row >= col
    tri_l = jnp.where(tri, 1.0, 0.0).astype(BF16)
    tri_u = jnp.where(row <= col, 1.0, 0.0).astype(BF16)
    lane = lax.broadcasted_iota(jnp.int32, (1, LANE), 1)
    for b in range(bb):
        ig_col = gi_ref[b] + bi_ref[...]
        lf_col = _log_sigmoid(gf_ref[b] + bf_ref[...])
        gt = gt_ref[b, 0]
        ig_row = gt[0:H_B] + b8_ref[0:H_B]
        lf_row = _log_sigmoid(gt[H_B:2 * H_B] + b8_ref[H_B:2 * H_B])
        if n_valid < L:
            rv = lax.broadcasted_iota(jnp.int32, (L, LANE), 0) < n_valid
            ig_col = jnp.where(rv, ig_col, -jnp.inf)
            lf_col = jnp.where(rv, lf_col, 0.0)
            cv = lax.broadcasted_iota(jnp.int32, (H_B, L), 1) < n_valid
            ig_row = jnp.where(cv, ig_row, -jnp.inf)
            lf_row = jnp.where(cv, lf_row, 0.0)
        b_col = _dot_exact_lhs01(tri_l, lf_col)
        b_row = _dot_exact_rhs01(lf_row, tri_u)
        m_row = m_s[b]
        a_col = b_col + m_row
        m_new = m_row
        hs_out = []
        for h in range(H_B):
            hsl = slice(h * DH_B, (h + 1) * DH_B)
            bc = b_col[:, h:h + 1]
            ac = a_col[:, h:h + 1]
            icol = ig_col[:, h:h + 1]
            D = bc - b_row[h:h + 1, :] + ig_row[h:h + 1, :]
            D = jnp.where(tri, D, -jnp.inf)
            mt = jnp.maximum(ac, jnp.max(D, axis=-1, keepdims=True))
            Dw = jnp.exp(D - mt)
            iw = jnp.exp(ac - mt)
            qf = q_ref[b, :, hsl]
            kf = k_ref[b, :, hsl] * (DH_B ** -0.5)
            vf = v_ref[b, :, hsl]
            qb, kb, vb = qf.astype(BF16), kf.astype(BF16), vf.astype(BF16)
            Cm = c_s[b * H_B + h]
            nv = n_s[b * H_B + h]
            sc = _dot_nt(qb, kb) * Dw
            num = iw * _dot_nt(qb, Cm.astype(BF16)) + jnp.dot(sc.astype(BF16), vb, preferred_element_type=F32)
            den = iw * jnp.sum(qf * nv, axis=-1, keepdims=True) + jnp.sum(sc, axis=-1, keepdims=True)
            hs_out.append(num / jnp.maximum(jnp.abs(den), jnp.exp(-mt)))
            mL = mt[L - 1:L, :]
            wL = jnp.exp(bc[L - 1:L, :] - bc + icol - mL)
            dec = jnp.exp(ac[L - 1:L, :] - mL)
            c_s[b * H_B + h] = dec * Cm + _dot_tn((vf * wL).astype(BF16), kb)
            n_s[b * H_B + h] = dec * nv + jnp.sum(wL * kf, axis=0, keepdims=True)
            m_new = jnp.where(lane == h, mL, m_new)
        m_s[b] = m_new
        h_ref[b] = jnp.concatenate(hs_out, axis=-1)

    @pl.when(c == pl.num_programs(1) - 1)
    def _():
        co_ref[...] = c_s[...]
        no_ref[...] = n_s[...]
        mo_ref[...] = m_s[...]


def mlstm(z3, gt, b_if, c0, n0, m0, *, L, n_valid, bb):
    b, s, _ = z3.shape
    nc = s // L
    bi_row = jnp.zeros((1, LANE), F32).at[0, :H_B].set(b_if[0])
    bf_row = jnp.zeros((1, LANE), F32).at[0, :H_B].set(b_if[1])
    b8 = b_if.reshape(2 * H_B, 1)
    c0r = c0.reshape(b * H_B, DH_B, DH_B)
    n0r = n0.reshape(b * H_B, 1, DH_B)
    m0r = jnp.zeros((b, 1, LANE), F32).at[:, 0, :H_B].set(m0)
    cw = B_W // LANE
    h, co, no, mo = pl.pallas_call(
        functools.partial(_mlstm_kernel, bb=bb, L=L, n_valid=n_valid),
        grid=(b // bb, nc),
        in_specs=[
            pl.BlockSpec((bb, L, B_W), lambda g, c: (g, c, AB_QB // B_W)),
            pl.BlockSpec((bb, L, B_W), lambda g, c: (g, c, AB_KB // B_W)),
            pl.BlockSpec((bb, L, B_W), lambda g, c: (g, c, AB_VB // B_W)),
            pl.BlockSpec((bb, L, LANE), lambda g, c: (g, c, AB_IG // LANE)),
            pl.BlockSpec((bb, L, LANE), lambda g, c: (g, c, AB_FG // LANE)),
            pl.BlockSpec((bb, 1, 2 * H_B, L), lambda g, c: (g, c, 0, 0)),
            pl.BlockSpec((1, LANE), lambda g, c: (0, 0)),
            pl.BlockSpec((1, LANE), lambda g, c: (0, 0)),
            pl.BlockSpec((2 * H_B, 1), lambda g, c: (0, 0)),
            pl.BlockSpec((bb * H_B, DH_B, DH_B), lambda g, c: (g, 0, 0)),
            pl.BlockSpec((bb * H_B, 1, DH_B), lambda g, c: (g, 0, 0)),
            pl.BlockSpec((bb, 1, LANE), lambda g, c: (g, 0, 0)),
        ],
        out_specs=[
            pl.BlockSpec((bb, L, B_W), lambda g, c: (g, c, 0)),
            pl.BlockSpec((bb * H_B, DH_B, DH_B), lambda g, c: (g, 0, 0)),
            pl.BlockSpec((bb * H_B, 1, DH_B), lambda g, c: (g, 0, 0)),
            pl.BlockSpec((bb, 1, LANE), lambda g, c: (g, 0, 0)),
        ],
        out_shape=[
            jax.ShapeDtypeStruct((b, s, B_W), F32),
            jax.ShapeDtypeStruct((b * H_B, DH_B, DH_B), F32),
            jax.ShapeDtypeStruct((b * H_B, 1, DH_B), F32),
            jax.ShapeDtypeStruct((b, 1, LANE), F32),
        ],
        scratch_shapes=[
            pltpu.VMEM((bb * H_B, DH_B, DH_B), F32),
            pltpu.VMEM((bb * H_B, 1, DH_B), F32),
            pltpu.VMEM((bb, 1, LANE), F32),
        ],
        compiler_params=_cparams(("parallel", "arbitrary")),
        name="mlstm_L%d" % L,
    )(z3, z3, z3, z3, z3, gt, bi_row, bf_row, b8, c0r, n0r, m0r)
    del cw
    return (h.reshape(b * s, B_W), co.reshape(b, H_B, DH_B, DH_B), no.reshape(b, H_B, DH_B),
            mo[:, 0, :H_B])


def _ab_merge_kernel(o0, o1, o2, l0, l1, l2, hb_ref, og_ref, g_ref, w_ref, r_ref, out_ref):
    a0, a1, a2 = l0[...], l1[...], l2[...]
    mx = jnp.maximum(jnp.maximum(a0, a1), a2)
    e0, e1, e2 = jnp.exp(a0 - mx), jnp.exp(a1 - mx), jnp.exp(a2 - mx)
    o_a = (e0 * o0[...] + e1 * o1[...] + e2 * o2[...]) / (e0 + e1 + e2)
    hb = hb_ref[...]
    parts = []
    for h in range(H_B):
        hs = slice(h * DH_B, (h + 1) * DH_B)
        x = hb[:, hs]
        parts.append(x * lax.rsqrt(jnp.mean(x * x, axis=-1, keepdims=True) + EPS))
    hbn = jnp.concatenate(parts, axis=-1) * g_ref[...] * jax.nn.sigmoid(og_ref[...])
    y = (jnp.dot(o_a.astype(BF16), w_ref[0:AW, :], preferred_element_type=F32)
         + jnp.dot(hbn.astype(BF16), w_ref[AW:AW + B_W, :], preferred_element_type=F32))
    out_ref[...] = r_ref[...] + y


def ab_merge(os_, ls_, hb, z2, g_mn, w_bf16, resid, *, tm_pref=512):
    m, d = resid.shape
    tm = _pick_tile(m, tm_pref)
    a_spec = pl.BlockSpec((tm, AW), lambda i: (i, 0))
    return pl.pallas_call(
        _ab_merge_kernel,
        grid=(m // tm,),
        in_specs=[a_spec] * 6 + [
            pl.BlockSpec((tm, B_W), lambda i: (i, 0)),
            pl.BlockSpec((tm, B_W), lambda i: (i, AB_OG // B_W)),
            pl.BlockSpec((1, B_W), lambda i: (0, 0)),
            pl.BlockSpec((AW + B_W, d), lambda i: (0, 0)),
            pl.BlockSpec((tm, d), lambda i: (i, 0)),
        ],
        out_specs=pl.BlockSpec((tm, d), lambda i: (i, 0)),
        out_shape=jax.ShapeDtypeStruct((m, d), F32),
        compiler_params=_cparams(("parallel",)),
        name="ab_merge",
    )(*os_, *ls_, hb, z2, g_mn.reshape(1, B_W), w_bf16, resid)


HKV = KVH_C * DH_C


def _compress_bias_kernel(pa_ref, pb_ref, w_ref, o_ref):
    w = w_ref[0]
    o_ref[0] = (jnp.dot(pa_ref[0].astype(BF16), w, preferred_element_type=F32)[:, 0:HKV]
                + jnp.dot(pb_ref[0].astype(BF16), w, preferred_element_type=F32)[:, HKV:2 * HKV])


def compress_bias(pa, pb, wab2):
    spec = lambda a: pl.BlockSpec((1,) + a.shape[1:], lambda i: (i,) + (0,) * (a.ndim - 1))
    return pl.pallas_call(
        _compress_bias_kernel,
        grid=(2,),
        in_specs=[spec(pa), spec(pb), spec(wab2)],
        out_specs=pl.BlockSpec((1, 1, HKV), lambda i: (i, 0, 0)),
        out_shape=jax.ShapeDtypeStruct((2, 1, HKV), F32),
        compiler_params=_cparams(("arbitrary",)),
        name="compress_bias",
    )(pa, pb, wab2)


def _compress_halves(load_rows, n, w_of_r):
    acc = jnp.zeros((n, 2 * HKV), F32)
    for r in range(CMP_STRIDE):
        acc = acc + jnp.dot(load_rows(r, n).astype(BF16), w_of_r(r), preferred_element_type=F32)
    return acc


def _compress_prompt_kernel(x_ref, wab_ref, bias_ref, w2_ref, o_ref):
    n = x_ref.shape[1] // CMP_STRIDE
    acc = _compress_halves(lambda r, m: x_ref[0, pl.ds(r, m, stride=CMP_STRIDE), :], n, lambda r: wab_ref[0, r])
    hid = acc[:, 0:HKV] + pltpu.roll(acc[:, HKV:2 * HKV], n - 1, 0) + bias_ref[0]
    o_ref[0] = jnp.dot(_gelu_tanh(hid).astype(BF16), w2_ref[0], preferred_element_type=F32)


def compress_prompt(z3, cw, bias):
    b, s, _ = z3.shape
    wab, w2 = cw
    n = s // CMP_STRIDE
    return pl.pallas_call(
        _compress_prompt_kernel,
        grid=(b, 2),
        in_specs=[
            pl.BlockSpec((1, s, HKV), lambda i, kv: (i, 0, C_CMP // HKV + kv)),
            pl.BlockSpec((1,) + wab.shape[1:], lambda i, kv: (kv, 0, 0, 0)),
            pl.BlockSpec((1, 1, HKV), lambda i, kv: (kv, 0, 0)),
            pl.BlockSpec((1, HKV, HKV), lambda i, kv: (kv, 0, 0)),
        ],
        out_specs=pl.BlockSpec((1, n, HKV), lambda i, kv: (i, 0, kv)),
        out_shape=jax.ShapeDtypeStruct((b, n, KVW), F32),
        compiler_params=_cparams(("parallel", "arbitrary")),
        name="compress_prompt",
    )(z3, wab, bias, w2)


def _compress_paged_kernel(pt_ref, *refs, pg):
    del pt_ref
    page_refs = refs[:pg + 1]
    wab_ref, bias_ref, w2_ref, o_ref, xs_ref = refs[pg + 1:]
    for j in range(pg + 1):
        t = page_refs[j][0].T
        xs_ref[0, j * PAGE_SIZE:(j + 1) * PAGE_SIZE, :] = t[:, 0:HKV]
        xs_ref[1, j * PAGE_SIZE:(j + 1) * PAGE_SIZE, :] = t[:, HKV:2 * HKV]
    cpp = PAGE_SIZE // CMP_STRIDE
    n = pg * cpp
    for kv in range(2):
        acc = _compress_halves(lambda r, m: xs_ref[kv, pl.ds(r, m, stride=CMP_STRIDE), :], n + cpp,
                               lambda r: wab_ref[kv, r])
        hid = acc[0:n, 0:HKV] + acc[1:n + 1, HKV:2 * HKV] + bias_ref[kv]
        o_ref[0, :, kv * HKV:(kv + 1) * HKV] = jnp.dot(_gelu_tanh(hid).astype(BF16), w2_ref[kv],
                                                        preferred_element_type=F32)


def compress_paged(pool_t, page_table, cw, bias, *, pg=32):
    db, n_pages = page_table.shape
    cpp = PAGE_SIZE // CMP_STRIDE
    pg = min(pg, n_pages)
    assert n_pages % pg == 0
    wab, w2 = cw
    full = lambda a: pl.BlockSpec(a.shape, lambda b, g, pt: (0,) * a.ndim)

    def page_spec(j):
        return pl.BlockSpec((1, KVW, PAGE_SIZE),
                            lambda b, g, pt: (pt[b, jnp.minimum(g * pg + j, n_pages - 1)], 0, 0))

    gs = pltpu.PrefetchScalarGridSpec(
        num_scalar_prefetch=1,
        grid=(db, n_pages // pg),
        in_specs=[page_spec(j) for j in range(pg + 1)] + [full(wab), full(bias), full(w2)],
        out_specs=pl.BlockSpec((1, pg * cpp, KVW), lambda b, g, pt: (b, g, 0)),
        scratch_shapes=[pltpu.VMEM((2, (pg + 1) * PAGE_SIZE, HKV), F32)],
    )
    return pl.pallas_call(
        functools.partial(_compress_paged_kernel, pg=pg),
        grid_spec=gs,
        out_shape=jax.ShapeDtypeStruct((db, n_pages * cpp, KVW), F32),
        compiler_params=_cparams(("parallel", "arbitrary")),
        name="compress_paged",
    )(page_table, *([pool_t] * (pg + 1)), wab, bias, w2)


def _stack_heads(x, t):
    del t
    return jnp.concatenate([x[:, g * DH_C:(g + 1) * DH_C] for g in range(G_C)], axis=0)


def _mask_rows(s, mask, fill):
    t, n = mask.shape
    return jnp.where(mask[None], s.reshape(G_C, t, n), fill).reshape(G_C * t, n)


def _cmp_branch(q2, kc, vc, t_pos, tq):
    n = kc.shape[0]
    s = _dot_nt(q2, kc)
    cend = lax.broadcasted_iota(jnp.int32, (tq, n), 1) * CMP_STRIDE + (CMP_BLK - 1)
    s = _mask_rows(s, cend <= t_pos, -jnp.inf)
    m = jnp.max(s, axis=-1, keepdims=True)
    m = jnp.where(m > -jnp.inf, m, 0.0)
    p = jnp.exp(s - m)
    p = p / jnp.maximum(jnp.sum(p, axis=-1, keepdims=True), 1e-30)
    o = jnp.dot(p.astype(BF16), vc, preferred_element_type=F32)
    pg = p[0:tq]
    for g in range(1, G_C):
        pg = pg + p[g * tq:(g + 1) * tq]
    return o, pg


def _select_blocks(imp, t_pos):
    tq, nsp = imp.shape
    j = lax.broadcasted_iota(jnp.int32, (tq, nsp), 1)
    jf = j.astype(F32)
    cur = t_pos // SLC_BLK
    forced = (j == 0) | (j == cur) | (j == cur - 1)
    work = jnp.where(forced, jnp.inf, jnp.where(j <= cur, imp, -jnp.inf))
    sel = jnp.zeros((tq, nsp), F32)
    for _ in range(N_SEL):
        mx = jnp.max(work, axis=-1, keepdims=True)
        first = jnp.min(jnp.where(work == mx, jf, float(nsp)), axis=-1, keepdims=True)
        pick = jf == first
        sel = jnp.where(pick, jnp.where(mx > -jnp.inf, 1.0, sel), sel)
        work = jnp.where(pick, -jnp.inf, work)
    return sel


def _flash_update(s, mask, v, m_ref, l_ref, acc_ref, v_channel_major=False):
    s = _mask_rows(s, mask, NEG_BIG)
    m_old = m_ref[...]
    m_new = jnp.maximum(m_old, jnp.max(s, axis=-1, keepdims=True))
    alpha = jnp.exp(m_old - m_new)
    p = jnp.exp(s - m_new)
    l_ref[...] = alpha * l_ref[...] + jnp.sum(p, axis=-1, keepdims=True)
    pv = _dot_nt(p.astype(BF16), v) if v_channel_major else jnp.dot(p.astype(BF16), v, preferred_element_type=F32)
    acc_ref[...] = alpha * acc_ref[...] + pv
    m_ref[...] = m_new


def _flash_finish(m_ref, l_ref, acc_ref):
    return jnp.where(m_ref[...] > 0.5 * NEG_BIG, acc_ref[...] / jnp.maximum(l_ref[...], 1e-30), 0.0)


def _softmax_av(s, mask, v):
    s = _mask_rows(s, mask, -jnp.inf)
    m = jnp.max(s, axis=-1, keepdims=True)
    m = jnp.where(m > -jnp.inf, m, 0.0)
    p = jnp.exp(s - m)
    p = p / jnp.maximum(jnp.sum(p, axis=-1, keepdims=True), 1e-30)
    return jnp.dot(p.astype(BF16), v, preferred_element_type=F32)


def _gate_mix(gates, h, o_cmp, o_sel, o_win, tq):
    outs = []
    for g in range(G_C):
        base = (h * G_C + g) * 3
        rs = slice(g * tq, (g + 1) * tq)
        outs.append(gates[:, base:base + 1] * o_cmp[rs] + gates[:, base + 1:base + 2] * o_sel[rs]
                    + gates[:, base + 2:base + 3] * o_win[rs])
    return jnp.concatenate(outs, axis=-1)


def _rope_q(qh, cos, sin):
    return jnp.concatenate(
        [_rope_lanes(qh[:, c * LANE:(c + 1) * LANE], cos, sin) for c in range(G_C * DH_C // LANE)], axis=-1)


SEL_TK = 1024


SEL_PHASE = DH_C


def _nsa_prompt_kernel(q_ref, gz_ref, cos_ref, sin_ref, kc_ref, kv_ref, ka_ref, va_ref, mimp_ref, o_ref,
                       qa_s, m_s, acc_s, *, nsp):
    i = pl.program_id(1)
    tq = QBLK_C
    rows = G_C * tq
    s0 = i * tq
    t_pos = s0 + lax.broadcasted_iota(jnp.int32, (tq, 1), 0)
    gates = jax.nn.sigmoid(gz_ref[0])
    cos, sin = cos_ref[...], sin_ref[...]
    scale = DH_C ** -0.5
    hw = G_C * DH_C
    tiles_per_phase = SEL_PHASE * SLC_BLK // SEL_TK
    last = (s0 + tq - 1) // SEL_TK
    q_rot, o_cmps, imps = [], [], []
    for h in range(KVH_C):
        qh = q_ref[0, :, h * hw:(h + 1) * hw] * scale
        q_rot.append(_rope_q(qh, cos, sin))
        kc = kc_ref[0, :, h * DH_C:(h + 1) * DH_C].astype(BF16)
        vc = kc_ref[0, :, KVH_C * DH_C + h * DH_C:KVH_C * DH_C + (h + 1) * DH_C].astype(BF16)
        o_cmp, pgrp = _cmp_branch(_stack_heads(qh, tq).astype(BF16), kc, vc, t_pos, tq)
        o_cmps.append(o_cmp)
        imps.append(_dot_exact_rhs01(pgrp, mimp_ref[...]))
    sel = _select_blocks(jnp.concatenate(imps, axis=0), jnp.concatenate([t_pos] * KVH_C, axis=0))
    selb_all = jnp.where(sel > 0.5, 0.0, NEG_BIG)

    for h in range(KVH_C):
        selb = selb_all[h * tq:(h + 1) * tq]
        for ph in range(nsp // SEL_PHASE):
            sb = selb[:, ph * SEL_PHASE:(ph + 1) * SEL_PHASE]
            qa_s[h, ph] = jnp.concatenate(
                [jnp.concatenate([x, sb] if h == 0 else [sb, x], axis=1)
                 for x in (q_rot[h][:, g * DH_C:(g + 1) * DH_C] for g in range(G_C))], axis=0).astype(BF16)
    m_s[...] = jnp.full(m_s.shape, NEG_BIG, F32)
    acc_s[...] = jnp.zeros(acc_s.shape, F32)

    def tile(kt, causal):
        k0 = pl.multiple_of(kt * SEL_TK, SEL_TK)
        ph = kt // tiles_per_phase
        for h in range(KVH_C):
            st = _dot_nt(ka_ref[0, h, pl.ds(k0, SEL_TK), :], qa_s[h, ph])
            if causal:
                kpos = k0 + lax.broadcasted_iota(jnp.int32, (SEL_TK, 1), 0)
                t_row = s0 + lax.broadcasted_iota(jnp.int32, (1, rows), 1) % tq
                st = jnp.where(kpos <= t_row, st, NEG_BIG)
            m_old = m_s[h]
            m_new = jnp.maximum(m_old, jnp.max(st, axis=0, keepdims=True))
            pt = jnp.exp(st - m_new).astype(BF16)
            acc_s[h] = (jnp.exp(m_old - m_new) * acc_s[h]
                        + jnp.dot(va_ref[0, h, :, pl.ds(k0, SEL_TK)], pt, preferred_element_type=F32))
            m_s[h] = m_new

    def earlier_tile(kt, carry):
        tile(kt, False)
        return carry

    lax.fori_loop(0, last, earlier_tile, 0)
    tile(last, True)

    def finish(acc, h):
        sums = acc[(1 - h) * DH_C:(1 - h) * DH_C + 1, :]
        return (acc[h * DH_C:(h + 1) * DH_C, :] / jnp.maximum(sums, 1e-30)).T

    nw = WIN_C + tq
    w0 = pl.multiple_of(jnp.maximum(s0 - WIN_C, 0), tq)
    dlt = (s0 + lax.broadcasted_iota(jnp.int32, (1, rows), 1) % tq) - (w0 + lax.broadcasted_iota(jnp.int32, (nw, 1), 0))
    in_win = (dlt >= 0) & (dlt < WIN_C)
    lane = lax.broadcasted_iota(jnp.int32, (1, LANE), 1) // DH_C
    kwc = kv_ref[0, pl.ds(w0, nw), 0:LANE]
    vwc = kv_ref[0, pl.ds(w0, nw), LANE:2 * LANE]
    for h in range(KVH_C):
        st = _dot_nt(jnp.where(lane == h, kwc, jnp.zeros_like(kwc)), qa_s[h, 0])
        st = jnp.where(in_win, st, NEG_BIG)
        pt = jnp.exp(st - jnp.max(st, axis=0, keepdims=True)).astype(BF16)
        o_win = finish(_dot_tn(jnp.where(lane == h, vwc, jnp.ones_like(vwc)), pt), h)
        o_ref[0, :, h * hw:(h + 1) * hw] = _gate_mix(gates, h, o_cmps[h], finish(acc_s[h], h), o_win, tq)


def _sel_operands(z3):
    s = z3.shape[1]
    lane = jnp.arange(LANE)
    hot = ((jnp.arange(s)[:, None] // SLC_BLK) % SEL_PHASE == lane[None, :] % SEL_PHASE).astype(BF16)
    ksel = z3[:, :, C_SEL:C_SEL + HKV].astype(BF16)
    vsel_t = jnp.swapaxes(z3[:, :, C_SEL + HKV:C_SEL + 2 * HKV].astype(BF16), 1, 2)
    ka = jnp.stack([jnp.where((lane // DH_C == h)[None, None, :], ksel, hot[None]) for h in range(KVH_C)], axis=1)
    va = jnp.stack([jnp.where((lane // DH_C == h)[None, :, None], vsel_t, jnp.ones_like(vsel_t))
                    for h in range(KVH_C)], axis=1)
    return ka, va


def nsa_prompt(z3, kc, kvw, ka, va, cos_t, sin_t, mimp):
    b, s, _ = z3.shape
    nsp = mimp.shape[1]
    nq = s // QBLK_C
    ncmp = kc.shape[1]
    rows = G_C * QBLK_C
    vm = pltpu.VMEM
    scratch = [
        vm((KVH_C, nsp // SEL_PHASE, rows, LANE), BF16),
        vm((KVH_C, 1, rows), F32),
        vm((KVH_C, LANE, rows), F32),
    ]
    return pl.pallas_call(
        functools.partial(_nsa_prompt_kernel, nsp=nsp),
        grid=(b, nq),
        in_specs=[
            pl.BlockSpec((1, QBLK_C, C_Q), lambda bb, i: (bb, i, 0)),
            pl.BlockSpec((1, QBLK_C, LANE), lambda bb, i: (bb, i, C_GATE // LANE)),
            pl.BlockSpec((QBLK_C, LANE), lambda bb, i: (i, 0)),
            pl.BlockSpec((QBLK_C, LANE), lambda bb, i: (i, 0)),
            pl.BlockSpec((1, ncmp, KVW), lambda bb, i: (bb, 0, 0)),
            pl.BlockSpec((1, s, KVW), lambda bb, i: (bb, 0, 0)),
            pl.BlockSpec((1, KVH_C, s, LANE), lambda bb, i: (bb, 0, 0, 0)),
            pl.BlockSpec((1, KVH_C, LANE, s), lambda bb, i: (bb, 0, 0, 0)),
            pl.BlockSpec(mimp.shape, lambda bb, i: (0, 0)),
        ],
        out_specs=pl.BlockSpec((1, QBLK_C, C_Q), lambda bb, i: (bb, i, 0)),
        out_shape=jax.ShapeDtypeStruct((b, s, C_Q), F32),
        scratch_shapes=scratch,
        compiler_params=_cparams(("parallel", "arbitrary")),
        name="nsa_prompt",
    )(z3, z3, cos_t, sin_t, kc, kvw, ka, va, mimp)


def _nsa_sample_kernel(pt_ref, *refs, pg, nsp, past, t_new):
    del pt_ref
    page_refs = refs[:pg]
    (q_ref, gz_ref, cos_ref, sin_ref, kc_ref, mimp_ref, hot_ref, seln_ref, winb_ref, winn_ref, winnt_ref,
     o_ref, wo_ref, q2r_s, qa_s, sel_s, ocmp_s, m_s, l_s, acc_s) = refs[pg:]
    g = pl.program_id(1)
    ng = pl.num_programs(1)
    tq = T_PAD
    rows = G_C * tq
    t_pos = past + lax.broadcasted_iota(jnp.int32, (tq, 1), 0)
    scale = DH_C ** -0.5
    hw = G_C * DH_C
    nk = pg * PAGE_SIZE
    gblk = nk // SLC_BLK

    @pl.when(g == 0)
    def _():
        cos, sin = cos_ref[...], sin_ref[...]
        q2rs, imps = [], []
        for h in range(KVH_C):
            qh = q_ref[0, :, h * hw:(h + 1) * hw] * scale
            q2 = _stack_heads(qh, tq).astype(BF16)
            q2rs.append(_stack_heads(_rope_q(qh, cos, sin), tq))
            q2r_s[h] = q2rs[h].astype(BF16)
            kc = kc_ref[0, :, h * DH_C:(h + 1) * DH_C].astype(BF16)
            vc = kc_ref[0, :, KVH_C * DH_C + h * DH_C:KVH_C * DH_C + (h + 1) * DH_C].astype(BF16)
            o_cmp, pgrp = _cmp_branch(q2, kc, vc, t_pos, tq)
            ocmp_s[h] = o_cmp
            imps.append(_dot_exact_rhs01(pgrp, mimp_ref[...]))
        sel_all = _select_blocks(jnp.concatenate(imps, axis=0), jnp.concatenate([t_pos] * KVH_C, axis=0))
        zpad = jnp.zeros((rows, DH_C - gblk), F32)
        for h in range(KVH_C):
            sel = sel_all[h * tq:(h + 1) * tq]
            sel_s[h] = sel
            selb = jnp.where(sel > 0.5, 0.0, NEG_BIG)
            for gg in range(qa_s.shape[1]):
                sb = jnp.concatenate([selb[:, gg * gblk:(gg + 1) * gblk]] * G_C, axis=0)
                qa_s[h, gg] = jnp.concatenate([q2rs[h], sb, zpad], axis=1).astype(BF16)
        m_s[...] = jnp.full(m_s.shape, NEG_BIG, F32)
        l_s[...] = jnp.zeros(l_s.shape, F32)
        acc_s[...] = jnp.zeros(acc_s.shape, F32)

    kvt = jnp.concatenate([r[0] for r in page_refs], axis=1).astype(BF16)
    hot = hot_ref[...]
    for h in range(KVH_C):
        kst = jnp.concatenate([kvt[h * DH_C:(h + 1) * DH_C, :], hot], axis=0)
        vst = kvt[KVH_C * DH_C + h * DH_C:KVH_C * DH_C + (h + 1) * DH_C, :]
        s = jnp.dot(qa_s[h, g], kst, preferred_element_type=F32)
        m_old = m_s[h]
        m_new = jnp.maximum(m_old, jnp.max(s, axis=-1, keepdims=True))
        alpha = jnp.exp(m_old - m_new)
        p = jnp.exp(s - m_new)
        l_s[h] = alpha * l_s[h] + jnp.sum(p, axis=-1, keepdims=True)
        acc_s[h] = alpha * acc_s[h] + _dot_nt(p.astype(BF16), vst)
        m_s[h] = m_new

    @pl.when(g == ng - 1)
    def _():
        gates = jax.nn.sigmoid(gz_ref[0])
        n_buf = winb_ref.shape[2]
        lane_n = lax.broadcasted_iota(jnp.int32, (tq, nsp), 1)
        u = lax.broadcasted_iota(jnp.int32, (tq, tq), 1)
        seln = seln_ref[0]
        winn = winn_ref[0]
        for h in range(KVH_C):
            kcol = h * DH_C
            vcol = KVH_C * DH_C + h * DH_C
            q2r = q2r_s[h]
            new_sel = jnp.max(jnp.where(lane_n == past // SLC_BLK, sel_s[h], 0.0), axis=-1, keepdims=True) > 0.5
            mk = new_sel & (past + u <= t_pos) & (u < t_new)
            _flash_update(_dot_nt(q2r, seln[:, kcol:kcol + DH_C].astype(BF16)), mk,
                          seln[:, vcol:vcol + DH_C].astype(BF16), m_s.at[h], l_s.at[h], acc_s.at[h])
            o_sel = _flash_finish(m_s.at[h], l_s.at[h], acc_s.at[h])
            kbt = winb_ref[0, kcol:kcol + DH_C, :].astype(BF16)
            vbt = winb_ref[0, vcol:vcol + DH_C, :].astype(BF16)
            pos_b = past - n_buf + lax.broadcasted_iota(jnp.int32, (tq, n_buf), 1)
            d_b = t_pos - pos_b
            m_b = (pos_b >= 0) & (d_b >= 0) & (d_b < WIN_C)
            d_n = t_pos - (past + u)
            m_n = (d_n >= 0) & (d_n < WIN_C) & (u < t_new)
            s_b = _mask_rows(jnp.dot(q2r, kbt, preferred_element_type=F32), m_b, -jnp.inf)
            s_n = _mask_rows(_dot_nt(q2r, winn[:, kcol:kcol + DH_C].astype(BF16)), m_n, -jnp.inf)
            mx = jnp.maximum(jnp.max(s_b, axis=-1, keepdims=True), jnp.max(s_n, axis=-1, keepdims=True))
            mx = jnp.where(mx > -jnp.inf, mx, 0.0)
            p_b, p_n = jnp.exp(s_b - mx), jnp.exp(s_n - mx)
            den = jnp.maximum(jnp.sum(p_b, axis=-1, keepdims=True) + jnp.sum(p_n, axis=-1, keepdims=True), 1e-30)
            o_win = (_dot_nt((p_b / den).astype(BF16), vbt)
                     + jnp.dot((p_n / den).astype(BF16), winn[:, vcol:vcol + DH_C].astype(BF16),
                               preferred_element_type=F32))
            o_ref[0, :, h * hw:(h + 1) * hw] = _gate_mix(gates, h, ocmp_s[h], o_sel, o_win, tq)
        wo_ref[0, :, 0:n_buf - t_new] = winb_ref[0, :, t_new:n_buf]
        wo_ref[0, :, n_buf - t_new:n_buf] = winnt_ref[0, :, 0:t_new]


def nsa_sample(z3, kc, sel_pool_t, win_buf_t, win_new_t, page_table, cos_t, sin_t, mimp, *, t_new, pg=16):
    db, n_pages = page_table.shape
    past = n_pages * PAGE_SIZE
    ncmp = kc.shape[1]
    nsp = mimp.shape[1]
    n_buf = win_buf_t.shape[2]
    pg = min(pg, n_pages)
    nk = pg * PAGE_SIZE
    assert n_pages % pg == 0 and nk // SLC_BLK <= DH_C
    rows = G_C * T_PAD
    hot = jnp.asarray(np.arange(nk)[None, :] // SLC_BLK == np.arange(DH_C)[:, None], dtype=BF16)

    def page_spec(j):
        return pl.BlockSpec((1, KVW, PAGE_SIZE), lambda b, g, pt: (pt[b, g * pg + j], 0, 0))

    gs = pltpu.PrefetchScalarGridSpec(
        num_scalar_prefetch=1,
        grid=(db, n_pages // pg),
        in_specs=[page_spec(j) for j in range(pg)] + [
            pl.BlockSpec((1, T_PAD, C_Q), lambda b, g, pt: (b, 0, 0)),
            pl.BlockSpec((1, T_PAD, LANE), lambda b, g, pt: (b, 0, C_GATE // LANE)),
            pl.BlockSpec((T_PAD, LANE), lambda b, g, pt: (0, 0)),
            pl.BlockSpec((T_PAD, LANE), lambda b, g, pt: (0, 0)),
            pl.BlockSpec((1, ncmp, KVW), lambda b, g, pt: (b, 0, 0)),
            pl.BlockSpec(mimp.shape, lambda b, g, pt: (0, 0)),
            pl.BlockSpec((DH_C, nk), lambda b, g, pt: (0, 0)),
            pl.BlockSpec((1, T_PAD, KVW), lambda b, g, pt: (b, 0, C_SEL // KVW)),
            pl.BlockSpec((1, KVW, n_buf), lambda b, g, pt: (b, 0, 0)),
            pl.BlockSpec((1, T_PAD, KVW), lambda b, g, pt: (b, 0, C_WIN // KVW)),
            pl.BlockSpec((1, KVW, T_PAD), lambda b, g, pt: (b, 0, 0)),
        ],
        out_specs=[
            pl.BlockSpec((1, T_PAD, C_Q), lambda b, g, pt: (b, 0, 0)),
            pl.BlockSpec((1, KVW, n_buf), lambda b, g, pt: (b, 0, 0)),
        ],
        scratch_shapes=[
            pltpu.VMEM((KVH_C, rows, DH_C), BF16),
            pltpu.VMEM((KVH_C, n_pages // pg, rows, LANE), BF16),
            pltpu.VMEM((KVH_C, T_PAD, nsp), F32),
            pltpu.VMEM((KVH_C, rows, DH_C), F32),
            pltpu.VMEM((KVH_C, rows, 1), F32),
            pltpu.VMEM((KVH_C, rows, 1), F32),
            pltpu.VMEM((KVH_C, rows, DH_C), F32),
        ],
    )
    return pl.pallas_call(
        functools.partial(_nsa_sample_kernel, pg=pg, nsp=nsp, past=past, t_new=t_new),
        grid_spec=gs,
        out_shape=[jax.ShapeDtypeStruct((db, T_PAD, C_Q), F32), jax.ShapeDtypeStruct((db, KVW, n_buf), F32)],
        compiler_params=_cparams(("parallel", "arbitrary")),
        name="nsa_sample",
    )(page_table, *([sel_pool_t] * pg), z3, z3, cos_t, sin_t, kc, mimp, hot, z3, win_buf_t, z3, win_new_t)


def _rope_tables(pos):
    half = DH_C // 2
    inv = 1.0 / (ROPE_THETA ** (jnp.arange(half, dtype=F32) / half))
    ang = pos.astype(F32)[:, None] * inv[None, :]
    cos, sin = jnp.cos(ang), jnp.sin(ang)
    return jnp.tile(cos, (1, 4)), jnp.concatenate([-sin, sin, -sin, sin], axis=-1)


def _prep_w_in_ab(w):
    cuts = np.cumsum([A_QW, A_QW, A_QW, B_W, B_W, B_W, H_B, H_B])
    qa, ka, va, qb, kb, vb, ig, fg, og = jnp.split(w, cuts, axis=-1)
    padg = lambda t: jnp.pad(t, ((0, 0), (0, LANE - H_B)))
    return jnp.concatenate([qb, kb, vb, og, qa, ka, va, padg(ig), padg(fg)], axis=-1).astype(BF16)


def _ab_rope_flags():
    f = np.zeros((AB_N // LANE,), np.int32)
    f[AB_QA // LANE:AB_VA // LANE] = 1
    return jnp.asarray(f)


def _c_rope_flags():
    f = np.zeros((C_N // LANE,), np.int32)
    f[C_SEL // LANE] = 1
    f[C_WIN // LANE] = 1
    return jnp.asarray(f)


def _prep_compress(w1, w2, pe):
    e2 = jnp.eye(KVH_C, dtype=F32)

    def half(w1h):
        return jnp.einsum('krde,hH->krhdHe', w1h, e2).reshape(2, CMP_STRIDE, HKV, HKV).astype(BF16)

    wab = jnp.concatenate([half(w1[:, :CMP_STRIDE]), half(w1[:, CMP_STRIDE:])], axis=-1)
    w2b = jnp.einsum('ked,hH->kheHd', w2, e2).reshape(2, HKV, HKV).astype(BF16)

    def pe_half(p):
        return jnp.broadcast_to(p[:, :, None, :], (2, CMP_STRIDE, KVH_C, DH_C)).reshape(2, 1, CMP_STRIDE * HKV)

    bias = compress_bias(pe_half(pe[:, :CMP_STRIDE]), pe_half(pe[:, CMP_STRIDE:]),
                         wab.reshape(2, CMP_STRIDE * HKV, 2 * HKV))
    return (wab, w2b), bias


def _channel_major(x, lead):
    perm = tuple(range(lead)) + (lead + 1, lead + 2, lead + 3, lead)
    xt = jnp.transpose(x, perm)
    return xt.reshape(x.shape[:lead] + (x.shape[lead + 1] * x.shape[lead + 2] * x.shape[lead + 3], x.shape[lead]))


def _row_major(xt, c0, c1, c2):
    lead, _, rows = xt.shape
    return jnp.transpose(xt.reshape(lead, c0, c1, c2, rows), (0, 4, 1, 2, 3))


def _importance_matrix(n_rows, n_cmp, n_slc, n_cols):
    ratio = SLC_BLK // CMP_STRIDE
    m = np.zeros((n_rows, n_cols), np.float32)
    for jblk in range(n_slc):
        for off in range(1 - CMP_BLK // CMP_STRIDE, ratio):
            i = ratio * jblk + off
            if 0 <= i < n_cmp:
                m[i, jblk] = 1.0
    return jnp.asarray(m, dtype=BF16)


def _gates_t(z3, L):
    b, s, _ = z3.shape
    g = jnp.concatenate([z3[..., AB_IG:AB_IG + H_B], z3[..., AB_FG:AB_FG + H_B]], axis=-1)
    return g.reshape(b, s // L, L, 2 * H_B).transpose(0, 1, 3, 2)


def kernel(x_prompt, x_sample, cache_a0_kv, cache_a1_kv, cache_a2_kv, state_b_C, state_b_n, state_b_m,
           cache_c_cmp_kv, cache_c_sel_kv, cache_c_win_kv, page_table, norm_g, w_in_ab, b_if, g_mlstm,
           w_out_ab, w_in_c, cmp_w1, cmp_w2, cmp_pe, w_out_c, w_ffn_gate, w_ffn_up, w_ffn_down, norm_final):
    B, S, D = x_prompt.shape
    DB, T, _ = x_sample.shape
    depth = norm_g.shape[0]
    n_pages = page_table.shape[1]
    past = n_pages * PAGE_SIZE
    caches_a = (cache_a0_kv, cache_a1_kv, cache_a2_kv)
    assert T <= T_PAD and S % (DIL_CFG[-1][1] * BLK_A) == 0 and S >= WIN_C + QBLK_C

    hp = x_prompt.reshape(B * S, D)
    hs = jnp.pad(x_sample, ((0, 0), (0, T_PAD - T), (0, 0))).reshape(DB * T_PAD, D)

    pos_p = jnp.arange(S)
    pos_s = past + jnp.arange(T_PAD)
    cos_p1, sin_p1 = _rope_tables(pos_p)
    cos_s1, sin_s1 = _rope_tables(pos_s)
    cos_p, sin_p = jnp.tile(cos_p1, (B, 1)), jnp.tile(sin_p1, (B, 1))
    cos_s, sin_s = jnp.tile(cos_s1, (DB, 1)), jnp.tile(sin_s1, (DB, 1))

    a_p, a_s = [[], [], []], [[], [], []]
    bC_p, bC_s, bn_p, bn_s, bm_p, bm_s = [], [], [], [], [], []
    cc_p, cc_s, csl_p, csl_s, cw_p, cw_s = [], [], [], [], [], []

    for layer in range(depth):
        if layer % 2 == 0:
            e = layer // 2
            w_in = _prep_w_in_ab(w_in_ab[e])
            flags = _ab_rope_flags()
            w_out = w_out_ab[e].astype(BF16)
            z = norm_proj(hp, norm_g[layer, 0], w_in, flags, cos_p, sin_p)
            z3 = z.reshape(B, S, AB_N)
            os_, ls_ = [], []
            for gi, (win, dil) in enumerate(DIL_CFG):
                o, l = dil_prompt(z3, gi, win, dil)
                os_.append(o)
                ls_.append(l)
                nb = min(win, S)
                kk = z3[:, S - nb:, AB_KA + gi * AW:AB_KA + (gi + 1) * AW].reshape(B, nb, H_A, DH_A)
                vv = z3[:, S - nb:, AB_VA + gi * AW:AB_VA + (gi + 1) * AW].reshape(B, nb, H_A, DH_A)
                a_p[gi].append(jnp.stack([kk, vv], axis=2))
            zc = jnp.zeros
            hb, Cp, n_p, m_p = mlstm(z3, _gates_t(z3, MLSTM_CHUNK), b_if[e],
                                     zc((B, H_B, DH_B, DH_B), F32), zc((B, H_B, DH_B), F32), zc((B, H_B), F32),
                                     L=MLSTM_CHUNK, n_valid=MLSTM_CHUNK, bb=B if B <= 2 else 1)
            hp = ab_merge(os_, ls_, hb, z, g_mlstm[e], w_out, hp)
            bC_p.append(Cp); bn_p.append(n_p); bm_p.append(m_p)
            z = norm_proj(hs, norm_g[layer, 0], w_in, flags, cos_s, sin_s)
            z3 = z.reshape(DB, T_PAD, AB_N)
            os_, ls_ = [], []
            for gi, (win, dil) in enumerate(DIL_CFG):
                new_t = jnp.concatenate(
                    [jnp.swapaxes(z3[:, :, AB_KA + gi * AW:AB_KA + (gi + 1) * AW], 1, 2),
                     jnp.swapaxes(z3[:, :, AB_VA + gi * AW:AB_VA + (gi + 1) * AW], 1, 2)], axis=1)
                o, l, co = dil_sample(_channel_major(caches_a[gi][e], 1), z3, new_t, gi, win, dil, T)
                os_.append(o)
                ls_.append(l)
                a_s[gi].append(_row_major(co, 2, H_A, DH_A))
            bbs = 4 if DB % 4 == 0 else 1
            hb, Cs, n_s, m_s = mlstm(z3, _gates_t(z3, T_PAD), b_if[e], state_b_C[e], state_b_n[e], state_b_m[e],
                                     L=T_PAD, n_valid=T, bb=bbs)
            hs = ab_merge(os_, ls_, hb, z, g_mlstm[e], w_out, hs)
            bC_s.append(Cs); bn_s.append(n_s); bm_s.append(m_s)
        else:
            o_i = layer // 2
            w_in = jnp.pad(w_in_c[o_i], ((0, 0), (0, C_N - w_in_c.shape[-1]))).astype(BF16)
            flags = _c_rope_flags()
            w_out = w_out_c[o_i].astype(BF16)
            cw, cbias = _prep_compress(cmp_w1[o_i], cmp_w2[o_i], cmp_pe[o_i])
            z = norm_proj(hp, norm_g[layer, 0], w_in, flags, cos_p, sin_p)
            z3 = z.reshape(B, S, C_N)
            kv_cmp = z3[:, :, C_CMP:C_CMP + KVW]
            kv_sel = z3[:, :, C_SEL:C_SEL + KVW]
            kv_win = z3[:, :, C_WIN:C_WIN + KVW]
            kc = compress_prompt(z3, cw, cbias)
            n_cmp = (S - CMP_BLK) // CMP_STRIDE + 1
            n_slc = S // SLC_BLK
            mimp = _importance_matrix(S // CMP_STRIDE, n_cmp, n_slc, -(-n_slc // LANE) * LANE)
            ka, va = _sel_operands(z3)
            o = nsa_prompt(z3, kc, kv_win.astype(BF16), ka, va, cos_p1, sin_p1, mimp)
            hp = out_proj(o.reshape(B * S, C_Q), w_out, hp)
            nw = min(WIN_C, S)
            sh = lambda t: t.reshape(t.shape[0], t.shape[1], 2, KVH_C, DH_C)
            cc_p.append(sh(kv_cmp)); csl_p.append(sh(kv_sel)); cw_p.append(sh(kv_win[:, S - nw:]))
            z = norm_proj(hs, norm_g[layer, 0], w_in, flags, cos_s, sin_s)
            z3 = z.reshape(DB, T_PAD, C_N)
            kc = compress_paged(_channel_major(cache_c_cmp_kv[o_i], 1), page_table, cw, cbias)
            full_len = past + T
            n_cmp = (full_len - CMP_BLK) // CMP_STRIDE + 1
            assert (n_cmp + 1) * CMP_STRIDE <= past
            n_slc = past // SLC_BLK + -(-T // SLC_BLK)
            nsp = -(-n_slc // LANE) * LANE
            mimp = _importance_matrix(past // CMP_STRIDE, n_cmp, n_slc, nsp)
            o, wo = nsa_sample(z3, kc, _channel_major(cache_c_sel_kv[o_i], 1),
                               _channel_major(cache_c_win_kv[o_i], 1),
                               jnp.swapaxes(z3[:, :, C_WIN:C_WIN + KVW], 1, 2),
                               page_table, cos_s1, sin_s1, mimp, t_new=T)
            hs = out_proj(o.reshape(DB * T_PAD, C_Q), w_out, hs)
            cc_s.append(sh(z3[:, :T, C_CMP:C_CMP + KVW])); csl_s.append(sh(z3[:, :T, C_SEL:C_SEL + KVW]))
            cw_s.append(_row_major(wo, 2, KVH_C, DH_C))
        last = layer == depth - 1
        wg, wu, wd = (w_ffn_gate[layer].astype(BF16), w_ffn_up[layer].astype(BF16), w_ffn_down[layer].astype(BF16))
        hp = ffn(hp, norm_g[layer, 1], wg, wu, wd, norm_final, final_norm=last)
        hs = ffn(hs, norm_g[layer, 1], wg, wu, wd, norm_final, final_norm=last)

    y_prompt = hp.reshape(B, S, D)
    y_sample = hs.reshape(DB, T_PAD, D)[:, :T]
    st = lambda xs: jnp.stack(xs, axis=0)
    return (y_prompt, y_sample,
            st(a_p[0]), st(a_s[0]), st(a_p[1]), st(a_s[1]), st(a_p[2]), st(a_s[2]),
            st(bC_p), st(bC_s), st(bn_p), st(bn_s), st(bm_p), st(bm_s),
            st(cc_p), st(cc_s), st(csl_p), st(csl_s), st(cw_p), st(cw_s))
```

```python
import functools
import math

import numpy as np
import jax
import jax.numpy as jnp
from jax import lax
from jax.experimental import pallas as pl
from jax.experimental.pallas import tpu as pltpu

F32 = jnp.float32
BF16 = jnp.bfloat16

PAGE_SIZE = 128
DIL_CFG = ((128, 1), (512, 4), (2048, 16))
N_DIL = 3
H_A = 4
DH_A = 64
BLK_A = 128
H_B = 4
DH_B = 128
MLSTM_CHUNK = 128
H_C = 16
KVH_C = 2
G_C = H_C // KVH_C
DH_C = 64
CMP_STRIDE = 16
CMP_BLK = 2 * CMP_STRIDE
CMP_HID = 64
SLC_BLK = 64
N_SEL = 16
WIN_C = 512
QBLK_C = 128
ROPE_THETA = 10000.0
EPS = 1e-6
A_QW = N_DIL * H_A * DH_A
B_W = H_B * DH_B
AW = H_A * DH_A
C_Q = H_C * DH_C
C_KV = 3 * 2 * KVH_C * DH_C
KVW = 2 * KVH_C * DH_C

LANE = 128
SUBLANE = 8
VMEM_LIMIT = 48 * 1024 * 1024

NEG_BIG = -1e30
T_PAD = SUBLANE

AB_QB, AB_KB, AB_VB, AB_OG = 0, B_W, 2 * B_W, 3 * B_W
AB_QA = 4 * B_W
AB_KA = AB_QA + A_QW
AB_VA = AB_KA + A_QW
AB_IG = AB_VA + A_QW
AB_FG = AB_IG + LANE
AB_N = AB_FG + LANE

C_CMP = C_Q
C_SEL = C_Q + KVW
C_WIN = C_Q + 2 * KVW
C_GATE = C_Q + 3 * KVW
C_N = 2048


def _cparams(sem, vmem=VMEM_LIMIT):
    return pltpu.CompilerParams(dimension_semantics=sem, vmem_limit_bytes=vmem)


def _pick_tile(m, pref):
    t = min(m, pref)
    while m % t:
        t //= 2
    return t


def _rope_lanes(x, cos, sin):
    lane = lax.broadcasted_iota(jnp.int32, x.shape, 1)
    first = (lane % DH_C) < (DH_C // 2)
    partner = jnp.where(first, pltpu.roll(x, LANE - DH_C // 2, 1), pltpu.roll(x, DH_C // 2, 1))
    return x * cos + partner * sin


def _split3(x):
    hi = x.astype(BF16)
    r1 = x - hi.astype(F32)
    mid = r1.astype(BF16)
    lo = (r1 - mid.astype(F32)).astype(BF16)
    return hi, mid, lo


def _dot_exact_rhs01(x, m01):
    hi, mid, lo = _split3(x)
    d = lambda a: jnp.dot(a, m01, preferred_element_type=F32)
    return d(hi) + d(mid) + d(lo)


def _dot_exact_lhs01(m01, x):
    hi, mid, lo = _split3(x)
    d = lambda a: jnp.dot(m01, a, preferred_element_type=F32)
    return d(hi) + d(mid) + d(lo)


def _dot_nt(a, b):
    return lax.dot_general(a, b, (((1,), (1,)), ((), ())), preferred_element_type=F32)


def _dot_tn(a, b):
    return lax.dot_general(a, b, (((0,), (0,)), ((), ())), preferred_element_type=F32)


def _log_sigmoid(x):
    return jnp.minimum(x, 0.0) - jnp.log1p(jnp.exp(-jnp.abs(x)))


def _gelu_tanh(x):
    return 0.5 * x * (1.0 + jnp.tanh(math.sqrt(2.0 / math.pi) * (x + 0.044715 * (x * x * x))))


def _rms_rows(x, g):
    ms = jnp.mean(x * x, axis=-1, keepdims=True)
    return x * lax.rsqrt(ms + EPS) * g


def _norm_proj_kernel(flags_ref, x_ref, g_ref, w_ref, cos_ref, sin_ref, o_ref, xn_ref, *, tn):
    j = pl.program_id(1)

    @pl.when(j == 0)
    def _():
        xn_ref[...] = _rms_rows(x_ref[...], g_ref[...]).astype(BF16)

    acc = jnp.dot(xn_ref[...], w_ref[...], preferred_element_type=F32)
    nchunk = tn // LANE
    for c in range(nchunk):
        a = acc[:, c * LANE:(c + 1) * LANE]
        flag = flags_ref[j * nchunk + c]

        @pl.when(flag == 1)
        def _():
            o_ref[:, c * LANE:(c + 1) * LANE] = _rope_lanes(a, cos_ref[...], sin_ref[...])

        @pl.when(flag == 0)
        def _():
            o_ref[:, c * LANE:(c + 1) * LANE] = a


def norm_proj(x, g, w_bf16, rope_flags, cos_t, sin_t, *, tm_pref=1024, tn=512):
    m, d = x.shape
    n = w_bf16.shape[1]
    tm = _pick_tile(m, tm_pref)
    grid = (m // tm, n // tn)
    gs = pltpu.PrefetchScalarGridSpec(
        num_scalar_prefetch=1,
        grid=grid,
        in_specs=[
            pl.BlockSpec((tm, d), lambda i, j, f: (i, 0)),
            pl.BlockSpec((1, d), lambda i, j, f: (0, 0)),
            pl.BlockSpec((d, tn), lambda i, j, f: (0, j)),
            pl.BlockSpec((tm, LANE), lambda i, j, f: (i, 0)),
            pl.BlockSpec((tm, LANE), lambda i, j, f: (i, 0)),
        ],
        out_specs=pl.BlockSpec((tm, tn), lambda i, j, f: (i, j)),
        scratch_shapes=[pltpu.VMEM((tm, d), BF16)],
    )
    return pl.pallas_call(
        functools.partial(_norm_proj_kernel, tn=tn),
        grid_spec=gs,
        out_shape=jax.ShapeDtypeStruct((m, n), F32),
        compiler_params=_cparams(("parallel", "arbitrary")),
        name="norm_proj",
    )(rope_flags, x, g.reshape(1, d), w_bf16, cos_t, sin_t)


def _ffn_kernel(x_ref, g_ref, wg_ref, wu_ref, wd_ref, gf_ref, o_ref, xn_ref, *, final_norm):
    f = pl.program_id(1)

    @pl.when(f == 0)
    def _():
        x = x_ref[...]
        xn_ref[...] = _rms_rows(x, g_ref[...]).astype(BF16)
        o_ref[...] = x

    xn = xn_ref[...]
    a = jnp.dot(xn, wg_ref[...], preferred_element_type=F32)
    u = jnp.dot(xn, wu_ref[...], preferred_element_type=F32)
    act = (a * jax.nn.sigmoid(a)) * u
    o_ref[...] += jnp.dot(act.astype(BF16), wd_ref[...], preferred_element_type=F32)

    if final_norm:
        @pl.when(f == pl.num_programs(1) - 1)
        def _():
            o_ref[...] = _rms_rows(o_ref[...], gf_ref[...])


def ffn(x, g, wg, wu, wd, g_final, *, final_norm, tm_pref=512):
    m, d = x.shape
    dff = wg.shape[1]
    tf = dff // 2 if (dff // 2) % LANE == 0 else dff
    tm = _pick_tile(m, tm_pref)
    grid = (m // tm, dff // tf)
    return pl.pallas_call(
        functools.partial(_ffn_kernel, final_norm=final_norm),
        grid=grid,
        in_specs=[
            pl.BlockSpec((tm, d), lambda i, f: (i, 0)),
            pl.BlockSpec((1, d), lambda i, f: (0, 0)),
            pl.BlockSpec((d, tf), lambda i, f: (0, f)),
            pl.BlockSpec((d, tf), lambda i, f: (0, f)),
            pl.BlockSpec((tf, d), lambda i, f: (f, 0)),
            pl.BlockSpec((1, d), lambda i, f: (0, 0)),
        ],
        out_specs=pl.BlockSpec((tm, d), lambda i, f: (i, 0)),
        out_shape=jax.ShapeDtypeStruct((m, d), F32),
        scratch_shapes=[pltpu.VMEM((tm, d), BF16)],
        compiler_params=_cparams(("parallel", "arbitrary")),
        name="ffn",
    )(x, g.reshape(1, d), wg, wu, wd, g_final.reshape(1, d))


def _out_proj_kernel(x_ref, w_ref, r_ref, o_ref):
    o_ref[...] = r_ref[...] + jnp.dot(x_ref[...].astype(BF16), w_ref[...], preferred_element_type=F32)


def out_proj(x, w_bf16, resid, *, tm_pref=512):
    m, k = x.shape
    n = w_bf16.shape[1]
    tm = _pick_tile(m, tm_pref)
    return pl.pallas_call(
        _out_proj_kernel,
        grid=(m // tm,),
        in_specs=[
            pl.BlockSpec((tm, k), lambda i: (i, 0)),
            pl.BlockSpec((k, n), lambda i: (0, 0)),
            pl.BlockSpec((tm, n), lambda i: (i, 0)),
        ],
        out_specs=pl.BlockSpec((tm, n), lambda i: (i, 0)),
        out_shape=jax.ShapeDtypeStruct((m, n), F32),
        compiler_params=_cparams(("parallel",)),
        name="out_proj",
    )(x, w_bf16, resid)


DIL_STEP_ROWS = 2048
DIL_UNROLL = 4


def _dil_prompt_kernel(q_ref, kp_ref, kc_ref, vp_ref, vc_ref, o_ref, l_ref, *, nback, dil, nblk):
    n = pl.program_id(2)
    span = dil * BLK_A
    qi = BLK_A + lax.broadcasted_iota(jnp.int32, (BLK_A, 2 * BLK_A), 0)
    ki = lax.broadcasted_iota(jnp.int32, (BLK_A, 2 * BLK_A), 1)
    rel = qi - ki
    band = (rel >= 0) & (rel <= nback)
    own = ki >= BLK_A

    def one(it, carry):
        j = it // dil
        r = it % dil
        start = j * span + r
        if dil > 1:
            take = lambda st: pl.ds(st, BLK_A, stride=dil)
        else:
            take = lambda st: pl.ds(pl.multiple_of(st, BLK_A), BLK_A)
        rows = take(start)
        prev_rows = take(jnp.maximum(start - span, r))
        first_rows = take(r)
        q = q_ref[0, rows, :] * (DH_A ** -0.5)
        kprev = jnp.where(j > 0, kc_ref[0, prev_rows, :], kp_ref[0, first_rows, :])
        vprev = jnp.where(j > 0, vc_ref[0, prev_rows, :], vp_ref[0, first_rows, :])
        kk = jnp.concatenate([kprev, kc_ref[0, rows, :]], axis=0)
        vv = jnp.concatenate([vprev, vc_ref[0, rows, :]], axis=0)
        mask = band & ((n * nblk + j > 0) | own)
        outs, lses = [], []
        for h in range(LANE // DH_A):
            hs = slice(h * DH_A, (h + 1) * DH_A)
            s = _dot_nt(q[:, hs].astype(BF16), kk[:, hs].astype(BF16))
            s = jnp.where(mask, s, -jnp.inf)
            m = jnp.max(s, axis=-1, keepdims=True)
            p = jnp.exp(s - m)
            l = jnp.sum(p, axis=-1, keepdims=True)
            o = jnp.dot(p.astype(BF16), vv[:, hs].astype(BF16), preferred_element_type=F32) / l
            outs.append(o)
            lses.append(jnp.broadcast_to(m + jnp.log(l), (BLK_A, DH_A)))
        o_ref[0, rows, :] = jnp.concatenate(outs, axis=-1)
        l_ref[0, rows, :] = jnp.concatenate(lses, axis=-1)
        return carry

    lax.fori_loop(0, nblk * dil, one, 0, unroll=DIL_UNROLL)


def dil_prompt(z3, gi, window, dil):
    b, s, _ = z3.shape
    nback = window // dil
    span = dil * BLK_A
    nblk = max(1, DIL_STEP_ROWS // span)
    rows = nblk * span
    assert s % rows == 0 and (nblk * dil) % DIL_UNROLL == 0
    nh = AW // LANE
    qo, ko, vo = (AB_QA + gi * AW) // LANE, (AB_KA + gi * AW) // LANE, (AB_VA + gi * AW) // LANE
    blk = (1, rows, LANE)
    pblk = (1, span, LANE)
    in_specs = [
        pl.BlockSpec(blk, lambda bb, hp, i: (bb, i, qo + hp)),
        pl.BlockSpec(pblk, lambda bb, hp, i: (bb, jnp.maximum(i * nblk - 1, 0), ko + hp)),
        pl.BlockSpec(blk, lambda bb, hp, i: (bb, i, ko + hp)),
        pl.BlockSpec(pblk, lambda bb, hp, i: (bb, jnp.maximum(i * nblk - 1, 0), vo + hp)),
        pl.BlockSpec(blk, lambda bb, hp, i: (bb, i, vo + hp)),
    ]
    out_spec = pl.BlockSpec(blk, lambda bb, hp, i: (bb, i, hp))
    o, l = pl.pallas_call(
        functools.partial(_dil_prompt_kernel, nback=nback, dil=dil, nblk=nblk),
        grid=(b, nh, s // rows),
        in_specs=in_specs,
        out_specs=[out_spec, out_spec],
        out_shape=[jax.ShapeDtypeStruct((b, s, AW), F32)] * 2,
        compiler_params=_cparams(("parallel", "parallel", "arbitrary")),
        name="dil_prompt_%d" % gi,
    )(z3, z3, z3, z3, z3)
    return o.reshape(b * s, AW), l.reshape(b * s, AW)


def _dil_sample_kernel(c_ref, q_ref, kn_ref, vn_ref, nt_ref, o_ref, l_ref, co_ref, *, window, dil, n_buf, t_new):
    rows = H_A * T_PAD
    q = q_ref[0] * (DH_A ** -0.5)
    q4 = jnp.concatenate([q] * H_A, axis=0)
    rr = lax.broadcasted_iota(jnp.int32, (rows, AW), 0)
    ll = lax.broadcasted_iota(jnp.int32, (rows, AW), 1)
    head_sel = (rr // T_PAD) == (ll // DH_A)
    qbd = jnp.where(head_sel, q4, 0.0).astype(BF16)
    kbt = c_ref[0, 0:AW, :].astype(BF16)
    vbt = c_ref[0, AW:2 * AW, :].astype(BF16)
    kn = kn_ref[0]
    vn = vn_ref[0]
    s_buf = jnp.dot(qbd, kbt, preferred_element_type=F32)
    s_new = _dot_nt(qbd, kn.astype(BF16))
    t_b = lax.broadcasted_iota(jnp.int32, (rows, n_buf), 0) % T_PAD
    c_b = lax.broadcasted_iota(jnp.int32, (rows, n_buf), 1)
    d_b = n_buf + t_b - c_b
    m_b = ((d_b % dil) == 0) & (d_b <= window)
    t_n = lax.broadcasted_iota(jnp.int32, (rows, T_PAD), 0) % T_PAD
    u_n = lax.broadcasted_iota(jnp.int32, (rows, T_PAD), 1)
    d_n = t_n - u_n
    m_n = (d_n >= 0) & ((d_n % dil) == 0) & (d_n <= window) & (u_n < t_new)
    s_buf = jnp.where(m_b, s_buf, NEG_BIG)
    s_new = jnp.where(m_n, s_new, NEG_BIG)
    mx = jnp.maximum(jnp.max(s_buf, axis=-1, keepdims=True), jnp.max(s_new, axis=-1, keepdims=True))
    p_b = jnp.where(m_b, jnp.exp(s_buf - mx), 0.0)
    p_n = jnp.where(m_n, jnp.exp(s_new - mx), 0.0)
    l = jnp.sum(p_b, axis=-1, keepdims=True) + jnp.sum(p_n, axis=-1, keepdims=True)
    l = jnp.maximum(l, 1e-30)
    acc = (_dot_nt(p_b.astype(BF16), vbt)
           + jnp.dot(p_n.astype(BF16), vn.astype(BF16), preferred_element_type=F32))
    res = jnp.where(head_sel, acc / l, 0.0)
    lse = jnp.where(head_sel, mx + jnp.log(l), 0.0)
    o = res[0:T_PAD]
    ls = lse[0:T_PAD]
    for h in range(1, H_A):
        o = o + res[h * T_PAD:(h + 1) * T_PAD]
        ls = ls + lse[h * T_PAD:(h + 1) * T_PAD]
    o_ref[0] = o
    l_ref[0] = ls
    co_ref[0, :, 0:n_buf - t_new] = c_ref[0, :, t_new:n_buf]
    co_ref[0, :, n_buf - t_new:n_buf] = nt_ref[0, :, 0:t_new]


def dil_sample(cache_t, z3, new_t, gi, window, dil, t_new):
    db, _, n_buf = cache_t.shape
    qo, ko, vo = AB_QA // AW + gi, AB_KA // AW + gi, AB_VA // AW + gi
    blk = (1, T_PAD, AW)
    o, l, co = pl.pallas_call(
        functools.partial(_dil_sample_kernel, window=window, dil=dil, n_buf=n_buf, t_new=t_new),
        grid=(db,),
        in_specs=[
            pl.BlockSpec((1, 2 * AW, n_buf), lambda b: (b, 0, 0)),
            pl.BlockSpec(blk, lambda b: (b, 0, qo)),
            pl.BlockSpec(blk, lambda b: (b, 0, ko)),
            pl.BlockSpec(blk, lambda b: (b, 0, vo)),
            pl.BlockSpec((1, 2 * AW, T_PAD), lambda b: (b, 0, 0)),
        ],
        out_specs=[
            pl.BlockSpec(blk, lambda b: (b, 0, 0)),
            pl.BlockSpec(blk, lambda b: (b, 0, 0)),
            pl.BlockSpec((1, 2 * AW, n_buf), lambda b: (b, 0, 0)),
        ],
        out_shape=[
            jax.ShapeDtypeStruct((db, T_PAD, AW), F32),
            jax.ShapeDtypeStruct((db, T_PAD, AW), F32),
            jax.ShapeDtypeStruct((db, 2 * AW, n_buf), F32),
        ],
        compiler_params=_cparams(("parallel",)),
        name="dil_sample_%d" % gi,
    )(cache_t, z3, z3, z3, new_t)
    return o.reshape(db * T_PAD, AW), l.reshape(db * T_PAD, AW), co


def _mlstm_kernel(q_ref, k_ref, v_ref, gi_ref, gf_ref, gt_ref, bi_ref, bf_ref, b8_ref,
                  c0_ref, n0_ref, m0_ref, h_ref, co_ref, no_ref, mo_ref,
                  c_s, n_s, m_s, *, bb, L, n_valid):
    c = pl.program_id(1)

    @pl.when(c == 0)
    def _():
        c_s[...] = c0_ref[...]
        n_s[...] = n0_ref[...]
        m_s[...] = m0_ref[...]

    row = lax.broadcasted_iota(jnp.int32, (L, L), 0)
    col = lax.broadcasted_iota(jnp.int32, (L, L), 1)
    tri = row >= col
    tri_l = jnp.where(tri, 1.0, 0.0).astype(BF16)
    tri_u = jnp.where(row <= col, 1.0, 0.0).astype(BF16)
    lane = lax.broadcasted_iota(jnp.int32, (1, LANE), 1)
    for b in range(bb):
        ig_col = gi_ref[b] + bi_ref[...]
        lf_col = _log_sigmoid(gf_ref[b] + bf_ref[...])
        gt = gt_ref[b, 0]
        ig_row = gt[0:H_B] + b8_ref[0:H_B]
        lf_row = _log_sigmoid(gt[H_B:2 * H_B] + b8_ref[H_B:2 * H_B])
        if n_valid < L:
            rv = lax.broadcasted_iota(jnp.int32, (L, LANE), 0) < n_valid
            ig_col = jnp.where(rv, ig_col, -jnp.inf)
            lf_col = jnp.where(rv, lf_col, 0.0)
            cv = lax.broadcasted_iota(jnp.int32, (H_B, L), 1) < n_valid
            ig_row = jnp.where(cv, ig_row, -jnp.inf)
            lf_row = jnp.where(cv, lf_row, 0.0)
        b_col = _dot_exact_lhs01(tri_l, lf_col)
        b_row = _dot_exact_rhs01(lf_row, tri_u)
        m_row = m_s[b]
        a_col = b_col + m_row
        m_new = m_row
        hs_out = []
        for h in range(H_B):
            hsl = slice(h * DH_B, (h + 1) * DH_B)
            bc = b_col[:, h:h + 1]
            ac = a_col[:, h:h + 1]
            icol = ig_col[:, h:h + 1]
            D = bc - b_row[h:h + 1, :] + ig_row[h:h + 1, :]
            D = jnp.where(tri, D, -jnp.inf)
            mt = jnp.maximum(ac, jnp.max(D, axis=-1, keepdims=True))
            Dw = jnp.exp(D - mt)
            iw = jnp.exp(ac - mt)
            qf = q_ref[b, :, hsl]
            kf = k_ref[b, :, hsl] * (DH_B ** -0.5)
            vf = v_ref[b, :, hsl]
            qb, kb, vb = qf.astype(BF16), kf.astype(BF16), vf.astype(BF16)
            Cm = c_s[b * H_B + h]
            nv = n_s[b * H_B + h]
            sc = _dot_nt(qb, kb) * Dw
            num = iw * _dot_nt(qb, Cm.astype(BF16)) + jnp.dot(sc.astype(BF16), vb, preferred_element_type=F32)
            den = iw * jnp.sum(qf * nv, axis=-1, keepdims=True) + jnp.sum(sc, axis=-1, keepdims=True)
            hs_out.append(num / jnp.maximum(jnp.abs(den), jnp.exp(-mt)))
            mL = mt[L - 1:L, :]
            wL = jnp.exp(bc[L - 1:L, :] - bc + icol - mL)
            dec = jnp.exp(ac[L - 1:L, :] - mL)
            c_s[b * H_B + h] = dec * Cm + _dot_tn((vf * wL).astype(BF16), kb)
            n_s[b * H_B + h] = dec * nv + jnp.sum(wL * kf, axis=0, keepdims=True)
            m_new = jnp.where(lane == h, mL, m_new)
        m_s[b] = m_new
        h_ref[b] = jnp.concatenate(hs_out, axis=-1)

    @pl.when(c == pl.num_programs(1) - 1)
    def _():
        co_ref[...] = c_s[...]
        no_ref[...] = n_s[...]
        mo_ref[...] = m_s[...]


def mlstm(z3, gt, b_if, c0, n0, m0, *, L, n_valid, bb):
    b, s, _ = z3.shape
    nc = s // L
    bi_row = jnp.zeros((1, LANE), F32).at[0, :H_B].set(b_if[0])
    bf_row = jnp.zeros((1, LANE), F32).at[0, :H_B].set(b_if[1])
    b8 = b_if.reshape(2 * H_B, 1)
    c0r = c0.reshape(b * H_B, DH_B, DH_B)
    n0r = n0.reshape(b * H_B, 1, DH_B)
    m0r = jnp.zeros((b, 1, LANE), F32).at[:, 0, :H_B].set(m0)
    cw = B_W // LANE
    h, co, no, mo = pl.pallas_call(
        functools.partial(_mlstm_kernel, bb=bb, L=L, n_valid=n_valid),
        grid=(b // bb, nc),
        in_specs=[
            pl.BlockSpec((bb, L, B_W), lambda g, c: (g, c, AB_QB // B_W)),
            pl.BlockSpec((bb, L, B_W), lambda g, c: (g, c, AB_KB // B_W)),
            pl.BlockSpec((bb, L, B_W), lambda g, c: (g, c, AB_VB // B_W)),
            pl.BlockSpec((bb, L, LANE), lambda g, c: (g, c, AB_IG // LANE)),
            pl.BlockSpec((bb, L, LANE), lambda g, c: (g, c, AB_FG // LANE)),
            pl.BlockSpec((bb, 1, 2 * H_B, L), lambda g, c: (g, c, 0, 0)),
            pl.BlockSpec((1, LANE), lambda g, c: (0, 0)),
            pl.BlockSpec((1, LANE), lambda g, c: (0, 0)),
            pl.BlockSpec((2 * H_B, 1), lambda g, c: (0, 0)),
            pl.BlockSpec((bb * H_B, DH_B, DH_B), lambda g, c: (g, 0, 0)),
            pl.BlockSpec((bb * H_B, 1, DH_B), lambda g, c: (g, 0, 0)),
            pl.BlockSpec((bb, 1, LANE), lambda g, c: (g, 0, 0)),
        ],
        out_specs=[
            pl.BlockSpec((bb, L, B_W), lambda g, c: (g, c, 0)),
            pl.BlockSpec((bb * H_B, DH_B, DH_B), lambda g, c: (g, 0, 0)),
            pl.BlockSpec((bb * H_B, 1, DH_B), lambda g, c: (g, 0, 0)),
            pl.BlockSpec((bb, 1, LANE), lambda g, c: (g, 0, 0)),
        ],
        out_shape=[
            jax.ShapeDtypeStruct((b, s, B_W), F32),
            jax.ShapeDtypeStruct((b * H_B, DH_B, DH_B), F32),
            jax.ShapeDtypeStruct((b * H_B, 1, DH_B), F32),
            jax.ShapeDtypeStruct((b, 1, LANE), F32),
        ],
        scratch_shapes=[
            pltpu.VMEM((bb * H_B, DH_B, DH_B), F32),
            pltpu.VMEM((bb * H_B, 1, DH_B), F32),
            pltpu.VMEM((bb, 1, LANE), F32),
        ],
        compiler_params=_cparams(("parallel", "arbitrary")),
        name="mlstm_L%d" % L,
    )(z3, z3, z3, z3, z3, gt, bi_row, bf_row, b8, c0r, n0r, m0r)
    del cw
    return (h.reshape(b * s, B_W), co.reshape(b, H_B, DH_B, DH_B), no.reshape(b, H_B, DH_B),
            mo[:, 0, :H_B])


def _ab_merge_kernel(o0, o1, o2, l0, l1, l2, hb_ref, og_ref, g_ref, w_ref, r_ref, out_ref):
    a0, a1, a2 = l0[...], l1[...], l2[...]
    mx = jnp.maximum(jnp.maximum(a0, a1), a2)
    e0, e1, e2 = jnp.exp(a0 - mx), jnp.exp(a1 - mx), jnp.exp(a2 - mx)
    o_a = (e0 * o0[...] + e1 * o1[...] + e2 * o2[...]) / (e0 + e1 + e2)
    hb = hb_ref[...]
    parts = []
    for h in range(H_B):
        hs = slice(h * DH_B, (h + 1) * DH_B)
        x = hb[:, hs]
        parts.append(x * lax.rsqrt(jnp.mean(x * x, axis=-1, keepdims=True) + EPS))
    hbn = jnp.concatenate(parts, axis=-1) * g_ref[...] * jax.nn.sigmoid(og_ref[...])
    y = (jnp.dot(o_a.astype(BF16), w_ref[0:AW, :], preferred_element_type=F32)
         + jnp.dot(hbn.astype(BF16), w_ref[AW:AW + B_W, :], preferred_element_type=F32))
    out_ref[...] = r_ref[...] + y


def ab_merge(os_, ls_, hb, z2, g_mn, w_bf16, resid, *, tm_pref=512):
    m, d = resid.shape
    tm = _pick_tile(m, tm_pref)
    a_spec = pl.BlockSpec((tm, AW), lambda i: (i, 0))
    return pl.pallas_call(
        _ab_merge_kernel,
        grid=(m // tm,),
        in_specs=[a_spec] * 6 + [
            pl.BlockSpec((tm, B_W), lambda i: (i, 0)),
            pl.BlockSpec((tm, B_W), lambda i: (i, AB_OG // B_W)),
            pl.BlockSpec((1, B_W), lambda i: (0, 0)),
            pl.BlockSpec((AW + B_W, d), lambda i: (0, 0)),
            pl.BlockSpec((tm, d), lambda i: (i, 0)),
        ],
        out_specs=pl.BlockSpec((tm, d), lambda i: (i, 0)),
        out_shape=jax.ShapeDtypeStruct((m, d), F32),
        compiler_params=_cparams(("parallel",)),
        name="ab_merge",
    )(*os_, *ls_, hb, z2, g_mn.reshape(1, B_W), w_bf16, resid)


HKV = KVH_C * DH_C


def _compress_bias_kernel(pa_ref, pb_ref, w_ref, o_ref):
    w = w_ref[0]
    o_ref[0] = (jnp.dot(pa_ref[0].astype(BF16), w, preferred_element_type=F32)[:, 0:HKV]
                + jnp.dot(pb_ref[0].astype(BF16), w, preferred_element_type=F32)[:, HKV:2 * HKV])


def compress_bias(pa, pb, wab2):
    spec = lambda a: pl.BlockSpec((1,) + a.shape[1:], lambda i: (i,) + (0,) * (a.ndim - 1))
    return pl.pallas_call(
        _compress_bias_kernel,
        grid=(2,),
        in_specs=[spec(pa), spec(pb), spec(wab2)],
        out_specs=pl.BlockSpec((1, 1, HKV), lambda i: (i, 0, 0)),
        out_shape=jax.ShapeDtypeStruct((2, 1, HKV), F32),
        compiler_params=_cparams(("arbitrary",)),
        name="compress_bias",
    )(pa, pb, wab2)


def _compress_halves(load_rows, n, w_of_r):
    acc = jnp.zeros((n, 2 * HKV), F32)
    for r in range(CMP_STRIDE):
        acc = acc + jnp.dot(load_rows(r, n).astype(BF16), w_of_r(r), preferred_element_type=F32)
    return acc


def _compress_prompt_kernel(x_ref, wab_ref, bias_ref, w2_ref, o_ref):
    n = x_ref.shape[1] // CMP_STRIDE
    acc = _compress_halves(lambda r, m: x_ref[0, pl.ds(r, m, stride=CMP_STRIDE), :], n, lambda r: wab_ref[0, r])
    hid = acc[:, 0:HKV] + pltpu.roll(acc[:, HKV:2 * HKV], n - 1, 0) + bias_ref[0]
    o_ref[0] = jnp.dot(_gelu_tanh(hid).astype(BF16), w2_ref[0], preferred_element_type=F32)


def compress_prompt(z3, cw, bias):
    b, s, _ = z3.shape
    wab, w2 = cw
    n = s // CMP_STRIDE
    return pl.pallas_call(
        _compress_prompt_kernel,
        grid=(b, 2),
        in_specs=[
            pl.BlockSpec((1, s, HKV), lambda i, kv: (i, 0, C_CMP // HKV + kv)),
            pl.BlockSpec((1,) + wab.shape[1:], lambda i, kv: (kv, 0, 0, 0)),
            pl.BlockSpec((1, 1, HKV), lambda i, kv: (kv, 0, 0)),
            pl.BlockSpec((1, HKV, HKV), lambda i, kv: (kv, 0, 0)),
        ],
        out_specs=pl.BlockSpec((1, n, HKV), lambda i, kv: (i, 0, kv)),
        out_shape=jax.ShapeDtypeStruct((b, n, KVW), F32),
        compiler_params=_cparams(("parallel", "arbitrary")),
        name="compress_prompt",
    )(z3, wab, bias, w2)


def _compress_paged_kernel(pt_ref, *refs, pg):
    del pt_ref
    page_refs = refs[:pg + 1]
    wab_ref, bias_ref, w2_ref, o_ref, xs_ref = refs[pg + 1:]
    for j in range(pg + 1):
        t = page_refs[j][0].T
        xs_ref[0, j * PAGE_SIZE:(j + 1) * PAGE_SIZE, :] = t[:, 0:HKV]
        xs_ref[1, j * PAGE_SIZE:(j + 1) * PAGE_SIZE, :] = t[:, HKV:2 * HKV]
    cpp = PAGE_SIZE // CMP_STRIDE
    n = pg * cpp
    for kv in range(2):
        acc = _compress_halves(lambda r, m: xs_ref[kv, pl.ds(r, m, stride=CMP_STRIDE), :], n + cpp,
                               lambda r: wab_ref[kv, r])
        hid = acc[0:n, 0:HKV] + acc[1:n + 1, HKV:2 * HKV] + bias_ref[kv]
        o_ref[0, :, kv * HKV:(kv + 1) * HKV] = jnp.dot(_gelu_tanh(hid).astype(BF16), w2_ref[kv],
                                                        preferred_element_type=F32)


def compress_paged(pool_t, page_table, cw, bias, *, pg=32):
    db, n_pages = page_table.shape
    cpp = PAGE_SIZE // CMP_STRIDE
    pg = min(pg, n_pages)
    assert n_pages % pg == 0
    wab, w2 = cw
    full = lambda a: pl.BlockSpec(a.shape, lambda b, g, pt: (0,) * a.ndim)

    def page_spec(j):
        return pl.BlockSpec((1, KVW, PAGE_SIZE),
                            lambda b, g, pt: (pt[b, jnp.minimum(g * pg + j, n_pages - 1)], 0, 0))

    gs = pltpu.PrefetchScalarGridSpec(
        num_scalar_prefetch=1,
        grid=(db, n_pages // pg),
        in_specs=[page_spec(j) for j in range(pg + 1)] + [full(wab), full(bias), full(w2)],
        out_specs=pl.BlockSpec((1, pg * cpp, KVW), lambda b, g, pt: (b, g, 0)),
        scratch_shapes=[pltpu.VMEM((2, (pg + 1) * PAGE_SIZE, HKV), F32)],
    )
    return pl.pallas_call(
        functools.partial(_compress_paged_kernel, pg=pg),
        grid_spec=gs,
        out_shape=jax.ShapeDtypeStruct((db, n_pages * cpp, KVW), F32),
        compiler_params=_cparams(("parallel", "arbitrary")),
        name="compress_paged",
    )(page_table, *([pool_t] * (pg + 1)), wab, bias, w2)


def _stack_heads(x, t):
    del t
    return jnp.concatenate([x[:, g * DH_C:(g + 1) * DH_C] for g in range(G_C)], axis=0)


def _mask_rows(s, mask, fill):
    t, n = mask.shape
    return jnp.where(mask[None], s.reshape(G_C, t, n), fill).reshape(G_C * t, n)


def _cmp_branch(q2, kc, vc, t_pos, tq):
    n = kc.shape[0]
    s = _dot_nt(q2, kc)
    cend = lax.broadcasted_iota(jnp.int32, (tq, n), 1) * CMP_STRIDE + (CMP_BLK - 1)
    s = _mask_rows(s, cend <= t_pos, -jnp.inf)
    m = jnp.max(s, axis=-1, keepdims=True)
    m = jnp.where(m > -jnp.inf, m, 0.0)
    p = jnp.exp(s - m)
    p = p / jnp.maximum(jnp.sum(p, axis=-1, keepdims=True), 1e-30)
    o = jnp.dot(p.astype(BF16), vc, preferred_element_type=F32)
    pg = p[0:tq]
    for g in range(1, G_C):
        pg = pg + p[g * tq:(g + 1) * tq]
    return o, pg


def _select_blocks(imp, t_pos):
    tq, nsp = imp.shape
    j = lax.broadcasted_iota(jnp.int32, (tq, nsp), 1)
    jf = j.astype(F32)
    cur = t_pos // SLC_BLK
    forced = (j == 0) | (j == cur) | (j == cur - 1)
    work = jnp.where(forced, jnp.inf, jnp.where(j <= cur, imp, -jnp.inf))
    sel = jnp.zeros((tq, nsp), F32)
    for _ in range(N_SEL):
        mx = jnp.max(work, axis=-1, keepdims=True)
        first = jnp.min(jnp.where(work == mx, jf, float(nsp)), axis=-1, keepdims=True)
        pick = jf == first
        sel = jnp.where(pick, jnp.where(mx > -jnp.inf, 1.0, sel), sel)
        work = jnp.where(pick, -jnp.inf, work)
    return sel


def _select_blocks_t(imp_t, t_row):
    nsp, n = imp_t.shape
    j = lax.broadcasted_iota(jnp.int32, (nsp, n), 0)
    jf = j.astype(F32)
    cur = t_row // SLC_BLK
    forced = (j == 0) | (j == cur) | (j == cur - 1)
    work = jnp.where(forced, jnp.inf, jnp.where(j <= cur, imp_t, -jnp.inf))
    sel = jnp.zeros((nsp, n), F32)
    for _ in range(N_SEL):
        mx = jnp.max(work, axis=0, keepdims=True)
        first = jnp.min(jnp.where(work == mx, jf, float(nsp)), axis=0, keepdims=True)
        pick = jf == first
        sel = jnp.where(pick, jnp.where(mx > -jnp.inf, 1.0, sel), sel)
        work = jnp.where(pick, -jnp.inf, work)
    return sel


def _flash_update(s, mask, v, m_ref, l_ref, acc_ref, v_channel_major=False):
    s = _mask_rows(s, mask, NEG_BIG)
    m_old = m_ref[...]
    m_new = jnp.maximum(m_old, jnp.max(s, axis=-1, keepdims=True))
    alpha = jnp.exp(m_old - m_new)
    p = jnp.exp(s - m_new)
    l_ref[...] = alpha * l_ref[...] + jnp.sum(p, axis=-1, keepdims=True)
    pv = _dot_nt(p.astype(BF16), v) if v_channel_major else jnp.dot(p.astype(BF16), v, preferred_element_type=F32)
    acc_ref[...] = alpha * acc_ref[...] + pv
    m_ref[...] = m_new


def _flash_finish(m_ref, l_ref, acc_ref):
    return jnp.where(m_ref[...] > 0.5 * NEG_BIG, acc_ref[...] / jnp.maximum(l_ref[...], 1e-30), 0.0)


def _softmax_av(s, mask, v):
    s = _mask_rows(s, mask, -jnp.inf)
    m = jnp.max(s, axis=-1, keepdims=True)
    m = jnp.where(m > -jnp.inf, m, 0.0)
    p = jnp.exp(s - m)
    p = p / jnp.maximum(jnp.sum(p, axis=-1, keepdims=True), 1e-30)
    return jnp.dot(p.astype(BF16), v, preferred_element_type=F32)


def _gate_mix(gates, h, o_cmp, o_sel, o_win, tq):
    outs = []
    for g in range(G_C):
        base = (h * G_C + g) * 3
        rs = slice(g * tq, (g + 1) * tq)
        outs.append(gates[:, base:base + 1] * o_cmp[rs] + gates[:, base + 1:base + 2] * o_sel[rs]
                    + gates[:, base + 2:base + 3] * o_win[rs])
    return jnp.concatenate(outs, axis=-1)


def _rope_q(qh, cos, sin):
    return jnp.concatenate(
        [_rope_lanes(qh[:, c * LANE:(c + 1) * LANE], cos, sin) for c in range(G_C * DH_C // LANE)], axis=-1)


SEL_TK = 1024


SEL_PHASE = DH_C


def _nsa_prompt_kernel(q_ref, gz_ref, cos_ref, sin_ref, kc_ref, kv_ref, ka_ref, va_ref, mimp_ref, o_ref,
                       qa_s, m_s, acc_s, *, nsp):
    i = pl.program_id(1)
    tq = QBLK_C
    rows = G_C * tq
    s0 = i * tq
    t_pos = s0 + lax.broadcasted_iota(jnp.int32, (tq, 1), 0)
    gates = jax.nn.sigmoid(gz_ref[0])
    cos, sin = cos_ref[...], sin_ref[...]
    scale = DH_C ** -0.5
    hw = G_C * DH_C
    ncmp = kc_ref.shape[1]
    t_row = s0 + lax.broadcasted_iota(jnp.int32, (1, rows), 1) % tq
    visible = (lax.broadcasted_iota(jnp.int32, (ncmp, 1), 0) * CMP_STRIDE + (CMP_BLK - 1)) <= t_row
    any_visible = t_row >= CMP_BLK - 1
    zpad = jnp.zeros((tq, DH_C), F32)
    kcc = kc_ref[0, :, 0:HKV].astype(BF16)
    vcc = kc_ref[0, :, HKV:2 * HKV].astype(BF16)
    q_rot, o_cmps, imps_t = [], [], []
    for h in range(KVH_C):
        qh = q_ref[0, :, h * hw:(h + 1) * hw] * scale
        q_rot.append(_rope_q(qh, cos, sin))
        qc = jnp.concatenate([jnp.concatenate([x, zpad] if h == 0 else [zpad, x], axis=1)
                              for x in (qh[:, g * DH_C:(g + 1) * DH_C] for g in range(G_C))], axis=0)
        st = jnp.where(visible, _dot_nt(kcc, qc.astype(BF16)), NEG_BIG)
        pt = jnp.exp(st - jnp.max(st, axis=0, keepdims=True))
        inv = jnp.where(any_visible, 1.0 / jnp.maximum(jnp.sum(pt, axis=0, keepdims=True), 1e-30), 0.0)
        pt = pt * inv
        o_cmps.append(_dot_tn(vcc, pt.astype(BF16))[h * DH_C:(h + 1) * DH_C, :].T)
        pgrp_t = pt[:, 0:tq]
        for g in range(1, G_C):
            pgrp_t = pgrp_t + pt[:, g * tq:(g + 1) * tq]
        imps_t.append(_dot_exact_lhs01(mimp_ref[...], pgrp_t))
    t_row2 = s0 + lax.broadcasted_iota(jnp.int32, (1, KVH_C * tq), 1) % tq
    sel_t = _select_blocks_t(jnp.concatenate(imps_t, axis=1), t_row2)

    for h in range(KVH_C):
        selb = jnp.where(sel_t[:, h * tq:(h + 1) * tq].T > 0.5, 0.0, NEG_BIG)
        for ph in range(nsp // SEL_PHASE):
            sb = selb[:, ph * SEL_PHASE:(ph + 1) * SEL_PHASE]
            qa_s[h, ph] = jnp.concatenate(
                [jnp.concatenate([x, sb] if h == 0 else [sb, x], axis=1)
                 for x in (q_rot[h][:, g * DH_C:(g + 1) * DH_C] for g in range(G_C))], axis=0).astype(BF16)
    m_s[...] = jnp.full(m_s.shape, NEG_BIG, F32)
    acc_s[...] = jnp.zeros(acc_s.shape, F32)

    def tile(k0, size, causal):
        k0 = pl.multiple_of(k0, size)
        ph = k0 // (SEL_PHASE * SLC_BLK)
        for h in range(KVH_C):
            st = _dot_nt(ka_ref[0, h, pl.ds(k0, size), :], qa_s[h, ph])
            if causal:
                kpos = k0 + lax.broadcasted_iota(jnp.int32, (size, 1), 0)
                t_row = s0 + lax.broadcasted_iota(jnp.int32, (1, rows), 1) % tq
                st = jnp.where(kpos <= t_row, st, NEG_BIG)
            m_old = m_s[h]
            m_new = jnp.maximum(m_old, jnp.max(st, axis=0, keepdims=True))
            pt = jnp.exp(st - m_new).astype(BF16)
            acc_s[h] = (jnp.exp(m_old - m_new) * acc_s[h]
                        + jnp.dot(va_ref[0, h, :, pl.ds(k0, size)], pt, preferred_element_type=F32))
            m_s[h] = m_new

    def full_tile(kt, carry):
        tile(kt * SEL_TK, SEL_TK, False)
        return carry

    half = SEL_TK // 2
    own = (s0 + tq - 1) // half
    lax.fori_loop(0, own // 2, full_tile, 0)

    @pl.when(own % 2 == 1)
    def _():
        tile((own - 1) * half, half, False)

    tile(own * half, half, True)

    def finish(acc, h):
        sums = acc[(1 - h) * DH_C:(1 - h) * DH_C + 1, :]
        return (acc[h * DH_C:(h + 1) * DH_C, :] / jnp.maximum(sums, 1e-30)).T

    nw = WIN_C + tq
    w0 = pl.multiple_of(jnp.maximum(s0 - WIN_C, 0), tq)
    dlt = (s0 + lax.broadcasted_iota(jnp.int32, (1, rows), 1) % tq) - (w0 + lax.broadcasted_iota(jnp.int32, (nw, 1), 0))
    in_win = (dlt >= 0) & (dlt < WIN_C)
    lane = lax.broadcasted_iota(jnp.int32, (1, LANE), 1) // DH_C
    kwc = kv_ref[0, pl.ds(w0, nw), 0:LANE]
    vwc = kv_ref[0, pl.ds(w0, nw), LANE:2 * LANE]
    for h in range(KVH_C):
        st = _dot_nt(jnp.where(lane == h, kwc, jnp.zeros_like(kwc)), qa_s[h, 0])
        st = jnp.where(in_win, st, NEG_BIG)
        pt = jnp.exp(st - jnp.max(st, axis=0, keepdims=True)).astype(BF16)
        o_win = finish(_dot_tn(jnp.where(lane == h, vwc, jnp.ones_like(vwc)), pt), h)
        o_ref[0, :, h * hw:(h + 1) * hw] = _gate_mix(gates, h, o_cmps[h], finish(acc_s[h], h), o_win, tq)


def _sel_operands(z3):
    s = z3.shape[1]
    lane = jnp.arange(LANE)
    hot = ((jnp.arange(s)[:, None] // SLC_BLK) % SEL_PHASE == lane[None, :] % SEL_PHASE).astype(BF16)
    ksel = z3[:, :, C_SEL:C_SEL + HKV].astype(BF16)
    vsel_t = jnp.swapaxes(z3[:, :, C_SEL + HKV:C_SEL + 2 * HKV].astype(BF16), 1, 2)
    ka = jnp.stack([jnp.where((lane // DH_C == h)[None, None, :], ksel, hot[None]) for h in range(KVH_C)], axis=1)
    va = jnp.stack([jnp.where((lane // DH_C == h)[None, :, None], vsel_t, jnp.ones_like(vsel_t))
                    for h in range(KVH_C)], axis=1)
    return ka, va


def nsa_prompt(z3, kc, kvw, ka, va, cos_t, sin_t, mimp):
    b, s, _ = z3.shape
    nsp = mimp.shape[0]
    nq = s // QBLK_C
    ncmp = kc.shape[1]
    rows = G_C * QBLK_C
    vm = pltpu.VMEM
    scratch = [
        vm((KVH_C, nsp // SEL_PHASE, rows, LANE), BF16),
        vm((KVH_C, 1, rows), F32),
        vm((KVH_C, LANE, rows), F32),
    ]
    return pl.pallas_call(
        functools.partial(_nsa_prompt_kernel, nsp=nsp),
        grid=(b, nq),
        in_specs=[
            pl.BlockSpec((1, QBLK_C, C_Q), lambda bb, i: (bb, i, 0)),
            pl.BlockSpec((1, QBLK_C, LANE), lambda bb, i: (bb, i, C_GATE // LANE)),
            pl.BlockSpec((QBLK_C, LANE), lambda bb, i: (i, 0)),
            pl.BlockSpec((QBLK_C, LANE), lambda bb, i: (i, 0)),
            pl.BlockSpec((1, ncmp, KVW), lambda bb, i: (bb, 0, 0)),
            pl.BlockSpec((1, s, KVW), lambda bb, i: (bb, 0, 0)),
            pl.BlockSpec((1, KVH_C, s, LANE), lambda bb, i: (bb, 0, 0, 0)),
            pl.BlockSpec((1, KVH_C, LANE, s), lambda bb, i: (bb, 0, 0, 0)),
            pl.BlockSpec(mimp.shape, lambda bb, i: (0, 0)),
        ],
        out_specs=pl.BlockSpec((1, QBLK_C, C_Q), lambda bb, i: (bb, i, 0)),
        out_shape=jax.ShapeDtypeStruct((b, s, C_Q), F32),
        scratch_shapes=scratch,
        compiler_params=_cparams(("parallel", "arbitrary")),
        name="nsa_prompt",
    )(z3, z3, cos_t, sin_t, kc, kvw, ka, va, mimp)


def _nsa_sample_kernel(pt_ref, *refs, pg, nsp, past, t_new):
    del pt_ref
    page_refs = refs[:pg]
    (q_ref, gz_ref, cos_ref, sin_ref, kc_ref, mimp_ref, hot_ref, seln_ref, winb_ref, winn_ref, winnt_ref,
     o_ref, wo_ref, q2r_s, qa_s, sel_s, ocmp_s, m_s, l_s, acc_s) = refs[pg:]
    g = pl.program_id(1)
    ng = pl.num_programs(1)
    tq = T_PAD
    rows = G_C * tq
    t_pos = past + lax.broadcasted_iota(jnp.int32, (tq, 1), 0)
    scale = DH_C ** -0.5
    hw = G_C * DH_C
    nk = pg * PAGE_SIZE
    gblk = nk // SLC_BLK

    @pl.when(g == 0)
    def _():
        cos, sin = cos_ref[...], sin_ref[...]
        q2rs, imps = [], []
        for h in range(KVH_C):
            qh = q_ref[0, :, h * hw:(h + 1) * hw] * scale
            q2 = _stack_heads(qh, tq).astype(BF16)
            q2rs.append(_stack_heads(_rope_q(qh, cos, sin), tq))
            q2r_s[h] = q2rs[h].astype(BF16)
            kc = kc_ref[0, :, h * DH_C:(h + 1) * DH_C].astype(BF16)
            vc = kc_ref[0, :, KVH_C * DH_C + h * DH_C:KVH_C * DH_C + (h + 1) * DH_C].astype(BF16)
            o_cmp, pgrp = _cmp_branch(q2, kc, vc, t_pos, tq)
            ocmp_s[h] = o_cmp
            imps.append(_dot_exact_rhs01(pgrp, mimp_ref[...]))
        sel_all = _select_blocks(jnp.concatenate(imps, axis=0), jnp.concatenate([t_pos] * KVH_C, axis=0))
        zpad = [jnp.zeros((rows, DH_C - gblk), F32)] if gblk < DH_C else []
        for h in range(KVH_C):
            sel = sel_all[h * tq:(h + 1) * tq]
            sel_s[h] = sel
            selb = jnp.where(sel > 0.5, 0.0, NEG_BIG)
            for gg in range(qa_s.shape[1]):
                sb = jnp.concatenate([selb[:, gg * gblk:(gg + 1) * gblk]] * G_C, axis=0)
                qa_s[h, gg] = jnp.concatenate([q2rs[h], sb] + zpad, axis=1).astype(BF16)
        m_s[...] = jnp.full(m_s.shape, NEG_BIG, F32)
        l_s[...] = jnp.zeros(l_s.shape, F32)
        acc_s[...] = jnp.zeros(acc_s.shape, F32)

    kvt = jnp.concatenate([r[0] for r in page_refs], axis=1).astype(BF16)
    hot = hot_ref[...]
    for h in range(KVH_C):
        kst = jnp.concatenate([kvt[h * DH_C:(h + 1) * DH_C, :], hot], axis=0)
        vst = kvt[KVH_C * DH_C + h * DH_C:KVH_C * DH_C + (h + 1) * DH_C, :]
        s = jnp.dot(qa_s[h, g], kst, preferred_element_type=F32)
        m_old = m_s[h]
        m_new = jnp.maximum(m_old, jnp.max(s, axis=-1, keepdims=True))
        alpha = jnp.exp(m_old - m_new)
        p = jnp.exp(s - m_new)
        l_s[h] = alpha * l_s[h] + jnp.sum(p, axis=-1, keepdims=True)
        acc_s[h] = alpha * acc_s[h] + _dot_nt(p.astype(BF16), vst)
        m_s[h] = m_new

    @pl.when(g == ng - 1)
    def _():
        gates = jax.nn.sigmoid(gz_ref[0])
        n_buf = winb_ref.shape[2]
        lane_n = lax.broadcasted_iota(jnp.int32, (tq, nsp), 1)
        u = lax.broadcasted_iota(jnp.int32, (tq, tq), 1)
        seln = seln_ref[0]
        winn = winn_ref[0]
        for h in range(KVH_C):
            kcol = h * DH_C
            vcol = KVH_C * DH_C + h * DH_C
            q2r = q2r_s[h]
            new_sel = jnp.max(jnp.where(lane_n == past // SLC_BLK, sel_s[h], 0.0), axis=-1, keepdims=True) > 0.5
            mk = new_sel & (past + u <= t_pos) & (u < t_new)
            _flash_update(_dot_nt(q2r, seln[:, kcol:kcol + DH_C].astype(BF16)), mk,
                          seln[:, vcol:vcol + DH_C].astype(BF16), m_s.at[h], l_s.at[h], acc_s.at[h])
            o_sel = _flash_finish(m_s.at[h], l_s.at[h], acc_s.at[h])
            kbt = winb_ref[0, kcol:kcol + DH_C, :].astype(BF16)
            vbt = winb_ref[0, vcol:vcol + DH_C, :].astype(BF16)
            pos_b = past - n_buf + lax.broadcasted_iota(jnp.int32, (tq, n_buf), 1)
            d_b = t_pos - pos_b
            m_b = (pos_b >= 0) & (d_b >= 0) & (d_b < WIN_C)
            d_n = t_pos - (past + u)
            m_n = (d_n >= 0) & (d_n < WIN_C) & (u < t_new)
            s_b = _mask_rows(jnp.dot(q2r, kbt, preferred_element_type=F32), m_b, -jnp.inf)
            s_n = _mask_rows(_dot_nt(q2r, winn[:, kcol:kcol + DH_C].astype(BF16)), m_n, -jnp.inf)
            mx = jnp.maximum(jnp.max(s_b, axis=-1, keepdims=True), jnp.max(s_n, axis=-1, keepdims=True))
            mx = jnp.where(mx > -jnp.inf, mx, 0.0)
            p_b, p_n = jnp.exp(s_b - mx), jnp.exp(s_n - mx)
            den = jnp.maximum(jnp.sum(p_b, axis=-1, keepdims=True) + jnp.sum(p_n, axis=-1, keepdims=True), 1e-30)
            o_win = (_dot_nt((p_b / den).astype(BF16), vbt)
                     + jnp.dot((p_n / den).astype(BF16), winn[:, vcol:vcol + DH_C].astype(BF16),
                               preferred_element_type=F32))
            o_ref[0, :, h * hw:(h + 1) * hw] = _gate_mix(gates, h, ocmp_s[h], o_sel, o_win, tq)
        wo_ref[0, :, 0:n_buf - t_new] = winb_ref[0, :, t_new:n_buf]
        wo_ref[0, :, n_buf - t_new:n_buf] = winnt_ref[0, :, 0:t_new]


def nsa_sample(z3, kc, sel_pool_t, win_buf_t, win_new_t, page_table, cos_t, sin_t, mimp, *, t_new, pg=32):
    db, n_pages = page_table.shape
    past = n_pages * PAGE_SIZE
    ncmp = kc.shape[1]
    nsp = mimp.shape[1]
    n_buf = win_buf_t.shape[2]
    pg = min(pg, n_pages)
    nk = pg * PAGE_SIZE
    assert n_pages % pg == 0 and nk // SLC_BLK <= DH_C
    rows = G_C * T_PAD
    hot = jnp.asarray(np.arange(nk)[None, :] // SLC_BLK == np.arange(DH_C)[:, None], dtype=BF16)

    def page_spec(j):
        return pl.BlockSpec((1, KVW, PAGE_SIZE), lambda b, g, pt: (pt[b, g * pg + j], 0, 0))

    gs = pltpu.PrefetchScalarGridSpec(
        num_scalar_prefetch=1,
        grid=(db, n_pages // pg),
        in_specs=[page_spec(j) for j in range(pg)] + [
            pl.BlockSpec((1, T_PAD, C_Q), lambda b, g, pt: (b, 0, 0)),
            pl.BlockSpec((1, T_PAD, LANE), lambda b, g, pt: (b, 0, C_GATE // LANE)),
            pl.BlockSpec((T_PAD, LANE), lambda b, g, pt: (0, 0)),
            pl.BlockSpec((T_PAD, LANE), lambda b, g, pt: (0, 0)),
            pl.BlockSpec((1, ncmp, KVW), lambda b, g, pt: (b, 0, 0)),
            pl.BlockSpec(mimp.shape, lambda b, g, pt: (0, 0)),
            pl.BlockSpec((DH_C, nk), lambda b, g, pt: (0, 0)),
            pl.BlockSpec((1, T_PAD, KVW), lambda b, g, pt: (b, 0, C_SEL // KVW)),
            pl.BlockSpec((1, KVW, n_buf), lambda b, g, pt: (b, 0, 0)),
            pl.BlockSpec((1, T_PAD, KVW), lambda b, g, pt: (b, 0, C_WIN // KVW)),
            pl.BlockSpec((1, KVW, T_PAD), lambda b, g, pt: (b, 0, 0)),
        ],
        out_specs=[
            pl.BlockSpec((1, T_PAD, C_Q), lambda b, g, pt: (b, 0, 0)),
            pl.BlockSpec((1, KVW, n_buf), lambda b, g, pt: (b, 0, 0)),
        ],
        scratch_shapes=[
            pltpu.VMEM((KVH_C, rows, DH_C), BF16),
            pltpu.VMEM((KVH_C, n_pages // pg, rows, LANE), BF16),
            pltpu.VMEM((KVH_C, T_PAD, nsp), F32),
            pltpu.VMEM((KVH_C, rows, DH_C), F32),
            pltpu.VMEM((KVH_C, rows, 1), F32),
            pltpu.VMEM((KVH_C, rows, 1), F32),
            pltpu.VMEM((KVH_C, rows, DH_C), F32),
        ],
    )
    return pl.pallas_call(
        functools.partial(_nsa_sample_kernel, pg=pg, nsp=nsp, past=past, t_new=t_new),
        grid_spec=gs,
        out_shape=[jax.ShapeDtypeStruct((db, T_PAD, C_Q), F32), jax.ShapeDtypeStruct((db, KVW, n_buf), F32)],
        compiler_params=_cparams(("parallel", "arbitrary")),
        name="nsa_sample",
    )(page_table, *([sel_pool_t] * pg), z3, z3, cos_t, sin_t, kc, mimp, hot, z3, win_buf_t, z3, win_new_t)


def _rope_tables(pos):
    half = DH_C // 2
    inv = 1.0 / (ROPE_THETA ** (jnp.arange(half, dtype=F32) / half))
    ang = pos.astype(F32)[:, None] * inv[None, :]
    cos, sin = jnp.cos(ang), jnp.sin(ang)
    return jnp.tile(cos, (1, 4)), jnp.concatenate([-sin, sin, -sin, sin], axis=-1)


def _prep_w_in_ab(w):
    cuts = np.cumsum([A_QW, A_QW, A_QW, B_W, B_W, B_W, H_B, H_B])
    qa, ka, va, qb, kb, vb, ig, fg, og = jnp.split(w, cuts, axis=-1)
    padg = lambda t: jnp.pad(t, ((0, 0), (0, LANE - H_B)))
    return jnp.concatenate([qb, kb, vb, og, qa, ka, va, padg(ig), padg(fg)], axis=-1).astype(BF16)


def _ab_rope_flags():
    f = np.zeros((AB_N // LANE,), np.int32)
    f[AB_QA // LANE:AB_VA // LANE] = 1
    return jnp.asarray(f)


def _c_rope_flags():
    f = np.zeros((C_N // LANE,), np.int32)
    f[C_SEL // LANE] = 1
    f[C_WIN // LANE] = 1
    return jnp.asarray(f)


def _prep_compress(w1, w2, pe):
    e2 = jnp.eye(KVH_C, dtype=F32)

    def half(w1h):
        return jnp.einsum('krde,hH->krhdHe', w1h, e2).reshape(2, CMP_STRIDE, HKV, HKV).astype(BF16)

    wab = jnp.concatenate([half(w1[:, :CMP_STRIDE]), half(w1[:, CMP_STRIDE:])], axis=-1)
    w2b = jnp.einsum('ked,hH->kheHd', w2, e2).reshape(2, HKV, HKV).astype(BF16)

    def pe_half(p):
        return jnp.broadcast_to(p[:, :, None, :], (2, CMP_STRIDE, KVH_C, DH_C)).reshape(2, 1, CMP_STRIDE * HKV)

    bias = compress_bias(pe_half(pe[:, :CMP_STRIDE]), pe_half(pe[:, CMP_STRIDE:]),
                         wab.reshape(2, CMP_STRIDE * HKV, 2 * HKV))
    return (wab, w2b), bias


def _channel_major(x, lead):
    perm = tuple(range(lead)) + (lead + 1, lead + 2, lead + 3, lead)
    xt = jnp.transpose(x, perm)
    return xt.reshape(x.shape[:lead] + (x.shape[lead + 1] * x.shape[lead + 2] * x.shape[lead + 3], x.shape[lead]))


def _row_major(xt, c0, c1, c2):
    lead, _, rows = xt.shape
    return jnp.transpose(xt.reshape(lead, c0, c1, c2, rows), (0, 4, 1, 2, 3))


def _importance_matrix(n_rows, n_cmp, n_slc, n_cols):
    ratio = SLC_BLK // CMP_STRIDE
    m = np.zeros((n_rows, n_cols), np.float32)
    for jblk in range(n_slc):
        for off in range(1 - CMP_BLK // CMP_STRIDE, ratio):
            i = ratio * jblk + off
            if 0 <= i < n_cmp:
                m[i, jblk] = 1.0
    return jnp.asarray(m, dtype=BF16)


def _gates_t(z3, L):
    b, s, _ = z3.shape
    g = jnp.concatenate([z3[..., AB_IG:AB_IG + H_B], z3[..., AB_FG:AB_FG + H_B]], axis=-1)
    return g.reshape(b, s // L, L, 2 * H_B).transpose(0, 1, 3, 2)


def kernel(x_prompt, x_sample, cache_a0_kv, cache_a1_kv, cache_a2_kv, state_b_C, state_b_n, state_b_m,
           cache_c_cmp_kv, cache_c_sel_kv, cache_c_win_kv, page_table, norm_g, w_in_ab, b_if, g_mlstm,
           w_out_ab, w_in_c, cmp_w1, cmp_w2, cmp_pe, w_out_c, w_ffn_gate, w_ffn_up, w_ffn_down, norm_final):
    B, S, D = x_prompt.shape
    DB, T, _ = x_sample.shape
    depth = norm_g.shape[0]
    n_pages = page_table.shape[1]
    past = n_pages * PAGE_SIZE
    caches_a = (cache_a0_kv, cache_a1_kv, cache_a2_kv)
    assert T <= T_PAD and S % (DIL_CFG[-1][1] * BLK_A) == 0 and S >= WIN_C + QBLK_C

    hp = x_prompt.reshape(B * S, D)
    hs = jnp.pad(x_sample, ((0, 0), (0, T_PAD - T), (0, 0))).reshape(DB * T_PAD, D)

    pos_p = jnp.arange(S)
    pos_s = past + jnp.arange(T_PAD)
    cos_p1, sin_p1 = _rope_tables(pos_p)
    cos_s1, sin_s1 = _rope_tables(pos_s)
    cos_p, sin_p = jnp.tile(cos_p1, (B, 1)), jnp.tile(sin_p1, (B, 1))
    cos_s, sin_s = jnp.tile(cos_s1, (DB, 1)), jnp.tile(sin_s1, (DB, 1))

    a_p, a_s = [[], [], []], [[], [], []]
    bC_p, bC_s, bn_p, bn_s, bm_p, bm_s = [], [], [], [], [], []
    cc_p, cc_s, csl_p, csl_s, cw_p, cw_s = [], [], [], [], [], []

    for layer in range(depth):
        if layer % 2 == 0:
            e = layer // 2
            w_in = _prep_w_in_ab(w_in_ab[e])
            flags = _ab_rope_flags()
            w_out = w_out_ab[e].astype(BF16)
            z = norm_proj(hp, norm_g[layer, 0], w_in, flags, cos_p, sin_p)
            z3 = z.reshape(B, S, AB_N)
            os_, ls_ = [], []
            for gi, (win, dil) in enumerate(DIL_CFG):
                o, l = dil_prompt(z3, gi, win, dil)
                os_.append(o)
                ls_.append(l)
                nb = min(win, S)
                kk = z3[:, S - nb:, AB_KA + gi * AW:AB_KA + (gi + 1) * AW].reshape(B, nb, H_A, DH_A)
                vv = z3[:, S - nb:, AB_VA + gi * AW:AB_VA + (gi + 1) * AW].reshape(B, nb, H_A, DH_A)
                a_p[gi].append(jnp.stack([kk, vv], axis=2))
            zc = jnp.zeros
            hb, Cp, n_p, m_p = mlstm(z3, _gates_t(z3, MLSTM_CHUNK), b_if[e],
                                     zc((B, H_B, DH_B, DH_B), F32), zc((B, H_B, DH_B), F32), zc((B, H_B), F32),
                                     L=MLSTM_CHUNK, n_valid=MLSTM_CHUNK, bb=B if B <= 2 else 1)
            hp = ab_merge(os_, ls_, hb, z, g_mlstm[e], w_out, hp)
            bC_p.append(Cp); bn_p.append(n_p); bm_p.append(m_p)
            z = norm_proj(hs, norm_g[layer, 0], w_in, flags, cos_s, sin_s)
            z3 = z.reshape(DB, T_PAD, AB_N)
            os_, ls_ = [], []
            for gi, (win, dil) in enumerate(DIL_CFG):
                new_t = jnp.concatenate(
                    [jnp.swapaxes(z3[:, :, AB_KA + gi * AW:AB_KA + (gi + 1) * AW], 1, 2),
                     jnp.swapaxes(z3[:, :, AB_VA + gi * AW:AB_VA + (gi + 1) * AW], 1, 2)], axis=1)
                o, l, co = dil_sample(_channel_major(caches_a[gi][e], 1), z3, new_t, gi, win, dil, T)
                os_.append(o)
                ls_.append(l)
                a_s[gi].append(_row_major(co, 2, H_A, DH_A))
            bbs = 4 if DB % 4 == 0 else 1
            hb, Cs, n_s, m_s = mlstm(z3, _gates_t(z3, T_PAD), b_if[e], state_b_C[e], state_b_n[e], state_b_m[e],
                                     L=T_PAD, n_valid=T, bb=bbs)
            hs = ab_merge(os_, ls_, hb, z, g_mlstm[e], w_out, hs)
            bC_s.append(Cs); bn_s.append(n_s); bm_s.append(m_s)
        else:
            o_i = layer // 2
            w_in = jnp.pad(w_in_c[o_i], ((0, 0), (0, C_N - w_in_c.shape[-1]))).astype(BF16)
            flags = _c_rope_flags()
            w_out = w_out_c[o_i].astype(BF16)
            cw, cbias = _prep_compress(cmp_w1[o_i], cmp_w2[o_i], cmp_pe[o_i])
            z = norm_proj(hp, norm_g[layer, 0], w_in, flags, cos_p, sin_p)
            z3 = z.reshape(B, S, C_N)
            kv_cmp = z3[:, :, C_CMP:C_CMP + KVW]
            kv_sel = z3[:, :, C_SEL:C_SEL + KVW]
            kv_win = z3[:, :, C_WIN:C_WIN + KVW]
            kc = compress_prompt(z3, cw, cbias)
            n_cmp = (S - CMP_BLK) // CMP_STRIDE + 1
            n_slc = S // SLC_BLK
            mimp = _importance_matrix(S // CMP_STRIDE, n_cmp, n_slc, -(-n_slc // LANE) * LANE)
            ka, va = _sel_operands(z3)
            o = nsa_prompt(z3, kc, kv_win.astype(BF16), ka, va, cos_p1, sin_p1, mimp.T)
            hp = out_proj(o.reshape(B * S, C_Q), w_out, hp)
            nw = min(WIN_C, S)
            sh = lambda t: t.reshape(t.shape[0], t.shape[1], 2, KVH_C, DH_C)
            cc_p.append(sh(kv_cmp)); csl_p.append(sh(kv_sel)); cw_p.append(sh(kv_win[:, S - nw:]))
            z = norm_proj(hs, norm_g[layer, 0], w_in, flags, cos_s, sin_s)
            z3 = z.reshape(DB, T_PAD, C_N)
            kc = compress_paged(_channel_major(cache_c_cmp_kv[o_i], 1), page_table, cw, cbias)
            full_len = past + T
            n_cmp = (full_len - CMP_BLK) // CMP_STRIDE + 1
            assert (n_cmp + 1) * CMP_STRIDE <= past
            n_slc = past // SLC_BLK + -(-T // SLC_BLK)
            nsp = -(-n_slc // LANE) * LANE
            mimp = _importance_matrix(past // CMP_STRIDE, n_cmp, n_slc, nsp)
            o, wo = nsa_sample(z3, kc, _channel_major(cache_c_sel_kv[o_i], 1),
                               _channel_major(cache_c_win_kv[o_i], 1),
                               jnp.swapaxes(z3[:, :, C_WIN:C_WIN + KVW], 1, 2),
                               page_table, cos_s1, sin_s1, mimp, t_new=T)
            hs = out_proj(o.reshape(DB * T_PAD, C_Q), w_out, hs)
            cc_s.append(sh(z3[:, :T, C_CMP:C_CMP + KVW])); csl_s.append(sh(z3[:, :T, C_SEL:C_SEL + KVW]))
            cw_s.append(_row_major(wo, 2, KVH_C, DH_C))
        last = layer == depth - 1
        wg, wu, wd = (w_ffn_gate[layer].astype(BF16), w_ffn_up[layer].astype(BF16), w_ffn_down[layer].astype(BF16))
        hp = ffn(hp, norm_g[layer, 1], wg, wu, wd, norm_final, final_norm=last)
        hs = ffn(hs, norm_g[layer, 1], wg, wu, wd, norm_final, final_norm=last)

    y_prompt = hp.reshape(B, S, D)
    y_sample = hs.reshape(DB, T_PAD, D)[:, :T]
    st = lambda xs: jnp.stack(xs, axis=0)
    return (y_prompt, y_sample,
            st(a_p[0]), st(a_s[0]), st(a_p[1]), st(a_s[1]), st(a_p[2]), st(a_s[2]),
            st(bC_p), st(bC_s), st(bn_p), st(bn_s), st(bm_p), st(bm_s),
            st(cc_p), st(cc_s), st(csl_p), st(csl_s), st(cw_p), st(cw_s))
```

```python
import functools
import math

import numpy as np
import jax
import jax.numpy as jnp
from jax import lax
from jax.experimental import pallas as pl
from jax.experimental.pallas import tpu as pltpu

F32 = jnp.float32
BF16 = jnp.bfloat16

PAGE_SIZE = 128
DIL_CFG = ((128, 1), (512, 4), (2048, 16))
N_DIL = 3
H_A = 4
DH_A = 64
BLK_A = 128
H_B = 4
DH_B = 128
MLSTM_CHUNK = 128
H_C = 16
KVH_C = 2
G_C = H_C // KVH_C
DH_C = 64
CMP_STRIDE = 16
CMP_BLK = 2 * CMP_STRIDE
CMP_HID = 64
SLC_BLK = 64
N_SEL = 16
WIN_C = 512
QBLK_C = 128
ROPE_THETA = 10000.0
EPS = 1e-6
A_QW = N_DIL * H_A * DH_A
B_W = H_B * DH_B
AW = H_A * DH_A
C_Q = H_C * DH_C
C_KV = 3 * 2 * KVH_C * DH_C
KVW = 2 * KVH_C * DH_C

LANE = 128
SUBLANE = 8
VMEM_LIMIT = 48 * 1024 * 1024

NEG_BIG = -1e30
T_PAD = SUBLANE

AB_QB, AB_KB, AB_VB, AB_OG = 0, B_W, 2 * B_W, 3 * B_W
AB_QA = 4 * B_W
AB_KA = AB_QA + A_QW
AB_VA = AB_KA + A_QW
AB_IG = AB_VA + A_QW
AB_FG = AB_IG + LANE
AB_N = AB_FG + LANE

C_CMP = C_Q
C_SEL = C_Q + KVW
C_WIN = C_Q + 2 * KVW
C_GATE = C_Q + 3 * KVW
C_N = 2048


def _cparams(sem, vmem=VMEM_LIMIT):
    return pltpu.CompilerParams(dimension_semantics=sem, vmem_limit_bytes=vmem)


def _pick_tile(m, pref):
    t = min(m, pref)
    while m % t:
        t //= 2
    return t


def _rope_lanes(x, cos, sin):
    lane = lax.broadcasted_iota(jnp.int32, x.shape, 1)
    first = (lane % DH_C) < (DH_C // 2)
    partner = jnp.where(first, pltpu.roll(x, LANE - DH_C // 2, 1), pltpu.roll(x, DH_C // 2, 1))
    return x * cos + partner * sin


def _split3(x):
    hi = x.astype(BF16)
    r1 = x - hi.astype(F32)
    mid = r1.astype(BF16)
    lo = (r1 - mid.astype(F32)).astype(BF16)
    return hi, mid, lo


def _dot_exact_rhs01(x, m01):
    hi, mid, lo = _split3(x)
    d = lambda a: jnp.dot(a, m01, preferred_element_type=F32)
    return d(hi) + d(mid) + d(lo)


def _dot_exact_lhs01(m01, x):
    hi, mid, lo = _split3(x)
    d = lambda a: jnp.dot(m01, a, preferred_element_type=F32)
    return d(hi) + d(mid) + d(lo)


def _dot_nt(a, b):
    return lax.dot_general(a, b, (((1,), (1,)), ((), ())), preferred_element_type=F32)


def _dot_tn(a, b):
    return lax.dot_general(a, b, (((0,), (0,)), ((), ())), preferred_element_type=F32)


def _log_sigmoid(x):
    return jnp.minimum(x, 0.0) - jnp.log1p(jnp.exp(-jnp.abs(x)))


def _gelu_tanh(x):
    return 0.5 * x * (1.0 + jnp.tanh(math.sqrt(2.0 / math.pi) * (x + 0.044715 * (x * x * x))))


def _rms_rows(x, g):
    ms = jnp.mean(x * x, axis=-1, keepdims=True)
    return x * lax.rsqrt(ms + EPS) * g


def _norm_proj_kernel(flags_ref, x_ref, g_ref, w_ref, cos_ref, sin_ref, o_ref, xn_ref, *, tn):
    j = pl.program_id(1)

    @pl.when(j == 0)
    def _():
        xn_ref[...] = _rms_rows(x_ref[...], g_ref[...]).astype(BF16)

    o_ref[...] = jnp.dot(xn_ref[...], w_ref[...], preferred_element_type=F32)
    nchunk = tn // LANE
    for c in range(nchunk):
        @pl.when(flags_ref[j * nchunk + c] == 1)
        def _():
            cs = slice(c * LANE, (c + 1) * LANE)
            o_ref[:, cs] = _rope_lanes(o_ref[:, cs], cos_ref[...], sin_ref[...])


def norm_proj(x, g, w_bf16, rope_flags, cos_t, sin_t, *, tm_pref=1024, tn=512):
    m, d = x.shape
    n = w_bf16.shape[1]
    tm = _pick_tile(m, tm_pref)
    grid = (m // tm, n // tn)
    gs = pltpu.PrefetchScalarGridSpec(
        num_scalar_prefetch=1,
        grid=grid,
        in_specs=[
            pl.BlockSpec((tm, d), lambda i, j, f: (i, 0)),
            pl.BlockSpec((1, d), lambda i, j, f: (0, 0)),
            pl.BlockSpec((d, tn), lambda i, j, f: (0, j)),
            pl.BlockSpec((tm, LANE), lambda i, j, f: (i, 0)),
            pl.BlockSpec((tm, LANE), lambda i, j, f: (i, 0)),
        ],
        out_specs=pl.BlockSpec((tm, tn), lambda i, j, f: (i, j)),
        scratch_shapes=[pltpu.VMEM((tm, d), BF16)],
    )
    return pl.pallas_call(
        functools.partial(_norm_proj_kernel, tn=tn),
        grid_spec=gs,
        out_shape=jax.ShapeDtypeStruct((m, n), F32),
        compiler_params=_cparams(("parallel", "arbitrary")),
        name="norm_proj",
    )(rope_flags, x, g.reshape(1, d), w_bf16, cos_t, sin_t)


def _ffn_kernel(x_ref, g_ref, wg_ref, wu_ref, wd_ref, gf_ref, o_ref, xn_ref, *, final_norm):
    f = pl.program_id(1)

    @pl.when(f == 0)
    def _():
        x = x_ref[...]
        xn_ref[...] = _rms_rows(x, g_ref[...]).astype(BF16)
        o_ref[...] = x

    xn = xn_ref[...]
    a = jnp.dot(xn, wg_ref[...], preferred_element_type=F32)
    u = jnp.dot(xn, wu_ref[...], preferred_element_type=F32)
    act = (a * jax.nn.sigmoid(a)) * u
    o_ref[...] += jnp.dot(act.astype(BF16), wd_ref[...], preferred_element_type=F32)

    if final_norm:
        @pl.when(f == pl.num_programs(1) - 1)
        def _():
            o_ref[...] = _rms_rows(o_ref[...], gf_ref[...])


def ffn(x, g, wg, wu, wd, g_final, *, final_norm, tm_pref=512):
    m, d = x.shape
    dff = wg.shape[1]
    tf = dff // 2 if (dff // 2) % LANE == 0 else dff
    tm = _pick_tile(m, tm_pref)
    grid = (m // tm, dff // tf)
    return pl.pallas_call(
        functools.partial(_ffn_kernel, final_norm=final_norm),
        grid=grid,
        in_specs=[
            pl.BlockSpec((tm, d), lambda i, f: (i, 0)),
            pl.BlockSpec((1, d), lambda i, f: (0, 0)),
            pl.BlockSpec((d, tf), lambda i, f: (0, f)),
            pl.BlockSpec((d, tf), lambda i, f: (0, f)),
            pl.BlockSpec((tf, d), lambda i, f: (f, 0)),
            pl.BlockSpec((1, d), lambda i, f: (0, 0)),
        ],
        out_specs=pl.BlockSpec((tm, d), lambda i, f: (i, 0)),
        out_shape=jax.ShapeDtypeStruct((m, d), F32),
        scratch_shapes=[pltpu.VMEM((tm, d), BF16)],
        compiler_params=_cparams(("parallel", "arbitrary")),
        name="ffn",
    )(x, g.reshape(1, d), wg, wu, wd, g_final.reshape(1, d))


def _out_proj_kernel(x_ref, w_ref, r_ref, o_ref):
    o_ref[...] = r_ref[...] + jnp.dot(x_ref[...].astype(BF16), w_ref[...], preferred_element_type=F32)


def out_proj(x, w_bf16, resid, *, tm_pref=512):
    m, k = x.shape
    n = w_bf16.shape[1]
    tm = _pick_tile(m, tm_pref)
    return pl.pallas_call(
        _out_proj_kernel,
        grid=(m // tm,),
        in_specs=[
            pl.BlockSpec((tm, k), lambda i: (i, 0)),
            pl.BlockSpec((k, n), lambda i: (0, 0)),
            pl.BlockSpec((tm, n), lambda i: (i, 0)),
        ],
        out_specs=pl.BlockSpec((tm, n), lambda i: (i, 0)),
        out_shape=jax.ShapeDtypeStruct((m, n), F32),
        compiler_params=_cparams(("parallel",)),
        name="out_proj",
    )(x, w_bf16, resid)


DIL_STEP_ROWS = 2048
DIL_UNROLL = 4


def _dil_prompt_kernel(q_ref, kp_ref, kc_ref, vp_ref, vc_ref, o_ref, l_ref, *, nback, dil, nblk):
    n = pl.program_id(2)
    span = dil * BLK_A
    qi = BLK_A + lax.broadcasted_iota(jnp.int32, (BLK_A, 2 * BLK_A), 0)
    ki = lax.broadcasted_iota(jnp.int32, (BLK_A, 2 * BLK_A), 1)
    rel = qi - ki
    band = (rel >= 0) & (rel <= nback)
    own = ki >= BLK_A

    def one(it, carry):
        j = it // dil
        r = it % dil
        start = j * span + r
        if dil > 1:
            take = lambda st: pl.ds(st, BLK_A, stride=dil)
        else:
            take = lambda st: pl.ds(pl.multiple_of(st, BLK_A), BLK_A)
        rows = take(start)
        prev_rows = take(jnp.maximum(start - span, r))
        first_rows = take(r)
        q = q_ref[0, rows, :] * (DH_A ** -0.5)
        kprev = jnp.where(j > 0, kc_ref[0, prev_rows, :], kp_ref[0, first_rows, :])
        vprev = jnp.where(j > 0, vc_ref[0, prev_rows, :], vp_ref[0, first_rows, :])
        kk = jnp.concatenate([kprev, kc_ref[0, rows, :]], axis=0)
        vv = jnp.concatenate([vprev, vc_ref[0, rows, :]], axis=0)
        mask = band & ((n * nblk + j > 0) | own)
        outs, lses = [], []
        for h in range(LANE // DH_A):
            hs = slice(h * DH_A, (h + 1) * DH_A)
            s = _dot_nt(q[:, hs].astype(BF16), kk[:, hs].astype(BF16))
            s = jnp.where(mask, s, -jnp.inf)
            m = jnp.max(s, axis=-1, keepdims=True)
            p = jnp.exp(s - m)
            l = jnp.sum(p, axis=-1, keepdims=True)
            o = jnp.dot(p.astype(BF16), vv[:, hs].astype(BF16), preferred_element_type=F32) / l
            outs.append(o)
            lses.append(jnp.broadcast_to(m + jnp.log(l), (BLK_A, DH_A)))
        o_ref[0, rows, :] = jnp.concatenate(outs, axis=-1)
        l_ref[0, rows, :] = jnp.concatenate(lses, axis=-1)
        return carry

    lax.fori_loop(0, nblk * dil, one, 0, unroll=DIL_UNROLL)


def dil_prompt(z3, gi, window, dil):
    b, s, _ = z3.shape
    nback = window // dil
    span = dil * BLK_A
    nblk = max(1, DIL_STEP_ROWS // span)
    rows = nblk * span
    assert s % rows == 0 and (nblk * dil) % DIL_UNROLL == 0
    nh = AW // LANE
    qo, ko, vo = (AB_QA + gi * AW) // LANE, (AB_KA + gi * AW) // LANE, (AB_VA + gi * AW) // LANE
    blk = (1, rows, LANE)
    pblk = (1, span, LANE)
    in_specs = [
        pl.BlockSpec(blk, lambda bb, hp, i: (bb, i, qo + hp)),
        pl.BlockSpec(pblk, lambda bb, hp, i: (bb, jnp.maximum(i * nblk - 1, 0), ko + hp)),
        pl.BlockSpec(blk, lambda bb, hp, i: (bb, i, ko + hp)),
        pl.BlockSpec(pblk, lambda bb, hp, i: (bb, jnp.maximum(i * nblk - 1, 0), vo + hp)),
        pl.BlockSpec(blk, lambda bb, hp, i: (bb, i, vo + hp)),
    ]
    out_spec = pl.BlockSpec(blk, lambda bb, hp, i: (bb, i, hp))
    o, l = pl.pallas_call(
        functools.partial(_dil_prompt_kernel, nback=nback, dil=dil, nblk=nblk),
        grid=(b, nh, s // rows),
        in_specs=in_specs,
        out_specs=[out_spec, out_spec],
        out_shape=[jax.ShapeDtypeStruct((b, s, AW), F32)] * 2,
        compiler_params=_cparams(("parallel", "parallel", "arbitrary")),
        name="dil_prompt_%d" % gi,
    )(z3, z3, z3, z3, z3)
    return o.reshape(b * s, AW), l.reshape(b * s, AW)


def _dil_sample_kernel(c_ref, q_ref, kn_ref, vn_ref, nt_ref, o_ref, l_ref, co_ref, *, window, dil, n_buf, t_new):
    rows = H_A * T_PAD
    q = q_ref[0] * (DH_A ** -0.5)
    q4 = jnp.concatenate([q] * H_A, axis=0)
    rr = lax.broadcasted_iota(jnp.int32, (rows, AW), 0)
    ll = lax.broadcasted_iota(jnp.int32, (rows, AW), 1)
    head_sel = (rr // T_PAD) == (ll // DH_A)
    qbd = jnp.where(head_sel, q4, 0.0).astype(BF16)
    kbt = c_ref[0, 0:AW, :].astype(BF16)
    vbt = c_ref[0, AW:2 * AW, :].astype(BF16)
    kn = kn_ref[0]
    vn = vn_ref[0]
    s_buf = jnp.dot(qbd, kbt, preferred_element_type=F32)
    s_new = _dot_nt(qbd, kn.astype(BF16))
    t_b = lax.broadcasted_iota(jnp.int32, (rows, n_buf), 0) % T_PAD
    c_b = lax.broadcasted_iota(jnp.int32, (rows, n_buf), 1)
    d_b = n_buf + t_b - c_b
    m_b = ((d_b % dil) == 0) & (d_b <= window)
    t_n = lax.broadcasted_iota(jnp.int32, (rows, T_PAD), 0) % T_PAD
    u_n = lax.broadcasted_iota(jnp.int32, (rows, T_PAD), 1)
    d_n = t_n - u_n
    m_n = (d_n >= 0) & ((d_n % dil) == 0) & (d_n <= window) & (u_n < t_new)
    s_buf = jnp.where(m_b, s_buf, NEG_BIG)
    s_new = jnp.where(m_n, s_new, NEG_BIG)
    mx = jnp.maximum(jnp.max(s_buf, axis=-1, keepdims=True), jnp.max(s_new, axis=-1, keepdims=True))
    p_b = jnp.where(m_b, jnp.exp(s_buf - mx), 0.0)
    p_n = jnp.where(m_n, jnp.exp(s_new - mx), 0.0)
    l = jnp.sum(p_b, axis=-1, keepdims=True) + jnp.sum(p_n, axis=-1, keepdims=True)
    l = jnp.maximum(l, 1e-30)
    acc = (_dot_nt(p_b.astype(BF16), vbt)
           + jnp.dot(p_n.astype(BF16), vn.astype(BF16), preferred_element_type=F32))
    res = jnp.where(head_sel, acc / l, 0.0)
    lse = jnp.where(head_sel, mx + jnp.log(l), 0.0)
    o = res[0:T_PAD]
    ls = lse[0:T_PAD]
    for h in range(1, H_A):
        o = o + res[h * T_PAD:(h + 1) * T_PAD]
        ls = ls + lse[h * T_PAD:(h + 1) * T_PAD]
    o_ref[0] = o
    l_ref[0] = ls
    co_ref[0, :, 0:n_buf - t_new] = c_ref[0, :, t_new:n_buf]
    co_ref[0, :, n_buf - t_new:n_buf] = nt_ref[0, :, 0:t_new]


def dil_sample(cache_t, z3, new_t, gi, window, dil, t_new):
    db, _, n_buf = cache_t.shape
    qo, ko, vo = AB_QA // AW + gi, AB_KA // AW + gi, AB_VA // AW + gi
    blk = (1, T_PAD, AW)
    o, l, co = pl.pallas_call(
        functools.partial(_dil_sample_kernel, window=window, dil=dil, n_buf=n_buf, t_new=t_new),
        grid=(db,),
        in_specs=[
            pl.BlockSpec((1, 2 * AW, n_buf), lambda b: (b, 0, 0)),
            pl.BlockSpec(blk, lambda b: (b, 0, qo)),
            pl.BlockSpec(blk, lambda b: (b, 0, ko)),
            pl.BlockSpec(blk, lambda b: (b, 0, vo)),
            pl.BlockSpec((1, 2 * AW, T_PAD), lambda b: (b, 0, 0)),
        ],
        out_specs=[
            pl.BlockSpec(blk, lambda b: (b, 0, 0)),
            pl.BlockSpec(blk, lambda b: (b, 0, 0)),
            pl.BlockSpec((1, 2 * AW, n_buf), lambda b: (b, 0, 0)),
        ],
        out_shape=[
            jax.ShapeDtypeStruct((db, T_PAD, AW), F32),
            jax.ShapeDtypeStruct((db, T_PAD, AW), F32),
            jax.ShapeDtypeStruct((db, 2 * AW, n_buf), F32),
        ],
        compiler_params=_cparams(("parallel",)),
        name="dil_sample_%d" % gi,
    )(cache_t, z3, z3, z3, new_t)
    return o.reshape(db * T_PAD, AW), l.reshape(db * T_PAD, AW), co


def _mlstm_kernel(q_ref, k_ref, v_ref, gi_ref, gf_ref, gt_ref, bi_ref, bf_ref, b8_ref,
                  c0_ref, n0_ref, m0_ref, h_ref, co_ref, no_ref, mo_ref,
                  c_s, n_s, m_s, *, bb, L, n_valid):
    c = pl.program_id(1)

    @pl.when(c == 0)
    def _():
        c_s[...] = c0_ref[...]
        n_s[...] = n0_ref[...]
        m_s[...] = m0_ref[...]

    row = lax.broadcasted_iota(jnp.int32, (L, L), 0)
    col = lax.broadcasted_iota(jnp.int32, (L, L), 1)
    tri = row >= col
    tri_l = jnp.where(tri, 1.0, 0.0).astype(BF16)
    tri_u = jnp.where(row <= col, 1.0, 0.0).astype(BF16)
    lane = lax.broadcasted_iota(jnp.int32, (1, LANE), 1)
    for b in range(bb):
        ig_col = gi_ref[b] + bi_ref[...]
        lf_col = _log_sigmoid(gf_ref[b] + bf_ref[...])
        gt = gt_ref[b, 0]
        ig_row = gt[0:H_B] + b8_ref[0:H_B]
        lf_row = _log_sigmoid(gt[H_B:2 * H_B] + b8_ref[H_B:2 * H_B])
        if n_valid < L:
            rv = lax.broadcasted_iota(jnp.int32, (L, LANE), 0) < n_valid
            ig_col = jnp.where(rv, ig_col, -jnp.inf)
            lf_col = jnp.where(rv, lf_col, 0.0)
            cv = lax.broadcasted_iota(jnp.int32, (H_B, L), 1) < n_valid
            ig_row = jnp.where(cv, ig_row, -jnp.inf)
            lf_row = jnp.where(cv, lf_row, 0.0)
        b_col = _dot_exact_lhs01(tri_l, lf_col)
        b_row = _dot_exact_rhs01(lf_row, tri_u)
        m_row = m_s[b]
        a_col = b_col + m_row
        m_new = m_row
        hs_out = []
        for h in range(H_B):
            hsl = slice(h * DH_B, (h + 1) * DH_B)
            bc = b_col[:, h:h + 1]
            ac = a_col[:, h:h + 1]
            icol = ig_col[:, h:h + 1]
            D = bc - b_row[h:h + 1, :] + ig_row[h:h + 1, :]
            D = jnp.where(tri, D, -jnp.inf)
            mt = jnp.maximum(ac, jnp.max(D, axis=-1, keepdims=True))
            Dw = jnp.exp(D - mt)
            iw = jnp.exp(ac - mt)
            qf = q_ref[b, :, hsl]
            kf = k_ref[b, :, hsl] * (DH_B ** -0.5)
            vf = v_ref[b, :, hsl]
            qb, kb, vb = qf.astype(BF16), kf.astype(BF16), vf.astype(BF16)
            Cm = c_s[b * H_B + h]
            nv = n_s[b * H_B + h]
            sc = _dot_nt(qb, kb) * Dw
            num = iw * _dot_nt(qb, Cm.astype(BF16)) + jnp.dot(sc.astype(BF16), vb, preferred_element_type=F32)
            den = iw * jnp.sum(qf * nv, axis=-1, keepdims=True) + jnp.sum(sc, axis=-1, keepdims=True)
            hs_out.append(num / jnp.maximum(jnp.abs(den), jnp.exp(-mt)))
            mL = mt[L - 1:L, :]
            wL = jnp.exp(bc[L - 1:L, :] - bc + icol - mL)
            dec = jnp.exp(ac[L - 1:L, :] - mL)
            c_s[b * H_B + h] = dec * Cm + _dot_tn((vf * wL).astype(BF16), kb)
            n_s[b * H_B + h] = dec * nv + jnp.sum(wL * kf, axis=0, keepdims=True)
            m_new = jnp.where(lane == h, mL, m_new)
        m_s[b] = m_new
        h_ref[b] = jnp.concatenate(hs_out, axis=-1)

    @pl.when(c == pl.num_programs(1) - 1)
    def _():
        co_ref[...] = c_s[...]
        no_ref[...] = n_s[...]
        mo_ref[...] = m_s[...]


def mlstm(z3, gt, b_if, c0, n0, m0, *, L, n_valid, bb):
    b, s, _ = z3.shape
    nc = s // L
    bi_row = jnp.zeros((1, LANE), F32).at[0, :H_B].set(b_if[0])
    bf_row = jnp.zeros((1, LANE), F32).at[0, :H_B].set(b_if[1])
    b8 = b_if.reshape(2 * H_B, 1)
    c0r = c0.reshape(b * H_B, DH_B, DH_B)
    n0r = n0.reshape(b * H_B, 1, DH_B)
    m0r = jnp.zeros((b, 1, LANE), F32).at[:, 0, :H_B].set(m0)
    cw = B_W // LANE
    h, co, no, mo = pl.pallas_call(
        functools.partial(_mlstm_kernel, bb=bb, L=L, n_valid=n_valid),
        grid=(b // bb, nc),
        in_specs=[
            pl.BlockSpec((bb, L, B_W), lambda g, c: (g, c, AB_QB // B_W)),
            pl.BlockSpec((bb, L, B_W), lambda g, c: (g, c, AB_KB // B_W)),
            pl.BlockSpec((bb, L, B_W), lambda g, c: (g, c, AB_VB // B_W)),
            pl.BlockSpec((bb, L, LANE), lambda g, c: (g, c, AB_IG // LANE)),
            pl.BlockSpec((bb, L, LANE), lambda g, c: (g, c, AB_FG // LANE)),
            pl.BlockSpec((bb, 1, 2 * H_B, L), lambda g, c: (g, c, 0, 0)),
            pl.BlockSpec((1, LANE), lambda g, c: (0, 0)),
            pl.BlockSpec((1, LANE), lambda g, c: (0, 0)),
            pl.BlockSpec((2 * H_B, 1), lambda g, c: (0, 0)),
            pl.BlockSpec((bb * H_B, DH_B, DH_B), lambda g, c: (g, 0, 0)),
            pl.BlockSpec((bb * H_B, 1, DH_B), lambda g, c: (g, 0, 0)),
            pl.BlockSpec((bb, 1, LANE), lambda g, c: (g, 0, 0)),
        ],
        out_specs=[
            pl.BlockSpec((bb, L, B_W), lambda g, c: (g, c, 0)),
            pl.BlockSpec((bb * H_B, DH_B, DH_B), lambda g, c: (g, 0, 0)),
            pl.BlockSpec((bb * H_B, 1, DH_B), lambda g, c: (g, 0, 0)),
            pl.BlockSpec((bb, 1, LANE), lambda g, c: (g, 0, 0)),
        ],
        out_shape=[
            jax.ShapeDtypeStruct((b, s, B_W), F32),
            jax.ShapeDtypeStruct((b * H_B, DH_B, DH_B), F32),
            jax.ShapeDtypeStruct((b * H_B, 1, DH_B), F32),
            jax.ShapeDtypeStruct((b, 1, LANE), F32),
        ],
        scratch_shapes=[
            pltpu.VMEM((bb * H_B, DH_B, DH_B), F32),
            pltpu.VMEM((bb * H_B, 1, DH_B), F32),
            pltpu.VMEM((bb, 1, LANE), F32),
        ],
        compiler_params=_cparams(("parallel", "arbitrary")),
        name="mlstm_L%d" % L,
    )(z3, z3, z3, z3, z3, gt, bi_row, bf_row, b8, c0r, n0r, m0r)
    del cw
    return (h.reshape(b * s, B_W), co.reshape(b, H_B, DH_B, DH_B), no.reshape(b, H_B, DH_B),
            mo[:, 0, :H_B])


def _ab_merge_kernel(o0, o1, o2, l0, l1, l2, hb_ref, og_ref, g_ref, w_ref, r_ref, out_ref):
    a0, a1, a2 = l0[...], l1[...], l2[...]
    mx = jnp.maximum(jnp.maximum(a0, a1), a2)
    e0, e1, e2 = jnp.exp(a0 - mx), jnp.exp(a1 - mx), jnp.exp(a2 - mx)
    o_a = (e0 * o0[...] + e1 * o1[...] + e2 * o2[...]) / (e0 + e1 + e2)
    hb = hb_ref[...]
    parts = []
    for h in range(H_B):
        hs = slice(h * DH_B, (h + 1) * DH_B)
        x = hb[:, hs]
        parts.append(x * lax.rsqrt(jnp.mean(x * x, axis=-1, keepdims=True) + EPS))
    hbn = jnp.concatenate(parts, axis=-1) * g_ref[...] * jax.nn.sigmoid(og_ref[...])
    y = (jnp.dot(o_a.astype(BF16), w_ref[0:AW, :], preferred_element_type=F32)
         + jnp.dot(hbn.astype(BF16), w_ref[AW:AW + B_W, :], preferred_element_type=F32))
    out_ref[...] = r_ref[...] + y


def ab_merge(os_, ls_, hb, z2, g_mn, w_bf16, resid, *, tm_pref=512):
    m, d = resid.shape
    tm = _pick_tile(m, tm_pref)
    a_spec = pl.BlockSpec((tm, AW), lambda i: (i, 0))
    return pl.pallas_call(
        _ab_merge_kernel,
        grid=(m // tm,),
        in_specs=[a_spec] * 6 + [
            pl.BlockSpec((tm, B_W), lambda i: (i, 0)),
            pl.BlockSpec((tm, B_W), lambda i: (i, AB_OG // B_W)),
            pl.BlockSpec((1, B_W), lambda i: (0, 0)),
            pl.BlockSpec((AW + B_W, d), lambda i: (0, 0)),
            pl.BlockSpec((tm, d), lambda i: (i, 0)),
        ],
        out_specs=pl.BlockSpec((tm, d), lambda i: (i, 0)),
        out_shape=jax.ShapeDtypeStruct((m, d), F32),
        compiler_params=_cparams(("parallel",)),
        name="ab_merge",
    )(*os_, *ls_, hb, z2, g_mn.reshape(1, B_W), w_bf16, resid)


HKV = KVH_C * DH_C


def _compress_bias_kernel(pa_ref, pb_ref, w_ref, o_ref):
    w = w_ref[0]
    o_ref[0] = (jnp.dot(pa_ref[0].astype(BF16), w, preferred_element_type=F32)[:, 0:HKV]
                + jnp.dot(pb_ref[0].astype(BF16), w, preferred_element_type=F32)[:, HKV:2 * HKV])


def compress_bias(pa, pb, wab2):
    spec = lambda a: pl.BlockSpec((1,) + a.shape[1:], lambda i: (i,) + (0,) * (a.ndim - 1))
    return pl.pallas_call(
        _compress_bias_kernel,
        grid=(2,),
        in_specs=[spec(pa), spec(pb), spec(wab2)],
        out_specs=pl.BlockSpec((1, 1, HKV), lambda i: (i, 0, 0)),
        out_shape=jax.ShapeDtypeStruct((2, 1, HKV), F32),
        compiler_params=_cparams(("arbitrary",)),
        name="compress_bias",
    )(pa, pb, wab2)


def _compress_halves(load_rows, n, w_all):
    x = jnp.concatenate([load_rows(r, n).astype(BF16) for r in range(CMP_STRIDE)], axis=1)
    return jnp.dot(x, w_all, preferred_element_type=F32)


def _compress_prompt_kernel(x_ref, wab_ref, bias_ref, w2_ref, o_ref):
    n = x_ref.shape[1] // CMP_STRIDE
    acc = _compress_halves(lambda r, m: x_ref[0, pl.ds(r, m, stride=CMP_STRIDE), :], n, wab_ref[0])
    hid = acc[:, 0:HKV] + pltpu.roll(acc[:, HKV:2 * HKV], n - 1, 0) + bias_ref[0]
    o_ref[0] = jnp.dot(_gelu_tanh(hid).astype(BF16), w2_ref[0], preferred_element_type=F32)


def compress_prompt(z3, cw, bias):
    b, s, _ = z3.shape
    wab, w2 = cw
    n = s // CMP_STRIDE
    return pl.pallas_call(
        _compress_prompt_kernel,
        grid=(b, 2),
        in_specs=[
            pl.BlockSpec((1, s, HKV), lambda i, kv: (i, 0, C_CMP // HKV + kv)),
            pl.BlockSpec((1,) + wab.shape[1:], lambda i, kv: (kv, 0, 0)),
            pl.BlockSpec((1, 1, HKV), lambda i, kv: (kv, 0, 0)),
            pl.BlockSpec((1, HKV, HKV), lambda i, kv: (kv, 0, 0)),
        ],
        out_specs=pl.BlockSpec((1, n, HKV), lambda i, kv: (i, 0, kv)),
        out_shape=jax.ShapeDtypeStruct((b, n, KVW), F32),
        compiler_params=_cparams(("parallel", "arbitrary")),
        name="compress_prompt",
    )(z3, wab, bias, w2)


def _compress_paged_kernel(pt_ref, *refs, pg):
    del pt_ref
    page_refs = refs[:pg + 1]
    wab_ref, bias_ref, w2_ref, o_ref, xs_ref = refs[pg + 1:]
    for j in range(pg + 1):
        t = page_refs[j][0].T
        xs_ref[0, j * PAGE_SIZE:(j + 1) * PAGE_SIZE, :] = t[:, 0:HKV]
        xs_ref[1, j * PAGE_SIZE:(j + 1) * PAGE_SIZE, :] = t[:, HKV:2 * HKV]
    cpp = PAGE_SIZE // CMP_STRIDE
    n = pg * cpp
    for kv in range(2):
        acc = _compress_halves(lambda r, m: xs_ref[kv, pl.ds(r, m, stride=CMP_STRIDE), :], n + cpp, wab_ref[kv])
        hid = acc[0:n, 0:HKV] + acc[1:n + 1, HKV:2 * HKV] + bias_ref[kv]
        o_ref[0, :, kv * HKV:(kv + 1) * HKV] = jnp.dot(_gelu_tanh(hid).astype(BF16), w2_ref[kv],
                                                        preferred_element_type=F32)


def compress_paged(pool_t, page_table, cw, bias, *, pg=32):
    db, n_pages = page_table.shape
    cpp = PAGE_SIZE // CMP_STRIDE
    pg = min(pg, n_pages)
    assert n_pages % pg == 0
    wab, w2 = cw
    full = lambda a: pl.BlockSpec(a.shape, lambda b, g, pt: (0,) * a.ndim)

    def page_spec(j):
        return pl.BlockSpec((1, KVW, PAGE_SIZE),
                            lambda b, g, pt: (pt[b, jnp.minimum(g * pg + j, n_pages - 1)], 0, 0))

    gs = pltpu.PrefetchScalarGridSpec(
        num_scalar_prefetch=1,
        grid=(db, n_pages // pg),
        in_specs=[page_spec(j) for j in range(pg + 1)] + [full(wab), full(bias), full(w2)],
        out_specs=pl.BlockSpec((1, pg * cpp, KVW), lambda b, g, pt: (b, g, 0)),
        scratch_shapes=[pltpu.VMEM((2, (pg + 1) * PAGE_SIZE, HKV), F32)],
    )
    return pl.pallas_call(
        functools.partial(_compress_paged_kernel, pg=pg),
        grid_spec=gs,
        out_shape=jax.ShapeDtypeStruct((db, n_pages * cpp, KVW), F32),
        compiler_params=_cparams(("parallel", "arbitrary")),
        name="compress_paged",
    )(page_table, *([pool_t] * (pg + 1)), wab, bias, w2)


def _stack_heads(x, t):
    del t
    return jnp.concatenate([x[:, g * DH_C:(g + 1) * DH_C] for g in range(G_C)], axis=0)


def _mask_rows(s, mask, fill):
    t, n = mask.shape
    return jnp.where(mask[None], s.reshape(G_C, t, n), fill).reshape(G_C * t, n)


def _cmp_branch(q2, kc, vc, t_pos, tq):
    n = kc.shape[0]
    s = _dot_nt(q2, kc)
    cend = lax.broadcasted_iota(jnp.int32, (tq, n), 1) * CMP_STRIDE + (CMP_BLK - 1)
    s = _mask_rows(s, cend <= t_pos, -jnp.inf)
    m = jnp.max(s, axis=-1, keepdims=True)
    m = jnp.where(m > -jnp.inf, m, 0.0)
    p = jnp.exp(s - m)
    p = p / jnp.maximum(jnp.sum(p, axis=-1, keepdims=True), 1e-30)
    o = jnp.dot(p.astype(BF16), vc, preferred_element_type=F32)
    pg = p[0:tq]
    for g in range(1, G_C):
        pg = pg + p[g * tq:(g + 1) * tq]
    return o, pg


def _select_blocks(imp, t_pos):
    tq, nsp = imp.shape
    j = lax.broadcasted_iota(jnp.int32, (tq, nsp), 1)
    jf = j.astype(F32)
    cur = t_pos // SLC_BLK
    forced = (j == 0) | (j == cur) | (j == cur - 1)
    work = jnp.where(forced, jnp.inf, jnp.where(j <= cur, imp, -jnp.inf))
    sel = jnp.zeros((tq, nsp), F32)
    for _ in range(N_SEL):
        mx = jnp.max(work, axis=-1, keepdims=True)
        first = jnp.min(jnp.where(work == mx, jf, float(nsp)), axis=-1, keepdims=True)
        pick = jf == first
        sel = jnp.where(pick, jnp.where(mx > -jnp.inf, 1.0, sel), sel)
        work = jnp.where(pick, -jnp.inf, work)
    return sel


def _select_blocks_t(imp_t, t_row):
    nsp, n = imp_t.shape
    j = lax.broadcasted_iota(jnp.int32, (nsp, n), 0)
    jf = j.astype(F32)
    cur = t_row // SLC_BLK
    forced = (j == 0) | (j == cur) | (j == cur - 1)
    work = jnp.where(forced, jnp.inf, jnp.where(j <= cur, imp_t, -jnp.inf))
    sel = jnp.zeros((nsp, n), F32)
    for _ in range(N_SEL):
        mx = jnp.max(work, axis=0, keepdims=True)
        first = jnp.min(jnp.where(work == mx, jf, float(nsp)), axis=0, keepdims=True)
        pick = jf == first
        sel = jnp.where(pick, jnp.where(mx > -jnp.inf, 1.0, sel), sel)
        work = jnp.where(pick, -jnp.inf, work)
    return sel


def _flash_update(s, mask, v, m_ref, l_ref, acc_ref, v_channel_major=False):
    s = _mask_rows(s, mask, NEG_BIG)
    m_old = m_ref[...]
    m_new = jnp.maximum(m_old, jnp.max(s, axis=-1, keepdims=True))
    alpha = jnp.exp(m_old - m_new)
    p = jnp.exp(s - m_new)
    l_ref[...] = alpha * l_ref[...] + jnp.sum(p, axis=-1, keepdims=True)
    pv = _dot_nt(p.astype(BF16), v) if v_channel_major else jnp.dot(p.astype(BF16), v, preferred_element_type=F32)
    acc_ref[...] = alpha * acc_ref[...] + pv
    m_ref[...] = m_new


def _flash_finish(m_ref, l_ref, acc_ref):
    return jnp.where(m_ref[...] > 0.5 * NEG_BIG, acc_ref[...] / jnp.maximum(l_ref[...], 1e-30), 0.0)


def _softmax_av(s, mask, v):
    s = _mask_rows(s, mask, -jnp.inf)
    m = jnp.max(s, axis=-1, keepdims=True)
    m = jnp.where(m > -jnp.inf, m, 0.0)
    p = jnp.exp(s - m)
    p = p / jnp.maximum(jnp.sum(p, axis=-1, keepdims=True), 1e-30)
    return jnp.dot(p.astype(BF16), v, preferred_element_type=F32)


def _gate_mix(gates, h, o_cmp, o_sel, o_win, tq):
    outs = []
    for g in range(G_C):
        base = (h * G_C + g) * 3
        rs = slice(g * tq, (g + 1) * tq)
        outs.append(gates[:, base:base + 1] * o_cmp[rs] + gates[:, base + 1:base + 2] * o_sel[rs]
                    + gates[:, base + 2:base + 3] * o_win[rs])
    return jnp.concatenate(outs, axis=-1)


def _rope_q(qh, cos, sin):
    return jnp.concatenate(
        [_rope_lanes(qh[:, c * LANE:(c + 1) * LANE], cos, sin) for c in range(G_C * DH_C // LANE)], axis=-1)


SEL_TK = 1024


SEL_PHASE = DH_C


def _nsa_prompt_kernel(q_ref, gz_ref, cos_ref, sin_ref, kc_ref, kv_ref, ka_ref, va_ref, mimp_ref, o_ref,
                       qa_s, m_s, acc_s, *, nsp):
    i = pl.program_id(1)
    tq = QBLK_C
    rows = G_C * tq
    s0 = i * tq
    t_pos = s0 + lax.broadcasted_iota(jnp.int32, (tq, 1), 0)
    gates = jax.nn.sigmoid(gz_ref[0])
    cos, sin = cos_ref[...], sin_ref[...]
    scale = DH_C ** -0.5
    hw = G_C * DH_C
    ncmp = kc_ref.shape[1]
    t_row = s0 + lax.broadcasted_iota(jnp.int32, (1, rows), 1) % tq
    visible = (lax.broadcasted_iota(jnp.int32, (ncmp, 1), 0) * CMP_STRIDE + (CMP_BLK - 1)) <= t_row
    any_visible = t_row >= CMP_BLK - 1
    zpad = jnp.zeros((tq, DH_C), F32)
    kcc = kc_ref[0, :, 0:HKV].astype(BF16)
    vcc = kc_ref[0, :, HKV:2 * HKV].astype(BF16)
    q_rot, o_cmps, imps_t = [], [], []
    for h in range(KVH_C):
        qh = q_ref[0, :, h * hw:(h + 1) * hw] * scale
        q_rot.append(_rope_q(qh, cos, sin))
        qc = jnp.concatenate([jnp.concatenate([x, zpad] if h == 0 else [zpad, x], axis=1)
                              for x in (qh[:, g * DH_C:(g + 1) * DH_C] for g in range(G_C))], axis=0)
        st = jnp.where(visible, _dot_nt(kcc, qc.astype(BF16)), NEG_BIG)
        pt = jnp.exp(st - jnp.max(st, axis=0, keepdims=True))
        inv = jnp.where(any_visible, 1.0 / jnp.maximum(jnp.sum(pt, axis=0, keepdims=True), 1e-30), 0.0)
        pt = pt * inv
        o_cmps.append(_dot_tn(vcc, pt.astype(BF16))[h * DH_C:(h + 1) * DH_C, :].T)
        pgrp_t = pt[:, 0:tq]
        for g in range(1, G_C):
            pgrp_t = pgrp_t + pt[:, g * tq:(g + 1) * tq]
        imps_t.append(_dot_exact_lhs01(mimp_ref[...], pgrp_t))
    t_row2 = s0 + lax.broadcasted_iota(jnp.int32, (1, KVH_C * tq), 1) % tq
    sel_t = _select_blocks_t(jnp.concatenate(imps_t, axis=1), t_row2)

    half_of_lane = lax.broadcasted_iota(jnp.int32, (1, LANE), 1) // DH_C

    def into_half(chunk, src_half, dst_half):
        moved = chunk if src_half == dst_half else pltpu.roll(chunk, DH_C, 1)
        return jnp.where(half_of_lane == dst_half, moved, 0.0)

    for h in range(KVH_C):
        selb = jnp.where(sel_t[:, h * tq:(h + 1) * tq].T > 0.5, 0.0, NEG_BIG)
        q_part = jnp.concatenate(
            [into_half(q_rot[h][:, (g // 2) * LANE:(g // 2 + 1) * LANE], g % 2, h) for g in range(G_C)], axis=0)
        for ph in range(nsp // SEL_PHASE):
            sb = into_half(selb[:, (ph // 2) * LANE:(ph // 2 + 1) * LANE], ph % 2, 1 - h)
            qa_s[h, ph] = (q_part + jnp.concatenate([sb] * G_C, axis=0)).astype(BF16)
    m_s[...] = jnp.full(m_s.shape, NEG_BIG, F32)
    acc_s[...] = jnp.zeros(acc_s.shape, F32)

    def tile(k0, size, causal):
        k0 = pl.multiple_of(k0, size)
        ph = k0 // (SEL_PHASE * SLC_BLK)
        for h in range(KVH_C):
            st = _dot_nt(ka_ref[0, h, pl.ds(k0, size), :], qa_s[h, ph])
            if causal:
                kpos = k0 + lax.broadcasted_iota(jnp.int32, (size, 1), 0)
                t_row = s0 + lax.broadcasted_iota(jnp.int32, (1, rows), 1) % tq
                st = jnp.where(kpos <= t_row, st, NEG_BIG)
            m_old = m_s[h]
            m_new = jnp.maximum(m_old, jnp.max(st, axis=0, keepdims=True))
            pt = jnp.exp(st - m_new).astype(BF16)
            acc_s[h] = (jnp.exp(m_old - m_new) * acc_s[h]
                        + jnp.dot(va_ref[0, h, :, pl.ds(k0, size)], pt, preferred_element_type=F32))
            m_s[h] = m_new

    def full_tile(kt, carry):
        tile(kt * SEL_TK, SEL_TK, False)
        return carry

    half = SEL_TK // 2
    own = (s0 + tq - 1) // half
    lax.fori_loop(0, own // 2, full_tile, 0)

    @pl.when(own % 2 == 1)
    def _():
        tile((own - 1) * half, half, False)

    tile(own * half, half, True)

    def finish(acc, h):
        sums = acc[(1 - h) * DH_C:(1 - h) * DH_C + 1, :]
        return (acc[h * DH_C:(h + 1) * DH_C, :] / jnp.maximum(sums, 1e-30)).T

    nw = WIN_C + tq
    w0 = pl.multiple_of(jnp.maximum(s0 - WIN_C, 0), tq)
    dlt = (s0 + lax.broadcasted_iota(jnp.int32, (1, rows), 1) % tq) - (w0 + lax.broadcasted_iota(jnp.int32, (nw, 1), 0))
    in_win = (dlt >= 0) & (dlt < WIN_C)
    lane = lax.broadcasted_iota(jnp.int32, (1, LANE), 1) // DH_C
    kwc = kv_ref[0, pl.ds(w0, nw), 0:LANE]
    vwc = kv_ref[0, pl.ds(w0, nw), LANE:2 * LANE]
    for h in range(KVH_C):
        st = _dot_nt(jnp.where(lane == h, kwc, jnp.zeros_like(kwc)), qa_s[h, 0])
        st = jnp.where(in_win, st, NEG_BIG)
        pt = jnp.exp(st - jnp.max(st, axis=0, keepdims=True)).astype(BF16)
        o_win = finish(_dot_tn(jnp.where(lane == h, vwc, jnp.ones_like(vwc)), pt), h)
        o_ref[0, :, h * hw:(h + 1) * hw] = _gate_mix(gates, h, o_cmps[h], finish(acc_s[h], h), o_win, tq)


def _cols_t_kernel(x_ref, o_ref):
    o_ref[0] = x_ref[0].T.astype(BF16)


def cols_channel_major(z3, col0, *, ts=1024):
    b, s, _ = z3.shape
    ts = _pick_tile(s, ts)
    return pl.pallas_call(
        _cols_t_kernel,
        grid=(b, s // ts),
        in_specs=[pl.BlockSpec((1, ts, LANE), lambda bb, i: (bb, i, col0 // LANE))],
        out_specs=pl.BlockSpec((1, LANE, ts), lambda bb, i: (bb, 0, i)),
        out_shape=jax.ShapeDtypeStruct((b, LANE, s), BF16),
        compiler_params=_cparams(("parallel", "parallel")),
        name="cols_channel_major",
    )(z3)


def _sel_operands(z3):
    s = z3.shape[1]
    lane = jnp.arange(LANE)
    hot = ((jnp.arange(s)[:, None] // SLC_BLK) % SEL_PHASE == lane[None, :] % SEL_PHASE).astype(BF16)
    ksel = z3[:, :, C_SEL:C_SEL + HKV].astype(BF16)
    vsel_t = cols_channel_major(z3, C_SEL + HKV)
    ka = jnp.stack([jnp.where((lane // DH_C == h)[None, None, :], ksel, hot[None]) for h in range(KVH_C)], axis=1)
    va = jnp.stack([jnp.where((lane // DH_C == h)[None, :, None], vsel_t, jnp.ones_like(vsel_t))
                    for h in range(KVH_C)], axis=1)
    return ka, va


def nsa_prompt(z3, kc, kvw, ka, va, cos_t, sin_t, mimp):
    b, s, _ = z3.shape
    nsp = mimp.shape[0]
    nq = s // QBLK_C
    ncmp = kc.shape[1]
    rows = G_C * QBLK_C
    vm = pltpu.VMEM
    scratch = [
        vm((KVH_C, nsp // SEL_PHASE, rows, LANE), BF16),
        vm((KVH_C, 1, rows), F32),
        vm((KVH_C, LANE, rows), F32),
    ]
    return pl.pallas_call(
        functools.partial(_nsa_prompt_kernel, nsp=nsp),
        grid=(b, nq),
        in_specs=[
            pl.BlockSpec((1, QBLK_C, C_Q), lambda bb, i: (bb, i, 0)),
            pl.BlockSpec((1, QBLK_C, LANE), lambda bb, i: (bb, i, C_GATE // LANE)),
            pl.BlockSpec((QBLK_C, LANE), lambda bb, i: (i, 0)),
            pl.BlockSpec((QBLK_C, LANE), lambda bb, i: (i, 0)),
            pl.BlockSpec((1, ncmp, KVW), lambda bb, i: (bb, 0, 0)),
            pl.BlockSpec((1, s, KVW), lambda bb, i: (bb, 0, 0)),
            pl.BlockSpec((1, KVH_C, s, LANE), lambda bb, i: (bb, 0, 0, 0)),
            pl.BlockSpec((1, KVH_C, LANE, s), lambda bb, i: (bb, 0, 0, 0)),
            pl.BlockSpec(mimp.shape, lambda bb, i: (0, 0)),
        ],
        out_specs=pl.BlockSpec((1, QBLK_C, C_Q), lambda bb, i: (bb, i, 0)),
        out_shape=jax.ShapeDtypeStruct((b, s, C_Q), F32),
        scratch_shapes=scratch,
        compiler_params=_cparams(("parallel", "arbitrary")),
        name="nsa_prompt",
    )(z3, z3, cos_t, sin_t, kc, kvw, ka, va, mimp)


def _nsa_sample_kernel(pt_ref, *refs, pg, nsp, past, t_new):
    del pt_ref
    page_refs = refs[:pg]
    (q_ref, gz_ref, cos_ref, sin_ref, kc_ref, mimp_ref, hot_ref, seln_ref, winb_ref, winn_ref, winnt_ref,
     o_ref, wo_ref, q2r_s, qa_s, sel_s, ocmp_s, m_s, l_s, acc_s) = refs[pg:]
    g = pl.program_id(1)
    ng = pl.num_programs(1)
    tq = T_PAD
    rows = G_C * tq
    t_pos = past + lax.broadcasted_iota(jnp.int32, (tq, 1), 0)
    scale = DH_C ** -0.5
    hw = G_C * DH_C
    nk = pg * PAGE_SIZE
    gblk = nk // SLC_BLK

    @pl.when(g == 0)
    def _():
        cos, sin = cos_ref[...], sin_ref[...]
        q2rs, imps = [], []
        for h in range(KVH_C):
            qh = q_ref[0, :, h * hw:(h + 1) * hw] * scale
            q2 = _stack_heads(qh, tq).astype(BF16)
            q2rs.append(_stack_heads(_rope_q(qh, cos, sin), tq))
            q2r_s[h] = q2rs[h].astype(BF16)
            kc = kc_ref[0, :, h * DH_C:(h + 1) * DH_C].astype(BF16)
            vc = kc_ref[0, :, KVH_C * DH_C + h * DH_C:KVH_C * DH_C + (h + 1) * DH_C].astype(BF16)
            o_cmp, pgrp = _cmp_branch(q2, kc, vc, t_pos, tq)
            ocmp_s[h] = o_cmp
            imps.append(_dot_exact_rhs01(pgrp, mimp_ref[...]))
        sel_all = _select_blocks(jnp.concatenate(imps, axis=0), jnp.concatenate([t_pos] * KVH_C, axis=0))
        zpad = [jnp.zeros((rows, DH_C - gblk), F32)] if gblk < DH_C else []
        for h in range(KVH_C):
            sel = sel_all[h * tq:(h + 1) * tq]
            sel_s[h] = sel
            selb = jnp.where(sel > 0.5, 0.0, NEG_BIG)
            for gg in range(qa_s.shape[1]):
                sb = jnp.concatenate([selb[:, gg * gblk:(gg + 1) * gblk]] * G_C, axis=0)
                qa_s[h, gg] = jnp.concatenate([q2rs[h], sb] + zpad, axis=1).astype(BF16)
        m_s[...] = jnp.full(m_s.shape, NEG_BIG, F32)
        l_s[...] = jnp.zeros(l_s.shape, F32)
        acc_s[...] = jnp.zeros(acc_s.shape, F32)

    kvt = jnp.concatenate([r[0] for r in page_refs], axis=1).astype(BF16)
    hot = hot_ref[...]
    for h in range(KVH_C):
        kst = jnp.concatenate([kvt[h * DH_C:(h + 1) * DH_C, :], hot], axis=0)
        vst = kvt[KVH_C * DH_C + h * DH_C:KVH_C * DH_C + (h + 1) * DH_C, :]
        s = jnp.dot(qa_s[h, g], kst, preferred_element_type=F32)
        m_old = m_s[h]
        m_new = jnp.maximum(m_old, jnp.max(s, axis=-1, keepdims=True))
        alpha = jnp.exp(m_old - m_new)
        p = jnp.exp(s - m_new)
        l_s[h] = alpha * l_s[h] + jnp.sum(p, axis=-1, keepdims=True)
        acc_s[h] = alpha * acc_s[h] + _dot_nt(p.astype(BF16), vst)
        m_s[h] = m_new

    @pl.when(g == ng - 1)
    def _():
        gates = jax.nn.sigmoid(gz_ref[0])
        n_buf = winb_ref.shape[2]
        lane_n = lax.broadcasted_iota(jnp.int32, (tq, nsp), 1)
        u = lax.broadcasted_iota(jnp.int32, (tq, tq), 1)
        seln = seln_ref[0]
        winn = winn_ref[0]
        for h in range(KVH_C):
            kcol = h * DH_C
            vcol = KVH_C * DH_C + h * DH_C
            q2r = q2r_s[h]
            new_sel = jnp.max(jnp.where(lane_n == past // SLC_BLK, sel_s[h], 0.0), axis=-1, keepdims=True) > 0.5
            mk = new_sel & (past + u <= t_pos) & (u < t_new)
            _flash_update(_dot_nt(q2r, seln[:, kcol:kcol + DH_C].astype(BF16)), mk,
                          seln[:, vcol:vcol + DH_C].astype(BF16), m_s.at[h], l_s.at[h], acc_s.at[h])
            o_sel = _flash_finish(m_s.at[h], l_s.at[h], acc_s.at[h])
            kbt = winb_ref[0, kcol:kcol + DH_C, :].astype(BF16)
            vbt = winb_ref[0, vcol:vcol + DH_C, :].astype(BF16)
            pos_b = past - n_buf + lax.broadcasted_iota(jnp.int32, (tq, n_buf), 1)
            d_b = t_pos - pos_b
            m_b = (pos_b >= 0) & (d_b >= 0) & (d_b < WIN_C)
            d_n = t_pos - (past + u)
            m_n = (d_n >= 0) & (d_n < WIN_C) & (u < t_new)
            s_b = _mask_rows(jnp.dot(q2r, kbt, preferred_element_type=F32), m_b, -jnp.inf)
            s_n = _mask_rows(_dot_nt(q2r, winn[:, kcol:kcol + DH_C].astype(BF16)), m_n, -jnp.inf)
            mx = jnp.maximum(jnp.max(s_b, axis=-1, keepdims=True), jnp.max(s_n, axis=-1, keepdims=True))
            mx = jnp.where(mx > -jnp.inf, mx, 0.0)
            p_b, p_n = jnp.exp(s_b - mx), jnp.exp(s_n - mx)
            den = jnp.maximum(jnp.sum(p_b, axis=-1, keepdims=True) + jnp.sum(p_n, axis=-1, keepdims=True), 1e-30)
            o_win = (_dot_nt((p_b / den).astype(BF16), vbt)
                     + jnp.dot((p_n / den).astype(BF16), winn[:, vcol:vcol + DH_C].astype(BF16),
                               preferred_element_type=F32))
            o_ref[0, :, h * hw:(h + 1) * hw] = _gate_mix(gates, h, ocmp_s[h], o_sel, o_win, tq)
        wo_ref[0, :, 0:n_buf - t_new] = winb_ref[0, :, t_new:n_buf]
        wo_ref[0, :, n_buf - t_new:n_buf] = winnt_ref[0, :, 0:t_new]


def nsa_sample(z3, kc, sel_pool_t, win_buf_t, win_new_t, page_table, cos_t, sin_t, mimp, *, t_new, pg=32):
    db, n_pages = page_table.shape
    past = n_pages * PAGE_SIZE
    ncmp = kc.shape[1]
    nsp = mimp.shape[1]
    n_buf = win_buf_t.shape[2]
    pg = min(pg, n_pages)
    nk = pg * PAGE_SIZE
    assert n_pages % pg == 0 and nk // SLC_BLK <= DH_C
    rows = G_C * T_PAD
    hot = jnp.asarray(np.arange(nk)[None, :] // SLC_BLK == np.arange(DH_C)[:, None], dtype=BF16)

    def page_spec(j):
        return pl.BlockSpec((1, KVW, PAGE_SIZE), lambda b, g, pt: (pt[b, g * pg + j], 0, 0))

    gs = pltpu.PrefetchScalarGridSpec(
        num_scalar_prefetch=1,
        grid=(db, n_pages // pg),
        in_specs=[page_spec(j) for j in range(pg)] + [
            pl.BlockSpec((1, T_PAD, C_Q), lambda b, g, pt: (b, 0, 0)),
            pl.BlockSpec((1, T_PAD, LANE), lambda b, g, pt: (b, 0, C_GATE // LANE)),
            pl.BlockSpec((T_PAD, LANE), lambda b, g, pt: (0, 0)),
            pl.BlockSpec((T_PAD, LANE), lambda b, g, pt: (0, 0)),
            pl.BlockSpec((1, ncmp, KVW), lambda b, g, pt: (b, 0, 0)),
            pl.BlockSpec(mimp.shape, lambda b, g, pt: (0, 0)),
            pl.BlockSpec((DH_C, nk), lambda b, g, pt: (0, 0)),
            pl.BlockSpec((1, T_PAD, KVW), lambda b, g, pt: (b, 0, C_SEL // KVW)),
            pl.BlockSpec((1, KVW, n_buf), lambda b, g, pt: (b, 0, 0)),
            pl.BlockSpec((1, T_PAD, KVW), lambda b, g, pt: (b, 0, C_WIN // KVW)),
            pl.BlockSpec((1, KVW, T_PAD), lambda b, g, pt: (b, 0, 0)),
        ],
        out_specs=[
            pl.BlockSpec((1, T_PAD, C_Q), lambda b, g, pt: (b, 0, 0)),
            pl.BlockSpec((1, KVW, n_buf), lambda b, g, pt: (b, 0, 0)),
        ],
        scratch_shapes=[
            pltpu.VMEM((KVH_C, rows, DH_C), BF16),
            pltpu.VMEM((KVH_C, n_pages // pg, rows, LANE), BF16),
            pltpu.VMEM((KVH_C, T_PAD, nsp), F32),
            pltpu.VMEM((KVH_C, rows, DH_C), F32),
            pltpu.VMEM((KVH_C, rows, 1), F32),
            pltpu.VMEM((KVH_C, rows, 1), F32),
            pltpu.VMEM((KVH_C, rows, DH_C), F32),
        ],
    )
    return pl.pallas_call(
        functools.partial(_nsa_sample_kernel, pg=pg, nsp=nsp, past=past, t_new=t_new),
        grid_spec=gs,
        out_shape=[jax.ShapeDtypeStruct((db, T_PAD, C_Q), F32), jax.ShapeDtypeStruct((db, KVW, n_buf), F32)],
        compiler_params=_cparams(("parallel", "arbitrary")),
        name="nsa_sample",
    )(page_table, *([sel_pool_t] * pg), z3, z3, cos_t, sin_t, kc, mimp, hot, z3, win_buf_t, z3, win_new_t)


def _rope_tables(pos):
    half = DH_C // 2
    inv = 1.0 / (ROPE_THETA ** (jnp.arange(half, dtype=F32) / half))
    ang = pos.astype(F32)[:, None] * inv[None, :]
    cos, sin = jnp.cos(ang), jnp.sin(ang)
    return jnp.tile(cos, (1, 4)), jnp.concatenate([-sin, sin, -sin, sin], axis=-1)


def _prep_w_in_ab(w):
    cuts = np.cumsum([A_QW, A_QW, A_QW, B_W, B_W, B_W, H_B, H_B])
    qa, ka, va, qb, kb, vb, ig, fg, og = jnp.split(w, cuts, axis=-1)
    padg = lambda t: jnp.pad(t, ((0, 0), (0, LANE - H_B)))
    return jnp.concatenate([qb, kb, vb, og, qa, ka, va, padg(ig), padg(fg)], axis=-1).astype(BF16)


def _ab_rope_flags():
    f = np.zeros((AB_N // LANE,), np.int32)
    f[AB_QA // LANE:AB_VA // LANE] = 1
    return jnp.asarray(f)


def _c_rope_flags():
    f = np.zeros((C_N // LANE,), np.int32)
    f[C_SEL // LANE] = 1
    f[C_WIN // LANE] = 1
    return jnp.asarray(f)


def _prep_compress(w1, w2, pe):
    e2 = jnp.eye(KVH_C, dtype=F32)

    def half(w1h):
        return jnp.einsum('krde,hH->krhdHe', w1h, e2).reshape(2, CMP_STRIDE, HKV, HKV).astype(BF16)

    wab = jnp.concatenate([half(w1[:, :CMP_STRIDE]), half(w1[:, CMP_STRIDE:])], axis=-1)
    w2b = jnp.einsum('ked,hH->kheHd', w2, e2).reshape(2, HKV, HKV).astype(BF16)

    def pe_half(p):
        return jnp.broadcast_to(p[:, :, None, :], (2, CMP_STRIDE, KVH_C, DH_C)).reshape(2, 1, CMP_STRIDE * HKV)

    wab = wab.reshape(2, CMP_STRIDE * HKV, 2 * HKV)
    bias = compress_bias(pe_half(pe[:, :CMP_STRIDE]), pe_half(pe[:, CMP_STRIDE:]), wab)
    return (wab, w2b), bias


def _channel_major(x, lead):
    perm = tuple(range(lead)) + (lead + 1, lead + 2, lead + 3, lead)
    xt = jnp.transpose(x, perm)
    return xt.reshape(x.shape[:lead] + (x.shape[lead + 1] * x.shape[lead + 2] * x.shape[lead + 3], x.shape[lead]))


def _row_major(xt, c0, c1, c2):
    lead, _, rows = xt.shape
    return jnp.transpose(xt.reshape(lead, c0, c1, c2, rows), (0, 4, 1, 2, 3))


def _importance_matrix(n_rows, n_cmp, n_slc, n_cols):
    ratio = SLC_BLK // CMP_STRIDE
    m = np.zeros((n_rows, n_cols), np.float32)
    for jblk in range(n_slc):
        for off in range(1 - CMP_BLK // CMP_STRIDE, ratio):
            i = ratio * jblk + off
            if 0 <= i < n_cmp:
                m[i, jblk] = 1.0
    return jnp.asarray(m, dtype=BF16)


def _gates_t(z3, L):
    b, s, _ = z3.shape
    g = jnp.concatenate([z3[..., AB_IG:AB_IG + H_B], z3[..., AB_FG:AB_FG + H_B]], axis=-1)
    return g.reshape(b, s // L, L, 2 * H_B).transpose(0, 1, 3, 2)


def kernel(x_prompt, x_sample, cache_a0_kv, cache_a1_kv, cache_a2_kv, state_b_C, state_b_n, state_b_m,
           cache_c_cmp_kv, cache_c_sel_kv, cache_c_win_kv, page_table, norm_g, w_in_ab, b_if, g_mlstm,
           w_out_ab, w_in_c, cmp_w1, cmp_w2, cmp_pe, w_out_c, w_ffn_gate, w_ffn_up, w_ffn_down, norm_final):
    B, S, D = x_prompt.shape
    DB, T, _ = x_sample.shape
    depth = norm_g.shape[0]
    n_pages = page_table.shape[1]
    past = n_pages * PAGE_SIZE
    caches_a = (cache_a0_kv, cache_a1_kv, cache_a2_kv)
    assert T <= T_PAD and S % (DIL_CFG[-1][1] * BLK_A) == 0 and S >= WIN_C + QBLK_C

    hp = x_prompt.reshape(B * S, D)
    hs = jnp.pad(x_sample, ((0, 0), (0, T_PAD - T), (0, 0))).reshape(DB * T_PAD, D)

    pos_p = jnp.arange(S)
    pos_s = past + jnp.arange(T_PAD)
    cos_p1, sin_p1 = _rope_tables(pos_p)
    cos_s1, sin_s1 = _rope_tables(pos_s)
    cos_p, sin_p = jnp.tile(cos_p1, (B, 1)), jnp.tile(sin_p1, (B, 1))
    cos_s, sin_s = jnp.tile(cos_s1, (DB, 1)), jnp.tile(sin_s1, (DB, 1))

    a_p, a_s = [[], [], []], [[], [], []]
    bC_p, bC_s, bn_p, bn_s, bm_p, bm_s = [], [], [], [], [], []
    cc_p, cc_s, csl_p, csl_s, cw_p, cw_s = [], [], [], [], [], []

    for layer in range(depth):
        if layer % 2 == 0:
            e = layer // 2
            w_in = _prep_w_in_ab(w_in_ab[e])
            flags = _ab_rope_flags()
            w_out = w_out_ab[e].astype(BF16)
            z = norm_proj(hp, norm_g[layer, 0], w_in, flags, cos_p, sin_p)
            z3 = z.reshape(B, S, AB_N)
            os_, ls_ = [], []
            for gi, (win, dil) in enumerate(DIL_CFG):
                o, l = dil_prompt(z3, gi, win, dil)
                os_.append(o)
                ls_.append(l)
                nb = min(win, S)
                kk = z3[:, S - nb:, AB_KA + gi * AW:AB_KA + (gi + 1) * AW].reshape(B, nb, H_A, DH_A)
                vv = z3[:, S - nb:, AB_VA + gi * AW:AB_VA + (gi + 1) * AW].reshape(B, nb, H_A, DH_A)
                a_p[gi].append(jnp.stack([kk, vv], axis=2))
            zc = jnp.zeros
            hb, Cp, n_p, m_p = mlstm(z3, _gates_t(z3, MLSTM_CHUNK), b_if[e],
                                     zc((B, H_B, DH_B, DH_B), F32), zc((B, H_B, DH_B), F32), zc((B, H_B), F32),
                                     L=MLSTM_CHUNK, n_valid=MLSTM_CHUNK, bb=B if B <= 2 else 1)
            hp = ab_merge(os_, ls_, hb, z, g_mlstm[e], w_out, hp)
            bC_p.append(Cp); bn_p.append(n_p); bm_p.append(m_p)
            z = norm_proj(hs, norm_g[layer, 0], w_in, flags, cos_s, sin_s)
            z3 = z.reshape(DB, T_PAD, AB_N)
            os_, ls_ = [], []
            for gi, (win, dil) in enumerate(DIL_CFG):
                new_t = jnp.concatenate(
                    [jnp.swapaxes(z3[:, :, AB_KA + gi * AW:AB_KA + (gi + 1) * AW], 1, 2),
                     jnp.swapaxes(z3[:, :, AB_VA + gi * AW:AB_VA + (gi + 1) * AW], 1, 2)], axis=1)
                o, l, co = dil_sample(_channel_major(caches_a[gi][e], 1), z3, new_t, gi, win, dil, T)
                os_.append(o)
                ls_.append(l)
                a_s[gi].append(_row_major(co, 2, H_A, DH_A))
            bbs = 4 if DB % 4 == 0 else 1
            hb, Cs, n_s, m_s = mlstm(z3, _gates_t(z3, T_PAD), b_if[e], state_b_C[e], state_b_n[e], state_b_m[e],
                                     L=T_PAD, n_valid=T, bb=bbs)
            hs = ab_merge(os_, ls_, hb, z, g_mlstm[e], w_out, hs)
            bC_s.append(Cs); bn_s.append(n_s); bm_s.append(m_s)
        else:
            o_i = layer // 2
            w_in = jnp.pad(w_in_c[o_i], ((0, 0), (0, C_N - w_in_c.shape[-1]))).astype(BF16)
            flags = _c_rope_flags()
            w_out = w_out_c[o_i].astype(BF16)
            cw, cbias = _prep_compress(cmp_w1[o_i], cmp_w2[o_i], cmp_pe[o_i])
            z = norm_proj(hp, norm_g[layer, 0], w_in, flags, cos_p, sin_p)
            z3 = z.reshape(B, S, C_N)
            kv_cmp = z3[:, :, C_CMP:C_CMP + KVW]
            kv_sel = z3[:, :, C_SEL:C_SEL + KVW]
            kv_win = z3[:, :, C_WIN:C_WIN + KVW]
            kc = compress_prompt(z3, cw, cbias)
            n_cmp = (S - CMP_BLK) // CMP_STRIDE + 1
            n_slc = S // SLC_BLK
            mimp = _importance_matrix(S // CMP_STRIDE, n_cmp, n_slc, -(-n_slc // LANE) * LANE)
            ka, va = _sel_operands(z3)
            o = nsa_prompt(z3, kc, kv_win.astype(BF16), ka, va, cos_p1, sin_p1, mimp.T)
            hp = out_proj(o.reshape(B * S, C_Q), w_out, hp)
            nw = min(WIN_C, S)
            sh = lambda t: t.reshape(t.shape[0], t.shape[1], 2, KVH_C, DH_C)
            cc_p.append(sh(kv_cmp)); csl_p.append(sh(kv_sel)); cw_p.append(sh(kv_win[:, S - nw:]))
            z = norm_proj(hs, norm_g[layer, 0], w_in, flags, cos_s, sin_s)
            z3 = z.reshape(DB, T_PAD, C_N)
            kc = compress_paged(_channel_major(cache_c_cmp_kv[o_i], 1), page_table, cw, cbias)
            full_len = past + T
            n_cmp = (full_len - CMP_BLK) // CMP_STRIDE + 1
            assert (n_cmp + 1) * CMP_STRIDE <= past
            n_slc = past // SLC_BLK + -(-T // SLC_BLK)
            nsp = -(-n_slc // LANE) * LANE
            mimp = _importance_matrix(past // CMP_STRIDE, n_cmp, n_slc, nsp)
            o, wo = nsa_sample(z3, kc, _channel_major(cache_c_sel_kv[o_i], 1),
                               _channel_major(cache_c_win_kv[o_i], 1),
                               jnp.swapaxes(z3[:, :, C_WIN:C_WIN + KVW], 1, 2),
                               page_table, cos_s1, sin_s1, mimp, t_new=T)
            hs = out_proj(o.reshape(DB * T_PAD, C_Q), w_out, hs)
            cc_s.append(sh(z3[:, :T, C_CMP:C_CMP + KVW])); csl_s.append(sh(z3[:, :T, C_SEL:C_SEL + KVW]))
            cw_s.append(_row_major(wo, 2, KVH_C, DH_C))
        last = layer == depth - 1
        wg, wu, wd = (w_ffn_gate[layer].astype(BF16), w_ffn_up[layer].astype(BF16), w_ffn_down[layer].astype(BF16))
        hp = ffn(hp, norm_g[layer, 1], wg, wu, wd, norm_final, final_norm=last)
        hs = ffn(hs, norm_g[layer, 1], wg, wu, wd, norm_final, final_norm=last)

    y_prompt = hp.reshape(B, S, D)
    y_sample = hs.reshape(DB, T_PAD, D)[:, :T]
    st = lambda xs: jnp.stack(xs, axis=0)
    return (y_prompt, y_sample,
            st(a_p[0]), st(a_s[0]), st(a_p[1]), st(a_s[1]), st(a_p[2]), st(a_s[2]),
            st(bC_p), st(bC_s), st(bn_p), st(bn_s), st(bm_p), st(bm_s),
            st(cc_p), st(cc_s), st(csl_p), st(csl_s), st(cw_p), st(cw_s))
```

```python
import functools
import math

import numpy as np
import jax
import jax.numpy as jnp
from jax import lax
from jax.experimental import pallas as pl
from jax.experimental.pallas import tpu as pltpu

F32 = jnp.float32
BF16 = jnp.bfloat16

PAGE_SIZE = 128
DIL_CFG = ((128, 1), (512, 4), (2048, 16))
N_DIL = 3
H_A = 4
DH_A = 64
BLK_A = 128
H_B = 4
DH_B = 128
MLSTM_CHUNK = 128
H_C = 16
KVH_C = 2
G_C = H_C // KVH_C
DH_C = 64
CMP_STRIDE = 16
CMP_BLK = 2 * CMP_STRIDE
CMP_HID = 64
SLC_BLK = 64
N_SEL = 16
WIN_C = 512
QBLK_C = 128
ROPE_THETA = 10000.0
EPS = 1e-6
A_QW = N_DIL * H_A * DH_A
B_W = H_B * DH_B
AW = H_A * DH_A
C_Q = H_C * DH_C
C_KV = 3 * 2 * KVH_C * DH_C
KVW = 2 * KVH_C * DH_C

LANE = 128
SUBLANE = 8
VMEM_LIMIT = 48 * 1024 * 1024

NEG_BIG = -1e30
T_PAD = SUBLANE

AB_QB, AB_KB, AB_VB, AB_OG = 0, B_W, 2 * B_W, 3 * B_W
AB_QA = 4 * B_W
AB_KA = AB_QA + A_QW
AB_VA = AB_KA + A_QW
AB_IG = AB_VA + A_QW
AB_FG = AB_IG + LANE
AB_N = AB_FG + LANE

C_CMP = C_Q
C_SEL = C_Q + KVW
C_WIN = C_Q + 2 * KVW
C_GATE = C_Q + 3 * KVW
C_N = 2048


def _cparams(sem, vmem=VMEM_LIMIT):
    return pltpu.CompilerParams(dimension_semantics=sem, vmem_limit_bytes=vmem)


def _pick_tile(m, pref):
    t = min(m, pref)
    while m % t:
        t //= 2
    return t


def _rope_lanes(x, cos, sin):
    lane = lax.broadcasted_iota(jnp.int32, x.shape, 1)
    first = (lane % DH_C) < (DH_C // 2)
    partner = jnp.where(first, pltpu.roll(x, LANE - DH_C // 2, 1), pltpu.roll(x, DH_C // 2, 1))
    return x * cos + partner * sin


def _split3(x):
    hi = x.astype(BF16)
    r1 = x - hi.astype(F32)
    mid = r1.astype(BF16)
    lo = (r1 - mid.astype(F32)).astype(BF16)
    return hi, mid, lo


def _dot_exact_rhs01(x, m01):
    hi, mid, lo = _split3(x)
    d = lambda a: jnp.dot(a, m01, preferred_element_type=F32)
    return d(hi) + d(mid) + d(lo)


def _dot_exact_lhs01(m01, x):
    hi, mid, lo = _split3(x)
    d = lambda a: jnp.dot(m01, a, preferred_element_type=F32)
    return d(hi) + d(mid) + d(lo)


def _dot_nt(a, b):
    return lax.dot_general(a, b, (((1,), (1,)), ((), ())), preferred_element_type=F32)


def _dot_tn(a, b):
    return lax.dot_general(a, b, (((0,), (0,)), ((), ())), preferred_element_type=F32)


def _log_sigmoid(x):
    return jnp.minimum(x, 0.0) - jnp.log1p(jnp.exp(-jnp.abs(x)))


def _gelu_tanh(x):
    return 0.5 * x * (1.0 + jnp.tanh(math.sqrt(2.0 / math.pi) * (x + 0.044715 * (x * x * x))))


def _rms_rows(x, g):
    ms = jnp.mean(x * x, axis=-1, keepdims=True)
    return x * lax.rsqrt(ms + EPS) * g


NORM_PROJ_TN = 512
NORM_PROJ_VMEM = 56 * 1024 * 1024


def _norm_proj_kernel(x_ref, g_ref, w_ref, cos_ref, sin_ref, o_ref, *, rope_chunks):
    xn = _rms_rows(x_ref[...], g_ref[...]).astype(BF16)
    n = o_ref.shape[1]
    for c0 in range(0, n, NORM_PROJ_TN):
        cs = slice(c0, min(c0 + NORM_PROJ_TN, n))
        o_ref[:, cs] = jnp.dot(xn, w_ref[:, cs], preferred_element_type=F32)
    for c in rope_chunks:
        cs = slice(c * LANE, (c + 1) * LANE)
        o_ref[:, cs] = _rope_lanes(o_ref[:, cs], cos_ref[...], sin_ref[...])


def norm_proj(x, g, w_bf16, rope_chunks, cos_t, sin_t, *, tm_pref=512):
    m, d = x.shape
    n = w_bf16.shape[1]
    tm = _pick_tile(m, tm_pref)
    return pl.pallas_call(
        functools.partial(_norm_proj_kernel, rope_chunks=tuple(rope_chunks)),
        grid=(m // tm,),
        in_specs=[
            pl.BlockSpec((tm, d), lambda i: (i, 0)),
            pl.BlockSpec((1, d), lambda i: (0, 0)),
            pl.BlockSpec((d, n), lambda i: (0, 0), pipeline_mode=pl.Buffered(1)),
            pl.BlockSpec((tm, LANE), lambda i: (i, 0)),
            pl.BlockSpec((tm, LANE), lambda i: (i, 0)),
        ],
        out_specs=pl.BlockSpec((tm, n), lambda i: (i, 0)),
        out_shape=jax.ShapeDtypeStruct((m, n), F32),
        compiler_params=_cparams(("parallel",), NORM_PROJ_VMEM),
        name="norm_proj",
    )(x, g.reshape(1, d), w_bf16, cos_t, sin_t)


def _ffn_kernel(x_ref, g_ref, wg_ref, wu_ref, wd_ref, gf_ref, o_ref, xn_ref, *, final_norm):
    f = pl.program_id(1)

    @pl.when(f == 0)
    def _():
        x = x_ref[...]
        xn_ref[...] = _rms_rows(x, g_ref[...]).astype(BF16)
        o_ref[...] = x

    xn = xn_ref[...]
    a = jnp.dot(xn, wg_ref[...], preferred_element_type=F32)
    u = jnp.dot(xn, wu_ref[...], preferred_element_type=F32)
    act = (a * jax.nn.sigmoid(a)) * u
    o_ref[...] += jnp.dot(act.astype(BF16), wd_ref[...], preferred_element_type=F32)

    if final_norm:
        @pl.when(f == pl.num_programs(1) - 1)
        def _():
            o_ref[...] = _rms_rows(o_ref[...], gf_ref[...])


def ffn(x, g, wg, wu, wd, g_final, *, final_norm, tm_pref=512):
    m, d = x.shape
    dff = wg.shape[1]
    tf = dff // 2 if (dff // 2) % LANE == 0 else dff
    tm = _pick_tile(m, tm_pref)
    grid = (m // tm, dff // tf)
    return pl.pallas_call(
        functools.partial(_ffn_kernel, final_norm=final_norm),
        grid=grid,
        in_specs=[
            pl.BlockSpec((tm, d), lambda i, f: (i, 0)),
            pl.BlockSpec((1, d), lambda i, f: (0, 0)),
            pl.BlockSpec((d, tf), lambda i, f: (0, f)),
            pl.BlockSpec((d, tf), lambda i, f: (0, f)),
            pl.BlockSpec((tf, d), lambda i, f: (f, 0)),
            pl.BlockSpec((1, d), lambda i, f: (0, 0)),
        ],
        out_specs=pl.BlockSpec((tm, d), lambda i, f: (i, 0)),
        out_shape=jax.ShapeDtypeStruct((m, d), F32),
        scratch_shapes=[pltpu.VMEM((tm, d), BF16)],
        compiler_params=_cparams(("parallel", "arbitrary")),
        name="ffn",
    )(x, g.reshape(1, d), wg, wu, wd, g_final.reshape(1, d))


def _out_proj_kernel(x_ref, w_ref, r_ref, o_ref):
    o_ref[...] = r_ref[...] + jnp.dot(x_ref[...].astype(BF16), w_ref[...], preferred_element_type=F32)


def out_proj(x, w_bf16, resid, *, tm_pref=512):
    m, k = x.shape
    n = w_bf16.shape[1]
    tm = _pick_tile(m, tm_pref)
    return pl.pallas_call(
        _out_proj_kernel,
        grid=(m // tm,),
        in_specs=[
            pl.BlockSpec((tm, k), lambda i: (i, 0)),
            pl.BlockSpec((k, n), lambda i: (0, 0)),
            pl.BlockSpec((tm, n), lambda i: (i, 0)),
        ],
        out_specs=pl.BlockSpec((tm, n), lambda i: (i, 0)),
        out_shape=jax.ShapeDtypeStruct((m, n), F32),
        compiler_params=_cparams(("parallel",)),
        name="out_proj",
    )(x, w_bf16, resid)


DIL_STEP_ROWS = 2048
DIL_UNROLL = 4


def _dil_prompt_kernel(q_ref, kp_ref, kc_ref, vp_ref, vc_ref, o_ref, l_ref, *, nback, dil, nblk):
    n = pl.program_id(2)
    span = dil * BLK_A
    qi = BLK_A + lax.broadcasted_iota(jnp.int32, (BLK_A, 2 * BLK_A), 0)
    ki = lax.broadcasted_iota(jnp.int32, (BLK_A, 2 * BLK_A), 1)
    rel = qi - ki
    band = (rel >= 0) & (rel <= nback)
    own = ki >= BLK_A

    def one(it, carry):
        j = it // dil
        r = it % dil
        start = j * span + r
        if dil > 1:
            take = lambda st: pl.ds(st, BLK_A, stride=dil)
        else:
            take = lambda st: pl.ds(pl.multiple_of(st, BLK_A), BLK_A)
        rows = take(start)
        prev_rows = take(jnp.maximum(start - span, r))
        first_rows = take(r)
        q = q_ref[0, rows, :] * (DH_A ** -0.5)
        kprev = jnp.where(j > 0, kc_ref[0, prev_rows, :], kp_ref[0, first_rows, :])
        vprev = jnp.where(j > 0, vc_ref[0, prev_rows, :], vp_ref[0, first_rows, :])
        kk = jnp.concatenate([kprev, kc_ref[0, rows, :]], axis=0)
        vv = jnp.concatenate([vprev, vc_ref[0, rows, :]], axis=0)
        mask = band & ((n * nblk + j > 0) | own)
        outs, lses = [], []
        for h in range(LANE // DH_A):
            hs = slice(h * DH_A, (h + 1) * DH_A)
            s = _dot_nt(q[:, hs].astype(BF16), kk[:, hs].astype(BF16))
            s = jnp.where(mask, s, -jnp.inf)
            m = jnp.max(s, axis=-1, keepdims=True)
            p = jnp.exp(s - m)
            l = jnp.sum(p, axis=-1, keepdims=True)
            o = jnp.dot(p.astype(BF16), vv[:, hs].astype(BF16), preferred_element_type=F32) / l
            outs.append(o)
            lses.append(jnp.broadcast_to(m + jnp.log(l), (BLK_A, DH_A)))
        o_ref[0, rows, :] = jnp.concatenate(outs, axis=-1)
        l_ref[0, rows, :] = jnp.concatenate(lses, axis=-1)
        return carry

    lax.fori_loop(0, nblk * dil, one, 0, unroll=DIL_UNROLL)


def dil_prompt(z3, gi, window, dil):
    b, s, _ = z3.shape
    nback = window // dil
    span = dil * BLK_A
    nblk = max(1, DIL_STEP_ROWS // span)
    rows = nblk * span
    assert s % rows == 0 and (nblk * dil) % DIL_UNROLL == 0
    nh = AW // LANE
    qo, ko, vo = (AB_QA + gi * AW) // LANE, (AB_KA + gi * AW) // LANE, (AB_VA + gi * AW) // LANE
    blk = (1, rows, LANE)
    pblk = (1, span, LANE)
    in_specs = [
        pl.BlockSpec(blk, lambda bb, hp, i: (bb, i, qo + hp)),
        pl.BlockSpec(pblk, lambda bb, hp, i: (bb, jnp.maximum(i * nblk - 1, 0), ko + hp)),
        pl.BlockSpec(blk, lambda bb, hp, i: (bb, i, ko + hp)),
        pl.BlockSpec(pblk, lambda bb, hp, i: (bb, jnp.maximum(i * nblk - 1, 0), vo + hp)),
        pl.BlockSpec(blk, lambda bb, hp, i: (bb, i, vo + hp)),
    ]
    out_spec = pl.BlockSpec(blk, lambda bb, hp, i: (bb, i, hp))
    o, l = pl.pallas_call(
        functools.partial(_dil_prompt_kernel, nback=nback, dil=dil, nblk=nblk),
        grid=(b, nh, s // rows),
        in_specs=in_specs,
        out_specs=[out_spec, out_spec],
        out_shape=[jax.ShapeDtypeStruct((b, s, AW), F32)] * 2,
        compiler_params=_cparams(("parallel", "parallel", "arbitrary")),
        name="dil_prompt_%d" % gi,
    )(z3, z3, z3, z3, z3)
    return o.reshape(b * s, AW), l.reshape(b * s, AW)


def _dil_sample_kernel(c_ref, q_ref, kn_ref, vn_ref, nt_ref, o_ref, l_ref, co_ref, *, window, dil, n_buf, t_new):
    rows = H_A * T_PAD
    q = q_ref[0] * (DH_A ** -0.5)
    q4 = jnp.concatenate([q] * H_A, axis=0)
    rr = lax.broadcasted_iota(jnp.int32, (rows, AW), 0)
    ll = lax.broadcasted_iota(jnp.int32, (rows, AW), 1)
    head_sel = (rr // T_PAD) == (ll // DH_A)
    qbd = jnp.where(head_sel, q4, 0.0).astype(BF16)
    kbt = c_ref[0, 0:AW, :].astype(BF16)
    vbt = c_ref[0, AW:2 * AW, :].astype(BF16)
    kn = kn_ref[0]
    vn = vn_ref[0]
    s_buf = jnp.dot(qbd, kbt, preferred_element_type=F32)
    s_new = _dot_nt(qbd, kn.astype(BF16))
    t_b = lax.broadcasted_iota(jnp.int32, (rows, n_buf), 0) % T_PAD
    c_b = lax.broadcasted_iota(jnp.int32, (rows, n_buf), 1)
    d_b = n_buf + t_b - c_b
    m_b = ((d_b % dil) == 0) & (d_b <= window)
    t_n = lax.broadcasted_iota(jnp.int32, (rows, T_PAD), 0) % T_PAD
    u_n = lax.broadcasted_iota(jnp.int32, (rows, T_PAD), 1)
    d_n = t_n - u_n
    m_n = (d_n >= 0) & ((d_n % dil) == 0) & (d_n <= window) & (u_n < t_new)
    s_buf = jnp.where(m_b, s_buf, NEG_BIG)
    s_new = jnp.where(m_n, s_new, NEG_BIG)
    mx = jnp.maximum(jnp.max(s_buf, axis=-1, keepdims=True), jnp.max(s_new, axis=-1, keepdims=True))
    p_b = jnp.where(m_b, jnp.exp(s_buf - mx), 0.0)
    p_n = jnp.where(m_n, jnp.exp(s_new - mx), 0.0)
    l = jnp.sum(p_b, axis=-1, keepdims=True) + jnp.sum(p_n, axis=-1, keepdims=True)
    l = jnp.maximum(l, 1e-30)
    acc = (_dot_nt(p_b.astype(BF16), vbt)
           + jnp.dot(p_n.astype(BF16), vn.astype(BF16), preferred_element_type=F32))
    res = jnp.where(head_sel, acc / l, 0.0)
    lse = jnp.where(head_sel, mx + jnp.log(l), 0.0)
    o = res[0:T_PAD]
    ls = lse[0:T_PAD]
    for h in range(1, H_A):
        o = o + res[h * T_PAD:(h + 1) * T_PAD]
        ls = ls + lse[h * T_PAD:(h + 1) * T_PAD]
    o_ref[0] = o
    l_ref[0] = ls
    co_ref[0, :, 0:n_buf - t_new] = c_ref[0, :, t_new:n_buf]
    co_ref[0, :, n_buf - t_new:n_buf] = nt_ref[0, :, 0:t_new]


def dil_sample(cache_t, z3, new_t, gi, window, dil, t_new):
    db, _, n_buf = cache_t.shape
    qo, ko, vo = AB_QA // AW + gi, AB_KA // AW + gi, AB_VA // AW + gi
    blk = (1, T_PAD, AW)
    o, l, co = pl.pallas_call(
        functools.partial(_dil_sample_kernel, window=window, dil=dil, n_buf=n_buf, t_new=t_new),
        grid=(db,),
        in_specs=[
            pl.BlockSpec((1, 2 * AW, n_buf), lambda b: (b, 0, 0)),
            pl.BlockSpec(blk, lambda b: (b, 0, qo)),
            pl.BlockSpec(blk, lambda b: (b, 0, ko)),
            pl.BlockSpec(blk, lambda b: (b, 0, vo)),
            pl.BlockSpec((1, 2 * AW, T_PAD), lambda b: (b, 0, 0)),
        ],
        out_specs=[
            pl.BlockSpec(blk, lambda b: (b, 0, 0)),
            pl.BlockSpec(blk, lambda b: (b, 0, 0)),
            pl.BlockSpec((1, 2 * AW, n_buf), lambda b: (b, 0, 0)),
        ],
        out_shape=[
            jax.ShapeDtypeStruct((db, T_PAD, AW), F32),
            jax.ShapeDtypeStruct((db, T_PAD, AW), F32),
            jax.ShapeDtypeStruct((db, 2 * AW, n_buf), F32),
        ],
        compiler_params=_cparams(("parallel",)),
        name="dil_sample_%d" % gi,
    )(cache_t, z3, z3, z3, new_t)
    return o.reshape(db * T_PAD, AW), l.reshape(db * T_PAD, AW), co


def _mlstm_kernel(q_ref, k_ref, v_ref, gi_ref, gf_ref, gt_ref, bi_ref, bf_ref, b8_ref,
                  c0_ref, n0_ref, m0_ref, h_ref, co_ref, no_ref, mo_ref,
                  c_s, n_s, m_s, *, bb, L, n_valid):
    c = pl.program_id(1)

    @pl.when(c == 0)
    def _():
        c_s[...] = c0_ref[...]
        n_s[...] = n0_ref[...]
        m_s[...] = m0_ref[...]

    row = lax.broadcasted_iota(jnp.int32, (L, L), 0)
    col = lax.broadcasted_iota(jnp.int32, (L, L), 1)
    tri = row >= col
    tri_l = jnp.where(tri, 1.0, 0.0).astype(BF16)
    tri_u = jnp.where(row <= col, 1.0, 0.0).astype(BF16)
    lane = lax.broadcasted_iota(jnp.int32, (1, LANE), 1)
    for b in range(bb):
        ig_col = gi_ref[b] + bi_ref[...]
        lf_col = _log_sigmoid(gf_ref[b] + bf_ref[...])
        gt = gt_ref[b, 0]
        ig_row = gt[0:H_B] + b8_ref[0:H_B]
        lf_row = _log_sigmoid(gt[H_B:2 * H_B] + b8_ref[H_B:2 * H_B])
        if n_valid < L:
            rv = lax.broadcasted_iota(jnp.int32, (L, LANE), 0) < n_valid
            ig_col = jnp.where(rv, ig_col, -jnp.inf)
            lf_col = jnp.where(rv, lf_col, 0.0)
            cv = lax.broadcasted_iota(jnp.int32, (H_B, L), 1) < n_valid
            ig_row = jnp.where(cv, ig_row, -jnp.inf)
            lf_row = jnp.where(cv, lf_row, 0.0)
        b_col = _dot_exact_lhs01(tri_l, lf_col)
        b_row = _dot_exact_rhs01(lf_row, tri_u)
        m_row = m_s[b]
        a_col = b_col + m_row
        m_new = m_row
        hs_out = []
        for h in range(H_B):
            hsl = slice(h * DH_B, (h + 1) * DH_B)
            bc = b_col[:, h:h + 1]
            ac = a_col[:, h:h + 1]
            icol = ig_col[:, h:h + 1]
            D = bc - b_row[h:h + 1, :] + ig_row[h:h + 1, :]
            D = jnp.where(tri, D, -jnp.inf)
            mt = jnp.maximum(ac, jnp.max(D, axis=-1, keepdims=True))
            Dw = jnp.exp(D - mt)
            iw = jnp.exp(ac - mt)
            qf = q_ref[b, :, hsl]
            kf = k_ref[b, :, hsl] * (DH_B ** -0.5)
            vf = v_ref[b, :, hsl]
            qb, kb, vb = qf.astype(BF16), kf.astype(BF16), vf.astype(BF16)
            Cm = c_s[b * H_B + h]
            nv = n_s[b * H_B + h]
            sc = _dot_nt(qb, kb) * Dw
            num = iw * _dot_nt(qb, Cm.astype(BF16)) + jnp.dot(sc.astype(BF16), vb, preferred_element_type=F32)
            den = iw * jnp.sum(qf * nv, axis=-1, keepdims=True) + jnp.sum(sc, axis=-1, keepdims=True)
            hs_out.append(num / jnp.maximum(jnp.abs(den), jnp.exp(-mt)))
            mL = mt[L - 1:L, :]
            wL = jnp.exp(bc[L - 1:L, :] - bc + icol - mL)
            dec = jnp.exp(ac[L - 1:L, :] - mL)
            c_s[b * H_B + h] = dec * Cm + _dot_tn((vf * wL).astype(BF16), kb)
            n_s[b * H_B + h] = dec * nv + jnp.sum(wL * kf, axis=0, keepdims=True)
            m_new = jnp.where(lane == h, mL, m_new)
        m_s[b] = m_new
        h_ref[b] = jnp.concatenate(hs_out, axis=-1)

    @pl.when(c == pl.num_programs(1) - 1)
    def _():
        co_ref[...] = c_s[...]
        no_ref[...] = n_s[...]
        mo_ref[...] = m_s[...]


def mlstm(z3, gt, b_if, c0, n0, m0, *, L, n_valid, bb):
    b, s, _ = z3.shape
    nc = s // L
    bi_row = jnp.zeros((1, LANE), F32).at[0, :H_B].set(b_if[0])
    bf_row = jnp.zeros((1, LANE), F32).at[0, :H_B].set(b_if[1])
    b8 = b_if.reshape(2 * H_B, 1)
    c0r = c0.reshape(b * H_B, DH_B, DH_B)
    n0r = n0.reshape(b * H_B, 1, DH_B)
    m0r = jnp.zeros((b, 1, LANE), F32).at[:, 0, :H_B].set(m0)
    cw = B_W // LANE
    h, co, no, mo = pl.pallas_call(
        functools.partial(_mlstm_kernel, bb=bb, L=L, n_valid=n_valid),
        grid=(b // bb, nc),
        in_specs=[
            pl.BlockSpec((bb, L, B_W), lambda g, c: (g, c, AB_QB // B_W)),
            pl.BlockSpec((bb, L, B_W), lambda g, c: (g, c, AB_KB // B_W)),
            pl.BlockSpec((bb, L, B_W), lambda g, c: (g, c, AB_VB // B_W)),
            pl.BlockSpec((bb, L, LANE), lambda g, c: (g, c, AB_IG // LANE)),
            pl.BlockSpec((bb, L, LANE), lambda g, c: (g, c, AB_FG // LANE)),
            pl.BlockSpec((bb, 1, 2 * H_B, L), lambda g, c: (g, c, 0, 0)),
            pl.BlockSpec((1, LANE), lambda g, c: (0, 0)),
            pl.BlockSpec((1, LANE), lambda g, c: (0, 0)),
            pl.BlockSpec((2 * H_B, 1), lambda g, c: (0, 0)),
            pl.BlockSpec((bb * H_B, DH_B, DH_B), lambda g, c: (g, 0, 0)),
            pl.BlockSpec((bb * H_B, 1, DH_B), lambda g, c: (g, 0, 0)),
            pl.BlockSpec((bb, 1, LANE), lambda g, c: (g, 0, 0)),
        ],
        out_specs=[
            pl.BlockSpec((bb, L, B_W), lambda g, c: (g, c, 0)),
            pl.BlockSpec((bb * H_B, DH_B, DH_B), lambda g, c: (g, 0, 0)),
            pl.BlockSpec((bb * H_B, 1, DH_B), lambda g, c: (g, 0, 0)),
            pl.BlockSpec((bb, 1, LANE), lambda g, c: (g, 0, 0)),
        ],
        out_shape=[
            jax.ShapeDtypeStruct((b, s, B_W), F32),
            jax.ShapeDtypeStruct((b * H_B, DH_B, DH_B), F32),
            jax.ShapeDtypeStruct((b * H_B, 1, DH_B), F32),
            jax.ShapeDtypeStruct((b, 1, LANE), F32),
        ],
        scratch_shapes=[
            pltpu.VMEM((bb * H_B, DH_B, DH_B), F32),
            pltpu.VMEM((bb * H_B, 1, DH_B), F32),
            pltpu.VMEM((bb, 1, LANE), F32),
        ],
        compiler_params=_cparams(("parallel", "arbitrary")),
        name="mlstm_L%d" % L,
    )(z3, z3, z3, z3, z3, gt, bi_row, bf_row, b8, c0r, n0r, m0r)
    del cw
    return (h.reshape(b * s, B_W), co.reshape(b, H_B, DH_B, DH_B), no.reshape(b, H_B, DH_B),
            mo[:, 0, :H_B])


def _ab_merge_kernel(o0, o1, o2, l0, l1, l2, hb_ref, og_ref, g_ref, w_ref, r_ref, out_ref):
    a0, a1, a2 = l0[...], l1[...], l2[...]
    mx = jnp.maximum(jnp.maximum(a0, a1), a2)
    e0, e1, e2 = jnp.exp(a0 - mx), jnp.exp(a1 - mx), jnp.exp(a2 - mx)
    o_a = (e0 * o0[...] + e1 * o1[...] + e2 * o2[...]) / (e0 + e1 + e2)
    hb = hb_ref[...]
    parts = []
    for h in range(H_B):
        hs = slice(h * DH_B, (h + 1) * DH_B)
        x = hb[:, hs]
        parts.append(x * lax.rsqrt(jnp.mean(x * x, axis=-1, keepdims=True) + EPS))
    hbn = jnp.concatenate(parts, axis=-1) * g_ref[...] * jax.nn.sigmoid(og_ref[...])
    y = (jnp.dot(o_a.astype(BF16), w_ref[0:AW, :], preferred_element_type=F32)
         + jnp.dot(hbn.astype(BF16), w_ref[AW:AW + B_W, :], preferred_element_type=F32))
    out_ref[...] = r_ref[...] + y


def ab_merge(os_, ls_, hb, z2, g_mn, w_bf16, resid, *, tm_pref=512):
    m, d = resid.shape
    tm = _pick_tile(m, tm_pref)
    a_spec = pl.BlockSpec((tm, AW), lambda i: (i, 0))
    return pl.pallas_call(
        _ab_merge_kernel,
        grid=(m // tm,),
        in_specs=[a_spec] * 6 + [
            pl.BlockSpec((tm, B_W), lambda i: (i, 0)),
            pl.BlockSpec((tm, B_W), lambda i: (i, AB_OG // B_W)),
            pl.BlockSpec((1, B_W), lambda i: (0, 0)),
            pl.BlockSpec((AW + B_W, d), lambda i: (0, 0)),
            pl.BlockSpec((tm, d), lambda i: (i, 0)),
        ],
        out_specs=pl.BlockSpec((tm, d), lambda i: (i, 0)),
        out_shape=jax.ShapeDtypeStruct((m, d), F32),
        compiler_params=_cparams(("parallel",)),
        name="ab_merge",
    )(*os_, *ls_, hb, z2, g_mn.reshape(1, B_W), w_bf16, resid)


HKV = KVH_C * DH_C


def _compress_bias_kernel(pa_ref, pb_ref, w_ref, o_ref):
    w = w_ref[0]
    o_ref[0] = (jnp.dot(pa_ref[0].astype(BF16), w, preferred_element_type=F32)[:, 0:HKV]
                + jnp.dot(pb_ref[0].astype(BF16), w, preferred_element_type=F32)[:, HKV:2 * HKV])


def compress_bias(pa, pb, wab2):
    spec = lambda a: pl.BlockSpec((1,) + a.shape[1:], lambda i: (i,) + (0,) * (a.ndim - 1))
    return pl.pallas_call(
        _compress_bias_kernel,
        grid=(2,),
        in_specs=[spec(pa), spec(pb), spec(wab2)],
        out_specs=pl.BlockSpec((1, 1, HKV), lambda i: (i, 0, 0)),
        out_shape=jax.ShapeDtypeStruct((2, 1, HKV), F32),
        compiler_params=_cparams(("arbitrary",)),
        name="compress_bias",
    )(pa, pb, wab2)


def _compress_halves(load_rows, n, w_all):
    x = jnp.concatenate([load_rows(r, n).astype(BF16) for r in range(CMP_STRIDE)], axis=1)
    return jnp.dot(x, w_all, preferred_element_type=F32)


def _compress_prompt_kernel(x_ref, wab_ref, bias_ref, w2_ref, o_ref):
    n = x_ref.shape[1] // CMP_STRIDE
    acc = _compress_halves(lambda r, m: x_ref[0, pl.ds(r, m, stride=CMP_STRIDE), :], n, wab_ref[0])
    hid = acc[:, 0:HKV] + pltpu.roll(acc[:, HKV:2 * HKV], n - 1, 0) + bias_ref[0]
    o_ref[0] = jnp.dot(_gelu_tanh(hid).astype(BF16), w2_ref[0], preferred_element_type=F32)


def compress_prompt(z3, cw, bias):
    b, s, _ = z3.shape
    wab, w2 = cw
    n = s // CMP_STRIDE
    return pl.pallas_call(
        _compress_prompt_kernel,
        grid=(b, 2),
        in_specs=[
            pl.BlockSpec((1, s, HKV), lambda i, kv: (i, 0, C_CMP // HKV + kv)),
            pl.BlockSpec((1,) + wab.shape[1:], lambda i, kv: (kv, 0, 0)),
            pl.BlockSpec((1, 1, HKV), lambda i, kv: (kv, 0, 0)),
            pl.BlockSpec((1, HKV, HKV), lambda i, kv: (kv, 0, 0)),
        ],
        out_specs=pl.BlockSpec((1, n, HKV), lambda i, kv: (i, 0, kv)),
        out_shape=jax.ShapeDtypeStruct((b, n, KVW), F32),
        compiler_params=_cparams(("parallel", "arbitrary")),
        name="compress_prompt",
    )(z3, wab, bias, w2)


def _compress_paged_kernel(pt_ref, *refs, pg):
    del pt_ref
    page_refs = refs[:pg + 1]
    wab_ref, bias_ref, w2_ref, o_ref, xs_ref = refs[pg + 1:]
    for j in range(pg + 1):
        t = page_refs[j][0].T
        xs_ref[0, j * PAGE_SIZE:(j + 1) * PAGE_SIZE, :] = t[:, 0:HKV]
        xs_ref[1, j * PAGE_SIZE:(j + 1) * PAGE_SIZE, :] = t[:, HKV:2 * HKV]
    cpp = PAGE_SIZE // CMP_STRIDE
    n = pg * cpp
    for kv in range(2):
        acc = _compress_halves(lambda r, m: xs_ref[kv, pl.ds(r, m, stride=CMP_STRIDE), :], n + cpp, wab_ref[kv])
        hid = acc[0:n, 0:HKV] + acc[1:n + 1, HKV:2 * HKV] + bias_ref[kv]
        o_ref[0, :, kv * HKV:(kv + 1) * HKV] = jnp.dot(_gelu_tanh(hid).astype(BF16), w2_ref[kv],
                                                        preferred_element_type=F32)


def compress_paged(pool_t, page_table, cw, bias, *, pg=32):
    db, n_pages = page_table.shape
    cpp = PAGE_SIZE // CMP_STRIDE
    pg = min(pg, n_pages)
    assert n_pages % pg == 0
    wab, w2 = cw
    full = lambda a: pl.BlockSpec(a.shape, lambda b, g, pt: (0,) * a.ndim)

    def page_spec(j):
        return pl.BlockSpec((1, KVW, PAGE_SIZE),
                            lambda b, g, pt: (pt[b, jnp.minimum(g * pg + j, n_pages - 1)], 0, 0))

    gs = pltpu.PrefetchScalarGridSpec(
        num_scalar_prefetch=1,
        grid=(db, n_pages // pg),
        in_specs=[page_spec(j) for j in range(pg + 1)] + [full(wab), full(bias), full(w2)],
        out_specs=pl.BlockSpec((1, pg * cpp, KVW), lambda b, g, pt: (b, g, 0)),
        scratch_shapes=[pltpu.VMEM((2, (pg + 1) * PAGE_SIZE, HKV), F32)],
    )
    return pl.pallas_call(
        functools.partial(_compress_paged_kernel, pg=pg),
        grid_spec=gs,
        out_shape=jax.ShapeDtypeStruct((db, n_pages * cpp, KVW), F32),
        compiler_params=_cparams(("parallel", "arbitrary")),
        name="compress_paged",
    )(page_table, *([pool_t] * (pg + 1)), wab, bias, w2)


def _stack_heads(x, t):
    del t
    return jnp.concatenate([x[:, g * DH_C:(g + 1) * DH_C] for g in range(G_C)], axis=0)


def _mask_rows(s, mask, fill):
    t, n = mask.shape
    return jnp.where(mask[None], s.reshape(G_C, t, n), fill).reshape(G_C * t, n)


def _cmp_branch(q2, kc, vc, t_pos, tq):
    n = kc.shape[0]
    s = _dot_nt(q2, kc)
    cend = lax.broadcasted_iota(jnp.int32, (tq, n), 1) * CMP_STRIDE + (CMP_BLK - 1)
    s = _mask_rows(s, cend <= t_pos, -jnp.inf)
    m = jnp.max(s, axis=-1, keepdims=True)
    m = jnp.where(m > -jnp.inf, m, 0.0)
    p = jnp.exp(s - m)
    p = p / jnp.maximum(jnp.sum(p, axis=-1, keepdims=True), 1e-30)
    o = jnp.dot(p.astype(BF16), vc, preferred_element_type=F32)
    pg = p[0:tq]
    for g in range(1, G_C):
        pg = pg + p[g * tq:(g + 1) * tq]
    return o, pg


def _select_blocks(imp, t_pos):
    tq, nsp = imp.shape
    j = lax.broadcasted_iota(jnp.int32, (tq, nsp), 1)
    jf = j.astype(F32)
    cur = t_pos // SLC_BLK
    forced = (j == 0) | (j == cur) | (j == cur - 1)
    work = jnp.where(forced, jnp.inf, jnp.where(j <= cur, imp, -jnp.inf))
    sel = jnp.zeros((tq, nsp), F32)
    for _ in range(N_SEL):
        mx = jnp.max(work, axis=-1, keepdims=True)
        first = jnp.min(jnp.where(work == mx, jf, float(nsp)), axis=-1, keepdims=True)
        pick = jf == first
        sel = jnp.where(pick, jnp.where(mx > -jnp.inf, 1.0, sel), sel)
        work = jnp.where(pick, -jnp.inf, work)
    return sel


def _select_blocks_t(imp_t, t_row):
    nsp, n = imp_t.shape
    j = lax.broadcasted_iota(jnp.int32, (nsp, n), 0)
    jf = j.astype(F32)
    cur = t_row // SLC_BLK
    forced = (j == 0) | (j == cur) | (j == cur - 1)
    work = jnp.where(forced, jnp.inf, jnp.where(j <= cur, imp_t, -jnp.inf))
    sel = jnp.zeros((nsp, n), F32)
    for _ in range(N_SEL):
        mx = jnp.max(work, axis=0, keepdims=True)
        first = jnp.min(jnp.where(work == mx, jf, float(nsp)), axis=0, keepdims=True)
        pick = jf == first
        sel = jnp.where(pick, jnp.where(mx > -jnp.inf, 1.0, sel), sel)
        work = jnp.where(pick, -jnp.inf, work)
    return sel


def _flash_update(s, mask, v, m_ref, l_ref, acc_ref, v_channel_major=False):
    s = _mask_rows(s, mask, NEG_BIG)
    m_old = m_ref[...]
    m_new = jnp.maximum(m_old, jnp.max(s, axis=-1, keepdims=True))
    alpha = jnp.exp(m_old - m_new)
    p = jnp.exp(s - m_new)
    l_ref[...] = alpha * l_ref[...] + jnp.sum(p, axis=-1, keepdims=True)
    pv = _dot_nt(p.astype(BF16), v) if v_channel_major else jnp.dot(p.astype(BF16), v, preferred_element_type=F32)
    acc_ref[...] = alpha * acc_ref[...] + pv
    m_ref[...] = m_new


def _flash_finish(m_ref, l_ref, acc_ref):
    return jnp.where(m_ref[...] > 0.5 * NEG_BIG, acc_ref[...] / jnp.maximum(l_ref[...], 1e-30), 0.0)


def _softmax_av(s, mask, v):
    s = _mask_rows(s, mask, -jnp.inf)
    m = jnp.max(s, axis=-1, keepdims=True)
    m = jnp.where(m > -jnp.inf, m, 0.0)
    p = jnp.exp(s - m)
    p = p / jnp.maximum(jnp.sum(p, axis=-1, keepdims=True), 1e-30)
    return jnp.dot(p.astype(BF16), v, preferred_element_type=F32)


def _gate_mix(gates, h, o_cmp, o_sel, o_win, tq):
    outs = []
    for g in range(G_C):
        base = (h * G_C + g) * 3
        rs = slice(g * tq, (g + 1) * tq)
        outs.append(gates[:, base:base + 1] * o_cmp[rs] + gates[:, base + 1:base + 2] * o_sel[rs]
                    + gates[:, base + 2:base + 3] * o_win[rs])
    return jnp.concatenate(outs, axis=-1)


def _rope_q(qh, cos, sin):
    return jnp.concatenate(
        [_rope_lanes(qh[:, c * LANE:(c + 1) * LANE], cos, sin) for c in range(G_C * DH_C // LANE)], axis=-1)


SEL_TK = 1024


SEL_PHASE = DH_C


def _nsa_prompt_kernel(q_ref, gz_ref, cos_ref, sin_ref, kc_ref, kv_ref, ka_ref, va_ref, mimp_ref, o_ref,
                       qa_s, m_s, acc_s, *, nsp):
    i = pl.program_id(1)
    tq = QBLK_C
    rows = G_C * tq
    s0 = i * tq
    t_pos = s0 + lax.broadcasted_iota(jnp.int32, (tq, 1), 0)
    gates = jax.nn.sigmoid(gz_ref[0])
    cos, sin = cos_ref[...], sin_ref[...]
    scale = DH_C ** -0.5
    hw = G_C * DH_C
    ncmp = kc_ref.shape[1]
    t_row = s0 + lax.broadcasted_iota(jnp.int32, (1, rows), 1) % tq
    visible = (lax.broadcasted_iota(jnp.int32, (ncmp, 1), 0) * CMP_STRIDE + (CMP_BLK - 1)) <= t_row
    any_visible = t_row >= CMP_BLK - 1
    zpad = jnp.zeros((tq, DH_C), F32)
    kcc = kc_ref[0, :, 0:HKV].astype(BF16)
    vcc = kc_ref[0, :, HKV:2 * HKV].astype(BF16)
    q_rot, o_cmps, imps_t = [], [], []
    for h in range(KVH_C):
        qh = q_ref[0, :, h * hw:(h + 1) * hw] * scale
        q_rot.append(_rope_q(qh, cos, sin))
        qc = jnp.concatenate([jnp.concatenate([x, zpad] if h == 0 else [zpad, x], axis=1)
                              for x in (qh[:, g * DH_C:(g + 1) * DH_C] for g in range(G_C))], axis=0)
        st = jnp.where(visible, _dot_nt(kcc, qc.astype(BF16)), NEG_BIG)
        pt = jnp.exp(st - jnp.max(st, axis=0, keepdims=True))
        inv = jnp.where(any_visible, 1.0 / jnp.maximum(jnp.sum(pt, axis=0, keepdims=True), 1e-30), 0.0)
        pt = pt * inv
        o_cmps.append(_dot_tn(vcc, pt.astype(BF16))[h * DH_C:(h + 1) * DH_C, :].T)
        pgrp_t = pt[:, 0:tq]
        for g in range(1, G_C):
            pgrp_t = pgrp_t + pt[:, g * tq:(g + 1) * tq]
        imps_t.append(_dot_exact_lhs01(mimp_ref[...], pgrp_t))
    t_row2 = s0 + lax.broadcasted_iota(jnp.int32, (1, KVH_C * tq), 1) % tq
    sel_t = _select_blocks_t(jnp.concatenate(imps_t, axis=1), t_row2)

    half_of_lane = lax.broadcasted_iota(jnp.int32, (1, LANE), 1) // DH_C

    def into_half(chunk, src_half, dst_half):
        moved = chunk if src_half == dst_half else pltpu.roll(chunk, DH_C, 1)
        return jnp.where(half_of_lane == dst_half, moved, 0.0)

    for h in range(KVH_C):
        selb = jnp.where(sel_t[:, h * tq:(h + 1) * tq].T > 0.5, 0.0, NEG_BIG)
        q_part = jnp.concatenate(
            [into_half(q_rot[h][:, (g // 2) * LANE:(g // 2 + 1) * LANE], g % 2, h) for g in range(G_C)], axis=0)
        for ph in range(nsp // SEL_PHASE):
            sb = into_half(selb[:, (ph // 2) * LANE:(ph // 2 + 1) * LANE], ph % 2, 1 - h)
            qa_s[h, ph] = (q_part + jnp.concatenate([sb] * G_C, axis=0)).astype(BF16)
    m_s[...] = jnp.full(m_s.shape, NEG_BIG, F32)
    acc_s[...] = jnp.zeros(acc_s.shape, F32)

    def tile(k0, size, causal):
        k0 = pl.multiple_of(k0, size)
        ph = k0 // (SEL_PHASE * SLC_BLK)
        for h in range(KVH_C):
            st = _dot_nt(ka_ref[0, h, pl.ds(k0, size), :], qa_s[h, ph])
            if causal:
                kpos = k0 + lax.broadcasted_iota(jnp.int32, (size, 1), 0)
                t_row = s0 + lax.broadcasted_iota(jnp.int32, (1, rows), 1) % tq
                st = jnp.where(kpos <= t_row, st, NEG_BIG)
            m_old = m_s[h]
            m_new = jnp.maximum(m_old, jnp.max(st, axis=0, keepdims=True))
            pt = jnp.exp(st - m_new).astype(BF16)
            acc_s[h] = (jnp.exp(m_old - m_new) * acc_s[h]
                        + jnp.dot(va_ref[0, h, :, pl.ds(k0, size)], pt, preferred_element_type=F32))
            m_s[h] = m_new

    def full_tile(kt, carry):
        tile(kt * SEL_TK, SEL_TK, False)
        return carry

    half = SEL_TK // 2
    own = (s0 + tq - 1) // half
    lax.fori_loop(0, own // 2, full_tile, 0)

    @pl.when(own % 2 == 1)
    def _():
        tile((own - 1) * half, half, False)

    tile(own * half, half, True)

    def finish(acc, h):
        sums = acc[(1 - h) * DH_C:(1 - h) * DH_C + 1, :]
        return (acc[h * DH_C:(h + 1) * DH_C, :] / jnp.maximum(sums, 1e-30)).T

    nw = WIN_C + tq
    w0 = pl.multiple_of(jnp.maximum(s0 - WIN_C, 0), tq)
    dlt = (s0 + lax.broadcasted_iota(jnp.int32, (1, rows), 1) % tq) - (w0 + lax.broadcasted_iota(jnp.int32, (nw, 1), 0))
    in_win = (dlt >= 0) & (dlt < WIN_C)
    lane = lax.broadcasted_iota(jnp.int32, (1, LANE), 1) // DH_C
    kwc = kv_ref[0, pl.ds(w0, nw), 0:LANE]
    vwc = kv_ref[0, pl.ds(w0, nw), LANE:2 * LANE]
    for h in range(KVH_C):
        st = _dot_nt(jnp.where(lane == h, kwc, jnp.zeros_like(kwc)), qa_s[h, 0])
        st = jnp.where(in_win, st, NEG_BIG)
        pt = jnp.exp(st - jnp.max(st, axis=0, keepdims=True)).astype(BF16)
        o_win = finish(_dot_tn(jnp.where(lane == h, vwc, jnp.ones_like(vwc)), pt), h)
        o_ref[0, :, h * hw:(h + 1) * hw] = _gate_mix(gates, h, o_cmps[h], finish(acc_s[h], h), o_win, tq)


def _cols_t_kernel(x_ref, o_ref):
    o_ref[0] = x_ref[0].T.astype(BF16)


def cols_channel_major(z3, col0, *, ts=1024):
    b, s, _ = z3.shape
    ts = _pick_tile(s, ts)
    return pl.pallas_call(
        _cols_t_kernel,
        grid=(b, s // ts),
        in_specs=[pl.BlockSpec((1, ts, LANE), lambda bb, i: (bb, i, col0 // LANE))],
        out_specs=pl.BlockSpec((1, LANE, ts), lambda bb, i: (bb, 0, i)),
        out_shape=jax.ShapeDtypeStruct((b, LANE, s), BF16),
        compiler_params=_cparams(("parallel", "parallel")),
        name="cols_channel_major",
    )(z3)


def _cols_cast_kernel(x_ref, o_ref):
    o_ref[...] = x_ref[...].astype(BF16)


def cols_bf16(z3, col0, width, *, ts=1024):
    b, s, _ = z3.shape
    ts = _pick_tile(s, ts)
    return pl.pallas_call(
        _cols_cast_kernel,
        grid=(b, s // ts),
        in_specs=[pl.BlockSpec((1, ts, width), lambda bb, i: (bb, i, col0 // width))],
        out_specs=pl.BlockSpec((1, ts, width), lambda bb, i: (bb, i, 0)),
        out_shape=jax.ShapeDtypeStruct((b, s, width), BF16),
        compiler_params=_cparams(("parallel", "parallel")),
        name="cols_bf16",
    )(z3)


def _sel_operands(z3):
    s = z3.shape[1]
    lane = jnp.arange(LANE)
    hot = ((jnp.arange(s)[:, None] // SLC_BLK) % SEL_PHASE == lane[None, :] % SEL_PHASE).astype(BF16)
    ksel = cols_bf16(z3, C_SEL, HKV)
    vsel_t = cols_channel_major(z3, C_SEL + HKV)
    ka = jnp.stack([jnp.where((lane // DH_C == h)[None, None, :], ksel, hot[None]) for h in range(KVH_C)], axis=1)
    va = jnp.stack([jnp.where((lane // DH_C == h)[None, :, None], vsel_t, jnp.ones_like(vsel_t))
                    for h in range(KVH_C)], axis=1)
    return ka, va


def nsa_prompt(z3, kc, kvw, ka, va, cos_t, sin_t, mimp):
    b, s, _ = z3.shape
    nsp = mimp.shape[0]
    nq = s // QBLK_C
    ncmp = kc.shape[1]
    rows = G_C * QBLK_C
    vm = pltpu.VMEM
    scratch = [
        vm((KVH_C, nsp // SEL_PHASE, rows, LANE), BF16),
        vm((KVH_C, 1, rows), F32),
        vm((KVH_C, LANE, rows), F32),
    ]
    return pl.pallas_call(
        functools.partial(_nsa_prompt_kernel, nsp=nsp),
        grid=(b, nq),
        in_specs=[
            pl.BlockSpec((1, QBLK_C, C_Q), lambda bb, i: (bb, i, 0)),
            pl.BlockSpec((1, QBLK_C, LANE), lambda bb, i: (bb, i, C_GATE // LANE)),
            pl.BlockSpec((QBLK_C, LANE), lambda bb, i: (i, 0)),
            pl.BlockSpec((QBLK_C, LANE), lambda bb, i: (i, 0)),
            pl.BlockSpec((1, ncmp, KVW), lambda bb, i: (bb, 0, 0)),
            pl.BlockSpec((1, s, KVW), lambda bb, i: (bb, 0, 0)),
            pl.BlockSpec((1, KVH_C, s, LANE), lambda bb, i: (bb, 0, 0, 0)),
            pl.BlockSpec((1, KVH_C, LANE, s), lambda bb, i: (bb, 0, 0, 0)),
            pl.BlockSpec(mimp.shape, lambda bb, i: (0, 0)),
        ],
        out_specs=pl.BlockSpec((1, QBLK_C, C_Q), lambda bb, i: (bb, i, 0)),
        out_shape=jax.ShapeDtypeStruct((b, s, C_Q), F32),
        scratch_shapes=scratch,
        compiler_params=_cparams(("parallel", "arbitrary")),
        name="nsa_prompt",
    )(z3, z3, cos_t, sin_t, kc, kvw, ka, va, mimp)


def _nsa_sample_kernel(pt_ref, *refs, pg, nsp, past, t_new):
    del pt_ref
    page_refs = refs[:pg]
    (q_ref, gz_ref, cos_ref, sin_ref, kc_ref, mimp_ref, hot_ref, seln_ref, winb_ref, winn_ref, winnt_ref,
     o_ref, wo_ref, q2r_s, qa_s, sel_s, ocmp_s, m_s, l_s, acc_s) = refs[pg:]
    g = pl.program_id(1)
    ng = pl.num_programs(1)
    tq = T_PAD
    rows = G_C * tq
    t_pos = past + lax.broadcasted_iota(jnp.int32, (tq, 1), 0)
    scale = DH_C ** -0.5
    hw = G_C * DH_C
    nk = pg * PAGE_SIZE
    gblk = nk // SLC_BLK

    @pl.when(g == 0)
    def _():
        cos, sin = cos_ref[...], sin_ref[...]
        q2rs, imps = [], []
        for h in range(KVH_C):
            qh = q_ref[0, :, h * hw:(h + 1) * hw] * scale
            q2 = _stack_heads(qh, tq).astype(BF16)
            q2rs.append(_stack_heads(_rope_q(qh, cos, sin), tq))
            q2r_s[h] = q2rs[h].astype(BF16)
            kc = kc_ref[0, :, h * DH_C:(h + 1) * DH_C].astype(BF16)
            vc = kc_ref[0, :, KVH_C * DH_C + h * DH_C:KVH_C * DH_C + (h + 1) * DH_C].astype(BF16)
            o_cmp, pgrp = _cmp_branch(q2, kc, vc, t_pos, tq)
            ocmp_s[h] = o_cmp
            imps.append(_dot_exact_rhs01(pgrp, mimp_ref[...]))
        sel_all = _select_blocks(jnp.concatenate(imps, axis=0), jnp.concatenate([t_pos] * KVH_C, axis=0))
        zpad = [jnp.zeros((rows, DH_C - gblk), F32)] if gblk < DH_C else []
        for h in range(KVH_C):
            sel = sel_all[h * tq:(h + 1) * tq]
            sel_s[h] = sel
            selb = jnp.where(sel > 0.5, 0.0, NEG_BIG)
            for gg in range(qa_s.shape[1]):
                sb = jnp.concatenate([selb[:, gg * gblk:(gg + 1) * gblk]] * G_C, axis=0)
                qa_s[h, gg] = jnp.concatenate([q2rs[h], sb] + zpad, axis=1).astype(BF16)
        m_s[...] = jnp.full(m_s.shape, NEG_BIG, F32)
        l_s[...] = jnp.zeros(l_s.shape, F32)
        acc_s[...] = jnp.zeros(acc_s.shape, F32)

    kvt = jnp.concatenate([r[0] for r in page_refs], axis=1).astype(BF16)
    hot = hot_ref[...]
    for h in range(KVH_C):
        kst = jnp.concatenate([kvt[h * DH_C:(h + 1) * DH_C, :], hot], axis=0)
        vst = kvt[KVH_C * DH_C + h * DH_C:KVH_C * DH_C + (h + 1) * DH_C, :]
        s = jnp.dot(qa_s[h, g], kst, preferred_element_type=F32)
        m_old = m_s[h]
        m_new = jnp.maximum(m_old, jnp.max(s, axis=-1, keepdims=True))
        alpha = jnp.exp(m_old - m_new)
        p = jnp.exp(s - m_new)
        l_s[h] = alpha * l_s[h] + jnp.sum(p, axis=-1, keepdims=True)
        acc_s[h] = alpha * acc_s[h] + _dot_nt(p.astype(BF16), vst)
        m_s[h] = m_new

    @pl.when(g == ng - 1)
    def _():
        gates = jax.nn.sigmoid(gz_ref[0])
        n_buf = winb_ref.shape[2]
        lane_n = lax.broadcasted_iota(jnp.int32, (tq, nsp), 1)
        u = lax.broadcasted_iota(jnp.int32, (tq, tq), 1)
        seln = seln_ref[0]
        winn = winn_ref[0]
        for h in range(KVH_C):
            kcol = h * DH_C
            vcol = KVH_C * DH_C + h * DH_C
            q2r = q2r_s[h]
            new_sel = jnp.max(jnp.where(lane_n == past // SLC_BLK, sel_s[h], 0.0), axis=-1, keepdims=True) > 0.5
            mk = new_sel & (past + u <= t_pos) & (u < t_new)
            _flash_update(_dot_nt(q2r, seln[:, kcol:kcol + DH_C].astype(BF16)), mk,
                          seln[:, vcol:vcol + DH_C].astype(BF16), m_s.at[h], l_s.at[h], acc_s.at[h])
            o_sel = _flash_finish(m_s.at[h], l_s.at[h], acc_s.at[h])
            kbt = winb_ref[0, kcol:kcol + DH_C, :].astype(BF16)
            vbt = winb_ref[0, vcol:vcol + DH_C, :].astype(BF16)
            pos_b = past - n_buf + lax.broadcasted_iota(jnp.int32, (tq, n_buf), 1)
            d_b = t_pos - pos_b
            m_b = (pos_b >= 0) & (d_b >= 0) & (d_b < WIN_C)
            d_n = t_pos - (past + u)
            m_n = (d_n >= 0) & (d_n < WIN_C) & (u < t_new)
            s_b = _mask_rows(jnp.dot(q2r, kbt, preferred_element_type=F32), m_b, -jnp.inf)
            s_n = _mask_rows(_dot_nt(q2r, winn[:, kcol:kcol + DH_C].astype(BF16)), m_n, -jnp.inf)
            mx = jnp.maximum(jnp.max(s_b, axis=-1, keepdims=True), jnp.max(s_n, axis=-1, keepdims=True))
            mx = jnp.where(mx > -jnp.inf, mx, 0.0)
            p_b, p_n = jnp.exp(s_b - mx), jnp.exp(s_n - mx)
            den = jnp.maximum(jnp.sum(p_b, axis=-1, keepdims=True) + jnp.sum(p_n, axis=-1, keepdims=True), 1e-30)
            o_win = (_dot_nt((p_b / den).astype(BF16), vbt)
                     + jnp.dot((p_n / den).astype(BF16), winn[:, vcol:vcol + DH_C].astype(BF16),
                               preferred_element_type=F32))
            o_ref[0, :, h * hw:(h + 1) * hw] = _gate_mix(gates, h, ocmp_s[h], o_sel, o_win, tq)
        wo_ref[0, :, 0:n_buf - t_new] = winb_ref[0, :, t_new:n_buf]
        wo_ref[0, :, n_buf - t_new:n_buf] = winnt_ref[0, :, 0:t_new]


def nsa_sample(z3, kc, sel_pool_t, win_buf_t, win_new_t, page_table, cos_t, sin_t, mimp, *, t_new, pg=32):
    db, n_pages = page_table.shape
    past = n_pages * PAGE_SIZE
    ncmp = kc.shape[1]
    nsp = mimp.shape[1]
    n_buf = win_buf_t.shape[2]
    pg = min(pg, n_pages)
    nk = pg * PAGE_SIZE
    assert n_pages % pg == 0 and nk // SLC_BLK <= DH_C
    rows = G_C * T_PAD
    hot = jnp.asarray(np.arange(nk)[None, :] // SLC_BLK == np.arange(DH_C)[:, None], dtype=BF16)

    def page_spec(j):
        return pl.BlockSpec((1, KVW, PAGE_SIZE), lambda b, g, pt: (pt[b, g * pg + j], 0, 0))

    gs = pltpu.PrefetchScalarGridSpec(
        num_scalar_prefetch=1,
        grid=(db, n_pages // pg),
        in_specs=[page_spec(j) for j in range(pg)] + [
            pl.BlockSpec((1, T_PAD, C_Q), lambda b, g, pt: (b, 0, 0)),
            pl.BlockSpec((1, T_PAD, LANE), lambda b, g, pt: (b, 0, C_GATE // LANE)),
            pl.BlockSpec((T_PAD, LANE), lambda b, g, pt: (0, 0)),
            pl.BlockSpec((T_PAD, LANE), lambda b, g, pt: (0, 0)),
            pl.BlockSpec((1, ncmp, KVW), lambda b, g, pt: (b, 0, 0)),
            pl.BlockSpec(mimp.shape, lambda b, g, pt: (0, 0)),
            pl.BlockSpec((DH_C, nk), lambda b, g, pt: (0, 0)),
            pl.BlockSpec((1, T_PAD, KVW), lambda b, g, pt: (b, 0, C_SEL // KVW)),
            pl.BlockSpec((1, KVW, n_buf), lambda b, g, pt: (b, 0, 0)),
            pl.BlockSpec((1, T_PAD, KVW), lambda b, g, pt: (b, 0, C_WIN // KVW)),
            pl.BlockSpec((1, KVW, T_PAD), lambda b, g, pt: (b, 0, 0)),
        ],
        out_specs=[
            pl.BlockSpec((1, T_PAD, C_Q), lambda b, g, pt: (b, 0, 0)),
            pl.BlockSpec((1, KVW, n_buf), lambda b, g, pt: (b, 0, 0)),
        ],
        scratch_shapes=[
            pltpu.VMEM((KVH_C, rows, DH_C), BF16),
            pltpu.VMEM((KVH_C, n_pages // pg, rows, LANE), BF16),
            pltpu.VMEM((KVH_C, T_PAD, nsp), F32),
            pltpu.VMEM((KVH_C, rows, DH_C), F32),
            pltpu.VMEM((KVH_C, rows, 1), F32),
            pltpu.VMEM((KVH_C, rows, 1), F32),
            pltpu.VMEM((KVH_C, rows, DH_C), F32),
        ],
    )
    return pl.pallas_call(
        functools.partial(_nsa_sample_kernel, pg=pg, nsp=nsp, past=past, t_new=t_new),
        grid_spec=gs,
        out_shape=[jax.ShapeDtypeStruct((db, T_PAD, C_Q), F32), jax.ShapeDtypeStruct((db, KVW, n_buf), F32)],
        compiler_params=_cparams(("parallel", "arbitrary")),
        name="nsa_sample",
    )(page_table, *([sel_pool_t] * pg), z3, z3, cos_t, sin_t, kc, mimp, hot, z3, win_buf_t, z3, win_new_t)


def _rope_tables(pos):
    half = DH_C // 2
    inv = 1.0 / (ROPE_THETA ** (jnp.arange(half, dtype=F32) / half))
    ang = pos.astype(F32)[:, None] * inv[None, :]
    cos, sin = jnp.cos(ang), jnp.sin(ang)
    return jnp.tile(cos, (1, 4)), jnp.concatenate([-sin, sin, -sin, sin], axis=-1)


def _prep_w_in_ab(w):
    cuts = np.cumsum([A_QW, A_QW, A_QW, B_W, B_W, B_W, H_B, H_B])
    qa, ka, va, qb, kb, vb, ig, fg, og = jnp.split(w, cuts, axis=-1)
    padg = lambda t: jnp.pad(t, ((0, 0), (0, LANE - H_B)))
    return jnp.concatenate([qb, kb, vb, og, qa, ka, va, padg(ig), padg(fg)], axis=-1).astype(BF16)


def _ab_rope_flags():
    return tuple(range(AB_QA // LANE, AB_VA // LANE))


def _c_rope_flags():
    return (C_SEL // LANE, C_WIN // LANE)


def _prep_compress(w1, w2, pe):
    e2 = jnp.eye(KVH_C, dtype=F32)

    def half(w1h):
        return jnp.einsum('krde,hH->krhdHe', w1h, e2).reshape(2, CMP_STRIDE, HKV, HKV).astype(BF16)

    wab = jnp.concatenate([half(w1[:, :CMP_STRIDE]), half(w1[:, CMP_STRIDE:])], axis=-1)
    w2b = jnp.einsum('ked,hH->kheHd', w2, e2).reshape(2, HKV, HKV).astype(BF16)

    def pe_half(p):
        return jnp.broadcast_to(p[:, :, None, :], (2, CMP_STRIDE, KVH_C, DH_C)).reshape(2, 1, CMP_STRIDE * HKV)

    wab = wab.reshape(2, CMP_STRIDE * HKV, 2 * HKV)
    bias = compress_bias(pe_half(pe[:, :CMP_STRIDE]), pe_half(pe[:, CMP_STRIDE:]), wab)
    return (wab, w2b), bias


def _channel_major(x, lead):
    perm = tuple(range(lead)) + (lead + 1, lead + 2, lead + 3, lead)
    xt = jnp.transpose(x, perm)
    return xt.reshape(x.shape[:lead] + (x.shape[lead + 1] * x.shape[lead + 2] * x.shape[lead + 3], x.shape[lead]))


def _row_major(xt, c0, c1, c2):
    lead, _, rows = xt.shape
    return jnp.transpose(xt.reshape(lead, c0, c1, c2, rows), (0, 4, 1, 2, 3))


def _importance_matrix(n_rows, n_cmp, n_slc, n_cols):
    ratio = SLC_BLK // CMP_STRIDE
    m = np.zeros((n_rows, n_cols), np.float32)
    for jblk in range(n_slc):
        for off in range(1 - CMP_BLK // CMP_STRIDE, ratio):
            i = ratio * jblk + off
            if 0 <= i < n_cmp:
                m[i, jblk] = 1.0
    return jnp.asarray(m, dtype=BF16)


def _gates_t(z3, L):
    b, s, _ = z3.shape
    g = jnp.concatenate([z3[..., AB_IG:AB_IG + H_B], z3[..., AB_FG:AB_FG + H_B]], axis=-1)
    return g.reshape(b, s // L, L, 2 * H_B).transpose(0, 1, 3, 2)


def kernel(x_prompt, x_sample, cache_a0_kv, cache_a1_kv, cache_a2_kv, state_b_C, state_b_n, state_b_m,
           cache_c_cmp_kv, cache_c_sel_kv, cache_c_win_kv, page_table, norm_g, w_in_ab, b_if, g_mlstm,
           w_out_ab, w_in_c, cmp_w1, cmp_w2, cmp_pe, w_out_c, w_ffn_gate, w_ffn_up, w_ffn_down, norm_final):
    B, S, D = x_prompt.shape
    DB, T, _ = x_sample.shape
    depth = norm_g.shape[0]
    n_pages = page_table.shape[1]
    past = n_pages * PAGE_SIZE
    caches_a = (cache_a0_kv, cache_a1_kv, cache_a2_kv)
    assert T <= T_PAD and S % (DIL_CFG[-1][1] * BLK_A) == 0 and S >= WIN_C + QBLK_C

    hp = x_prompt.reshape(B * S, D)
    hs = jnp.pad(x_sample, ((0, 0), (0, T_PAD - T), (0, 0))).reshape(DB * T_PAD, D)

    pos_p = jnp.arange(S)
    pos_s = past + jnp.arange(T_PAD)
    cos_p1, sin_p1 = _rope_tables(pos_p)
    cos_s1, sin_s1 = _rope_tables(pos_s)
    cos_p, sin_p = jnp.tile(cos_p1, (B, 1)), jnp.tile(sin_p1, (B, 1))
    cos_s, sin_s = jnp.tile(cos_s1, (DB, 1)), jnp.tile(sin_s1, (DB, 1))

    a_p, a_s = [[], [], []], [[], [], []]
    bC_p, bC_s, bn_p, bn_s, bm_p, bm_s = [], [], [], [], [], []
    cc_p, cc_s, csl_p, csl_s, cw_p, cw_s = [], [], [], [], [], []

    for layer in range(depth):
        if layer % 2 == 0:
            e = layer // 2
            w_in = _prep_w_in_ab(w_in_ab[e])
            flags = _ab_rope_flags()
            w_out = w_out_ab[e].astype(BF16)
            z = norm_proj(hp, norm_g[layer, 0], w_in, flags, cos_p, sin_p)
            z3 = z.reshape(B, S, AB_N)
            os_, ls_ = [], []
            for gi, (win, dil) in enumerate(DIL_CFG):
                o, l = dil_prompt(z3, gi, win, dil)
                os_.append(o)
                ls_.append(l)
                nb = min(win, S)
                kk = z3[:, S - nb:, AB_KA + gi * AW:AB_KA + (gi + 1) * AW].reshape(B, nb, H_A, DH_A)
                vv = z3[:, S - nb:, AB_VA + gi * AW:AB_VA + (gi + 1) * AW].reshape(B, nb, H_A, DH_A)
                a_p[gi].append(jnp.stack([kk, vv], axis=2))
            zc = jnp.zeros
            hb, Cp, n_p, m_p = mlstm(z3, _gates_t(z3, MLSTM_CHUNK), b_if[e],
                                     zc((B, H_B, DH_B, DH_B), F32), zc((B, H_B, DH_B), F32), zc((B, H_B), F32),
                                     L=MLSTM_CHUNK, n_valid=MLSTM_CHUNK, bb=B if B <= 2 else 1)
            hp = ab_merge(os_, ls_, hb, z, g_mlstm[e], w_out, hp)
            bC_p.append(Cp); bn_p.append(n_p); bm_p.append(m_p)
            z = norm_proj(hs, norm_g[layer, 0], w_in, flags, cos_s, sin_s)
            z3 = z.reshape(DB, T_PAD, AB_N)
            os_, ls_ = [], []
            for gi, (win, dil) in enumerate(DIL_CFG):
                new_t = jnp.concatenate(
                    [jnp.swapaxes(z3[:, :, AB_KA + gi * AW:AB_KA + (gi + 1) * AW], 1, 2),
                     jnp.swapaxes(z3[:, :, AB_VA + gi * AW:AB_VA + (gi + 1) * AW], 1, 2)], axis=1)
                o, l, co = dil_sample(_channel_major(caches_a[gi][e], 1), z3, new_t, gi, win, dil, T)
                os_.append(o)
                ls_.append(l)
                a_s[gi].append(_row_major(co, 2, H_A, DH_A))
            bbs = 4 if DB % 4 == 0 else 1
            hb, Cs, n_s, m_s = mlstm(z3, _gates_t(z3, T_PAD), b_if[e], state_b_C[e], state_b_n[e], state_b_m[e],
                                     L=T_PAD, n_valid=T, bb=bbs)
            hs = ab_merge(os_, ls_, hb, z, g_mlstm[e], w_out, hs)
            bC_s.append(Cs); bn_s.append(n_s); bm_s.append(m_s)
        else:
            o_i = layer // 2
            w_in = jnp.pad(w_in_c[o_i], ((0, 0), (0, C_N - w_in_c.shape[-1]))).astype(BF16)
            flags = _c_rope_flags()
            w_out = w_out_c[o_i].astype(BF16)
            cw, cbias = _prep_compress(cmp_w1[o_i], cmp_w2[o_i], cmp_pe[o_i])
            z = norm_proj(hp, norm_g[layer, 0], w_in, flags, cos_p, sin_p)
            z3 = z.reshape(B, S, C_N)
            kv_cmp = z3[:, :, C_CMP:C_CMP + KVW]
            kv_sel = z3[:, :, C_SEL:C_SEL + KVW]
            kv_win = z3[:, :, C_WIN:C_WIN + KVW]
            kc = compress_prompt(z3, cw, cbias)
            n_cmp = (S - CMP_BLK) // CMP_STRIDE + 1
            n_slc = S // SLC_BLK
            mimp = _importance_matrix(S // CMP_STRIDE, n_cmp, n_slc, -(-n_slc // LANE) * LANE)
            ka, va = _sel_operands(z3)
            o = nsa_prompt(z3, kc, cols_bf16(z3, C_WIN, KVW), ka, va, cos_p1, sin_p1, mimp.T)
            hp = out_proj(o.reshape(B * S, C_Q), w_out, hp)
            nw = min(WIN_C, S)
            sh = lambda t: t.reshape(t.shape[0], t.shape[1], 2, KVH_C, DH_C)
            cc_p.append(sh(kv_cmp)); csl_p.append(sh(kv_sel)); cw_p.append(sh(kv_win[:, S - nw:]))
            z = norm_proj(hs, norm_g[layer, 0], w_in, flags, cos_s, sin_s)
            z3 = z.reshape(DB, T_PAD, C_N)
            kc = compress_paged(_channel_major(cache_c_cmp_kv[o_i], 1), page_table, cw, cbias)
            full_len = past + T
            n_cmp = (full_len - CMP_BLK) // CMP_STRIDE + 1
            assert (n_cmp + 1) * CMP_STRIDE <= past
            n_slc = past // SLC_BLK + -(-T // SLC_BLK)
            nsp = -(-n_slc // LANE) * LANE
            mimp = _importance_matrix(past // CMP_STRIDE, n_cmp, n_slc, nsp)
            o, wo = nsa_sample(z3, kc, _channel_major(cache_c_sel_kv[o_i], 1),
                               _channel_major(cache_c_win_kv[o_i], 1),
                               jnp.swapaxes(z3[:, :, C_WIN:C_WIN + KVW], 1, 2),
                               page_table, cos_s1, sin_s1, mimp, t_new=T)
            hs = out_proj(o.reshape(DB * T_PAD, C_Q), w_out, hs)
            cc_s.append(sh(z3[:, :T, C_CMP:C_CMP + KVW])); csl_s.append(sh(z3[:, :T, C_SEL:C_SEL + KVW]))
            cw_s.append(_row_major(wo, 2, KVH_C, DH_C))
        last = layer == depth - 1
        wg, wu, wd = (w_ffn_gate[layer].astype(BF16), w_ffn_up[layer].astype(BF16), w_ffn_down[layer].astype(BF16))
        hp = ffn(hp, norm_g[layer, 1], wg, wu, wd, norm_final, final_norm=last)
        hs = ffn(hs, norm_g[layer, 1], wg, wu, wd, norm_final, final_norm=last)

    y_prompt = hp.reshape(B, S, D)
    y_sample = hs.reshape(DB, T_PAD, D)[:, :T]
    st = lambda xs: jnp.stack(xs, axis=0)
    return (y_prompt, y_sample,
            st(a_p[0]), st(a_s[0]), st(a_p[1]), st(a_s[1]), st(a_p[2]), st(a_s[2]),
            st(bC_p), st(bC_s), st(bn_p), st(bn_s), st(bm_p), st(bm_s),
            st(cc_p), st(cc_s), st(csl_p), st(csl_s), st(cw_p), st(cw_s))
```

```python
import functools
import math

import numpy as np
import jax
import jax.numpy as jnp
from jax import lax
from jax.experimental import pallas as pl
from jax.experimental.pallas import tpu as pltpu

F32 = jnp.float32
BF16 = jnp.bfloat16

PAGE_SIZE = 128
DIL_CFG = ((128, 1), (512, 4), (2048, 16))
N_DIL = 3
H_A = 4
DH_A = 64
BLK_A = 128
H_B = 4
DH_B = 128
MLSTM_CHUNK = 128
H_C = 16
KVH_C = 2
G_C = H_C // KVH_C
DH_C = 64
CMP_STRIDE = 16
CMP_BLK = 2 * CMP_STRIDE
CMP_HID = 64
SLC_BLK = 64
N_SEL = 16
WIN_C = 512
QBLK_C = 128
ROPE_THETA = 10000.0
EPS = 1e-6
A_QW = N_DIL * H_A * DH_A
B_W = H_B * DH_B
AW = H_A * DH_A
C_Q = H_C * DH_C
C_KV = 3 * 2 * KVH_C * DH_C
KVW = 2 * KVH_C * DH_C

LANE = 128
SUBLANE = 8
VMEM_LIMIT = 48 * 1024 * 1024

NEG_BIG = -1e30
T_PAD = SUBLANE

AB_QB, AB_KB, AB_VB, AB_OG = 0, B_W, 2 * B_W, 3 * B_W
AB_QA = 4 * B_W
AB_KA = AB_QA + A_QW
AB_VA = AB_KA + A_QW
AB_IG = AB_VA + A_QW
AB_FG = AB_IG + LANE
AB_N = AB_FG + LANE

C_CMP = C_Q
C_SEL = C_Q + KVW
C_WIN = C_Q + 2 * KVW
C_GATE = C_Q + 3 * KVW
C_N = 2048


def _cparams(sem, vmem=VMEM_LIMIT):
    return pltpu.CompilerParams(dimension_semantics=sem, vmem_limit_bytes=vmem)


def _pick_tile(m, pref):
    t = min(m, pref)
    while m % t:
        t //= 2
    return t


def _rope_lanes(x, cos, sin):
    lane = lax.broadcasted_iota(jnp.int32, x.shape, 1)
    first = (lane % DH_C) < (DH_C // 2)
    partner = jnp.where(first, pltpu.roll(x, LANE - DH_C // 2, 1), pltpu.roll(x, DH_C // 2, 1))
    return x * cos + partner * sin


def _split3(x):
    hi = x.astype(BF16)
    r1 = x - hi.astype(F32)
    mid = r1.astype(BF16)
    lo = (r1 - mid.astype(F32)).astype(BF16)
    return hi, mid, lo


def _dot_exact_rhs01(x, m01):
    hi, mid, lo = _split3(x)
    d = lambda a: jnp.dot(a, m01, preferred_element_type=F32)
    return d(hi) + d(mid) + d(lo)


def _dot_exact_lhs01(m01, x):
    hi, mid, lo = _split3(x)
    d = lambda a: jnp.dot(m01, a, preferred_element_type=F32)
    return d(hi) + d(mid) + d(lo)


def _dot_nt(a, b):
    return lax.dot_general(a, b, (((1,), (1,)), ((), ())), preferred_element_type=F32)


def _dot_tn(a, b):
    return lax.dot_general(a, b, (((0,), (0,)), ((), ())), preferred_element_type=F32)


def _log_sigmoid(x):
    return jnp.minimum(x, 0.0) - jnp.log1p(jnp.exp(-jnp.abs(x)))


def _gelu_tanh(x):
    return 0.5 * x * (1.0 + jnp.tanh(math.sqrt(2.0 / math.pi) * (x + 0.044715 * (x * x * x))))


def _rms_rows(x, g):
    ms = jnp.mean(x * x, axis=-1, keepdims=True)
    return x * lax.rsqrt(ms + EPS) * g


NORM_PROJ_TN = 512
RESIDENT_W_VMEM = 56 * 1024 * 1024


def _norm_proj_kernel(x_ref, g_ref, w_ref, cos_ref, sin_ref, o_ref, *, rope_chunks):
    xn = _rms_rows(x_ref[...], g_ref[...]).astype(BF16)
    n = o_ref.shape[1]
    for c0 in range(0, n, NORM_PROJ_TN):
        cs = slice(c0, min(c0 + NORM_PROJ_TN, n))
        o_ref[:, cs] = jnp.dot(xn, w_ref[:, cs], preferred_element_type=F32)
    for c in rope_chunks:
        cs = slice(c * LANE, (c + 1) * LANE)
        o_ref[:, cs] = _rope_lanes(o_ref[:, cs], cos_ref[...], sin_ref[...])


def norm_proj(x, g, w_bf16, rope_chunks, cos_t, sin_t, *, tm_pref=512):
    m, d = x.shape
    n = w_bf16.shape[1]
    tm = _pick_tile(m, tm_pref)
    return pl.pallas_call(
        functools.partial(_norm_proj_kernel, rope_chunks=tuple(rope_chunks)),
        grid=(m // tm,),
        in_specs=[
            pl.BlockSpec((tm, d), lambda i: (i, 0)),
            pl.BlockSpec((1, d), lambda i: (0, 0)),
            pl.BlockSpec((d, n), lambda i: (0, 0), pipeline_mode=pl.Buffered(1)),
            pl.BlockSpec((tm, LANE), lambda i: (i, 0)),
            pl.BlockSpec((tm, LANE), lambda i: (i, 0)),
        ],
        out_specs=pl.BlockSpec((tm, n), lambda i: (i, 0)),
        out_shape=jax.ShapeDtypeStruct((m, n), F32),
        compiler_params=_cparams(("parallel",), RESIDENT_W_VMEM),
        name="norm_proj",
    )(x, g.reshape(1, d), w_bf16, cos_t, sin_t)


def _ffn_kernel(x_ref, g_ref, wg_ref, wu_ref, wd_ref, gf_ref, o_ref, *, final_norm, tf):
    x = x_ref[...]
    xn = _rms_rows(x, g_ref[...]).astype(BF16)
    y = x
    for f0 in range(0, wg_ref.shape[1], tf):
        fs = slice(f0, f0 + tf)
        a = jnp.dot(xn, wg_ref[:, fs], preferred_element_type=F32)
        u = jnp.dot(xn, wu_ref[:, fs], preferred_element_type=F32)
        act = (a * jax.nn.sigmoid(a)) * u
        y = y + jnp.dot(act.astype(BF16), wd_ref[fs, :], preferred_element_type=F32)
    o_ref[...] = _rms_rows(y, gf_ref[...]) if final_norm else y


def ffn(x, g, wg, wu, wd, g_final, *, final_norm, tm_pref=512):
    m, d = x.shape
    dff = wg.shape[1]
    tf = dff // 2 if (dff // 2) % LANE == 0 else dff
    tm = _pick_tile(m, tm_pref)
    resident = lambda a: pl.BlockSpec(a.shape, lambda i: (0, 0), pipeline_mode=pl.Buffered(1))
    return pl.pallas_call(
        functools.partial(_ffn_kernel, final_norm=final_norm, tf=tf),
        grid=(m // tm,),
        in_specs=[
            pl.BlockSpec((tm, d), lambda i: (i, 0)),
            pl.BlockSpec((1, d), lambda i: (0, 0)),
            resident(wg), resident(wu), resident(wd),
            pl.BlockSpec((1, d), lambda i: (0, 0)),
        ],
        out_specs=pl.BlockSpec((tm, d), lambda i: (i, 0)),
        out_shape=jax.ShapeDtypeStruct((m, d), F32),
        compiler_params=_cparams(("parallel",), RESIDENT_W_VMEM),
        name="ffn",
    )(x, g.reshape(1, d), wg, wu, wd, g_final.reshape(1, d))


def _out_proj_kernel(x_ref, w_ref, r_ref, o_ref):
    o_ref[...] = r_ref[...] + jnp.dot(x_ref[...].astype(BF16), w_ref[...], preferred_element_type=F32)


def out_proj(x, w_bf16, resid, *, tm_pref=512):
    m, k = x.shape
    n = w_bf16.shape[1]
    tm = _pick_tile(m, tm_pref)
    return pl.pallas_call(
        _out_proj_kernel,
        grid=(m // tm,),
        in_specs=[
            pl.BlockSpec((tm, k), lambda i: (i, 0)),
            pl.BlockSpec((k, n), lambda i: (0, 0)),
            pl.BlockSpec((tm, n), lambda i: (i, 0)),
        ],
        out_specs=pl.BlockSpec((tm, n), lambda i: (i, 0)),
        out_shape=jax.ShapeDtypeStruct((m, n), F32),
        compiler_params=_cparams(("parallel",)),
        name="out_proj",
    )(x, w_bf16, resid)


DIL_STEP_ROWS = 2048
DIL_UNROLL = 4


def _dil_prompt_kernel(q_ref, kp_ref, kc_ref, vp_ref, vc_ref, o_ref, l_ref, *, nback, dil, nblk):
    n = pl.program_id(2)
    span = dil * BLK_A
    qi = BLK_A + lax.broadcasted_iota(jnp.int32, (BLK_A, 2 * BLK_A), 0)
    ki = lax.broadcasted_iota(jnp.int32, (BLK_A, 2 * BLK_A), 1)
    rel = qi - ki
    band = (rel >= 0) & (rel <= nback)
    own = ki >= BLK_A
    head_of_lane = lax.broadcasted_iota(jnp.int32, (1, LANE), 1) // DH_A

    def one(it, carry):
        j = it // dil
        r = it % dil
        start = j * span + r
        if dil > 1:
            take = lambda st: pl.ds(st, BLK_A, stride=dil)
        else:
            take = lambda st: pl.ds(pl.multiple_of(st, BLK_A), BLK_A)
        rows = take(start)
        prev_rows = take(jnp.maximum(start - span, r))
        first_rows = take(r)
        q = q_ref[0, rows, :] * (DH_A ** -0.5)
        kprev = jnp.where(j > 0, kc_ref[0, prev_rows, :], kp_ref[0, first_rows, :])
        vprev = jnp.where(j > 0, vc_ref[0, prev_rows, :], vp_ref[0, first_rows, :])
        kk = jnp.concatenate([kprev, kc_ref[0, rows, :]], axis=0)
        vv = jnp.concatenate([vprev, vc_ref[0, rows, :]], axis=0)
        mask = band & ((n * nblk + j > 0) | own)
        nh = LANE // DH_A
        qbd = jnp.concatenate([jnp.where(head_of_lane == h, q, 0.0) for h in range(nh)], axis=0).astype(BF16)
        s = _dot_nt(qbd, kk.astype(BF16)).reshape(nh, BLK_A, 2 * BLK_A)
        s = jnp.where(mask[None], s, -jnp.inf)
        m = jnp.max(s, axis=-1, keepdims=True)
        p = jnp.exp(s - m)
        l = jnp.sum(p, axis=-1, keepdims=True)
        pv = jnp.dot(p.reshape(nh * BLK_A, 2 * BLK_A).astype(BF16), vv.astype(BF16),
                     preferred_element_type=F32).reshape(nh, BLK_A, LANE) / l
        lse = m + jnp.log(l)
        o, ls = pv[0], jnp.broadcast_to(lse[0], (BLK_A, LANE))
        for h in range(1, nh):
            o = jnp.where(head_of_lane == h, pv[h], o)
            ls = jnp.where(head_of_lane == h, lse[h], ls)
        o_ref[0, rows, :] = o
        l_ref[0, rows, :] = ls
        return carry

    lax.fori_loop(0, nblk * dil, one, 0, unroll=DIL_UNROLL)


def dil_prompt(z3, gi, window, dil):
    b, s, _ = z3.shape
    nback = window // dil
    span = dil * BLK_A
    nblk = max(1, DIL_STEP_ROWS // span)
    rows = nblk * span
    assert s % rows == 0 and (nblk * dil) % DIL_UNROLL == 0
    nh = AW // LANE
    qo, ko, vo = (AB_QA + gi * AW) // LANE, (AB_KA + gi * AW) // LANE, (AB_VA + gi * AW) // LANE
    blk = (1, rows, LANE)
    pblk = (1, span, LANE)
    in_specs = [
        pl.BlockSpec(blk, lambda bb, hp, i: (bb, i, qo + hp)),
        pl.BlockSpec(pblk, lambda bb, hp, i: (bb, jnp.maximum(i * nblk - 1, 0), ko + hp)),
        pl.BlockSpec(blk, lambda bb, hp, i: (bb, i, ko + hp)),
        pl.BlockSpec(pblk, lambda bb, hp, i: (bb, jnp.maximum(i * nblk - 1, 0), vo + hp)),
        pl.BlockSpec(blk, lambda bb, hp, i: (bb, i, vo + hp)),
    ]
    out_spec = pl.BlockSpec(blk, lambda bb, hp, i: (bb, i, hp))
    o, l = pl.pallas_call(
        functools.partial(_dil_prompt_kernel, nback=nback, dil=dil, nblk=nblk),
        grid=(b, nh, s // rows),
        in_specs=in_specs,
        out_specs=[out_spec, out_spec],
        out_shape=[jax.ShapeDtypeStruct((b, s, AW), F32)] * 2,
        compiler_params=_cparams(("parallel", "parallel", "arbitrary")),
        name="dil_prompt_%d" % gi,
    )(z3, z3, z3, z3, z3)
    return o.reshape(b * s, AW), l.reshape(b * s, AW)


def _dil_sample_kernel(c_ref, q_ref, kn_ref, vn_ref, nt_ref, o_ref, l_ref, co_ref, *, window, dil, n_buf, t_new):
    rows = H_A * T_PAD
    q = q_ref[0] * (DH_A ** -0.5)
    q4 = jnp.concatenate([q] * H_A, axis=0)
    rr = lax.broadcasted_iota(jnp.int32, (rows, AW), 0)
    ll = lax.broadcasted_iota(jnp.int32, (rows, AW), 1)
    head_sel = (rr // T_PAD) == (ll // DH_A)
    qbd = jnp.where(head_sel, q4, 0.0).astype(BF16)
    kbt = c_ref[0, 0:AW, :].astype(BF16)
    vbt = c_ref[0, AW:2 * AW, :].astype(BF16)
    kn = kn_ref[0]
    vn = vn_ref[0]
    s_buf = jnp.dot(qbd, kbt, preferred_element_type=F32)
    s_new = _dot_nt(qbd, kn.astype(BF16))
    t_b = lax.broadcasted_iota(jnp.int32, (rows, n_buf), 0) % T_PAD
    c_b = lax.broadcasted_iota(jnp.int32, (rows, n_buf), 1)
    d_b = n_buf + t_b - c_b
    m_b = ((d_b % dil) == 0) & (d_b <= window)
    t_n = lax.broadcasted_iota(jnp.int32, (rows, T_PAD), 0) % T_PAD
    u_n = lax.broadcasted_iota(jnp.int32, (rows, T_PAD), 1)
    d_n = t_n - u_n
    m_n = (d_n >= 0) & ((d_n % dil) == 0) & (d_n <= window) & (u_n < t_new)
    s_buf = jnp.where(m_b, s_buf, NEG_BIG)
    s_new = jnp.where(m_n, s_new, NEG_BIG)
    mx = jnp.maximum(jnp.max(s_buf, axis=-1, keepdims=True), jnp.max(s_new, axis=-1, keepdims=True))
    p_b = jnp.where(m_b, jnp.exp(s_buf - mx), 0.0)
    p_n = jnp.where(m_n, jnp.exp(s_new - mx), 0.0)
    l = jnp.sum(p_b, axis=-1, keepdims=True) + jnp.sum(p_n, axis=-1, keepdims=True)
    l = jnp.maximum(l, 1e-30)
    acc = (_dot_nt(p_b.astype(BF16), vbt)
           + jnp.dot(p_n.astype(BF16), vn.astype(BF16), preferred_element_type=F32))
    res = jnp.where(head_sel, acc / l, 0.0)
    lse = jnp.where(head_sel, mx + jnp.log(l), 0.0)
    o = res[0:T_PAD]
    ls = lse[0:T_PAD]
    for h in range(1, H_A):
        o = o + res[h * T_PAD:(h + 1) * T_PAD]
        ls = ls + lse[h * T_PAD:(h + 1) * T_PAD]
    o_ref[0] = o
    l_ref[0] = ls
    co_ref[0, :, 0:n_buf - t_new] = c_ref[0, :, t_new:n_buf]
    co_ref[0, :, n_buf - t_new:n_buf] = nt_ref[0, :, 0:t_new]


def dil_sample(cache_t, z3, new_t, gi, window, dil, t_new):
    db, _, n_buf = cache_t.shape
    qo, ko, vo = AB_QA // AW + gi, AB_KA // AW + gi, AB_VA // AW + gi
    blk = (1, T_PAD, AW)
    o, l, co = pl.pallas_call(
        functools.partial(_dil_sample_kernel, window=window, dil=dil, n_buf=n_buf, t_new=t_new),
        grid=(db,),
        in_specs=[
            pl.BlockSpec((1, 2 * AW, n_buf), lambda b: (b, 0, 0)),
            pl.BlockSpec(blk, lambda b: (b, 0, qo)),
            pl.BlockSpec(blk, lambda b: (b, 0, ko)),
            pl.BlockSpec(blk, lambda b: (b, 0, vo)),
            pl.BlockSpec((1, 2 * AW, T_PAD), lambda b: (b, 0, 0)),
        ],
        out_specs=[
            pl.BlockSpec(blk, lambda b: (b, 0, 0)),
            pl.BlockSpec(blk, lambda b: (b, 0, 0)),
            pl.BlockSpec((1, 2 * AW, n_buf), lambda b: (b, 0, 0)),
        ],
        out_shape=[
            jax.ShapeDtypeStruct((db, T_PAD, AW), F32),
            jax.ShapeDtypeStruct((db, T_PAD, AW), F32),
            jax.ShapeDtypeStruct((db, 2 * AW, n_buf), F32),
        ],
        compiler_params=_cparams(("parallel",)),
        name="dil_sample_%d" % gi,
    )(cache_t, z3, z3, z3, new_t)
    return o.reshape(db * T_PAD, AW), l.reshape(db * T_PAD, AW), co


def _mlstm_kernel(q_ref, k_ref, v_ref, gi_ref, gf_ref, gt_ref, bi_ref, bf_ref, b8_ref,
                  c0_ref, n0_ref, m0_ref, h_ref, co_ref, no_ref, mo_ref,
                  c_s, n_s, m_s, *, bb, L, n_valid):
    c = pl.program_id(1)

    @pl.when(c == 0)
    def _():
        c_s[...] = c0_ref[...]
        n_s[...] = n0_ref[...]
        m_s[...] = m0_ref[...]

    row = lax.broadcasted_iota(jnp.int32, (L, L), 0)
    col = lax.broadcasted_iota(jnp.int32, (L, L), 1)
    tri = row >= col
    tri_l = jnp.where(tri, 1.0, 0.0).astype(BF16)
    tri_u = jnp.where(row <= col, 1.0, 0.0).astype(BF16)
    lane = lax.broadcasted_iota(jnp.int32, (1, LANE), 1)
    for b in range(bb):
        ig_col = gi_ref[b] + bi_ref[...]
        lf_col = _log_sigmoid(gf_ref[b] + bf_ref[...])
        gt = gt_ref[b, 0]
        ig_row = gt[0:H_B] + b8_ref[0:H_B]
        lf_row = _log_sigmoid(gt[H_B:2 * H_B] + b8_ref[H_B:2 * H_B])
        if n_valid < L:
            rv = lax.broadcasted_iota(jnp.int32, (L, LANE), 0) < n_valid
            ig_col = jnp.where(rv, ig_col, -jnp.inf)
            lf_col = jnp.where(rv, lf_col, 0.0)
            cv = lax.broadcasted_iota(jnp.int32, (H_B, L), 1) < n_valid
            ig_row = jnp.where(cv, ig_row, -jnp.inf)
            lf_row = jnp.where(cv, lf_row, 0.0)
        b_col = _dot_exact_lhs01(tri_l, lf_col)
        b_row = _dot_exact_rhs01(lf_row, tri_u)
        m_row = m_s[b]
        a_col = b_col + m_row
        m_new = m_row
        hs_out = []
        for h in range(H_B):
            hsl = slice(h * DH_B, (h + 1) * DH_B)
            bc = b_col[:, h:h + 1]
            ac = a_col[:, h:h + 1]
            icol = ig_col[:, h:h + 1]
            D = bc - b_row[h:h + 1, :] + ig_row[h:h + 1, :]
            D = jnp.where(tri, D, -jnp.inf)
            mt = jnp.maximum(ac, jnp.max(D, axis=-1, keepdims=True))
            Dw = jnp.exp(D - mt)
            iw = jnp.exp(ac - mt)
            qf = q_ref[b, :, hsl]
            kf = k_ref[b, :, hsl] * (DH_B ** -0.5)
            vf = v_ref[b, :, hsl]
            qb, kb, vb = qf.astype(BF16), kf.astype(BF16), vf.astype(BF16)
            Cm = c_s[b * H_B + h]
            nv = n_s[b * H_B + h]
            sc = _dot_nt(qb, kb) * Dw
            num = iw * _dot_nt(qb, Cm.astype(BF16)) + jnp.dot(sc.astype(BF16), vb, preferred_element_type=F32)
            den = iw * jnp.sum(qf * nv, axis=-1, keepdims=True) + jnp.sum(sc, axis=-1, keepdims=True)
            hs_out.append(num / jnp.maximum(jnp.abs(den), jnp.exp(-mt)))
            mL = mt[L - 1:L, :]
            wL = jnp.exp(bc[L - 1:L, :] - bc + icol - mL)
            dec = jnp.exp(ac[L - 1:L, :] - mL)
            c_s[b * H_B + h] = dec * Cm + _dot_tn((vf * wL).astype(BF16), kb)
            n_s[b * H_B + h] = dec * nv + jnp.sum(wL * kf, axis=0, keepdims=True)
            m_new = jnp.where(lane == h, mL, m_new)
        m_s[b] = m_new
        h_ref[b] = jnp.concatenate(hs_out, axis=-1)

    @pl.when(c == pl.num_programs(1) - 1)
    def _():
        co_ref[...] = c_s[...]
        no_ref[...] = n_s[...]
        mo_ref[...] = m_s[...]


def mlstm(z3, gt, b_if, c0, n0, m0, *, L, n_valid, bb):
    b, s, _ = z3.shape
    nc = s // L
    bi_row = jnp.zeros((1, LANE), F32).at[0, :H_B].set(b_if[0])
    bf_row = jnp.zeros((1, LANE), F32).at[0, :H_B].set(b_if[1])
    b8 = b_if.reshape(2 * H_B, 1)
    c0r = c0.reshape(b * H_B, DH_B, DH_B)
    n0r = n0.reshape(b * H_B, 1, DH_B)
    m0r = jnp.zeros((b, 1, LANE), F32).at[:, 0, :H_B].set(m0)
    cw = B_W // LANE
    h, co, no, mo = pl.pallas_call(
        functools.partial(_mlstm_kernel, bb=bb, L=L, n_valid=n_valid),
        grid=(b // bb, nc),
        in_specs=[
            pl.BlockSpec((bb, L, B_W), lambda g, c: (g, c, AB_QB // B_W)),
            pl.BlockSpec((bb, L, B_W), lambda g, c: (g, c, AB_KB // B_W)),
            pl.BlockSpec((bb, L, B_W), lambda g, c: (g, c, AB_VB // B_W)),
            pl.BlockSpec((bb, L, LANE), lambda g, c: (g, c, AB_IG // LANE)),
            pl.BlockSpec((bb, L, LANE), lambda g, c: (g, c, AB_FG // LANE)),
            pl.BlockSpec((bb, 1, 2 * H_B, L), lambda g, c: (g, c, 0, 0)),
            pl.BlockSpec((1, LANE), lambda g, c: (0, 0)),
            pl.BlockSpec((1, LANE), lambda g, c: (0, 0)),
            pl.BlockSpec((2 * H_B, 1), lambda g, c: (0, 0)),
            pl.BlockSpec((bb * H_B, DH_B, DH_B), lambda g, c: (g, 0, 0)),
            pl.BlockSpec((bb * H_B, 1, DH_B), lambda g, c: (g, 0, 0)),
            pl.BlockSpec((bb, 1, LANE), lambda g, c: (g, 0, 0)),
        ],
        out_specs=[
            pl.BlockSpec((bb, L, B_W), lambda g, c: (g, c, 0)),
            pl.BlockSpec((bb * H_B, DH_B, DH_B), lambda g, c: (g, 0, 0)),
            pl.BlockSpec((bb * H_B, 1, DH_B), lambda g, c: (g, 0, 0)),
            pl.BlockSpec((bb, 1, LANE), lambda g, c: (g, 0, 0)),
        ],
        out_shape=[
            jax.ShapeDtypeStruct((b, s, B_W), F32),
            jax.ShapeDtypeStruct((b * H_B, DH_B, DH_B), F32),
            jax.ShapeDtypeStruct((b * H_B, 1, DH_B), F32),
            jax.ShapeDtypeStruct((b, 1, LANE), F32),
        ],
        scratch_shapes=[
            pltpu.VMEM((bb * H_B, DH_B, DH_B), F32),
            pltpu.VMEM((bb * H_B, 1, DH_B), F32),
            pltpu.VMEM((bb, 1, LANE), F32),
        ],
        compiler_params=_cparams(("parallel", "arbitrary")),
        name="mlstm_L%d" % L,
    )(z3, z3, z3, z3, z3, gt, bi_row, bf_row, b8, c0r, n0r, m0r)
    del cw
    return (h.reshape(b * s, B_W), co.reshape(b, H_B, DH_B, DH_B), no.reshape(b, H_B, DH_B),
            mo[:, 0, :H_B])


def _ab_merge_kernel(o0, o1, o2, l0, l1, l2, hb_ref, og_ref, g_ref, w_ref, r_ref, out_ref):
    a0, a1, a2 = l0[...], l1[...], l2[...]
    mx = jnp.maximum(jnp.maximum(a0, a1), a2)
    e0, e1, e2 = jnp.exp(a0 - mx), jnp.exp(a1 - mx), jnp.exp(a2 - mx)
    o_a = (e0 * o0[...] + e1 * o1[...] + e2 * o2[...]) / (e0 + e1 + e2)
    hb = hb_ref[...]
    parts = []
    for h in range(H_B):
        hs = slice(h * DH_B, (h + 1) * DH_B)
        x = hb[:, hs]
        parts.append(x * lax.rsqrt(jnp.mean(x * x, axis=-1, keepdims=True) + EPS))
    hbn = jnp.concatenate(parts, axis=-1) * g_ref[...] * jax.nn.sigmoid(og_ref[...])
    y = (jnp.dot(o_a.astype(BF16), w_ref[0:AW, :], preferred_element_type=F32)
         + jnp.dot(hbn.astype(BF16), w_ref[AW:AW + B_W, :], preferred_element_type=F32))
    out_ref[...] = r_ref[...] + y


def ab_merge(os_, ls_, hb, z2, g_mn, w_bf16, resid, *, tm_pref=512):
    m, d = resid.shape
    tm = _pick_tile(m, tm_pref)
    a_spec = pl.BlockSpec((tm, AW), lambda i: (i, 0))
    return pl.pallas_call(
        _ab_merge_kernel,
        grid=(m // tm,),
        in_specs=[a_spec] * 6 + [
            pl.BlockSpec((tm, B_W), lambda i: (i, 0)),
            pl.BlockSpec((tm, B_W), lambda i: (i, AB_OG // B_W)),
            pl.BlockSpec((1, B_W), lambda i: (0, 0)),
            pl.BlockSpec((AW + B_W, d), lambda i: (0, 0)),
            pl.BlockSpec((tm, d), lambda i: (i, 0)),
        ],
        out_specs=pl.BlockSpec((tm, d), lambda i: (i, 0)),
        out_shape=jax.ShapeDtypeStruct((m, d), F32),
        compiler_params=_cparams(("parallel",)),
        name="ab_merge",
    )(*os_, *ls_, hb, z2, g_mn.reshape(1, B_W), w_bf16, resid)


HKV = KVH_C * DH_C


def _compress_bias_kernel(pa_ref, pb_ref, w_ref, o_ref):
    w = w_ref[0]
    o_ref[0] = (jnp.dot(pa_ref[0].astype(BF16), w, preferred_element_type=F32)[:, 0:HKV]
                + jnp.dot(pb_ref[0].astype(BF16), w, preferred_element_type=F32)[:, HKV:2 * HKV])


def compress_bias(pa, pb, wab2):
    spec = lambda a: pl.BlockSpec((1,) + a.shape[1:], lambda i: (i,) + (0,) * (a.ndim - 1))
    return pl.pallas_call(
        _compress_bias_kernel,
        grid=(2,),
        in_specs=[spec(pa), spec(pb), spec(wab2)],
        out_specs=pl.BlockSpec((1, 1, HKV), lambda i: (i, 0, 0)),
        out_shape=jax.ShapeDtypeStruct((2, 1, HKV), F32),
        compiler_params=_cparams(("arbitrary",)),
        name="compress_bias",
    )(pa, pb, wab2)


def _compress_halves(load_rows, n, w_all):
    x = jnp.concatenate([load_rows(r, n).astype(BF16) for r in range(CMP_STRIDE)], axis=1)
    return jnp.dot(x, w_all, preferred_element_type=F32)


def _compress_prompt_kernel(x_ref, wab_ref, bias_ref, w2_ref, o_ref):
    n = x_ref.shape[1] // CMP_STRIDE
    acc = _compress_halves(lambda r, m: x_ref[0, pl.ds(r, m, stride=CMP_STRIDE), :], n, wab_ref[0])
    hid = acc[:, 0:HKV] + pltpu.roll(acc[:, HKV:2 * HKV], n - 1, 0) + bias_ref[0]
    o_ref[0] = jnp.dot(_gelu_tanh(hid).astype(BF16), w2_ref[0], preferred_element_type=F32)


def compress_prompt(z3, cw, bias):
    b, s, _ = z3.shape
    wab, w2 = cw
    n = s // CMP_STRIDE
    return pl.pallas_call(
        _compress_prompt_kernel,
        grid=(b, 2),
        in_specs=[
            pl.BlockSpec((1, s, HKV), lambda i, kv: (i, 0, C_CMP // HKV + kv)),
            pl.BlockSpec((1,) + wab.shape[1:], lambda i, kv: (kv, 0, 0)),
            pl.BlockSpec((1, 1, HKV), lambda i, kv: (kv, 0, 0)),
            pl.BlockSpec((1, HKV, HKV), lambda i, kv: (kv, 0, 0)),
        ],
        out_specs=pl.BlockSpec((1, n, HKV), lambda i, kv: (i, 0, kv)),
        out_shape=jax.ShapeDtypeStruct((b, n, KVW), F32),
        compiler_params=_cparams(("parallel", "arbitrary")),
        name="compress_prompt",
    )(z3, wab, bias, w2)


def _compress_paged_kernel(pt_ref, *refs, pg):
    del pt_ref
    page_refs = refs[:pg + 1]
    wab_ref, bias_ref, w2_ref, o_ref, xs_ref = refs[pg + 1:]
    for j in range(pg + 1):
        t = page_refs[j][0].T
        xs_ref[0, j * PAGE_SIZE:(j + 1) * PAGE_SIZE, :] = t[:, 0:HKV]
        xs_ref[1, j * PAGE_SIZE:(j + 1) * PAGE_SIZE, :] = t[:, HKV:2 * HKV]
    cpp = PAGE_SIZE // CMP_STRIDE
    n = pg * cpp
    for kv in range(2):
        acc = _compress_halves(lambda r, m: xs_ref[kv, pl.ds(r, m, stride=CMP_STRIDE), :], n + cpp, wab_ref[kv])
        hid = acc[0:n, 0:HKV] + acc[1:n + 1, HKV:2 * HKV] + bias_ref[kv]
        o_ref[0, :, kv * HKV:(kv + 1) * HKV] = jnp.dot(_gelu_tanh(hid).astype(BF16), w2_ref[kv],
                                                        preferred_element_type=F32)


def compress_paged(pool_t, page_table, cw, bias, *, pg=32):
    db, n_pages = page_table.shape
    cpp = PAGE_SIZE // CMP_STRIDE
    pg = min(pg, n_pages)
    assert n_pages % pg == 0
    wab, w2 = cw
    full = lambda a: pl.BlockSpec(a.shape, lambda b, g, pt: (0,) * a.ndim)

    def page_spec(j):
        return pl.BlockSpec((1, KVW, PAGE_SIZE),
                            lambda b, g, pt: (pt[b, jnp.minimum(g * pg + j, n_pages - 1)], 0, 0))

    gs = pltpu.PrefetchScalarGridSpec(
        num_scalar_prefetch=1,
        grid=(db, n_pages // pg),
        in_specs=[page_spec(j) for j in range(pg + 1)] + [full(wab), full(bias), full(w2)],
        out_specs=pl.BlockSpec((1, pg * cpp, KVW), lambda b, g, pt: (b, g, 0)),
        scratch_shapes=[pltpu.VMEM((2, (pg + 1) * PAGE_SIZE, HKV), F32)],
    )
    return pl.pallas_call(
        functools.partial(_compress_paged_kernel, pg=pg),
        grid_spec=gs,
        out_shape=jax.ShapeDtypeStruct((db, n_pages * cpp, KVW), F32),
        compiler_params=_cparams(("parallel", "arbitrary")),
        name="compress_paged",
    )(page_table, *([pool_t] * (pg + 1)), wab, bias, w2)


def _stack_heads(x, t):
    del t
    return jnp.concatenate([x[:, g * DH_C:(g + 1) * DH_C] for g in range(G_C)], axis=0)


def _mask_rows(s, mask, fill):
    t, n = mask.shape
    return jnp.where(mask[None], s.reshape(G_C, t, n), fill).reshape(G_C * t, n)


def _cmp_branch(q2, kc, vc, t_pos, tq):
    n = kc.shape[0]
    s = _dot_nt(q2, kc)
    cend = lax.broadcasted_iota(jnp.int32, (tq, n), 1) * CMP_STRIDE + (CMP_BLK - 1)
    s = _mask_rows(s, cend <= t_pos, -jnp.inf)
    m = jnp.max(s, axis=-1, keepdims=True)
    m = jnp.where(m > -jnp.inf, m, 0.0)
    p = jnp.exp(s - m)
    p = p / jnp.maximum(jnp.sum(p, axis=-1, keepdims=True), 1e-30)
    o = jnp.dot(p.astype(BF16), vc, preferred_element_type=F32)
    pg = p[0:tq]
    for g in range(1, G_C):
        pg = pg + p[g * tq:(g + 1) * tq]
    return o, pg


def _select_blocks(imp, t_pos):
    tq, nsp = imp.shape
    j = lax.broadcasted_iota(jnp.int32, (tq, nsp), 1)
    jf = j.astype(F32)
    cur = t_pos // SLC_BLK
    forced = (j == 0) | (j == cur) | (j == cur - 1)
    work = jnp.where(forced, jnp.inf, jnp.where(j <= cur, imp, -jnp.inf))
    sel = jnp.zeros((tq, nsp), F32)
    for _ in range(N_SEL):
        mx = jnp.max(work, axis=-1, keepdims=True)
        first = jnp.min(jnp.where(work == mx, jf, float(nsp)), axis=-1, keepdims=True)
        pick = jf == first
        sel = jnp.where(pick, jnp.where(mx > -jnp.inf, 1.0, sel), sel)
        work = jnp.where(pick, -jnp.inf, work)
    return sel


def _select_blocks_t(imp_t, t_row):
    nsp, n = imp_t.shape
    j = lax.broadcasted_iota(jnp.int32, (nsp, n), 0)
    jf = j.astype(F32)
    cur = t_row // SLC_BLK
    forced = (j == 0) | (j == cur) | (j == cur - 1)
    work = jnp.where(forced, jnp.inf, jnp.where(j <= cur, imp_t, -jnp.inf))
    sel = jnp.zeros((nsp, n), F32)
    for _ in range(N_SEL):
        mx = jnp.max(work, axis=0, keepdims=True)
        first = jnp.min(jnp.where(work == mx, jf, float(nsp)), axis=0, keepdims=True)
        pick = jf == first
        sel = jnp.where(pick, jnp.where(mx > -jnp.inf, 1.0, sel), sel)
        work = jnp.where(pick, -jnp.inf, work)
    return sel


def _flash_update(s, mask, v, m_ref, l_ref, acc_ref, v_channel_major=False):
    s = _mask_rows(s, mask, NEG_BIG)
    m_old = m_ref[...]
    m_new = jnp.maximum(m_old, jnp.max(s, axis=-1, keepdims=True))
    alpha = jnp.exp(m_old - m_new)
    p = jnp.exp(s - m_new)
    l_ref[...] = alpha * l_ref[...] + jnp.sum(p, axis=-1, keepdims=True)
    pv = _dot_nt(p.astype(BF16), v) if v_channel_major else jnp.dot(p.astype(BF16), v, preferred_element_type=F32)
    acc_ref[...] = alpha * acc_ref[...] + pv
    m_ref[...] = m_new


def _flash_finish(m_ref, l_ref, acc_ref):
    return jnp.where(m_ref[...] > 0.5 * NEG_BIG, acc_ref[...] / jnp.maximum(l_ref[...], 1e-30), 0.0)


def _softmax_av(s, mask, v):
    s = _mask_rows(s, mask, -jnp.inf)
    m = jnp.max(s, axis=-1, keepdims=True)
    m = jnp.where(m > -jnp.inf, m, 0.0)
    p = jnp.exp(s - m)
    p = p / jnp.maximum(jnp.sum(p, axis=-1, keepdims=True), 1e-30)
    return jnp.dot(p.astype(BF16), v, preferred_element_type=F32)


def _gate_mix(gates, h, o_cmp, o_sel, o_win, tq):
    outs = []
    for g in range(G_C):
        base = (h * G_C + g) * 3
        rs = slice(g * tq, (g + 1) * tq)
        outs.append(gates[:, base:base + 1] * o_cmp[rs] + gates[:, base + 1:base + 2] * o_sel[rs]
                    + gates[:, base + 2:base + 3] * o_win[rs])
    return jnp.concatenate(outs, axis=-1)


def _rope_q(qh, cos, sin):
    return jnp.concatenate(
        [_rope_lanes(qh[:, c * LANE:(c + 1) * LANE], cos, sin) for c in range(G_C * DH_C // LANE)], axis=-1)


SEL_TK = 1024


SEL_PHASE = DH_C


def _nsa_prompt_kernel(q_ref, gz_ref, cos_ref, sin_ref, kc_ref, kv_ref, ka_ref, va_ref, mimp_ref, o_ref,
                       qa_s, m_s, acc_s, *, nsp):
    i = pl.program_id(1)
    tq = QBLK_C
    rows = G_C * tq
    s0 = i * tq
    t_pos = s0 + lax.broadcasted_iota(jnp.int32, (tq, 1), 0)
    gates = jax.nn.sigmoid(gz_ref[0])
    cos, sin = cos_ref[...], sin_ref[...]
    scale = DH_C ** -0.5
    hw = G_C * DH_C
    ncmp = kc_ref.shape[1]
    t_row = s0 + lax.broadcasted_iota(jnp.int32, (1, rows), 1) % tq
    visible = (lax.broadcasted_iota(jnp.int32, (ncmp, 1), 0) * CMP_STRIDE + (CMP_BLK - 1)) <= t_row
    any_visible = t_row >= CMP_BLK - 1
    zpad = jnp.zeros((tq, DH_C), F32)
    kcc = kc_ref[0, :, 0:HKV].astype(BF16)
    vcc = kc_ref[0, :, HKV:2 * HKV].astype(BF16)
    q_rot, o_cmps, imps_t = [], [], []
    for h in range(KVH_C):
        qh = q_ref[0, :, h * hw:(h + 1) * hw] * scale
        q_rot.append(_rope_q(qh, cos, sin))
        qc = jnp.concatenate([jnp.concatenate([x, zpad] if h == 0 else [zpad, x], axis=1)
                              for x in (qh[:, g * DH_C:(g + 1) * DH_C] for g in range(G_C))], axis=0)
        st = jnp.where(visible, _dot_nt(kcc, qc.astype(BF16)), NEG_BIG)
        pt = jnp.exp(st - jnp.max(st, axis=0, keepdims=True))
        inv = jnp.where(any_visible, 1.0 / jnp.maximum(jnp.sum(pt, axis=0, keepdims=True), 1e-30), 0.0)
        pt = pt * inv
        o_cmps.append(_dot_tn(vcc, pt.astype(BF16))[h * DH_C:(h + 1) * DH_C, :].T)
        pgrp_t = pt[:, 0:tq]
        for g in range(1, G_C):
            pgrp_t = pgrp_t + pt[:, g * tq:(g + 1) * tq]
        imps_t.append(_dot_exact_lhs01(mimp_ref[...], pgrp_t))
    t_row2 = s0 + lax.broadcasted_iota(jnp.int32, (1, KVH_C * tq), 1) % tq
    sel_t = _select_blocks_t(jnp.concatenate(imps_t, axis=1), t_row2)

    half_of_lane = lax.broadcasted_iota(jnp.int32, (1, LANE), 1) // DH_C

    def into_half(chunk, src_half, dst_half):
        moved = chunk if src_half == dst_half else pltpu.roll(chunk, DH_C, 1)
        return jnp.where(half_of_lane == dst_half, moved, 0.0)

    for h in range(KVH_C):
        selb = jnp.where(sel_t[:, h * tq:(h + 1) * tq].T > 0.5, 0.0, NEG_BIG)
        q_part = jnp.concatenate(
            [into_half(q_rot[h][:, (g // 2) * LANE:(g // 2 + 1) * LANE], g % 2, h) for g in range(G_C)], axis=0)
        for ph in range(nsp // SEL_PHASE):
            sb = into_half(selb[:, (ph // 2) * LANE:(ph // 2 + 1) * LANE], ph % 2, 1 - h)
            qa_s[h, ph] = (q_part + jnp.concatenate([sb] * G_C, axis=0)).astype(BF16)
    m_s[...] = jnp.full(m_s.shape, NEG_BIG, F32)
    acc_s[...] = jnp.zeros(acc_s.shape, F32)

    def tile(k0, size, causal):
        k0 = pl.multiple_of(k0, size)
        ph = k0 // (SEL_PHASE * SLC_BLK)
        for h in range(KVH_C):
            st = _dot_nt(ka_ref[0, h, pl.ds(k0, size), :], qa_s[h, ph])
            if causal:
                kpos = k0 + lax.broadcasted_iota(jnp.int32, (size, 1), 0)
                t_row = s0 + lax.broadcasted_iota(jnp.int32, (1, rows), 1) % tq
                st = jnp.where(kpos <= t_row, st, NEG_BIG)
            m_old = m_s[h]
            m_new = jnp.maximum(m_old, jnp.max(st, axis=0, keepdims=True))
            pt = jnp.exp(st - m_new).astype(BF16)
            acc_s[h] = (jnp.exp(m_old - m_new) * acc_s[h]
                        + jnp.dot(va_ref[0, h, :, pl.ds(k0, size)], pt, preferred_element_type=F32))
            m_s[h] = m_new

    def full_tile(kt, carry):
        tile(kt * SEL_TK, SEL_TK, False)
        return carry

    half = SEL_TK // 2
    own = (s0 + tq - 1) // half
    lax.fori_loop(0, own // 2, full_tile, 0)

    @pl.when(own % 2 == 1)
    def _():
        tile((own - 1) * half, half, False)

    tile(own * half, half, True)

    def finish(acc, h):
        sums = acc[(1 - h) * DH_C:(1 - h) * DH_C + 1, :]
        return (acc[h * DH_C:(h + 1) * DH_C, :] / jnp.maximum(sums, 1e-30)).T

    nw = WIN_C + tq
    w0 = pl.multiple_of(jnp.maximum(s0 - WIN_C, 0), tq)
    dlt = (s0 + lax.broadcasted_iota(jnp.int32, (1, rows), 1) % tq) - (w0 + lax.broadcasted_iota(jnp.int32, (nw, 1), 0))
    in_win = (dlt >= 0) & (dlt < WIN_C)
    lane = lax.broadcasted_iota(jnp.int32, (1, LANE), 1) // DH_C
    kwc = kv_ref[0, pl.ds(w0, nw), 0:LANE]
    vwc = kv_ref[0, pl.ds(w0, nw), LANE:2 * LANE]
    for h in range(KVH_C):
        st = _dot_nt(jnp.where(lane == h, kwc, jnp.zeros_like(kwc)), qa_s[h, 0])
        st = jnp.where(in_win, st, NEG_BIG)
        pt = jnp.exp(st - jnp.max(st, axis=0, keepdims=True)).astype(BF16)
        o_win = finish(_dot_tn(jnp.where(lane == h, vwc, jnp.ones_like(vwc)), pt), h)
        o_ref[0, :, h * hw:(h + 1) * hw] = _gate_mix(gates, h, o_cmps[h], finish(acc_s[h], h), o_win, tq)


def _cols_t_kernel(x_ref, o_ref):
    o_ref[0] = x_ref[0].T.astype(BF16)


def cols_channel_major(z3, col0, *, ts=1024):
    b, s, _ = z3.shape
    ts = _pick_tile(s, ts)
    return pl.pallas_call(
        _cols_t_kernel,
        grid=(b, s // ts),
        in_specs=[pl.BlockSpec((1, ts, LANE), lambda bb, i: (bb, i, col0 // LANE))],
        out_specs=pl.BlockSpec((1, LANE, ts), lambda bb, i: (bb, 0, i)),
        out_shape=jax.ShapeDtypeStruct((b, LANE, s), BF16),
        compiler_params=_cparams(("parallel", "parallel")),
        name="cols_channel_major",
    )(z3)


def _cols_cast_kernel(x_ref, o_ref):
    o_ref[...] = x_ref[...].astype(BF16)


def cols_bf16(z3, col0, width, *, ts=1024):
    b, s, _ = z3.shape
    ts = _pick_tile(s, ts)
    return pl.pallas_call(
        _cols_cast_kernel,
        grid=(b, s // ts),
        in_specs=[pl.BlockSpec((1, ts, width), lambda bb, i: (bb, i, col0 // width))],
        out_specs=pl.BlockSpec((1, ts, width), lambda bb, i: (bb, i, 0)),
        out_shape=jax.ShapeDtypeStruct((b, s, width), BF16),
        compiler_params=_cparams(("parallel", "parallel")),
        name="cols_bf16",
    )(z3)


def _sel_operands(z3):
    s = z3.shape[1]
    lane = jnp.arange(LANE)
    hot = ((jnp.arange(s)[:, None] // SLC_BLK) % SEL_PHASE == lane[None, :] % SEL_PHASE).astype(BF16)
    ksel = cols_bf16(z3, C_SEL, HKV)
    vsel_t = cols_channel_major(z3, C_SEL + HKV)
    ka = jnp.stack([jnp.where((lane // DH_C == h)[None, None, :], ksel, hot[None]) for h in range(KVH_C)], axis=1)
    va = jnp.stack([jnp.where((lane // DH_C == h)[None, :, None], vsel_t, jnp.ones_like(vsel_t))
                    for h in range(KVH_C)], axis=1)
    return ka, va


def nsa_prompt(z3, kc, kvw, ka, va, cos_t, sin_t, mimp):
    b, s, _ = z3.shape
    nsp = mimp.shape[0]
    nq = s // QBLK_C
    ncmp = kc.shape[1]
    rows = G_C * QBLK_C
    vm = pltpu.VMEM
    scratch = [
        vm((KVH_C, nsp // SEL_PHASE, rows, LANE), BF16),
        vm((KVH_C, 1, rows), F32),
        vm((KVH_C, LANE, rows), F32),
    ]
    return pl.pallas_call(
        functools.partial(_nsa_prompt_kernel, nsp=nsp),
        grid=(b, nq),
        in_specs=[
            pl.BlockSpec((1, QBLK_C, C_Q), lambda bb, i: (bb, i, 0)),
            pl.BlockSpec((1, QBLK_C, LANE), lambda bb, i: (bb, i, C_GATE // LANE)),
            pl.BlockSpec((QBLK_C, LANE), lambda bb, i: (i, 0)),
            pl.BlockSpec((QBLK_C, LANE), lambda bb, i: (i, 0)),
            pl.BlockSpec((1, ncmp, KVW), lambda bb, i: (bb, 0, 0)),
            pl.BlockSpec((1, s, KVW), lambda bb, i: (bb, 0, 0)),
            pl.BlockSpec((1, KVH_C, s, LANE), lambda bb, i: (bb, 0, 0, 0)),
            pl.BlockSpec((1, KVH_C, LANE, s), lambda bb, i: (bb, 0, 0, 0)),
            pl.BlockSpec(mimp.shape, lambda bb, i: (0, 0)),
        ],
        out_specs=pl.BlockSpec((1, QBLK_C, C_Q), lambda bb, i: (bb, i, 0)),
        out_shape=jax.ShapeDtypeStruct((b, s, C_Q), F32),
        scratch_shapes=scratch,
        compiler_params=_cparams(("parallel", "arbitrary")),
        name="nsa_prompt",
    )(z3, z3, cos_t, sin_t, kc, kvw, ka, va, mimp)


def _nsa_sample_kernel(pt_ref, *refs, pg, nsp, past, t_new):
    del pt_ref
    page_refs = refs[:pg]
    (q_ref, gz_ref, cos_ref, sin_ref, kc_ref, mimp_ref, hot_ref, seln_ref, winb_ref, winn_ref, winnt_ref,
     o_ref, wo_ref, q2r_s, qa_s, sel_s, ocmp_s, m_s, l_s, acc_s) = refs[pg:]
    g = pl.program_id(1)
    ng = pl.num_programs(1)
    tq = T_PAD
    rows = G_C * tq
    t_pos = past + lax.broadcasted_iota(jnp.int32, (tq, 1), 0)
    scale = DH_C ** -0.5
    hw = G_C * DH_C
    nk = pg * PAGE_SIZE
    gblk = nk // SLC_BLK

    @pl.when(g == 0)
    def _():
        cos, sin = cos_ref[...], sin_ref[...]
        q2rs, imps = [], []
        for h in range(KVH_C):
            qh = q_ref[0, :, h * hw:(h + 1) * hw] * scale
            q2 = _stack_heads(qh, tq).astype(BF16)
            q2rs.append(_stack_heads(_rope_q(qh, cos, sin), tq))
            q2r_s[h] = q2rs[h].astype(BF16)
            kc = kc_ref[0, :, h * DH_C:(h + 1) * DH_C].astype(BF16)
            vc = kc_ref[0, :, KVH_C * DH_C + h * DH_C:KVH_C * DH_C + (h + 1) * DH_C].astype(BF16)
            o_cmp, pgrp = _cmp_branch(q2, kc, vc, t_pos, tq)
            ocmp_s[h] = o_cmp
            imps.append(_dot_exact_rhs01(pgrp, mimp_ref[...]))
        sel_all = _select_blocks(jnp.concatenate(imps, axis=0), jnp.concatenate([t_pos] * KVH_C, axis=0))
        zpad = [jnp.zeros((rows, DH_C - gblk), F32)] if gblk < DH_C else []
        for h in range(KVH_C):
            sel = sel_all[h * tq:(h + 1) * tq]
            sel_s[h] = sel
            selb = jnp.where(sel > 0.5, 0.0, NEG_BIG)
            for gg in range(qa_s.shape[1]):
                sb = jnp.concatenate([selb[:, gg * gblk:(gg + 1) * gblk]] * G_C, axis=0)
                qa_s[h, gg] = jnp.concatenate([q2rs[h], sb] + zpad, axis=1).astype(BF16)
        m_s[...] = jnp.full(m_s.shape, NEG_BIG, F32)
        l_s[...] = jnp.zeros(l_s.shape, F32)
        acc_s[...] = jnp.zeros(acc_s.shape, F32)

    kvt = jnp.concatenate([r[0] for r in page_refs], axis=1).astype(BF16)
    hot = hot_ref[...]
    for h in range(KVH_C):
        kst = jnp.concatenate([kvt[h * DH_C:(h + 1) * DH_C, :], hot], axis=0)
        vst = kvt[KVH_C * DH_C + h * DH_C:KVH_C * DH_C + (h + 1) * DH_C, :]
        s = jnp.dot(qa_s[h, g], kst, preferred_element_type=F32)
        m_old = m_s[h]
        m_new = jnp.maximum(m_old, jnp.max(s, axis=-1, keepdims=True))
        alpha = jnp.exp(m_old - m_new)
        p = jnp.exp(s - m_new)
        l_s[h] = alpha * l_s[h] + jnp.sum(p, axis=-1, keepdims=True)
        acc_s[h] = alpha * acc_s[h] + _dot_nt(p.astype(BF16), vst)
        m_s[h] = m_new

    @pl.when(g == ng - 1)
    def _():
        gates = jax.nn.sigmoid(gz_ref[0])
        n_buf = winb_ref.shape[2]
        lane_n = lax.broadcasted_iota(jnp.int32, (tq, nsp), 1)
        u = lax.broadcasted_iota(jnp.int32, (tq, tq), 1)
        seln = seln_ref[0]
        winn = winn_ref[0]
        for h in range(KVH_C):
            kcol = h * DH_C
            vcol = KVH_C * DH_C + h * DH_C
            q2r = q2r_s[h]
            new_sel = jnp.max(jnp.where(lane_n == past // SLC_BLK, sel_s[h], 0.0), axis=-1, keepdims=True) > 0.5
            mk = new_sel & (past + u <= t_pos) & (u < t_new)
            _flash_update(_dot_nt(q2r, seln[:, kcol:kcol + DH_C].astype(BF16)), mk,
                          seln[:, vcol:vcol + DH_C].astype(BF16), m_s.at[h], l_s.at[h], acc_s.at[h])
            o_sel = _flash_finish(m_s.at[h], l_s.at[h], acc_s.at[h])
            kbt = winb_ref[0, kcol:kcol + DH_C, :].astype(BF16)
            vbt = winb_ref[0, vcol:vcol + DH_C, :].astype(BF16)
            pos_b = past - n_buf + lax.broadcasted_iota(jnp.int32, (tq, n_buf), 1)
            d_b = t_pos - pos_b
            m_b = (pos_b >= 0) & (d_b >= 0) & (d_b < WIN_C)
            d_n = t_pos - (past + u)
            m_n = (d_n >= 0) & (d_n < WIN_C) & (u < t_new)
            s_b = _mask_rows(jnp.dot(q2r, kbt, preferred_element_type=F32), m_b, -jnp.inf)
            s_n = _mask_rows(_dot_nt(q2r, winn[:, kcol:kcol + DH_C].astype(BF16)), m_n, -jnp.inf)
            mx = jnp.maximum(jnp.max(s_b, axis=-1, keepdims=True), jnp.max(s_n, axis=-1, keepdims=True))
            mx = jnp.where(mx > -jnp.inf, mx, 0.0)
            p_b, p_n = jnp.exp(s_b - mx), jnp.exp(s_n - mx)
            den = jnp.maximum(jnp.sum(p_b, axis=-1, keepdims=True) + jnp.sum(p_n, axis=-1, keepdims=True), 1e-30)
            o_win = (_dot_nt((p_b / den).astype(BF16), vbt)
                     + jnp.dot((p_n / den).astype(BF16), winn[:, vcol:vcol + DH_C].astype(BF16),
                               preferred_element_type=F32))
            o_ref[0, :, h * hw:(h + 1) * hw] = _gate_mix(gates, h, ocmp_s[h], o_sel, o_win, tq)
        wo_ref[0, :, 0:n_buf - t_new] = winb_ref[0, :, t_new:n_buf]
        wo_ref[0, :, n_buf - t_new:n_buf] = winnt_ref[0, :, 0:t_new]


def nsa_sample(z3, kc, sel_pool_t, win_buf_t, win_new_t, page_table, cos_t, sin_t, mimp, *, t_new, pg=32):
    db, n_pages = page_table.shape
    past = n_pages * PAGE_SIZE
    ncmp = kc.shape[1]
    nsp = mimp.shape[1]
    n_buf = win_buf_t.shape[2]
    pg = min(pg, n_pages)
    nk = pg * PAGE_SIZE
    assert n_pages % pg == 0 and nk // SLC_BLK <= DH_C
    rows = G_C * T_PAD
    hot = jnp.asarray(np.arange(nk)[None, :] // SLC_BLK == np.arange(DH_C)[:, None], dtype=BF16)

    def page_spec(j):
        return pl.BlockSpec((1, KVW, PAGE_SIZE), lambda b, g, pt: (pt[b, g * pg + j], 0, 0))

    gs = pltpu.PrefetchScalarGridSpec(
        num_scalar_prefetch=1,
        grid=(db, n_pages // pg),
        in_specs=[page_spec(j) for j in range(pg)] + [
            pl.BlockSpec((1, T_PAD, C_Q), lambda b, g, pt: (b, 0, 0)),
            pl.BlockSpec((1, T_PAD, LANE), lambda b, g, pt: (b, 0, C_GATE // LANE)),
            pl.BlockSpec((T_PAD, LANE), lambda b, g, pt: (0, 0)),
            pl.BlockSpec((T_PAD, LANE), lambda b, g, pt: (0, 0)),
            pl.BlockSpec((1, ncmp, KVW), lambda b, g, pt: (b, 0, 0)),
            pl.BlockSpec(mimp.shape, lambda b, g, pt: (0, 0)),
            pl.BlockSpec((DH_C, nk), lambda b, g, pt: (0, 0)),
            pl.BlockSpec((1, T_PAD, KVW), lambda b, g, pt: (b, 0, C_SEL // KVW)),
            pl.BlockSpec((1, KVW, n_buf), lambda b, g, pt: (b, 0, 0)),
            pl.BlockSpec((1, T_PAD, KVW), lambda b, g, pt: (b, 0, C_WIN // KVW)),
            pl.BlockSpec((1, KVW, T_PAD), lambda b, g, pt: (b, 0, 0)),
        ],
        out_specs=[
            pl.BlockSpec((1, T_PAD, C_Q), lambda b, g, pt: (b, 0, 0)),
            pl.BlockSpec((1, KVW, n_buf), lambda b, g, pt: (b, 0, 0)),
        ],
        scratch_shapes=[
            pltpu.VMEM((KVH_C, rows, DH_C), BF16),
            pltpu.VMEM((KVH_C, n_pages // pg, rows, LANE), BF16),
            pltpu.VMEM((KVH_C, T_PAD, nsp), F32),
            pltpu.VMEM((KVH_C, rows, DH_C), F32),
            pltpu.VMEM((KVH_C, rows, 1), F32),
            pltpu.VMEM((KVH_C, rows, 1), F32),
            pltpu.VMEM((KVH_C, rows, DH_C), F32),
        ],
    )
    return pl.pallas_call(
        functools.partial(_nsa_sample_kernel, pg=pg, nsp=nsp, past=past, t_new=t_new),
        grid_spec=gs,
        out_shape=[jax.ShapeDtypeStruct((db, T_PAD, C_Q), F32), jax.ShapeDtypeStruct((db, KVW, n_buf), F32)],
        compiler_params=_cparams(("parallel", "arbitrary")),
        name="nsa_sample",
    )(page_table, *([sel_pool_t] * pg), z3, z3, cos_t, sin_t, kc, mimp, hot, z3, win_buf_t, z3, win_new_t)


def _rope_tables(pos):
    half = DH_C // 2
    inv = 1.0 / (ROPE_THETA ** (jnp.arange(half, dtype=F32) / half))
    ang = pos.astype(F32)[:, None] * inv[None, :]
    cos, sin = jnp.cos(ang), jnp.sin(ang)
    return jnp.tile(cos, (1, 4)), jnp.concatenate([-sin, sin, -sin, sin], axis=-1)


def _prep_w_in_ab(w):
    cuts = np.cumsum([A_QW, A_QW, A_QW, B_W, B_W, B_W, H_B, H_B])
    qa, ka, va, qb, kb, vb, ig, fg, og = jnp.split(w, cuts, axis=-1)
    padg = lambda t: jnp.pad(t, ((0, 0), (0, LANE - H_B)))
    return jnp.concatenate([qb, kb, vb, og, qa, ka, va, padg(ig), padg(fg)], axis=-1).astype(BF16)


def _ab_rope_flags():
    return tuple(range(AB_QA // LANE, AB_VA // LANE))


def _c_rope_flags():
    return (C_SEL // LANE, C_WIN // LANE)


def _prep_compress(w1, w2, pe):
    e2 = jnp.eye(KVH_C, dtype=F32)

    def half(w1h):
        return jnp.einsum('krde,hH->krhdHe', w1h, e2).reshape(2, CMP_STRIDE, HKV, HKV).astype(BF16)

    wab = jnp.concatenate([half(w1[:, :CMP_STRIDE]), half(w1[:, CMP_STRIDE:])], axis=-1)
    w2b = jnp.einsum('ked,hH->kheHd', w2, e2).reshape(2, HKV, HKV).astype(BF16)

    def pe_half(p):
        return jnp.broadcast_to(p[:, :, None, :], (2, CMP_STRIDE, KVH_C, DH_C)).reshape(2, 1, CMP_STRIDE * HKV)

    wab = wab.reshape(2, CMP_STRIDE * HKV, 2 * HKV)
    bias = compress_bias(pe_half(pe[:, :CMP_STRIDE]), pe_half(pe[:, CMP_STRIDE:]), wab)
    return (wab, w2b), bias


def _channel_major(x, lead):
    perm = tuple(range(lead)) + (lead + 1, lead + 2, lead + 3, lead)
    xt = jnp.transpose(x, perm)
    return xt.reshape(x.shape[:lead] + (x.shape[lead + 1] * x.shape[lead + 2] * x.shape[lead + 3], x.shape[lead]))


def _row_major(xt, c0, c1, c2):
    lead, _, rows = xt.shape
    return jnp.transpose(xt.reshape(lead, c0, c1, c2, rows), (0, 4, 1, 2, 3))


def _importance_matrix(n_rows, n_cmp, n_slc, n_cols):
    ratio = SLC_BLK // CMP_STRIDE
    m = np.zeros((n_rows, n_cols), np.float32)
    for jblk in range(n_slc):
        for off in range(1 - CMP_BLK // CMP_STRIDE, ratio):
            i = ratio * jblk + off
            if 0 <= i < n_cmp:
                m[i, jblk] = 1.0
    return jnp.asarray(m, dtype=BF16)


def _gates_t(z3, L):
    b, s, _ = z3.shape
    g = jnp.concatenate([z3[..., AB_IG:AB_IG + H_B], z3[..., AB_FG:AB_FG + H_B]], axis=-1)
    return g.reshape(b, s // L, L, 2 * H_B).transpose(0, 1, 3, 2)


def kernel(x_prompt, x_sample, cache_a0_kv, cache_a1_kv, cache_a2_kv, state_b_C, state_b_n, state_b_m,
           cache_c_cmp_kv, cache_c_sel_kv, cache_c_win_kv, page_table, norm_g, w_in_ab, b_if, g_mlstm,
           w_out_ab, w_in_c, cmp_w1, cmp_w2, cmp_pe, w_out_c, w_ffn_gate, w_ffn_up, w_ffn_down, norm_final):
    B, S, D = x_prompt.shape
    DB, T, _ = x_sample.shape
    depth = norm_g.shape[0]
    n_pages = page_table.shape[1]
    past = n_pages * PAGE_SIZE
    caches_a = (cache_a0_kv, cache_a1_kv, cache_a2_kv)
    assert T <= T_PAD and S % (DIL_CFG[-1][1] * BLK_A) == 0 and S >= WIN_C + QBLK_C

    hp = x_prompt.reshape(B * S, D)
    hs = jnp.pad(x_sample, ((0, 0), (0, T_PAD - T), (0, 0))).reshape(DB * T_PAD, D)

    pos_p = jnp.arange(S)
    pos_s = past + jnp.arange(T_PAD)
    cos_p1, sin_p1 = _rope_tables(pos_p)
    cos_s1, sin_s1 = _rope_tables(pos_s)
    cos_p, sin_p = jnp.tile(cos_p1, (B, 1)), jnp.tile(sin_p1, (B, 1))
    cos_s, sin_s = jnp.tile(cos_s1, (DB, 1)), jnp.tile(sin_s1, (DB, 1))

    a_p, a_s = [[], [], []], [[], [], []]
    bC_p, bC_s, bn_p, bn_s, bm_p, bm_s = [], [], [], [], [], []
    cc_p, cc_s, csl_p, csl_s, cw_p, cw_s = [], [], [], [], [], []

    for layer in range(depth):
        if layer % 2 == 0:
            e = layer // 2
            w_in = _prep_w_in_ab(w_in_ab[e])
            flags = _ab_rope_flags()
            w_out = w_out_ab[e].astype(BF16)
            z = norm_proj(hp, norm_g[layer, 0], w_in, flags, cos_p, sin_p)
            z3 = z.reshape(B, S, AB_N)
            os_, ls_ = [], []
            for gi, (win, dil) in enumerate(DIL_CFG):
                o, l = dil_prompt(z3, gi, win, dil)
                os_.append(o)
                ls_.append(l)
                nb = min(win, S)
                kk = z3[:, S - nb:, AB_KA + gi * AW:AB_KA + (gi + 1) * AW].reshape(B, nb, H_A, DH_A)
                vv = z3[:, S - nb:, AB_VA + gi * AW:AB_VA + (gi + 1) * AW].reshape(B, nb, H_A, DH_A)
                a_p[gi].append(jnp.stack([kk, vv], axis=2))
            zc = jnp.zeros
            hb, Cp, n_p, m_p = mlstm(z3, _gates_t(z3, MLSTM_CHUNK), b_if[e],
                                     zc((B, H_B, DH_B, DH_B), F32), zc((B, H_B, DH_B), F32), zc((B, H_B), F32),
                                     L=MLSTM_CHUNK, n_valid=MLSTM_CHUNK, bb=B if B <= 2 else 1)
            hp = ab_merge(os_, ls_, hb, z, g_mlstm[e], w_out, hp)
            bC_p.append(Cp); bn_p.append(n_p); bm_p.append(m_p)
            z = norm_proj(hs, norm_g[layer, 0], w_in, flags, cos_s, sin_s)
            z3 = z.reshape(DB, T_PAD, AB_N)
            os_, ls_ = [], []
            for gi, (win, dil) in enumerate(DIL_CFG):
                new_t = jnp.concatenate(
                    [jnp.swapaxes(z3[:, :, AB_KA + gi * AW:AB_KA + (gi + 1) * AW], 1, 2),
                     jnp.swapaxes(z3[:, :, AB_VA + gi * AW:AB_VA + (gi + 1) * AW], 1, 2)], axis=1)
                o, l, co = dil_sample(_channel_major(caches_a[gi][e], 1), z3, new_t, gi, win, dil, T)
                os_.append(o)
                ls_.append(l)
                a_s[gi].append(_row_major(co, 2, H_A, DH_A))
            bbs = 4 if DB % 4 == 0 else 1
            hb, Cs, n_s, m_s = mlstm(z3, _gates_t(z3, T_PAD), b_if[e], state_b_C[e], state_b_n[e], state_b_m[e],
                                     L=T_PAD, n_valid=T, bb=bbs)
            hs = ab_merge(os_, ls_, hb, z, g_mlstm[e], w_out, hs)
            bC_s.append(Cs); bn_s.append(n_s); bm_s.append(m_s)
        else:
            o_i = layer // 2
            w_in = jnp.pad(w_in_c[o_i], ((0, 0), (0, C_N - w_in_c.shape[-1]))).astype(BF16)
            flags = _c_rope_flags()
            w_out = w_out_c[o_i].astype(BF16)
            cw, cbias = _prep_compress(cmp_w1[o_i], cmp_w2[o_i], cmp_pe[o_i])
            z = norm_proj(hp, norm_g[layer, 0], w_in, flags, cos_p, sin_p)
            z3 = z.reshape(B, S, C_N)
            kv_cmp = z3[:, :, C_CMP:C_CMP + KVW]
            kv_sel = z3[:, :, C_SEL:C_SEL + KVW]
            kv_win = z3[:, :, C_WIN:C_WIN + KVW]
            kc = compress_prompt(z3, cw, cbias)
            n_cmp = (S - CMP_BLK) // CMP_STRIDE + 1
            n_slc = S // SLC_BLK
            mimp = _importance_matrix(S // CMP_STRIDE, n_cmp, n_slc, -(-n_slc // LANE) * LANE)
            ka, va = _sel_operands(z3)
            o = nsa_prompt(z3, kc, cols_bf16(z3, C_WIN, KVW), ka, va, cos_p1, sin_p1, mimp.T)
            hp = out_proj(o.reshape(B * S, C_Q), w_out, hp)
            nw = min(WIN_C, S)
            sh = lambda t: t.reshape(t.shape[0], t.shape[1], 2, KVH_C, DH_C)
            cc_p.append(sh(kv_cmp)); csl_p.append(sh(kv_sel)); cw_p.append(sh(kv_win[:, S - nw:]))
            z = norm_proj(hs, norm_g[layer, 0], w_in, flags, cos_s, sin_s)
            z3 = z.reshape(DB, T_PAD, C_N)
            kc = compress_paged(_channel_major(cache_c_cmp_kv[o_i], 1), page_table, cw, cbias)
            full_len = past + T
            n_cmp = (full_len - CMP_BLK) // CMP_STRIDE + 1
            assert (n_cmp + 1) * CMP_STRIDE <= past
            n_slc = past // SLC_BLK + -(-T // SLC_BLK)
            nsp = -(-n_slc // LANE) * LANE
            mimp = _importance_matrix(past // CMP_STRIDE, n_cmp, n_slc, nsp)
            o, wo = nsa_sample(z3, kc, _channel_major(cache_c_sel_kv[o_i], 1),
                               _channel_major(cache_c_win_kv[o_i], 1),
                               jnp.swapaxes(z3[:, :, C_WIN:C_WIN + KVW], 1, 2),
                               page_table, cos_s1, sin_s1, mimp, t_new=T)
            hs = out_proj(o.reshape(DB * T_PAD, C_Q), w_out, hs)
            cc_s.append(sh(z3[:, :T, C_CMP:C_CMP + KVW])); csl_s.append(sh(z3[:, :T, C_SEL:C_SEL + KVW]))
            cw_s.append(_row_major(wo, 2, KVH_C, DH_C))
        last = layer == depth - 1
        wg, wu, wd = (w_ffn_gate[layer].astype(BF16), w_ffn_up[layer].astype(BF16), w_ffn_down[layer].astype(BF16))
        hp = ffn(hp, norm_g[layer, 1], wg, wu, wd, norm_final, final_norm=last)
        hs = ffn(hs, norm_g[layer, 1], wg, wu, wd, norm_final, final_norm=last)

    y_prompt = hp.reshape(B, S, D)
    y_sample = hs.reshape(DB, T_PAD, D)[:, :T]
    st = lambda xs: jnp.stack(xs, axis=0)
    return (y_prompt, y_sample,
            st(a_p[0]), st(a_s[0]), st(a_p[1]), st(a_s[1]), st(a_p[2]), st(a_s[2]),
            st(bC_p), st(bC_s), st(bn_p), st(bn_s), st(bm_p), st(bm_s),
            st(cc_p), st(cc_s), st(csl_p), st(csl_s), st(cw_p), st(cw_s))
```

```python
import functools
import math

import numpy as np
import jax
import jax.numpy as jnp
from jax import lax
from jax.experimental import pallas as pl
from jax.experimental.pallas import tpu as pltpu

F32 = jnp.float32
BF16 = jnp.bfloat16

PAGE_SIZE = 128
DIL_CFG = ((128, 1), (512, 4), (2048, 16))
N_DIL = 3
H_A = 4
DH_A = 64
BLK_A = 128
H_B = 4
DH_B = 128
MLSTM_CHUNK = 128
H_C = 16
KVH_C = 2
G_C = H_C // KVH_C
DH_C = 64
CMP_STRIDE = 16
CMP_BLK = 2 * CMP_STRIDE
CMP_HID = 64
SLC_BLK = 64
N_SEL = 16
WIN_C = 512
QBLK_C = 128
ROPE_THETA = 10000.0
EPS = 1e-6
A_QW = N_DIL * H_A * DH_A
B_W = H_B * DH_B
AW = H_A * DH_A
C_Q = H_C * DH_C
C_KV = 3 * 2 * KVH_C * DH_C
KVW = 2 * KVH_C * DH_C

LANE = 128
SUBLANE = 8
VMEM_LIMIT = 48 * 1024 * 1024

NEG_BIG = -1e30
T_PAD = SUBLANE

AB_QB, AB_KB, AB_VB, AB_OG = 0, B_W, 2 * B_W, 3 * B_W
AB_QA = 4 * B_W
AB_KA = AB_QA + A_QW
AB_VA = AB_KA + A_QW
AB_IG = AB_VA + A_QW
AB_FG = AB_IG + LANE
AB_N = AB_FG + LANE

C_CMP = C_Q
C_SEL = C_Q + KVW
C_WIN = C_Q + 2 * KVW
C_GATE = C_Q + 3 * KVW
C_N = 2048


def _cparams(sem, vmem=VMEM_LIMIT):
    return pltpu.CompilerParams(dimension_semantics=sem, vmem_limit_bytes=vmem)


def _pick_tile(m, pref):
    t = min(m, pref)
    while m % t:
        t //= 2
    return t


def _rope_lanes(x, cos, sin):
    lane = lax.broadcasted_iota(jnp.int32, x.shape, 1)
    first = (lane % DH_C) < (DH_C // 2)
    partner = jnp.where(first, pltpu.roll(x, LANE - DH_C // 2, 1), pltpu.roll(x, DH_C // 2, 1))
    return x * cos + partner * sin


def _split3(x):
    hi = x.astype(BF16)
    r1 = x - hi.astype(F32)
    mid = r1.astype(BF16)
    lo = (r1 - mid.astype(F32)).astype(BF16)
    return hi, mid, lo


def _dot_exact_rhs01(x, m01):
    hi, mid, lo = _split3(x)
    d = lambda a: jnp.dot(a, m01, preferred_element_type=F32)
    return d(hi) + d(mid) + d(lo)


def _dot_exact_lhs01(m01, x):
    hi, mid, lo = _split3(x)
    d = lambda a: jnp.dot(m01, a, preferred_element_type=F32)
    return d(hi) + d(mid) + d(lo)


def _dot_nt(a, b):
    return lax.dot_general(a, b, (((1,), (1,)), ((), ())), preferred_element_type=F32)


def _dot_tn(a, b):
    return lax.dot_general(a, b, (((0,), (0,)), ((), ())), preferred_element_type=F32)


def _log_sigmoid(x):
    return jnp.minimum(x, 0.0) - jnp.log1p(jnp.exp(-jnp.abs(x)))


def _gelu_tanh(x):
    return 0.5 * x * (1.0 + jnp.tanh(math.sqrt(2.0 / math.pi) * (x + 0.044715 * (x * x * x))))


def _rms_rows(x, g):
    ms = jnp.mean(x * x, axis=-1, keepdims=True)
    return x * lax.rsqrt(ms + EPS) * g


NORM_PROJ_TN = 512
RESIDENT_W_VMEM = 56 * 1024 * 1024


def _norm_proj_kernel(x_ref, g_ref, w_ref, cos_ref, sin_ref, o_ref, *, rope_chunks):
    xn = _rms_rows(x_ref[...], g_ref[...]).astype(BF16)
    n = o_ref.shape[1]
    for c0 in range(0, n, NORM_PROJ_TN):
        cs = slice(c0, min(c0 + NORM_PROJ_TN, n))
        o_ref[:, cs] = jnp.dot(xn, w_ref[:, cs], preferred_element_type=F32)
    for c in rope_chunks:
        cs = slice(c * LANE, (c + 1) * LANE)
        o_ref[:, cs] = _rope_lanes(o_ref[:, cs], cos_ref[...], sin_ref[...])


def norm_proj(x, g, w_bf16, rope_chunks, cos_t, sin_t, *, tm_pref=512):
    m, d = x.shape
    n = w_bf16.shape[1]
    tm = _pick_tile(m, tm_pref)
    return pl.pallas_call(
        functools.partial(_norm_proj_kernel, rope_chunks=tuple(rope_chunks)),
        grid=(m // tm,),
        in_specs=[
            pl.BlockSpec((tm, d), lambda i: (i, 0)),
            pl.BlockSpec((1, d), lambda i: (0, 0)),
            pl.BlockSpec((d, n), lambda i: (0, 0), pipeline_mode=pl.Buffered(1)),
            pl.BlockSpec((tm, LANE), lambda i: (i, 0)),
            pl.BlockSpec((tm, LANE), lambda i: (i, 0)),
        ],
        out_specs=pl.BlockSpec((tm, n), lambda i: (i, 0)),
        out_shape=jax.ShapeDtypeStruct((m, n), F32),
        compiler_params=_cparams(("parallel",), RESIDENT_W_VMEM),
        name="norm_proj",
    )(x, g.reshape(1, d), w_bf16, cos_t, sin_t)


def _ffn_kernel(x_ref, g_ref, wg_ref, wu_ref, wd_ref, gf_ref, o_ref, *, final_norm, tf):
    x = x_ref[...]
    xn = _rms_rows(x, g_ref[...]).astype(BF16)
    y = x
    for f0 in range(0, wg_ref.shape[1], tf):
        fs = slice(f0, f0 + tf)
        a = jnp.dot(xn, wg_ref[:, fs], preferred_element_type=F32)
        u = jnp.dot(xn, wu_ref[:, fs], preferred_element_type=F32)
        act = (a * jax.nn.sigmoid(a)) * u
        y = y + jnp.dot(act.astype(BF16), wd_ref[fs, :], preferred_element_type=F32)
    o_ref[...] = _rms_rows(y, gf_ref[...]) if final_norm else y


def ffn(x, g, wg, wu, wd, g_final, *, final_norm, tm_pref=512):
    m, d = x.shape
    dff = wg.shape[1]
    tf = dff // 2 if (dff // 2) % LANE == 0 else dff
    tm = _pick_tile(m, tm_pref)
    resident = lambda a: pl.BlockSpec(a.shape, lambda i: (0, 0), pipeline_mode=pl.Buffered(1))
    return pl.pallas_call(
        functools.partial(_ffn_kernel, final_norm=final_norm, tf=tf),
        grid=(m // tm,),
        in_specs=[
            pl.BlockSpec((tm, d), lambda i: (i, 0)),
            pl.BlockSpec((1, d), lambda i: (0, 0)),
            resident(wg), resident(wu), resident(wd),
            pl.BlockSpec((1, d), lambda i: (0, 0)),
        ],
        out_specs=pl.BlockSpec((tm, d), lambda i: (i, 0)),
        out_shape=jax.ShapeDtypeStruct((m, d), F32),
        compiler_params=_cparams(("parallel",), RESIDENT_W_VMEM),
        name="ffn",
    )(x, g.reshape(1, d), wg, wu, wd, g_final.reshape(1, d))


def _out_proj_kernel(x_ref, w_ref, r_ref, o_ref):
    o_ref[...] = r_ref[...] + jnp.dot(x_ref[...].astype(BF16), w_ref[...], preferred_element_type=F32)


def out_proj(x, w_bf16, resid, *, tm_pref=512):
    m, k = x.shape
    n = w_bf16.shape[1]
    tm = _pick_tile(m, tm_pref)
    return pl.pallas_call(
        _out_proj_kernel,
        grid=(m // tm,),
        in_specs=[
            pl.BlockSpec((tm, k), lambda i: (i, 0)),
            pl.BlockSpec((k, n), lambda i: (0, 0)),
            pl.BlockSpec((tm, n), lambda i: (i, 0)),
        ],
        out_specs=pl.BlockSpec((tm, n), lambda i: (i, 0)),
        out_shape=jax.ShapeDtypeStruct((m, n), F32),
        compiler_params=_cparams(("parallel",)),
        name="out_proj",
    )(x, w_bf16, resid)


DIL_STEP_ROWS = 2048
DIL_UNROLL = 16


def _dil_prompt_kernel(q_ref, kp_ref, kc_ref, vp_ref, vc_ref, o_ref, l_ref, *, nback, dil, nblk):
    n = pl.program_id(2)
    span = dil * BLK_A
    qi = BLK_A + lax.broadcasted_iota(jnp.int32, (BLK_A, 2 * BLK_A), 0)
    ki = lax.broadcasted_iota(jnp.int32, (BLK_A, 2 * BLK_A), 1)
    rel = qi - ki
    band = (rel >= 0) & (rel <= nback)
    own = ki >= BLK_A
    head_of_lane = lax.broadcasted_iota(jnp.int32, (1, LANE), 1) // DH_A

    def one(it, carry):
        j = it // dil
        r = it % dil
        start = j * span + r
        if dil > 1:
            take = lambda st: pl.ds(st, BLK_A, stride=dil)
        else:
            take = lambda st: pl.ds(pl.multiple_of(st, BLK_A), BLK_A)
        rows = take(start)
        prev_rows = take(jnp.maximum(start - span, r))
        first_rows = take(r)
        q = q_ref[0, rows, :] * (DH_A ** -0.5)
        kprev = jnp.where(j > 0, kc_ref[0, prev_rows, :], kp_ref[0, first_rows, :])
        vprev = jnp.where(j > 0, vc_ref[0, prev_rows, :], vp_ref[0, first_rows, :])
        kk = jnp.concatenate([kprev, kc_ref[0, rows, :]], axis=0)
        vv = jnp.concatenate([vprev, vc_ref[0, rows, :]], axis=0)
        mask = band & ((n * nblk + j > 0) | own)
        nh = LANE // DH_A
        qbd = jnp.concatenate([jnp.where(head_of_lane == h, q, 0.0) for h in range(nh)], axis=0).astype(BF16)
        s = _dot_nt(qbd, kk.astype(BF16)).reshape(nh, BLK_A, 2 * BLK_A)
        s = jnp.where(mask[None], s, -jnp.inf)
        m = jnp.max(s, axis=-1, keepdims=True)
        p = jnp.exp(s - m)
        l = jnp.sum(p, axis=-1, keepdims=True)
        pv = jnp.dot(p.reshape(nh * BLK_A, 2 * BLK_A).astype(BF16), vv.astype(BF16),
                     preferred_element_type=F32).reshape(nh, BLK_A, LANE) / l
        lse = m + jnp.log(l)
        o, ls = pv[0], jnp.broadcast_to(lse[0], (BLK_A, LANE))
        for h in range(1, nh):
            o = jnp.where(head_of_lane == h, pv[h], o)
            ls = jnp.where(head_of_lane == h, lse[h], ls)
        o_ref[0, rows, :] = o
        l_ref[0, rows, :] = ls
        return carry

    lax.fori_loop(0, nblk * dil, one, 0, unroll=DIL_UNROLL)


def dil_prompt(z3, gi, window, dil):
    b, s, _ = z3.shape
    nback = window // dil
    span = dil * BLK_A
    nblk = max(1, DIL_STEP_ROWS // span)
    rows = nblk * span
    assert s % rows == 0 and (nblk * dil) % DIL_UNROLL == 0
    nh = AW // LANE
    qo, ko, vo = (AB_QA + gi * AW) // LANE, (AB_KA + gi * AW) // LANE, (AB_VA + gi * AW) // LANE
    blk = (1, rows, LANE)
    pblk = (1, span, LANE)
    in_specs = [
        pl.BlockSpec(blk, lambda bb, hp, i: (bb, i, qo + hp)),
        pl.BlockSpec(pblk, lambda bb, hp, i: (bb, jnp.maximum(i * nblk - 1, 0), ko + hp)),
        pl.BlockSpec(blk, lambda bb, hp, i: (bb, i, ko + hp)),
        pl.BlockSpec(pblk, lambda bb, hp, i: (bb, jnp.maximum(i * nblk - 1, 0), vo + hp)),
        pl.BlockSpec(blk, lambda bb, hp, i: (bb, i, vo + hp)),
    ]
    out_spec = pl.BlockSpec(blk, lambda bb, hp, i: (bb, i, hp))
    o, l = pl.pallas_call(
        functools.partial(_dil_prompt_kernel, nback=nback, dil=dil, nblk=nblk),
        grid=(b, nh, s // rows),
        in_specs=in_specs,
        out_specs=[out_spec, out_spec],
        out_shape=[jax.ShapeDtypeStruct((b, s, AW), F32)] * 2,
        compiler_params=_cparams(("parallel", "parallel", "arbitrary")),
        name="dil_prompt_%d" % gi,
    )(z3, z3, z3, z3, z3)
    return o.reshape(b * s, AW), l.reshape(b * s, AW)


def _dil_sample_kernel(c_ref, q_ref, kn_ref, vn_ref, nt_ref, o_ref, l_ref, co_ref, *, window, dil, n_buf, t_new):
    rows = H_A * T_PAD
    q = q_ref[0] * (DH_A ** -0.5)
    q4 = jnp.concatenate([q] * H_A, axis=0)
    rr = lax.broadcasted_iota(jnp.int32, (rows, AW), 0)
    ll = lax.broadcasted_iota(jnp.int32, (rows, AW), 1)
    head_sel = (rr // T_PAD) == (ll // DH_A)
    qbd = jnp.where(head_sel, q4, 0.0).astype(BF16)
    kbt = c_ref[0, 0:AW, :].astype(BF16)
    vbt = c_ref[0, AW:2 * AW, :].astype(BF16)
    kn = kn_ref[0]
    vn = vn_ref[0]
    s_buf = jnp.dot(qbd, kbt, preferred_element_type=F32)
    s_new = _dot_nt(qbd, kn.astype(BF16))
    t_b = lax.broadcasted_iota(jnp.int32, (rows, n_buf), 0) % T_PAD
    c_b = lax.broadcasted_iota(jnp.int32, (rows, n_buf), 1)
    d_b = n_buf + t_b - c_b
    m_b = ((d_b % dil) == 0) & (d_b <= window)
    t_n = lax.broadcasted_iota(jnp.int32, (rows, T_PAD), 0) % T_PAD
    u_n = lax.broadcasted_iota(jnp.int32, (rows, T_PAD), 1)
    d_n = t_n - u_n
    m_n = (d_n >= 0) & ((d_n % dil) == 0) & (d_n <= window) & (u_n < t_new)
    s_buf = jnp.where(m_b, s_buf, NEG_BIG)
    s_new = jnp.where(m_n, s_new, NEG_BIG)
    mx = jnp.maximum(jnp.max(s_buf, axis=-1, keepdims=True), jnp.max(s_new, axis=-1, keepdims=True))
    p_b = jnp.where(m_b, jnp.exp(s_buf - mx), 0.0)
    p_n = jnp.where(m_n, jnp.exp(s_new - mx), 0.0)
    l = jnp.sum(p_b, axis=-1, keepdims=True) + jnp.sum(p_n, axis=-1, keepdims=True)
    l = jnp.maximum(l, 1e-30)
    acc = (_dot_nt(p_b.astype(BF16), vbt)
           + jnp.dot(p_n.astype(BF16), vn.astype(BF16), preferred_element_type=F32))
    res = jnp.where(head_sel, acc / l, 0.0)
    lse = jnp.where(head_sel, mx + jnp.log(l), 0.0)
    o = res[0:T_PAD]
    ls = lse[0:T_PAD]
    for h in range(1, H_A):
        o = o + res[h * T_PAD:(h + 1) * T_PAD]
        ls = ls + lse[h * T_PAD:(h + 1) * T_PAD]
    o_ref[0] = o
    l_ref[0] = ls
    co_ref[0, :, 0:n_buf - t_new] = c_ref[0, :, t_new:n_buf]
    co_ref[0, :, n_buf - t_new:n_buf] = nt_ref[0, :, 0:t_new]


def dil_sample(cache_t, z3, new_t, gi, window, dil, t_new):
    db, _, n_buf = cache_t.shape
    qo, ko, vo = AB_QA // AW + gi, AB_KA // AW + gi, AB_VA // AW + gi
    blk = (1, T_PAD, AW)
    o, l, co = pl.pallas_call(
        functools.partial(_dil_sample_kernel, window=window, dil=dil, n_buf=n_buf, t_new=t_new),
        grid=(db,),
        in_specs=[
            pl.BlockSpec((1, 2 * AW, n_buf), lambda b: (b, 0, 0)),
            pl.BlockSpec(blk, lambda b: (b, 0, qo)),
            pl.BlockSpec(blk, lambda b: (b, 0, ko)),
            pl.BlockSpec(blk, lambda b: (b, 0, vo)),
            pl.BlockSpec((1, 2 * AW, T_PAD), lambda b: (b, 0, 0)),
        ],
        out_specs=[
            pl.BlockSpec(blk, lambda b: (b, 0, 0)),
            pl.BlockSpec(blk, lambda b: (b, 0, 0)),
            pl.BlockSpec((1, 2 * AW, n_buf), lambda b: (b, 0, 0)),
        ],
        out_shape=[
            jax.ShapeDtypeStruct((db, T_PAD, AW), F32),
            jax.ShapeDtypeStruct((db, T_PAD, AW), F32),
            jax.ShapeDtypeStruct((db, 2 * AW, n_buf), F32),
        ],
        compiler_params=_cparams(("parallel",)),
        name="dil_sample_%d" % gi,
    )(cache_t, z3, z3, z3, new_t)
    return o.reshape(db * T_PAD, AW), l.reshape(db * T_PAD, AW), co


def _mlstm_kernel(q_ref, k_ref, v_ref, gi_ref, gf_ref, gt_ref, bi_ref, bf_ref, b8_ref,
                  c0_ref, n0_ref, m0_ref, h_ref, co_ref, no_ref, mo_ref,
                  c_s, n_s, m_s, *, bb, L, n_valid):
    c = pl.program_id(1)

    @pl.when(c == 0)
    def _():
        c_s[...] = c0_ref[...]
        n_s[...] = n0_ref[...]
        m_s[...] = m0_ref[...]

    row = lax.broadcasted_iota(jnp.int32, (L, L), 0)
    col = lax.broadcasted_iota(jnp.int32, (L, L), 1)
    tri = row >= col
    tri_l = jnp.where(tri, 1.0, 0.0).astype(BF16)
    tri_u = jnp.where(row <= col, 1.0, 0.0).astype(BF16)
    lane = lax.broadcasted_iota(jnp.int32, (1, LANE), 1)
    for b in range(bb):
        ig_col = gi_ref[b] + bi_ref[...]
        lf_col = _log_sigmoid(gf_ref[b] + bf_ref[...])
        gt = gt_ref[b, 0]
        ig_row = gt[0:H_B] + b8_ref[0:H_B]
        lf_row = _log_sigmoid(gt[H_B:2 * H_B] + b8_ref[H_B:2 * H_B])
        if n_valid < L:
            rv = lax.broadcasted_iota(jnp.int32, (L, LANE), 0) < n_valid
            ig_col = jnp.where(rv, ig_col, -jnp.inf)
            lf_col = jnp.where(rv, lf_col, 0.0)
            cv = lax.broadcasted_iota(jnp.int32, (H_B, L), 1) < n_valid
            ig_row = jnp.where(cv, ig_row, -jnp.inf)
            lf_row = jnp.where(cv, lf_row, 0.0)
        b_col = _dot_exact_lhs01(tri_l, lf_col)
        b_row = _dot_exact_rhs01(lf_row, tri_u)
        m_row = m_s[b]
        a_col = b_col + m_row
        m_new = m_row
        hs_out = []
        for h in range(H_B):
            hsl = slice(h * DH_B, (h + 1) * DH_B)
            bc = b_col[:, h:h + 1]
            ac = a_col[:, h:h + 1]
            icol = ig_col[:, h:h + 1]
            D = bc - b_row[h:h + 1, :] + ig_row[h:h + 1, :]
            D = jnp.where(tri, D, -jnp.inf)
            mt = jnp.maximum(ac, jnp.max(D, axis=-1, keepdims=True))
            Dw = jnp.exp(D - mt)
            iw = jnp.exp(ac - mt)
            qf = q_ref[b, :, hsl]
            kf = k_ref[b, :, hsl] * (DH_B ** -0.5)
            vf = v_ref[b, :, hsl]
            qb, kb, vb = qf.astype(BF16), kf.astype(BF16), vf.astype(BF16)
            Cm = c_s[b * H_B + h]
            nv = n_s[b * H_B + h]
            sc = _dot_nt(qb, kb) * Dw
            num = iw * _dot_nt(qb, Cm.astype(BF16)) + jnp.dot(sc.astype(BF16), vb, preferred_element_type=F32)
            den = iw * jnp.sum(qf * nv, axis=-1, keepdims=True) + jnp.sum(sc, axis=-1, keepdims=True)
            hs_out.append(num / jnp.maximum(jnp.abs(den), jnp.exp(-mt)))
            mL = mt[L - 1:L, :]
            wL = jnp.exp(bc[L - 1:L, :] - bc + icol - mL)
            dec = jnp.exp(ac[L - 1:L, :] - mL)
            c_s[b * H_B + h] = dec * Cm + _dot_tn((vf * wL).astype(BF16), kb)
            n_s[b * H_B + h] = dec * nv + jnp.sum(wL * kf, axis=0, keepdims=True)
            m_new = jnp.where(lane == h, mL, m_new)
        m_s[b] = m_new
        h_ref[b] = jnp.concatenate(hs_out, axis=-1)

    @pl.when(c == pl.num_programs(1) - 1)
    def _():
        co_ref[...] = c_s[...]
        no_ref[...] = n_s[...]
        mo_ref[...] = m_s[...]


def mlstm(z3, gt, b_if, c0, n0, m0, *, L, n_valid, bb):
    b, s, _ = z3.shape
    nc = s // L
    bi_row = jnp.zeros((1, LANE), F32).at[0, :H_B].set(b_if[0])
    bf_row = jnp.zeros((1, LANE), F32).at[0, :H_B].set(b_if[1])
    b8 = b_if.reshape(2 * H_B, 1)
    c0r = c0.reshape(b * H_B, DH_B, DH_B)
    n0r = n0.reshape(b * H_B, 1, DH_B)
    m0r = jnp.zeros((b, 1, LANE), F32).at[:, 0, :H_B].set(m0)
    cw = B_W // LANE
    h, co, no, mo = pl.pallas_call(
        functools.partial(_mlstm_kernel, bb=bb, L=L, n_valid=n_valid),
        grid=(b // bb, nc),
        in_specs=[
            pl.BlockSpec((bb, L, B_W), lambda g, c: (g, c, AB_QB // B_W)),
            pl.BlockSpec((bb, L, B_W), lambda g, c: (g, c, AB_KB // B_W)),
            pl.BlockSpec((bb, L, B_W), lambda g, c: (g, c, AB_VB // B_W)),
            pl.BlockSpec((bb, L, LANE), lambda g, c: (g, c, AB_IG // LANE)),
            pl.BlockSpec((bb, L, LANE), lambda g, c: (g, c, AB_FG // LANE)),
            pl.BlockSpec((bb, 1, 2 * H_B, L), lambda g, c: (g, c, 0, 0)),
            pl.BlockSpec((1, LANE), lambda g, c: (0, 0)),
            pl.BlockSpec((1, LANE), lambda g, c: (0, 0)),
            pl.BlockSpec((2 * H_B, 1), lambda g, c: (0, 0)),
            pl.BlockSpec((bb * H_B, DH_B, DH_B), lambda g, c: (g, 0, 0)),
            pl.BlockSpec((bb * H_B, 1, DH_B), lambda g, c: (g, 0, 0)),
            pl.BlockSpec((bb, 1, LANE), lambda g, c: (g, 0, 0)),
        ],
        out_specs=[
            pl.BlockSpec((bb, L, B_W), lambda g, c: (g, c, 0)),
            pl.BlockSpec((bb * H_B, DH_B, DH_B), lambda g, c: (g, 0, 0)),
            pl.BlockSpec((bb * H_B, 1, DH_B), lambda g, c: (g, 0, 0)),
            pl.BlockSpec((bb, 1, LANE), lambda g, c: (g, 0, 0)),
        ],
        out_shape=[
            jax.ShapeDtypeStruct((b, s, B_W), F32),
            jax.ShapeDtypeStruct((b * H_B, DH_B, DH_B), F32),
            jax.ShapeDtypeStruct((b * H_B, 1, DH_B), F32),
            jax.ShapeDtypeStruct((b, 1, LANE), F32),
        ],
        scratch_shapes=[
            pltpu.VMEM((bb * H_B, DH_B, DH_B), F32),
            pltpu.VMEM((bb * H_B, 1, DH_B), F32),
            pltpu.VMEM((bb, 1, LANE), F32),
        ],
        compiler_params=_cparams(("parallel", "arbitrary")),
        name="mlstm_L%d" % L,
    )(z3, z3, z3, z3, z3, gt, bi_row, bf_row, b8, c0r, n0r, m0r)
    del cw
    return (h.reshape(b * s, B_W), co.reshape(b, H_B, DH_B, DH_B), no.reshape(b, H_B, DH_B),
            mo[:, 0, :H_B])


def _ab_merge_kernel(o0, o1, o2, l0, l1, l2, hb_ref, og_ref, g_ref, w_ref, r_ref, out_ref):
    a0, a1, a2 = l0[...], l1[...], l2[...]
    mx = jnp.maximum(jnp.maximum(a0, a1), a2)
    e0, e1, e2 = jnp.exp(a0 - mx), jnp.exp(a1 - mx), jnp.exp(a2 - mx)
    o_a = (e0 * o0[...] + e1 * o1[...] + e2 * o2[...]) / (e0 + e1 + e2)
    hb = hb_ref[...]
    parts = []
    for h in range(H_B):
        hs = slice(h * DH_B, (h + 1) * DH_B)
        x = hb[:, hs]
        parts.append(x * lax.rsqrt(jnp.mean(x * x, axis=-1, keepdims=True) + EPS))
    hbn = jnp.concatenate(parts, axis=-1) * g_ref[...] * jax.nn.sigmoid(og_ref[...])
    y = (jnp.dot(o_a.astype(BF16), w_ref[0:AW, :], preferred_element_type=F32)
         + jnp.dot(hbn.astype(BF16), w_ref[AW:AW + B_W, :], preferred_element_type=F32))
    out_ref[...] = r_ref[...] + y


def ab_merge(os_, ls_, hb, z2, g_mn, w_bf16, resid, *, tm_pref=512):
    m, d = resid.shape
    tm = _pick_tile(m, tm_pref)
    a_spec = pl.BlockSpec((tm, AW), lambda i: (i, 0))
    return pl.pallas_call(
        _ab_merge_kernel,
        grid=(m // tm,),
        in_specs=[a_spec] * 6 + [
            pl.BlockSpec((tm, B_W), lambda i: (i, 0)),
            pl.BlockSpec((tm, B_W), lambda i: (i, AB_OG // B_W)),
            pl.BlockSpec((1, B_W), lambda i: (0, 0)),
            pl.BlockSpec((AW + B_W, d), lambda i: (0, 0)),
            pl.BlockSpec((tm, d), lambda i: (i, 0)),
        ],
        out_specs=pl.BlockSpec((tm, d), lambda i: (i, 0)),
        out_shape=jax.ShapeDtypeStruct((m, d), F32),
        compiler_params=_cparams(("parallel",)),
        name="ab_merge",
    )(*os_, *ls_, hb, z2, g_mn.reshape(1, B_W), w_bf16, resid)


HKV = KVH_C * DH_C


def _compress_bias_kernel(pa_ref, pb_ref, w_ref, o_ref):
    w = w_ref[0]
    o_ref[0] = (jnp.dot(pa_ref[0].astype(BF16), w, preferred_element_type=F32)[:, 0:HKV]
                + jnp.dot(pb_ref[0].astype(BF16), w, preferred_element_type=F32)[:, HKV:2 * HKV])


def compress_bias(pa, pb, wab2):
    spec = lambda a: pl.BlockSpec((1,) + a.shape[1:], lambda i: (i,) + (0,) * (a.ndim - 1))
    return pl.pallas_call(
        _compress_bias_kernel,
        grid=(2,),
        in_specs=[spec(pa), spec(pb), spec(wab2)],
        out_specs=pl.BlockSpec((1, 1, HKV), lambda i: (i, 0, 0)),
        out_shape=jax.ShapeDtypeStruct((2, 1, HKV), F32),
        compiler_params=_cparams(("arbitrary",)),
        name="compress_bias",
    )(pa, pb, wab2)


def _compress_halves(load_rows, n, w_all):
    x = jnp.concatenate([load_rows(r, n).astype(BF16) for r in range(CMP_STRIDE)], axis=1)
    return jnp.dot(x, w_all, preferred_element_type=F32)


def _compress_prompt_kernel(x_ref, wab_ref, bias_ref, w2_ref, o_ref):
    n = x_ref.shape[1] // CMP_STRIDE
    acc = _compress_halves(lambda r, m: x_ref[0, pl.ds(r, m, stride=CMP_STRIDE), :], n, wab_ref[0])
    hid = acc[:, 0:HKV] + pltpu.roll(acc[:, HKV:2 * HKV], n - 1, 0) + bias_ref[0]
    o_ref[0] = jnp.dot(_gelu_tanh(hid).astype(BF16), w2_ref[0], preferred_element_type=F32)


def compress_prompt(z3, cw, bias):
    b, s, _ = z3.shape
    wab, w2 = cw
    n = s // CMP_STRIDE
    return pl.pallas_call(
        _compress_prompt_kernel,
        grid=(b, 2),
        in_specs=[
            pl.BlockSpec((1, s, HKV), lambda i, kv: (i, 0, C_CMP // HKV + kv)),
            pl.BlockSpec((1,) + wab.shape[1:], lambda i, kv: (kv, 0, 0)),
            pl.BlockSpec((1, 1, HKV), lambda i, kv: (kv, 0, 0)),
            pl.BlockSpec((1, HKV, HKV), lambda i, kv: (kv, 0, 0)),
        ],
        out_specs=pl.BlockSpec((1, n, HKV), lambda i, kv: (i, 0, kv)),
        out_shape=jax.ShapeDtypeStruct((b, n, KVW), F32),
        compiler_params=_cparams(("parallel", "arbitrary")),
        name="compress_prompt",
    )(z3, wab, bias, w2)


def _compress_paged_kernel(pt_ref, *refs, pg):
    del pt_ref
    page_refs = refs[:pg + 1]
    wab_ref, bias_ref, w2_ref, o_ref, xs_ref = refs[pg + 1:]
    for j in range(pg + 1):
        t = page_refs[j][0].T
        xs_ref[0, j * PAGE_SIZE:(j + 1) * PAGE_SIZE, :] = t[:, 0:HKV]
        xs_ref[1, j * PAGE_SIZE:(j + 1) * PAGE_SIZE, :] = t[:, HKV:2 * HKV]
    cpp = PAGE_SIZE // CMP_STRIDE
    n = pg * cpp
    for kv in range(2):
        acc = _compress_halves(lambda r, m: xs_ref[kv, pl.ds(r, m, stride=CMP_STRIDE), :], n + cpp, wab_ref[kv])
        hid = acc[0:n, 0:HKV] + acc[1:n + 1, HKV:2 * HKV] + bias_ref[kv]
        o_ref[0, :, kv * HKV:(kv + 1) * HKV] = jnp.dot(_gelu_tanh(hid).astype(BF16), w2_ref[kv],
                                                        preferred_element_type=F32)


def compress_paged(pool_t, page_table, cw, bias, *, pg=32):
    db, n_pages = page_table.shape
    cpp = PAGE_SIZE // CMP_STRIDE
    pg = min(pg, n_pages)
    assert n_pages % pg == 0
    wab, w2 = cw
    full = lambda a: pl.BlockSpec(a.shape, lambda b, g, pt: (0,) * a.ndim)

    def page_spec(j):
        return pl.BlockSpec((1, KVW, PAGE_SIZE),
                            lambda b, g, pt: (pt[b, jnp.minimum(g * pg + j, n_pages - 1)], 0, 0))

    gs = pltpu.PrefetchScalarGridSpec(
        num_scalar_prefetch=1,
        grid=(db, n_pages // pg),
        in_specs=[page_spec(j) for j in range(pg + 1)] + [full(wab), full(bias), full(w2)],
        out_specs=pl.BlockSpec((1, pg * cpp, KVW), lambda b, g, pt: (b, g, 0)),
        scratch_shapes=[pltpu.VMEM((2, (pg + 1) * PAGE_SIZE, HKV), F32)],
    )
    return pl.pallas_call(
        functools.partial(_compress_paged_kernel, pg=pg),
        grid_spec=gs,
        out_shape=jax.ShapeDtypeStruct((db, n_pages * cpp, KVW), F32),
        compiler_params=_cparams(("parallel", "arbitrary")),
        name="compress_paged",
    )(page_table, *([pool_t] * (pg + 1)), wab, bias, w2)


def _stack_heads(x, t):
    del t
    return jnp.concatenate([x[:, g * DH_C:(g + 1) * DH_C] for g in range(G_C)], axis=0)


def _mask_rows(s, mask, fill):
    t, n = mask.shape
    return jnp.where(mask[None], s.reshape(G_C, t, n), fill).reshape(G_C * t, n)


def _cmp_branch(q2, kc, vc, t_pos, tq):
    n = kc.shape[0]
    s = _dot_nt(q2, kc)
    cend = lax.broadcasted_iota(jnp.int32, (tq, n), 1) * CMP_STRIDE + (CMP_BLK - 1)
    s = _mask_rows(s, cend <= t_pos, -jnp.inf)
    m = jnp.max(s, axis=-1, keepdims=True)
    m = jnp.where(m > -jnp.inf, m, 0.0)
    p = jnp.exp(s - m)
    p = p / jnp.maximum(jnp.sum(p, axis=-1, keepdims=True), 1e-30)
    o = jnp.dot(p.astype(BF16), vc, preferred_element_type=F32)
    pg = p[0:tq]
    for g in range(1, G_C):
        pg = pg + p[g * tq:(g + 1) * tq]
    return o, pg


def _select_blocks(imp, t_pos):
    tq, nsp = imp.shape
    j = lax.broadcasted_iota(jnp.int32, (tq, nsp), 1)
    jf = j.astype(F32)
    cur = t_pos // SLC_BLK
    forced = (j == 0) | (j == cur) | (j == cur - 1)
    work = jnp.where(forced, jnp.inf, jnp.where(j <= cur, imp, -jnp.inf))
    sel = jnp.zeros((tq, nsp), F32)
    for _ in range(N_SEL):
        mx = jnp.max(work, axis=-1, keepdims=True)
        first = jnp.min(jnp.where(work == mx, jf, float(nsp)), axis=-1, keepdims=True)
        pick = jf == first
        sel = jnp.where(pick, jnp.where(mx > -jnp.inf, 1.0, sel), sel)
        work = jnp.where(pick, -jnp.inf, work)
    return sel


def _select_blocks_t(imp_t, t_row):
    nsp, n = imp_t.shape
    j = lax.broadcasted_iota(jnp.int32, (nsp, n), 0)
    jf = j.astype(F32)
    cur = t_row // SLC_BLK
    forced = (j == 0) | (j == cur) | (j == cur - 1)
    work = jnp.where(forced, jnp.inf, jnp.where(j <= cur, imp_t, -jnp.inf))
    sel = jnp.zeros((nsp, n), F32)
    for _ in range(N_SEL):
        mx = jnp.max(work, axis=0, keepdims=True)
        first = jnp.min(jnp.where(work == mx, jf, float(nsp)), axis=0, keepdims=True)
        pick = jf == first
        sel = jnp.where(pick, jnp.where(mx > -jnp.inf, 1.0, sel), sel)
        work = jnp.where(pick, -jnp.inf, work)
    return sel


def _flash_update(s, mask, v, m_ref, l_ref, acc_ref, v_channel_major=False):
    s = _mask_rows(s, mask, NEG_BIG)
    m_old = m_ref[...]
    m_new = jnp.maximum(m_old, jnp.max(s, axis=-1, keepdims=True))
    alpha = jnp.exp(m_old - m_new)
    p = jnp.exp(s - m_new)
    l_ref[...] = alpha * l_ref[...] + jnp.sum(p, axis=-1, keepdims=True)
    pv = _dot_nt(p.astype(BF16), v) if v_channel_major else jnp.dot(p.astype(BF16), v, preferred_element_type=F32)
    acc_ref[...] = alpha * acc_ref[...] + pv
    m_ref[...] = m_new


def _flash_finish(m_ref, l_ref, acc_ref):
    return jnp.where(m_ref[...] > 0.5 * NEG_BIG, acc_ref[...] / jnp.maximum(l_ref[...], 1e-30), 0.0)


def _softmax_av(s, mask, v):
    s = _mask_rows(s, mask, -jnp.inf)
    m = jnp.max(s, axis=-1, keepdims=True)
    m = jnp.where(m > -jnp.inf, m, 0.0)
    p = jnp.exp(s - m)
    p = p / jnp.maximum(jnp.sum(p, axis=-1, keepdims=True), 1e-30)
    return jnp.dot(p.astype(BF16), v, preferred_element_type=F32)


def _gate_mix(gates, h, o_cmp, o_sel, o_win, tq):
    outs = []
    for g in range(G_C):
        base = (h * G_C + g) * 3
        rs = slice(g * tq, (g + 1) * tq)
        outs.append(gates[:, base:base + 1] * o_cmp[rs] + gates[:, base + 1:base + 2] * o_sel[rs]
                    + gates[:, base + 2:base + 3] * o_win[rs])
    return jnp.concatenate(outs, axis=-1)


def _rope_q(qh, cos, sin):
    return jnp.concatenate(
        [_rope_lanes(qh[:, c * LANE:(c + 1) * LANE], cos, sin) for c in range(G_C * DH_C // LANE)], axis=-1)


SEL_TK = 1024


SEL_PHASE = DH_C


def _nsa_prompt_kernel(q_ref, gz_ref, cos_ref, sin_ref, kc_ref, kv_ref, ka_ref, va_ref, mimp_ref, o_ref,
                       qa_s, m_s, acc_s, *, nsp):
    i = pl.program_id(1)
    tq = QBLK_C
    rows = G_C * tq
    s0 = i * tq
    t_pos = s0 + lax.broadcasted_iota(jnp.int32, (tq, 1), 0)
    gates = jax.nn.sigmoid(gz_ref[0])
    cos, sin = cos_ref[...], sin_ref[...]
    scale = DH_C ** -0.5
    hw = G_C * DH_C
    ncmp = kc_ref.shape[1]
    t_row = s0 + lax.broadcasted_iota(jnp.int32, (1, rows), 1) % tq
    visible = (lax.broadcasted_iota(jnp.int32, (ncmp, 1), 0) * CMP_STRIDE + (CMP_BLK - 1)) <= t_row
    any_visible = t_row >= CMP_BLK - 1
    zpad = jnp.zeros((tq, DH_C), F32)
    kcc = kc_ref[0, :, 0:HKV].astype(BF16)
    vcc = kc_ref[0, :, HKV:2 * HKV].astype(BF16)
    q_rot, o_cmps, imps_t = [], [], []
    for h in range(KVH_C):
        qh = q_ref[0, :, h * hw:(h + 1) * hw] * scale
        q_rot.append(_rope_q(qh, cos, sin))
        qc = jnp.concatenate([jnp.concatenate([x, zpad] if h == 0 else [zpad, x], axis=1)
                              for x in (qh[:, g * DH_C:(g + 1) * DH_C] for g in range(G_C))], axis=0)
        st = jnp.where(visible, _dot_nt(kcc, qc.astype(BF16)), NEG_BIG)
        pt = jnp.exp(st - jnp.max(st, axis=0, keepdims=True))
        inv = jnp.where(any_visible, 1.0 / jnp.maximum(jnp.sum(pt, axis=0, keepdims=True), 1e-30), 0.0)
        pt = pt * inv
        o_cmps.append(_dot_tn(vcc, pt.astype(BF16))[h * DH_C:(h + 1) * DH_C, :].T)
        pgrp_t = pt[:, 0:tq]
        for g in range(1, G_C):
            pgrp_t = pgrp_t + pt[:, g * tq:(g + 1) * tq]
        imps_t.append(_dot_exact_lhs01(mimp_ref[...], pgrp_t))
    t_row2 = s0 + lax.broadcasted_iota(jnp.int32, (1, KVH_C * tq), 1) % tq
    sel_t = _select_blocks_t(jnp.concatenate(imps_t, axis=1), t_row2)

    half_of_lane = lax.broadcasted_iota(jnp.int32, (1, LANE), 1) // DH_C

    def into_half(chunk, src_half, dst_half):
        moved = chunk if src_half == dst_half else pltpu.roll(chunk, DH_C, 1)
        return jnp.where(half_of_lane == dst_half, moved, 0.0)

    for h in range(KVH_C):
        selb = jnp.where(sel_t[:, h * tq:(h + 1) * tq].T > 0.5, 0.0, NEG_BIG)
        q_part = jnp.concatenate(
            [into_half(q_rot[h][:, (g // 2) * LANE:(g // 2 + 1) * LANE], g % 2, h) for g in range(G_C)], axis=0)
        for ph in range(nsp // SEL_PHASE):
            sb = into_half(selb[:, (ph // 2) * LANE:(ph // 2 + 1) * LANE], ph % 2, 1 - h)
            qa_s[h, ph] = (q_part + jnp.concatenate([sb] * G_C, axis=0)).astype(BF16)
    m_s[...] = jnp.full(m_s.shape, NEG_BIG, F32)
    acc_s[...] = jnp.zeros(acc_s.shape, F32)

    def tile(k0, size, causal):
        k0 = pl.multiple_of(k0, size)
        ph = k0 // (SEL_PHASE * SLC_BLK)
        for h in range(KVH_C):
            st = _dot_nt(ka_ref[0, h, pl.ds(k0, size), :], qa_s[h, ph])
            if causal:
                kpos = k0 + lax.broadcasted_iota(jnp.int32, (size, 1), 0)
                t_row = s0 + lax.broadcasted_iota(jnp.int32, (1, rows), 1) % tq
                st = jnp.where(kpos <= t_row, st, NEG_BIG)
            m_old = m_s[h]
            m_new = jnp.maximum(m_old, jnp.max(st, axis=0, keepdims=True))
            pt = jnp.exp(st - m_new).astype(BF16)
            acc_s[h] = (jnp.exp(m_old - m_new) * acc_s[h]
                        + jnp.dot(va_ref[0, h, :, pl.ds(k0, size)], pt, preferred_element_type=F32))
            m_s[h] = m_new

    def full_tile(kt, carry):
        tile(kt * SEL_TK, SEL_TK, False)
        return carry

    half = SEL_TK // 2
    own = (s0 + tq - 1) // half
    lax.fori_loop(0, own // 2, full_tile, 0)

    @pl.when(own % 2 == 1)
    def _():
        tile((own - 1) * half, half, False)

    tile(own * half, half, True)

    def finish(acc, h):
        sums = acc[(1 - h) * DH_C:(1 - h) * DH_C + 1, :]
        return (acc[h * DH_C:(h + 1) * DH_C, :] / jnp.maximum(sums, 1e-30)).T

    nw = WIN_C + tq
    w0 = pl.multiple_of(jnp.maximum(s0 - WIN_C, 0), tq)
    dlt = (s0 + lax.broadcasted_iota(jnp.int32, (1, rows), 1) % tq) - (w0 + lax.broadcasted_iota(jnp.int32, (nw, 1), 0))
    in_win = (dlt >= 0) & (dlt < WIN_C)
    lane = lax.broadcasted_iota(jnp.int32, (1, LANE), 1) // DH_C
    kwc = kv_ref[0, pl.ds(w0, nw), 0:LANE]
    vwc = kv_ref[0, pl.ds(w0, nw), LANE:2 * LANE]
    for h in range(KVH_C):
        st = _dot_nt(jnp.where(lane == h, kwc, jnp.zeros_like(kwc)), qa_s[h, 0])
        st = jnp.where(in_win, st, NEG_BIG)
        pt = jnp.exp(st - jnp.max(st, axis=0, keepdims=True)).astype(BF16)
        o_win = finish(_dot_tn(jnp.where(lane == h, vwc, jnp.ones_like(vwc)), pt), h)
        o_ref[0, :, h * hw:(h + 1) * hw] = _gate_mix(gates, h, o_cmps[h], finish(acc_s[h], h), o_win, tq)


def _cols_t_kernel(x_ref, o_ref):
    o_ref[0] = x_ref[0].T.astype(BF16)


def cols_channel_major(z3, col0, *, ts=1024):
    b, s, _ = z3.shape
    ts = _pick_tile(s, ts)
    return pl.pallas_call(
        _cols_t_kernel,
        grid=(b, s // ts),
        in_specs=[pl.BlockSpec((1, ts, LANE), lambda bb, i: (bb, i, col0 // LANE))],
        out_specs=pl.BlockSpec((1, LANE, ts), lambda bb, i: (bb, 0, i)),
        out_shape=jax.ShapeDtypeStruct((b, LANE, s), BF16),
        compiler_params=_cparams(("parallel", "parallel")),
        name="cols_channel_major",
    )(z3)


def _cols_cast_kernel(x_ref, o_ref):
    o_ref[...] = x_ref[...].astype(BF16)


def cols_bf16(z3, col0, width, *, ts=1024):
    b, s, _ = z3.shape
    ts = _pick_tile(s, ts)
    return pl.pallas_call(
        _cols_cast_kernel,
        grid=(b, s // ts),
        in_specs=[pl.BlockSpec((1, ts, width), lambda bb, i: (bb, i, col0 // width))],
        out_specs=pl.BlockSpec((1, ts, width), lambda bb, i: (bb, i, 0)),
        out_shape=jax.ShapeDtypeStruct((b, s, width), BF16),
        compiler_params=_cparams(("parallel", "parallel")),
        name="cols_bf16",
    )(z3)


def _sel_operands(z3):
    s = z3.shape[1]
    lane = jnp.arange(LANE)
    hot = ((jnp.arange(s)[:, None] // SLC_BLK) % SEL_PHASE == lane[None, :] % SEL_PHASE).astype(BF16)
    ksel = cols_bf16(z3, C_SEL, HKV)
    vsel_t = cols_channel_major(z3, C_SEL + HKV)
    ka = jnp.stack([jnp.where((lane // DH_C == h)[None, None, :], ksel, hot[None]) for h in range(KVH_C)], axis=1)
    va = jnp.stack([jnp.where((lane // DH_C == h)[None, :, None], vsel_t, jnp.ones_like(vsel_t))
                    for h in range(KVH_C)], axis=1)
    return ka, va


def nsa_prompt(z3, kc, kvw, ka, va, cos_t, sin_t, mimp):
    b, s, _ = z3.shape
    nsp = mimp.shape[0]
    nq = s // QBLK_C
    ncmp = kc.shape[1]
    rows = G_C * QBLK_C
    vm = pltpu.VMEM
    scratch = [
        vm((KVH_C, nsp // SEL_PHASE, rows, LANE), BF16),
        vm((KVH_C, 1, rows), F32),
        vm((KVH_C, LANE, rows), F32),
    ]
    return pl.pallas_call(
        functools.partial(_nsa_prompt_kernel, nsp=nsp),
        grid=(b, nq),
        in_specs=[
            pl.BlockSpec((1, QBLK_C, C_Q), lambda bb, i: (bb, i, 0)),
            pl.BlockSpec((1, QBLK_C, LANE), lambda bb, i: (bb, i, C_GATE // LANE)),
            pl.BlockSpec((QBLK_C, LANE), lambda bb, i: (i, 0)),
            pl.BlockSpec((QBLK_C, LANE), lambda bb, i: (i, 0)),
            pl.BlockSpec((1, ncmp, KVW), lambda bb, i: (bb, 0, 0)),
            pl.BlockSpec((1, s, KVW), lambda bb, i: (bb, 0, 0)),
            pl.BlockSpec((1, KVH_C, s, LANE), lambda bb, i: (bb, 0, 0, 0)),
            pl.BlockSpec((1, KVH_C, LANE, s), lambda bb, i: (bb, 0, 0, 0)),
            pl.BlockSpec(mimp.shape, lambda bb, i: (0, 0)),
        ],
        out_specs=pl.BlockSpec((1, QBLK_C, C_Q), lambda bb, i: (bb, i, 0)),
        out_shape=jax.ShapeDtypeStruct((b, s, C_Q), F32),
        scratch_shapes=scratch,
        compiler_params=_cparams(("parallel", "arbitrary")),
        name="nsa_prompt",
    )(z3, z3, cos_t, sin_t, kc, kvw, ka, va, mimp)


def _nsa_sample_kernel(pt_ref, *refs, pg, nsp, past, t_new):
    del pt_ref
    page_refs = refs[:pg]
    (q_ref, gz_ref, cos_ref, sin_ref, kc_ref, mimp_ref, hot_ref, seln_ref, winb_ref, winn_ref, winnt_ref,
     o_ref, wo_ref, q2r_s, qa_s, sel_s, ocmp_s, m_s, l_s, acc_s) = refs[pg:]
    g = pl.program_id(1)
    ng = pl.num_programs(1)
    tq = T_PAD
    rows = G_C * tq
    t_pos = past + lax.broadcasted_iota(jnp.int32, (tq, 1), 0)
    scale = DH_C ** -0.5
    hw = G_C * DH_C
    nk = pg * PAGE_SIZE
    gblk = nk // SLC_BLK

    @pl.when(g == 0)
    def _():
        cos, sin = cos_ref[...], sin_ref[...]
        q2rs, imps = [], []
        for h in range(KVH_C):
            qh = q_ref[0, :, h * hw:(h + 1) * hw] * scale
            q2 = _stack_heads(qh, tq).astype(BF16)
            q2rs.append(_stack_heads(_rope_q(qh, cos, sin), tq))
            q2r_s[h] = q2rs[h].astype(BF16)
            kc = kc_ref[0, :, h * DH_C:(h + 1) * DH_C].astype(BF16)
            vc = kc_ref[0, :, KVH_C * DH_C + h * DH_C:KVH_C * DH_C + (h + 1) * DH_C].astype(BF16)
            o_cmp, pgrp = _cmp_branch(q2, kc, vc, t_pos, tq)
            ocmp_s[h] = o_cmp
            imps.append(_dot_exact_rhs01(pgrp, mimp_ref[...]))
        sel_all = _select_blocks(jnp.concatenate(imps, axis=0), jnp.concatenate([t_pos] * KVH_C, axis=0))
        zpad = [jnp.zeros((rows, DH_C - gblk), F32)] if gblk < DH_C else []
        for h in range(KVH_C):
            sel = sel_all[h * tq:(h + 1) * tq]
            sel_s[h] = sel
            selb = jnp.where(sel > 0.5, 0.0, NEG_BIG)
            for gg in range(qa_s.shape[1]):
                sb = jnp.concatenate([selb[:, gg * gblk:(gg + 1) * gblk]] * G_C, axis=0)
                qa_s[h, gg] = jnp.concatenate([q2rs[h], sb] + zpad, axis=1).astype(BF16)
        m_s[...] = jnp.full(m_s.shape, NEG_BIG, F32)
        l_s[...] = jnp.zeros(l_s.shape, F32)
        acc_s[...] = jnp.zeros(acc_s.shape, F32)

    kvt = jnp.concatenate([r[0] for r in page_refs], axis=1).astype(BF16)
    hot = hot_ref[...]
    for h in range(KVH_C):
        kst = jnp.concatenate([kvt[h * DH_C:(h + 1) * DH_C, :], hot], axis=0)
        vst = kvt[KVH_C * DH_C + h * DH_C:KVH_C * DH_C + (h + 1) * DH_C, :]
        s = jnp.dot(qa_s[h, g], kst, preferred_element_type=F32)
        m_old = m_s[h]
        m_new = jnp.maximum(m_old, jnp.max(s, axis=-1, keepdims=True))
        alpha = jnp.exp(m_old - m_new)
        p = jnp.exp(s - m_new)
        l_s[h] = alpha * l_s[h] + jnp.sum(p, axis=-1, keepdims=True)
        acc_s[h] = alpha * acc_s[h] + _dot_nt(p.astype(BF16), vst)
        m_s[h] = m_new

    @pl.when(g == ng - 1)
    def _():
        gates = jax.nn.sigmoid(gz_ref[0])
        n_buf = winb_ref.shape[2]
        lane_n = lax.broadcasted_iota(jnp.int32, (tq, nsp), 1)
        u = lax.broadcasted_iota(jnp.int32, (tq, tq), 1)
        seln = seln_ref[0]
        winn = winn_ref[0]
        for h in range(KVH_C):
            kcol = h * DH_C
            vcol = KVH_C * DH_C + h * DH_C
            q2r = q2r_s[h]
            new_sel = jnp.max(jnp.where(lane_n == past // SLC_BLK, sel_s[h], 0.0), axis=-1, keepdims=True) > 0.5
            mk = new_sel & (past + u <= t_pos) & (u < t_new)
            _flash_update(_dot_nt(q2r, seln[:, kcol:kcol + DH_C].astype(BF16)), mk,
                          seln[:, vcol:vcol + DH_C].astype(BF16), m_s.at[h], l_s.at[h], acc_s.at[h])
            o_sel = _flash_finish(m_s.at[h], l_s.at[h], acc_s.at[h])
            kbt = winb_ref[0, kcol:kcol + DH_C, :].astype(BF16)
            vbt = winb_ref[0, vcol:vcol + DH_C, :].astype(BF16)
            pos_b = past - n_buf + lax.broadcasted_iota(jnp.int32, (tq, n_buf), 1)
            d_b = t_pos - pos_b
            m_b = (pos_b >= 0) & (d_b >= 0) & (d_b < WIN_C)
            d_n = t_pos - (past + u)
            m_n = (d_n >= 0) & (d_n < WIN_C) & (u < t_new)
            s_b = _mask_rows(jnp.dot(q2r, kbt, preferred_element_type=F32), m_b, -jnp.inf)
            s_n = _mask_rows(_dot_nt(q2r, winn[:, kcol:kcol + DH_C].astype(BF16)), m_n, -jnp.inf)
            mx = jnp.maximum(jnp.max(s_b, axis=-1, keepdims=True), jnp.max(s_n, axis=-1, keepdims=True))
            mx = jnp.where(mx > -jnp.inf, mx, 0.0)
            p_b, p_n = jnp.exp(s_b - mx), jnp.exp(s_n - mx)
            den = jnp.maximum(jnp.sum(p_b, axis=-1, keepdims=True) + jnp.sum(p_n, axis=-1, keepdims=True), 1e-30)
            o_win = (_dot_nt((p_b / den).astype(BF16), vbt)
                     + jnp.dot((p_n / den).astype(BF16), winn[:, vcol:vcol + DH_C].astype(BF16),
                               preferred_element_type=F32))
            o_ref[0, :, h * hw:(h + 1) * hw] = _gate_mix(gates, h, ocmp_s[h], o_sel, o_win, tq)
        wo_ref[0, :, 0:n_buf - t_new] = winb_ref[0, :, t_new:n_buf]
        wo_ref[0, :, n_buf - t_new:n_buf] = winnt_ref[0, :, 0:t_new]


def nsa_sample(z3, kc, sel_pool_t, win_buf_t, win_new_t, page_table, cos_t, sin_t, mimp, *, t_new, pg=32):
    db, n_pages = page_table.shape
    past = n_pages * PAGE_SIZE
    ncmp = kc.shape[1]
    nsp = mimp.shape[1]
    n_buf = win_buf_t.shape[2]
    pg = min(pg, n_pages)
    nk = pg * PAGE_SIZE
    assert n_pages % pg == 0 and nk // SLC_BLK <= DH_C
    rows = G_C * T_PAD
    hot = jnp.asarray(np.arange(nk)[None, :] // SLC_BLK == np.arange(DH_C)[:, None], dtype=BF16)

    def page_spec(j):
        return pl.BlockSpec((1, KVW, PAGE_SIZE), lambda b, g, pt: (pt[b, g * pg + j], 0, 0))

    gs = pltpu.PrefetchScalarGridSpec(
        num_scalar_prefetch=1,
        grid=(db, n_pages // pg),
        in_specs=[page_spec(j) for j in range(pg)] + [
            pl.BlockSpec((1, T_PAD, C_Q), lambda b, g, pt: (b, 0, 0)),
            pl.BlockSpec((1, T_PAD, LANE), lambda b, g, pt: (b, 0, C_GATE // LANE)),
            pl.BlockSpec((T_PAD, LANE), lambda b, g, pt: (0, 0)),
            pl.BlockSpec((T_PAD, LANE), lambda b, g, pt: (0, 0)),
            pl.BlockSpec((1, ncmp, KVW), lambda b, g, pt: (b, 0, 0)),
            pl.BlockSpec(mimp.shape, lambda b, g, pt: (0, 0)),
            pl.BlockSpec((DH_C, nk), lambda b, g, pt: (0, 0)),
            pl.BlockSpec((1, T_PAD, KVW), lambda b, g, pt: (b, 0, C_SEL // KVW)),
            pl.BlockSpec((1, KVW, n_buf), lambda b, g, pt: (b, 0, 0)),
            pl.BlockSpec((1, T_PAD, KVW), lambda b, g, pt: (b, 0, C_WIN // KVW)),
            pl.BlockSpec((1, KVW, T_PAD), lambda b, g, pt: (b, 0, 0)),
        ],
        out_specs=[
            pl.BlockSpec((1, T_PAD, C_Q), lambda b, g, pt: (b, 0, 0)),
            pl.BlockSpec((1, KVW, n_buf), lambda b, g, pt: (b, 0, 0)),
        ],
        scratch_shapes=[
            pltpu.VMEM((KVH_C, rows, DH_C), BF16),
            pltpu.VMEM((KVH_C, n_pages // pg, rows, LANE), BF16),
            pltpu.VMEM((KVH_C, T_PAD, nsp), F32),
            pltpu.VMEM((KVH_C, rows, DH_C), F32),
            pltpu.VMEM((KVH_C, rows, 1), F32),
            pltpu.VMEM((KVH_C, rows, 1), F32),
            pltpu.VMEM((KVH_C, rows, DH_C), F32),
        ],
    )
    return pl.pallas_call(
        functools.partial(_nsa_sample_kernel, pg=pg, nsp=nsp, past=past, t_new=t_new),
        grid_spec=gs,
        out_shape=[jax.ShapeDtypeStruct((db, T_PAD, C_Q), F32), jax.ShapeDtypeStruct((db, KVW, n_buf), F32)],
        compiler_params=_cparams(("parallel", "arbitrary")),
        name="nsa_sample",
    )(page_table, *([sel_pool_t] * pg), z3, z3, cos_t, sin_t, kc, mimp, hot, z3, win_buf_t, z3, win_new_t)


def _rope_tables(pos):
    half = DH_C // 2
    inv = 1.0 / (ROPE_THETA ** (jnp.arange(half, dtype=F32) / half))
    ang = pos.astype(F32)[:, None] * inv[None, :]
    cos, sin = jnp.cos(ang), jnp.sin(ang)
    return jnp.tile(cos, (1, 4)), jnp.concatenate([-sin, sin, -sin, sin], axis=-1)


def _prep_w_in_ab(w):
    cuts = np.cumsum([A_QW, A_QW, A_QW, B_W, B_W, B_W, H_B, H_B])
    qa, ka, va, qb, kb, vb, ig, fg, og = jnp.split(w, cuts, axis=-1)
    padg = lambda t: jnp.pad(t, ((0, 0), (0, LANE - H_B)))
    return jnp.concatenate([qb, kb, vb, og, qa, ka, va, padg(ig), padg(fg)], axis=-1).astype(BF16)


def _ab_rope_flags():
    return tuple(range(AB_QA // LANE, AB_VA // LANE))


def _c_rope_flags():
    return (C_SEL // LANE, C_WIN // LANE)


def _prep_compress(w1, w2, pe):
    e2 = jnp.eye(KVH_C, dtype=F32)

    def half(w1h):
        return jnp.einsum('krde,hH->krhdHe', w1h, e2).reshape(2, CMP_STRIDE, HKV, HKV).astype(BF16)

    wab = jnp.concatenate([half(w1[:, :CMP_STRIDE]), half(w1[:, CMP_STRIDE:])], axis=-1)
    w2b = jnp.einsum('ked,hH->kheHd', w2, e2).reshape(2, HKV, HKV).astype(BF16)

    def pe_half(p):
        return jnp.broadcast_to(p[:, :, None, :], (2, CMP_STRIDE, KVH_C, DH_C)).reshape(2, 1, CMP_STRIDE * HKV)

    wab = wab.reshape(2, CMP_STRIDE * HKV, 2 * HKV)
    bias = compress_bias(pe_half(pe[:, :CMP_STRIDE]), pe_half(pe[:, CMP_STRIDE:]), wab)
    return (wab, w2b), bias


def _channel_major(x, lead):
    perm = tuple(range(lead)) + (lead + 1, lead + 2, lead + 3, lead)
    xt = jnp.transpose(x, perm)
    return xt.reshape(x.shape[:lead] + (x.shape[lead + 1] * x.shape[lead + 2] * x.shape[lead + 3], x.shape[lead]))


def _row_major(xt, c0, c1, c2):
    lead, _, rows = xt.shape
    return jnp.transpose(xt.reshape(lead, c0, c1, c2, rows), (0, 4, 1, 2, 3))


def _importance_matrix(n_rows, n_cmp, n_slc, n_cols):
    ratio = SLC_BLK // CMP_STRIDE
    m = np.zeros((n_rows, n_cols), np.float32)
    for jblk in range(n_slc):
        for off in range(1 - CMP_BLK // CMP_STRIDE, ratio):
            i = ratio * jblk + off
            if 0 <= i < n_cmp:
                m[i, jblk] = 1.0
    return jnp.asarray(m, dtype=BF16)


def _gates_t(z3, L):
    b, s, _ = z3.shape
    g = jnp.concatenate([z3[..., AB_IG:AB_IG + H_B], z3[..., AB_FG:AB_FG + H_B]], axis=-1)
    return g.reshape(b, s // L, L, 2 * H_B).transpose(0, 1, 3, 2)


def kernel(x_prompt, x_sample, cache_a0_kv, cache_a1_kv, cache_a2_kv, state_b_C, state_b_n, state_b_m,
           cache_c_cmp_kv, cache_c_sel_kv, cache_c_win_kv, page_table, norm_g, w_in_ab, b_if, g_mlstm,
           w_out_ab, w_in_c, cmp_w1, cmp_w2, cmp_pe, w_out_c, w_ffn_gate, w_ffn_up, w_ffn_down, norm_final):
    B, S, D = x_prompt.shape
    DB, T, _ = x_sample.shape
    depth = norm_g.shape[0]
    n_pages = page_table.shape[1]
    past = n_pages * PAGE_SIZE
    caches_a = (cache_a0_kv, cache_a1_kv, cache_a2_kv)
    assert T <= T_PAD and S % (DIL_CFG[-1][1] * BLK_A) == 0 and S >= WIN_C + QBLK_C

    hp = x_prompt.reshape(B * S, D)
    hs = jnp.pad(x_sample, ((0, 0), (0, T_PAD - T), (0, 0))).reshape(DB * T_PAD, D)

    pos_p = jnp.arange(S)
    pos_s = past + jnp.arange(T_PAD)
    cos_p1, sin_p1 = _rope_tables(pos_p)
    cos_s1, sin_s1 = _rope_tables(pos_s)
    cos_p, sin_p = jnp.tile(cos_p1, (B, 1)), jnp.tile(sin_p1, (B, 1))
    cos_s, sin_s = jnp.tile(cos_s1, (DB, 1)), jnp.tile(sin_s1, (DB, 1))

    a_p, a_s = [[], [], []], [[], [], []]
    bC_p, bC_s, bn_p, bn_s, bm_p, bm_s = [], [], [], [], [], []
    cc_p, cc_s, csl_p, csl_s, cw_p, cw_s = [], [], [], [], [], []

    for layer in range(depth):
        if layer % 2 == 0:
            e = layer // 2
            w_in = _prep_w_in_ab(w_in_ab[e])
            flags = _ab_rope_flags()
            w_out = w_out_ab[e].astype(BF16)
            z = norm_proj(hp, norm_g[layer, 0], w_in, flags, cos_p, sin_p)
            z3 = z.reshape(B, S, AB_N)
            os_, ls_ = [], []
            for gi, (win, dil) in enumerate(DIL_CFG):
                o, l = dil_prompt(z3, gi, win, dil)
                os_.append(o)
                ls_.append(l)
                nb = min(win, S)
                kk = z3[:, S - nb:, AB_KA + gi * AW:AB_KA + (gi + 1) * AW].reshape(B, nb, H_A, DH_A)
                vv = z3[:, S - nb:, AB_VA + gi * AW:AB_VA + (gi + 1) * AW].reshape(B, nb, H_A, DH_A)
                a_p[gi].append(jnp.stack([kk, vv], axis=2))
            zc = jnp.zeros
            hb, Cp, n_p, m_p = mlstm(z3, _gates_t(z3, MLSTM_CHUNK), b_if[e],
                                     zc((B, H_B, DH_B, DH_B), F32), zc((B, H_B, DH_B), F32), zc((B, H_B), F32),
                                     L=MLSTM_CHUNK, n_valid=MLSTM_CHUNK, bb=B if B <= 2 else 1)
            hp = ab_merge(os_, ls_, hb, z, g_mlstm[e], w_out, hp)
            bC_p.append(Cp); bn_p.append(n_p); bm_p.append(m_p)
            z = norm_proj(hs, norm_g[layer, 0], w_in, flags, cos_s, sin_s)
            z3 = z.reshape(DB, T_PAD, AB_N)
            os_, ls_ = [], []
            for gi, (win, dil) in enumerate(DIL_CFG):
                new_t = jnp.concatenate(
                    [jnp.swapaxes(z3[:, :, AB_KA + gi * AW:AB_KA + (gi + 1) * AW], 1, 2),
                     jnp.swapaxes(z3[:, :, AB_VA + gi * AW:AB_VA + (gi + 1) * AW], 1, 2)], axis=1)
                o, l, co = dil_sample(_channel_major(caches_a[gi][e], 1), z3, new_t, gi, win, dil, T)
                os_.append(o)
                ls_.append(l)
                a_s[gi].append(_row_major(co, 2, H_A, DH_A))
            bbs = 4 if DB % 4 == 0 else 1
            hb, Cs, n_s, m_s = mlstm(z3, _gates_t(z3, T_PAD), b_if[e], state_b_C[e], state_b_n[e], state_b_m[e],
                                     L=T_PAD, n_valid=T, bb=bbs)
            hs = ab_merge(os_, ls_, hb, z, g_mlstm[e], w_out, hs)
            bC_s.append(Cs); bn_s.append(n_s); bm_s.append(m_s)
        else:
            o_i = layer // 2
            w_in = jnp.pad(w_in_c[o_i], ((0, 0), (0, C_N - w_in_c.shape[-1]))).astype(BF16)
            flags = _c_rope_flags()
            w_out = w_out_c[o_i].astype(BF16)
            cw, cbias = _prep_compress(cmp_w1[o_i], cmp_w2[o_i], cmp_pe[o_i])
            z = norm_proj(hp, norm_g[layer, 0], w_in, flags, cos_p, sin_p)
            z3 = z.reshape(B, S, C_N)
            kv_cmp = z3[:, :, C_CMP:C_CMP + KVW]
            kv_sel = z3[:, :, C_SEL:C_SEL + KVW]
            kv_win = z3[:, :, C_WIN:C_WIN + KVW]
            kc = compress_prompt(z3, cw, cbias)
            n_cmp = (S - CMP_BLK) // CMP_STRIDE + 1
            n_slc = S // SLC_BLK
            mimp = _importance_matrix(S // CMP_STRIDE, n_cmp, n_slc, -(-n_slc // LANE) * LANE)
            ka, va = _sel_operands(z3)
            o = nsa_prompt(z3, kc, cols_bf16(z3, C_WIN, KVW), ka, va, cos_p1, sin_p1, mimp.T)
            hp = out_proj(o.reshape(B * S, C_Q), w_out, hp)
            nw = min(WIN_C, S)
            sh = lambda t: t.reshape(t.shape[0], t.shape[1], 2, KVH_C, DH_C)
            cc_p.append(sh(kv_cmp)); csl_p.append(sh(kv_sel)); cw_p.append(sh(kv_win[:, S - nw:]))
            z = norm_proj(hs, norm_g[layer, 0], w_in, flags, cos_s, sin_s)
            z3 = z.reshape(DB, T_PAD, C_N)
            kc = compress_paged(_channel_major(cache_c_cmp_kv[o_i], 1), page_table, cw, cbias)
            full_len = past + T
            n_cmp = (full_len - CMP_BLK) // CMP_STRIDE + 1
            assert (n_cmp + 1) * CMP_STRIDE <= past
            n_slc = past // SLC_BLK + -(-T // SLC_BLK)
            nsp = -(-n_slc // LANE) * LANE
            mimp = _importance_matrix(past // CMP_STRIDE, n_cmp, n_slc, nsp)
            o, wo = nsa_sample(z3, kc, _channel_major(cache_c_sel_kv[o_i], 1),
                               _channel_major(cache_c_win_kv[o_i], 1),
                               jnp.swapaxes(z3[:, :, C_WIN:C_WIN + KVW], 1, 2),
                               page_table, cos_s1, sin_s1, mimp, t_new=T)
            hs = out_proj(o.reshape(DB * T_PAD, C_Q), w_out, hs)
            cc_s.append(sh(z3[:, :T, C_CMP:C_CMP + KVW])); csl_s.append(sh(z3[:, :T, C_SEL:C_SEL + KVW]))
            cw_s.append(_row_major(wo, 2, KVH_C, DH_C))
        last = layer == depth - 1
        wg, wu, wd = (w_ffn_gate[layer].astype(BF16), w_ffn_up[layer].astype(BF16), w_ffn_down[layer].astype(BF16))
        hp = ffn(hp, norm_g[layer, 1], wg, wu, wd, norm_final, final_norm=last)
        hs = ffn(hs, norm_g[layer, 1], wg, wu, wd, norm_final, final_norm=last)

    y_prompt = hp.reshape(B, S, D)
    y_sample = hs.reshape(DB, T_PAD, D)[:, :T]
    st = lambda xs: jnp.stack(xs, axis=0)
    return (y_prompt, y_sample,
            st(a_p[0]), st(a_s[0]), st(a_p[1]), st(a_s[1]), st(a_p[2]), st(a_s[2]),
            st(bC_p), st(bC_s), st(bn_p), st(bn_s), st(bm_p), st(bm_s),
            st(cc_p), st(cc_s), st(csl_p), st(csl_s), st(cw_p), st(cw_s))
```

```python
import functools
import math

import numpy as np
import jax
import jax.numpy as jnp
from jax import lax
from jax.experimental import pallas as pl
from jax.experimental.pallas import tpu as pltpu

F32 = jnp.float32
BF16 = jnp.bfloat16

PAGE_SIZE = 128
DIL_CFG = ((128, 1), (512, 4), (2048, 16))
N_DIL = 3
H_A = 4
DH_A = 64
BLK_A = 128
H_B = 4
DH_B = 128
MLSTM_CHUNK = 128
H_C = 16
KVH_C = 2
G_C = H_C // KVH_C
DH_C = 64
CMP_STRIDE = 16
CMP_BLK = 2 * CMP_STRIDE
CMP_HID = 64
SLC_BLK = 64
N_SEL = 16
WIN_C = 512
QBLK_C = 128
ROPE_THETA = 10000.0
EPS = 1e-6
A_QW = N_DIL * H_A * DH_A
B_W = H_B * DH_B
AW = H_A * DH_A
C_Q = H_C * DH_C
C_KV = 3 * 2 * KVH_C * DH_C
KVW = 2 * KVH_C * DH_C

LANE = 128
SUBLANE = 8
VMEM_LIMIT = 48 * 1024 * 1024

NEG_BIG = -1e30
T_PAD = SUBLANE

AB_QB, AB_KB, AB_VB, AB_OG = 0, B_W, 2 * B_W, 3 * B_W
AB_QA = 4 * B_W
AB_KA = AB_QA + A_QW
AB_VA = AB_KA + A_QW
AB_IG = AB_VA + A_QW
AB_FG = AB_IG + LANE
AB_N = AB_FG + LANE

C_CMP = C_Q
C_SEL = C_Q + KVW
C_WIN = C_Q + 2 * KVW
C_GATE = C_Q + 3 * KVW
C_N = 2048


def _cparams(sem, vmem=VMEM_LIMIT):
    return pltpu.CompilerParams(dimension_semantics=sem, vmem_limit_bytes=vmem)


def _pick_tile(m, pref):
    t = min(m, pref)
    while m % t:
        t //= 2
    return t


def _rope_lanes(x, cos, sin):
    lane = lax.broadcasted_iota(jnp.int32, x.shape, 1)
    first = (lane % DH_C) < (DH_C // 2)
    partner = jnp.where(first, pltpu.roll(x, LANE - DH_C // 2, 1), pltpu.roll(x, DH_C // 2, 1))
    return x * cos + partner * sin


def _split3(x):
    hi = x.astype(BF16)
    r1 = x - hi.astype(F32)
    mid = r1.astype(BF16)
    lo = (r1 - mid.astype(F32)).astype(BF16)
    return hi, mid, lo


def _dot_exact_rhs01(x, m01):
    hi, mid, lo = _split3(x)
    d = lambda a: jnp.dot(a, m01, preferred_element_type=F32)
    return d(hi) + d(mid) + d(lo)


def _dot_exact_lhs01(m01, x):
    hi, mid, lo = _split3(x)
    d = lambda a: jnp.dot(m01, a, preferred_element_type=F32)
    return d(hi) + d(mid) + d(lo)


def _dot_nt(a, b):
    return lax.dot_general(a, b, (((1,), (1,)), ((), ())), preferred_element_type=F32)


def _dot_tn(a, b):
    return lax.dot_general(a, b, (((0,), (0,)), ((), ())), preferred_element_type=F32)


def _log_sigmoid(x):
    return jnp.minimum(x, 0.0) - jnp.log1p(jnp.exp(-jnp.abs(x)))


def _gelu_tanh(x):
    return 0.5 * x * (1.0 + jnp.tanh(math.sqrt(2.0 / math.pi) * (x + 0.044715 * (x * x * x))))


def _rms_rows(x, g):
    ms = jnp.mean(x * x, axis=-1, keepdims=True)
    return x * lax.rsqrt(ms + EPS) * g


NORM_PROJ_TN = 512
RESIDENT_W_VMEM = 56 * 1024 * 1024


def _norm_proj_kernel(x_ref, g_ref, w_ref, cos_ref, sin_ref, o_ref, *, rope_chunks):
    xn = _rms_rows(x_ref[...], g_ref[...]).astype(BF16)
    n = o_ref.shape[1]
    for c0 in range(0, n, NORM_PROJ_TN):
        cs = slice(c0, min(c0 + NORM_PROJ_TN, n))
        o_ref[:, cs] = jnp.dot(xn, w_ref[:, cs], preferred_element_type=F32)
    for c in rope_chunks:
        cs = slice(c * LANE, (c + 1) * LANE)
        o_ref[:, cs] = _rope_lanes(o_ref[:, cs], cos_ref[...], sin_ref[...])


def norm_proj(x, g, w_bf16, rope_chunks, cos_t, sin_t, *, tm_pref=512):
    m, d = x.shape
    n = w_bf16.shape[1]
    tm = _pick_tile(m, tm_pref)
    return pl.pallas_call(
        functools.partial(_norm_proj_kernel, rope_chunks=tuple(rope_chunks)),
        grid=(m // tm,),
        in_specs=[
            pl.BlockSpec((tm, d), lambda i: (i, 0)),
            pl.BlockSpec((1, d), lambda i: (0, 0)),
            pl.BlockSpec((d, n), lambda i: (0, 0), pipeline_mode=pl.Buffered(1)),
            pl.BlockSpec((tm, LANE), lambda i: (i, 0)),
            pl.BlockSpec((tm, LANE), lambda i: (i, 0)),
        ],
        out_specs=pl.BlockSpec((tm, n), lambda i: (i, 0)),
        out_shape=jax.ShapeDtypeStruct((m, n), F32),
        compiler_params=_cparams(("parallel",), RESIDENT_W_VMEM),
        name="norm_proj",
    )(x, g.reshape(1, d), w_bf16, cos_t, sin_t)


def _ffn_kernel(x_ref, g_ref, wg_ref, wu_ref, wd_ref, gf_ref, o_ref, *, final_norm, tf):
    x = x_ref[...]
    xn = _rms_rows(x, g_ref[...]).astype(BF16)
    y = x
    for f0 in range(0, wg_ref.shape[1], tf):
        fs = slice(f0, f0 + tf)
        a = jnp.dot(xn, wg_ref[:, fs], preferred_element_type=F32)
        u = jnp.dot(xn, wu_ref[:, fs], preferred_element_type=F32)
        act = (a * jax.nn.sigmoid(a)) * u
        y = y + jnp.dot(act.astype(BF16), wd_ref[fs, :], preferred_element_type=F32)
    o_ref[...] = _rms_rows(y, gf_ref[...]) if final_norm else y


def ffn(x, g, wg, wu, wd, g_final, *, final_norm, tm_pref=512):
    m, d = x.shape
    dff = wg.shape[1]
    tf = dff
    tm = _pick_tile(m, tm_pref)
    resident = lambda a: pl.BlockSpec(a.shape, lambda i: (0, 0), pipeline_mode=pl.Buffered(1))
    return pl.pallas_call(
        functools.partial(_ffn_kernel, final_norm=final_norm, tf=tf),
        grid=(m // tm,),
        in_specs=[
            pl.BlockSpec((tm, d), lambda i: (i, 0)),
            pl.BlockSpec((1, d), lambda i: (0, 0)),
            resident(wg), resident(wu), resident(wd),
            pl.BlockSpec((1, d), lambda i: (0, 0)),
        ],
        out_specs=pl.BlockSpec((tm, d), lambda i: (i, 0)),
        out_shape=jax.ShapeDtypeStruct((m, d), F32),
        compiler_params=_cparams(("parallel",), RESIDENT_W_VMEM),
        name="ffn",
    )(x, g.reshape(1, d), wg, wu, wd, g_final.reshape(1, d))


def _out_proj_kernel(x_ref, w_ref, r_ref, o_ref):
    o_ref[...] = r_ref[...] + jnp.dot(x_ref[...].astype(BF16), w_ref[...], preferred_element_type=F32)


def out_proj(x, w_bf16, resid, *, tm_pref=512):
    m, k = x.shape
    n = w_bf16.shape[1]
    tm = _pick_tile(m, tm_pref)
    return pl.pallas_call(
        _out_proj_kernel,
        grid=(m // tm,),
        in_specs=[
            pl.BlockSpec((tm, k), lambda i: (i, 0)),
            pl.BlockSpec((k, n), lambda i: (0, 0)),
            pl.BlockSpec((tm, n), lambda i: (i, 0)),
        ],
        out_specs=pl.BlockSpec((tm, n), lambda i: (i, 0)),
        out_shape=jax.ShapeDtypeStruct((m, n), F32),
        compiler_params=_cparams(("parallel",)),
        name="out_proj",
    )(x, w_bf16, resid)


DIL_STEP_ROWS = 2048
DIL_UNROLL = 16


def _dil_prompt_kernel(q_ref, kp_ref, kc_ref, vp_ref, vc_ref, o_ref, l_ref, *, nback, dil, nblk):
    n = pl.program_id(2)
    span = dil * BLK_A
    qi = BLK_A + lax.broadcasted_iota(jnp.int32, (BLK_A, 2 * BLK_A), 0)
    ki = lax.broadcasted_iota(jnp.int32, (BLK_A, 2 * BLK_A), 1)
    rel = qi - ki
    band = (rel >= 0) & (rel <= nback)
    own = ki >= BLK_A
    head_of_lane = lax.broadcasted_iota(jnp.int32, (1, LANE), 1) // DH_A

    def one(it, carry):
        j = it // dil
        r = it % dil
        start = j * span + r
        if dil > 1:
            take = lambda st: pl.ds(st, BLK_A, stride=dil)
        else:
            take = lambda st: pl.ds(pl.multiple_of(st, BLK_A), BLK_A)
        rows = take(start)
        prev_rows = take(jnp.maximum(start - span, r))
        first_rows = take(r)
        q = q_ref[0, rows, :] * (DH_A ** -0.5)
        kprev = jnp.where(j > 0, kc_ref[0, prev_rows, :], kp_ref[0, first_rows, :])
        vprev = jnp.where(j > 0, vc_ref[0, prev_rows, :], vp_ref[0, first_rows, :])
        kk = jnp.concatenate([kprev, kc_ref[0, rows, :]], axis=0)
        vv = jnp.concatenate([vprev, vc_ref[0, rows, :]], axis=0)
        mask = band & ((n * nblk + j > 0) | own)
        nh = LANE // DH_A
        qbd = jnp.concatenate([jnp.where(head_of_lane == h, q, 0.0) for h in range(nh)], axis=0).astype(BF16)
        s = _dot_nt(qbd, kk.astype(BF16)).reshape(nh, BLK_A, 2 * BLK_A)
        s = jnp.where(mask[None], s, -jnp.inf)
        m = jnp.max(s, axis=-1, keepdims=True)
        p = jnp.exp(s - m)
        l = jnp.sum(p, axis=-1, keepdims=True)
        pv = jnp.dot(p.reshape(nh * BLK_A, 2 * BLK_A).astype(BF16), vv.astype(BF16),
                     preferred_element_type=F32).reshape(nh, BLK_A, LANE) / l
        lse = m + jnp.log(l)
        o, ls = pv[0], jnp.broadcast_to(lse[0], (BLK_A, LANE))
        for h in range(1, nh):
            o = jnp.where(head_of_lane == h, pv[h], o)
            ls = jnp.where(head_of_lane == h, lse[h], ls)
        o_ref[0, rows, :] = o
        l_ref[0, rows, :] = ls
        return carry

    lax.fori_loop(0, nblk * dil, one, 0, unroll=DIL_UNROLL)


def dil_prompt(z3, gi, window, dil):
    b, s, _ = z3.shape
    nback = window // dil
    span = dil * BLK_A
    nblk = max(1, DIL_STEP_ROWS // span)
    rows = nblk * span
    assert s % rows == 0 and (nblk * dil) % DIL_UNROLL == 0
    nh = AW // LANE
    qo, ko, vo = (AB_QA + gi * AW) // LANE, (AB_KA + gi * AW) // LANE, (AB_VA + gi * AW) // LANE
    blk = (1, rows, LANE)
    pblk = (1, span, LANE)
    in_specs = [
        pl.BlockSpec(blk, lambda bb, hp, i: (bb, i, qo + hp)),
        pl.BlockSpec(pblk, lambda bb, hp, i: (bb, jnp.maximum(i * nblk - 1, 0), ko + hp)),
        pl.BlockSpec(blk, lambda bb, hp, i: (bb, i, ko + hp)),
        pl.BlockSpec(pblk, lambda bb, hp, i: (bb, jnp.maximum(i * nblk - 1, 0), vo + hp)),
        pl.BlockSpec(blk, lambda bb, hp, i: (bb, i, vo + hp)),
    ]
    out_spec = pl.BlockSpec(blk, lambda bb, hp, i: (bb, i, hp))
    o, l = pl.pallas_call(
        functools.partial(_dil_prompt_kernel, nback=nback, dil=dil, nblk=nblk),
        grid=(b, nh, s // rows),
        in_specs=in_specs,
        out_specs=[out_spec, out_spec],
        out_shape=[jax.ShapeDtypeStruct((b, s, AW), F32)] * 2,
        compiler_params=_cparams(("parallel", "parallel", "arbitrary")),
        name="dil_prompt_%d" % gi,
    )(z3, z3, z3, z3, z3)
    return o.reshape(b * s, AW), l.reshape(b * s, AW)


def _dil_sample_kernel(c_ref, q_ref, kn_ref, vn_ref, nt_ref, o_ref, l_ref, co_ref, *, window, dil, n_buf, t_new):
    rows = H_A * T_PAD
    q = q_ref[0] * (DH_A ** -0.5)
    q4 = jnp.concatenate([q] * H_A, axis=0)
    rr = lax.broadcasted_iota(jnp.int32, (rows, AW), 0)
    ll = lax.broadcasted_iota(jnp.int32, (rows, AW), 1)
    head_sel = (rr // T_PAD) == (ll // DH_A)
    qbd = jnp.where(head_sel, q4, 0.0).astype(BF16)
    kbt = c_ref[0, 0:AW, :].astype(BF16)
    vbt = c_ref[0, AW:2 * AW, :].astype(BF16)
    kn = kn_ref[0]
    vn = vn_ref[0]
    s_buf = jnp.dot(qbd, kbt, preferred_element_type=F32)
    s_new = _dot_nt(qbd, kn.astype(BF16))
    t_b = lax.broadcasted_iota(jnp.int32, (rows, n_buf), 0) % T_PAD
    c_b = lax.broadcasted_iota(jnp.int32, (rows, n_buf), 1)
    d_b = n_buf + t_b - c_b
    m_b = ((d_b % dil) == 0) & (d_b <= window)
    t_n = lax.broadcasted_iota(jnp.int32, (rows, T_PAD), 0) % T_PAD
    u_n = lax.broadcasted_iota(jnp.int32, (rows, T_PAD), 1)
    d_n = t_n - u_n
    m_n = (d_n >= 0) & ((d_n % dil) == 0) & (d_n <= window) & (u_n < t_new)
    s_buf = jnp.where(m_b, s_buf, NEG_BIG)
    s_new = jnp.where(m_n, s_new, NEG_BIG)
    mx = jnp.maximum(jnp.max(s_buf, axis=-1, keepdims=True), jnp.max(s_new, axis=-1, keepdims=True))
    p_b = jnp.where(m_b, jnp.exp(s_buf - mx), 0.0)
    p_n = jnp.where(m_n, jnp.exp(s_new - mx), 0.0)
    l = jnp.sum(p_b, axis=-1, keepdims=True) + jnp.sum(p_n, axis=-1, keepdims=True)
    l = jnp.maximum(l, 1e-30)
    acc = (_dot_nt(p_b.astype(BF16), vbt)
           + jnp.dot(p_n.astype(BF16), vn.astype(BF16), preferred_element_type=F32))
    res = jnp.where(head_sel, acc / l, 0.0)
    lse = jnp.where(head_sel, mx + jnp.log(l), 0.0)
    o = res[0:T_PAD]
    ls = lse[0:T_PAD]
    for h in range(1, H_A):
        o = o + res[h * T_PAD:(h + 1) * T_PAD]
        ls = ls + lse[h * T_PAD:(h + 1) * T_PAD]
    o_ref[0] = o
    l_ref[0] = ls
    co_ref[0, :, 0:n_buf - t_new] = c_ref[0, :, t_new:n_buf]
    co_ref[0, :, n_buf - t_new:n_buf] = nt_ref[0, :, 0:t_new]


def dil_sample(cache_t, z3, new_t, gi, window, dil, t_new):
    db, _, n_buf = cache_t.shape
    qo, ko, vo = AB_QA // AW + gi, AB_KA // AW + gi, AB_VA // AW + gi
    blk = (1, T_PAD, AW)
    o, l, co = pl.pallas_call(
        functools.partial(_dil_sample_kernel, window=window, dil=dil, n_buf=n_buf, t_new=t_new),
        grid=(db,),
        in_specs=[
            pl.BlockSpec((1, 2 * AW, n_buf), lambda b: (b, 0, 0)),
            pl.BlockSpec(blk, lambda b: (b, 0, qo)),
            pl.BlockSpec(blk, lambda b: (b, 0, ko)),
            pl.BlockSpec(blk, lambda b: (b, 0, vo)),
            pl.BlockSpec((1, 2 * AW, T_PAD), lambda b: (b, 0, 0)),
        ],
        out_specs=[
            pl.BlockSpec(blk, lambda b: (b, 0, 0)),
            pl.BlockSpec(blk, lambda b: (b, 0, 0)),
            pl.BlockSpec((1, 2 * AW, n_buf), lambda b: (b, 0, 0)),
        ],
        out_shape=[
            jax.ShapeDtypeStruct((db, T_PAD, AW), F32),
            jax.ShapeDtypeStruct((db, T_PAD, AW), F32),
            jax.ShapeDtypeStruct((db, 2 * AW, n_buf), F32),
        ],
        compiler_params=_cparams(("parallel",)),
        name="dil_sample_%d" % gi,
    )(cache_t, z3, z3, z3, new_t)
    return o.reshape(db * T_PAD, AW), l.reshape(db * T_PAD, AW), co


def _mlstm_kernel(q_ref, k_ref, v_ref, gi_ref, gf_ref, gt_ref, bi_ref, bf_ref, b8_ref,
                  c0_ref, n0_ref, m0_ref, h_ref, co_ref, no_ref, mo_ref,
                  c_s, n_s, m_s, *, bb, L, n_valid):
    c = pl.program_id(1)

    @pl.when(c == 0)
    def _():
        c_s[...] = c0_ref[...]
        n_s[...] = n0_ref[...]
        m_s[...] = m0_ref[...]

    row = lax.broadcasted_iota(jnp.int32, (L, L), 0)
    col = lax.broadcasted_iota(jnp.int32, (L, L), 1)
    tri = row >= col
    tri_l = jnp.where(tri, 1.0, 0.0).astype(BF16)
    tri_u = jnp.where(row <= col, 1.0, 0.0).astype(BF16)
    lane = lax.broadcasted_iota(jnp.int32, (1, LANE), 1)
    for b in range(bb):
        ig_col = gi_ref[b] + bi_ref[...]
        lf_col = _log_sigmoid(gf_ref[b] + bf_ref[...])
        gt = gt_ref[b, 0]
        ig_row = gt[0:H_B] + b8_ref[0:H_B]
        lf_row = _log_sigmoid(gt[H_B:2 * H_B] + b8_ref[H_B:2 * H_B])
        if n_valid < L:
            rv = lax.broadcasted_iota(jnp.int32, (L, LANE), 0) < n_valid
            ig_col = jnp.where(rv, ig_col, -jnp.inf)
            lf_col = jnp.where(rv, lf_col, 0.0)
            cv = lax.broadcasted_iota(jnp.int32, (H_B, L), 1) < n_valid
            ig_row = jnp.where(cv, ig_row, -jnp.inf)
            lf_row = jnp.where(cv, lf_row, 0.0)
        b_col = _dot_exact_lhs01(tri_l, lf_col)
        b_row = _dot_exact_rhs01(lf_row, tri_u)
        m_row = m_s[b]
        a_col = b_col + m_row
        m_new = m_row
        hs_out = []
        for h in range(H_B):
            hsl = slice(h * DH_B, (h + 1) * DH_B)
            bc = b_col[:, h:h + 1]
            ac = a_col[:, h:h + 1]
            icol = ig_col[:, h:h + 1]
            D = bc - b_row[h:h + 1, :] + ig_row[h:h + 1, :]
            D = jnp.where(tri, D, -jnp.inf)
            mt = jnp.maximum(ac, jnp.max(D, axis=-1, keepdims=True))
            Dw = jnp.exp(D - mt)
            iw = jnp.exp(ac - mt)
            qf = q_ref[b, :, hsl]
            kf = k_ref[b, :, hsl] * (DH_B ** -0.5)
            vf = v_ref[b, :, hsl]
            qb, kb, vb = qf.astype(BF16), kf.astype(BF16), vf.astype(BF16)
            Cm = c_s[b * H_B + h]
            nv = n_s[b * H_B + h]
            sc = _dot_nt(qb, kb) * Dw
            num = iw * _dot_nt(qb, Cm.astype(BF16)) + jnp.dot(sc.astype(BF16), vb, preferred_element_type=F32)
            den = iw * jnp.sum(qf * nv, axis=-1, keepdims=True) + jnp.sum(sc, axis=-1, keepdims=True)
            hs_out.append(num / jnp.maximum(jnp.abs(den), jnp.exp(-mt)))
            mL = mt[L - 1:L, :]
            wL = jnp.exp(bc[L - 1:L, :] - bc + icol - mL)
            dec = jnp.exp(ac[L - 1:L, :] - mL)
            c_s[b * H_B + h] = dec * Cm + _dot_tn((vf * wL).astype(BF16), kb)
            n_s[b * H_B + h] = dec * nv + jnp.sum(wL * kf, axis=0, keepdims=True)
            m_new = jnp.where(lane == h, mL, m_new)
        m_s[b] = m_new
        h_ref[b] = jnp.concatenate(hs_out, axis=-1)

    @pl.when(c == pl.num_programs(1) - 1)
    def _():
        co_ref[...] = c_s[...]
        no_ref[...] = n_s[...]
        mo_ref[...] = m_s[...]


def mlstm(z3, gt, b_if, c0, n0, m0, *, L, n_valid, bb):
    b, s, _ = z3.shape
    nc = s // L
    bi_row = jnp.zeros((1, LANE), F32).at[0, :H_B].set(b_if[0])
    bf_row = jnp.zeros((1, LANE), F32).at[0, :H_B].set(b_if[1])
    b8 = b_if.reshape(2 * H_B, 1)
    c0r = c0.reshape(b * H_B, DH_B, DH_B)
    n0r = n0.reshape(b * H_B, 1, DH_B)
    m0r = jnp.zeros((b, 1, LANE), F32).at[:, 0, :H_B].set(m0)
    cw = B_W // LANE
    h, co, no, mo = pl.pallas_call(
        functools.partial(_mlstm_kernel, bb=bb, L=L, n_valid=n_valid),
        grid=(b // bb, nc),
        in_specs=[
            pl.BlockSpec((bb, L, B_W), lambda g, c: (g, c, AB_QB // B_W)),
            pl.BlockSpec((bb, L, B_W), lambda g, c: (g, c, AB_KB // B_W)),
            pl.BlockSpec((bb, L, B_W), lambda g, c: (g, c, AB_VB // B_W)),
            pl.BlockSpec((bb, L, LANE), lambda g, c: (g, c, AB_IG // LANE)),
            pl.BlockSpec((bb, L, LANE), lambda g, c: (g, c, AB_FG // LANE)),
            pl.BlockSpec((bb, 1, 2 * H_B, L), lambda g, c: (g, c, 0, 0)),
            pl.BlockSpec((1, LANE), lambda g, c: (0, 0)),
            pl.BlockSpec((1, LANE), lambda g, c: (0, 0)),
            pl.BlockSpec((2 * H_B, 1), lambda g, c: (0, 0)),
            pl.BlockSpec((bb * H_B, DH_B, DH_B), lambda g, c: (g, 0, 0)),
            pl.BlockSpec((bb * H_B, 1, DH_B), lambda g, c: (g, 0, 0)),
            pl.BlockSpec((bb, 1, LANE), lambda g, c: (g, 0, 0)),
        ],
        out_specs=[
            pl.BlockSpec((bb, L, B_W), lambda g, c: (g, c, 0)),
            pl.BlockSpec((bb * H_B, DH_B, DH_B), lambda g, c: (g, 0, 0)),
            pl.BlockSpec((bb * H_B, 1, DH_B), lambda g, c: (g, 0, 0)),
            pl.BlockSpec((bb, 1, LANE), lambda g, c: (g, 0, 0)),
        ],
        out_shape=[
            jax.ShapeDtypeStruct((b, s, B_W), F32),
            jax.ShapeDtypeStruct((b * H_B, DH_B, DH_B), F32),
            jax.ShapeDtypeStruct((b * H_B, 1, DH_B), F32),
            jax.ShapeDtypeStruct((b, 1, LANE), F32),
        ],
        scratch_shapes=[
            pltpu.VMEM((bb * H_B, DH_B, DH_B), F32),
            pltpu.VMEM((bb * H_B, 1, DH_B), F32),
            pltpu.VMEM((bb, 1, LANE), F32),
        ],
        compiler_params=_cparams(("parallel", "arbitrary")),
        name="mlstm_L%d" % L,
    )(z3, z3, z3, z3, z3, gt, bi_row, bf_row, b8, c0r, n0r, m0r)
    del cw
    return (h.reshape(b * s, B_W), co.reshape(b, H_B, DH_B, DH_B), no.reshape(b, H_B, DH_B),
            mo[:, 0, :H_B])


def _ab_merge_kernel(o0, o1, o2, l0, l1, l2, hb_ref, og_ref, g_ref, w_ref, r_ref, out_ref):
    a0, a1, a2 = l0[...], l1[...], l2[...]
    mx = jnp.maximum(jnp.maximum(a0, a1), a2)
    e0, e1, e2 = jnp.exp(a0 - mx), jnp.exp(a1 - mx), jnp.exp(a2 - mx)
    o_a = (e0 * o0[...] + e1 * o1[...] + e2 * o2[...]) / (e0 + e1 + e2)
    hb = hb_ref[...]
    parts = []
    for h in range(H_B):
        hs = slice(h * DH_B, (h + 1) * DH_B)
        x = hb[:, hs]
        parts.append(x * lax.rsqrt(jnp.mean(x * x, axis=-1, keepdims=True) + EPS))
    hbn = jnp.concatenate(parts, axis=-1) * g_ref[...] * jax.nn.sigmoid(og_ref[...])
    y = (jnp.dot(o_a.astype(BF16), w_ref[0:AW, :], preferred_element_type=F32)
         + jnp.dot(hbn.astype(BF16), w_ref[AW:AW + B_W, :], preferred_element_type=F32))
    out_ref[...] = r_ref[...] + y


def ab_merge(os_, ls_, hb, z2, g_mn, w_bf16, resid, *, tm_pref=512):
    m, d = resid.shape
    tm = _pick_tile(m, tm_pref)
    a_spec = pl.BlockSpec((tm, AW), lambda i: (i, 0))
    return pl.pallas_call(
        _ab_merge_kernel,
        grid=(m // tm,),
        in_specs=[a_spec] * 6 + [
            pl.BlockSpec((tm, B_W), lambda i: (i, 0)),
            pl.BlockSpec((tm, B_W), lambda i: (i, AB_OG // B_W)),
            pl.BlockSpec((1, B_W), lambda i: (0, 0)),
            pl.BlockSpec((AW + B_W, d), lambda i: (0, 0)),
            pl.BlockSpec((tm, d), lambda i: (i, 0)),
        ],
        out_specs=pl.BlockSpec((tm, d), lambda i: (i, 0)),
        out_shape=jax.ShapeDtypeStruct((m, d), F32),
        compiler_params=_cparams(("parallel",)),
        name="ab_merge",
    )(*os_, *ls_, hb, z2, g_mn.reshape(1, B_W), w_bf16, resid)


HKV = KVH_C * DH_C


def _compress_bias_kernel(pa_ref, pb_ref, w_ref, o_ref):
    w = w_ref[0]
    o_ref[0] = (jnp.dot(pa_ref[0].astype(BF16), w, preferred_element_type=F32)[:, 0:HKV]
                + jnp.dot(pb_ref[0].astype(BF16), w, preferred_element_type=F32)[:, HKV:2 * HKV])


def compress_bias(pa, pb, wab2):
    spec = lambda a: pl.BlockSpec((1,) + a.shape[1:], lambda i: (i,) + (0,) * (a.ndim - 1))
    return pl.pallas_call(
        _compress_bias_kernel,
        grid=(2,),
        in_specs=[spec(pa), spec(pb), spec(wab2)],
        out_specs=pl.BlockSpec((1, 1, HKV), lambda i: (i, 0, 0)),
        out_shape=jax.ShapeDtypeStruct((2, 1, HKV), F32),
        compiler_params=_cparams(("arbitrary",)),
        name="compress_bias",
    )(pa, pb, wab2)


def _compress_halves(load_rows, n, w_all):
    x = jnp.concatenate([load_rows(r, n).astype(BF16) for r in range(CMP_STRIDE)], axis=1)
    return jnp.dot(x, w_all, preferred_element_type=F32)


def _compress_prompt_kernel(x_ref, wab_ref, bias_ref, w2_ref, o_ref):
    n = x_ref.shape[1] // CMP_STRIDE
    acc = _compress_halves(lambda r, m: x_ref[0, pl.ds(r, m, stride=CMP_STRIDE), :], n, wab_ref[0])
    hid = acc[:, 0:HKV] + pltpu.roll(acc[:, HKV:2 * HKV], n - 1, 0) + bias_ref[0]
    o_ref[0] = jnp.dot(_gelu_tanh(hid).astype(BF16), w2_ref[0], preferred_element_type=F32)


def compress_prompt(z3, cw, bias):
    b, s, _ = z3.shape
    wab, w2 = cw
    n = s // CMP_STRIDE
    return pl.pallas_call(
        _compress_prompt_kernel,
        grid=(b, 2),
        in_specs=[
            pl.BlockSpec((1, s, HKV), lambda i, kv: (i, 0, C_CMP // HKV + kv)),
            pl.BlockSpec((1,) + wab.shape[1:], lambda i, kv: (kv, 0, 0)),
            pl.BlockSpec((1, 1, HKV), lambda i, kv: (kv, 0, 0)),
            pl.BlockSpec((1, HKV, HKV), lambda i, kv: (kv, 0, 0)),
        ],
        out_specs=pl.BlockSpec((1, n, HKV), lambda i, kv: (i, 0, kv)),
        out_shape=jax.ShapeDtypeStruct((b, n, KVW), F32),
        compiler_params=_cparams(("parallel", "arbitrary")),
        name="compress_prompt",
    )(z3, wab, bias, w2)


def _compress_paged_kernel(pt_ref, *refs, pg):
    del pt_ref
    page_refs = refs[:pg + 1]
    wab_ref, bias_ref, w2_ref, o_ref, xs_ref = refs[pg + 1:]
    for j in range(pg + 1):
        t = page_refs[j][0].T
        xs_ref[0, j * PAGE_SIZE:(j + 1) * PAGE_SIZE, :] = t[:, 0:HKV]
        xs_ref[1, j * PAGE_SIZE:(j + 1) * PAGE_SIZE, :] = t[:, HKV:2 * HKV]
    cpp = PAGE_SIZE // CMP_STRIDE
    n = pg * cpp
    for kv in range(2):
        acc = _compress_halves(lambda r, m: xs_ref[kv, pl.ds(r, m, stride=CMP_STRIDE), :], n + cpp, wab_ref[kv])
        hid = acc[0:n, 0:HKV] + acc[1:n + 1, HKV:2 * HKV] + bias_ref[kv]
        o_ref[0, :, kv * HKV:(kv + 1) * HKV] = jnp.dot(_gelu_tanh(hid).astype(BF16), w2_ref[kv],
                                                        preferred_element_type=F32)


def compress_paged(pool_t, page_table, cw, bias, *, pg=32):
    db, n_pages = page_table.shape
    cpp = PAGE_SIZE // CMP_STRIDE
    pg = min(pg, n_pages)
    assert n_pages % pg == 0
    wab, w2 = cw
    full = lambda a: pl.BlockSpec(a.shape, lambda b, g, pt: (0,) * a.ndim)

    def page_spec(j):
        return pl.BlockSpec((1, KVW, PAGE_SIZE),
                            lambda b, g, pt: (pt[b, jnp.minimum(g * pg + j, n_pages - 1)], 0, 0))

    gs = pltpu.PrefetchScalarGridSpec(
        num_scalar_prefetch=1,
        grid=(db, n_pages // pg),
        in_specs=[page_spec(j) for j in range(pg + 1)] + [full(wab), full(bias), full(w2)],
        out_specs=pl.BlockSpec((1, pg * cpp, KVW), lambda b, g, pt: (b, g, 0)),
        scratch_shapes=[pltpu.VMEM((2, (pg + 1) * PAGE_SIZE, HKV), F32)],
    )
    return pl.pallas_call(
        functools.partial(_compress_paged_kernel, pg=pg),
        grid_spec=gs,
        out_shape=jax.ShapeDtypeStruct((db, n_pages * cpp, KVW), F32),
        compiler_params=_cparams(("parallel", "arbitrary")),
        name="compress_paged",
    )(page_table, *([pool_t] * (pg + 1)), wab, bias, w2)


def _stack_heads(x, t):
    del t
    return jnp.concatenate([x[:, g * DH_C:(g + 1) * DH_C] for g in range(G_C)], axis=0)


def _mask_rows(s, mask, fill):
    t, n = mask.shape
    return jnp.where(mask[None], s.reshape(G_C, t, n), fill).reshape(G_C * t, n)


def _cmp_branch(q2, kc, vc, t_pos, tq):
    n = kc.shape[0]
    s = _dot_nt(q2, kc)
    cend = lax.broadcasted_iota(jnp.int32, (tq, n), 1) * CMP_STRIDE + (CMP_BLK - 1)
    s = _mask_rows(s, cend <= t_pos, -jnp.inf)
    m = jnp.max(s, axis=-1, keepdims=True)
    m = jnp.where(m > -jnp.inf, m, 0.0)
    p = jnp.exp(s - m)
    p = p / jnp.maximum(jnp.sum(p, axis=-1, keepdims=True), 1e-30)
    o = jnp.dot(p.astype(BF16), vc, preferred_element_type=F32)
    pg = p[0:tq]
    for g in range(1, G_C):
        pg = pg + p[g * tq:(g + 1) * tq]
    return o, pg


def _select_blocks(imp, t_pos):
    tq, nsp = imp.shape
    j = lax.broadcasted_iota(jnp.int32, (tq, nsp), 1)
    jf = j.astype(F32)
    cur = t_pos // SLC_BLK
    forced = (j == 0) | (j == cur) | (j == cur - 1)
    work = jnp.where(forced, jnp.inf, jnp.where(j <= cur, imp, -jnp.inf))
    sel = jnp.zeros((tq, nsp), F32)
    for _ in range(N_SEL):
        mx = jnp.max(work, axis=-1, keepdims=True)
        first = jnp.min(jnp.where(work == mx, jf, float(nsp)), axis=-1, keepdims=True)
        pick = jf == first
        sel = jnp.where(pick, jnp.where(mx > -jnp.inf, 1.0, sel), sel)
        work = jnp.where(pick, -jnp.inf, work)
    return sel


def _select_blocks_t(imp_t, t_row):
    nsp, n = imp_t.shape
    j = lax.broadcasted_iota(jnp.int32, (nsp, n), 0)
    jf = j.astype(F32)
    cur = t_row // SLC_BLK
    forced = (j == 0) | (j == cur) | (j == cur - 1)
    work = jnp.where(forced, jnp.inf, jnp.where(j <= cur, imp_t, -jnp.inf))
    sel = jnp.zeros((nsp, n), F32)
    for _ in range(N_SEL):
        mx = jnp.max(work, axis=0, keepdims=True)
        first = jnp.min(jnp.where(work == mx, jf, float(nsp)), axis=0, keepdims=True)
        pick = jf == first
        sel = jnp.where(pick, jnp.where(mx > -jnp.inf, 1.0, sel), sel)
        work = jnp.where(pick, -jnp.inf, work)
    return sel


def _flash_update(s, mask, v, m_ref, l_ref, acc_ref, v_channel_major=False):
    s = _mask_rows(s, mask, NEG_BIG)
    m_old = m_ref[...]
    m_new = jnp.maximum(m_old, jnp.max(s, axis=-1, keepdims=True))
    alpha = jnp.exp(m_old - m_new)
    p = jnp.exp(s - m_new)
    l_ref[...] = alpha * l_ref[...] + jnp.sum(p, axis=-1, keepdims=True)
    pv = _dot_nt(p.astype(BF16), v) if v_channel_major else jnp.dot(p.astype(BF16), v, preferred_element_type=F32)
    acc_ref[...] = alpha * acc_ref[...] + pv
    m_ref[...] = m_new


def _flash_finish(m_ref, l_ref, acc_ref):
    return jnp.where(m_ref[...] > 0.5 * NEG_BIG, acc_ref[...] / jnp.maximum(l_ref[...], 1e-30), 0.0)


def _softmax_av(s, mask, v):
    s = _mask_rows(s, mask, -jnp.inf)
    m = jnp.max(s, axis=-1, keepdims=True)
    m = jnp.where(m > -jnp.inf, m, 0.0)
    p = jnp.exp(s - m)
    p = p / jnp.maximum(jnp.sum(p, axis=-1, keepdims=True), 1e-30)
    return jnp.dot(p.astype(BF16), v, preferred_element_type=F32)


def _gate_mix(gates, h, o_cmp, o_sel, o_win, tq):
    outs = []
    for g in range(G_C):
        base = (h * G_C + g) * 3
        rs = slice(g * tq, (g + 1) * tq)
        outs.append(gates[:, base:base + 1] * o_cmp[rs] + gates[:, base + 1:base + 2] * o_sel[rs]
                    + gates[:, base + 2:base + 3] * o_win[rs])
    return jnp.concatenate(outs, axis=-1)


def _rope_q(qh, cos, sin):
    return jnp.concatenate(
        [_rope_lanes(qh[:, c * LANE:(c + 1) * LANE], cos, sin) for c in range(G_C * DH_C // LANE)], axis=-1)


SEL_TK = 1024


SEL_PHASE = DH_C


def _nsa_prompt_kernel(q_ref, gz_ref, cos_ref, sin_ref, kc_ref, kv_ref, ka_ref, va_ref, mimp_ref, o_ref,
                       qa_s, m_s, acc_s, *, nsp):
    i = pl.program_id(1)
    tq = QBLK_C
    rows = G_C * tq
    s0 = i * tq
    t_pos = s0 + lax.broadcasted_iota(jnp.int32, (tq, 1), 0)
    gates = jax.nn.sigmoid(gz_ref[0])
    cos, sin = cos_ref[...], sin_ref[...]
    scale = DH_C ** -0.5
    hw = G_C * DH_C
    ncmp = kc_ref.shape[1]
    t_row = s0 + lax.broadcasted_iota(jnp.int32, (1, rows), 1) % tq
    visible = (lax.broadcasted_iota(jnp.int32, (ncmp, 1), 0) * CMP_STRIDE + (CMP_BLK - 1)) <= t_row
    any_visible = t_row >= CMP_BLK - 1
    zpad = jnp.zeros((tq, DH_C), F32)
    kcc = kc_ref[0, :, 0:HKV].astype(BF16)
    vcc = kc_ref[0, :, HKV:2 * HKV].astype(BF16)
    q_rot, o_cmps, imps_t = [], [], []
    for h in range(KVH_C):
        qh = q_ref[0, :, h * hw:(h + 1) * hw] * scale
        q_rot.append(_rope_q(qh, cos, sin))
        qc = jnp.concatenate([jnp.concatenate([x, zpad] if h == 0 else [zpad, x], axis=1)
                              for x in (qh[:, g * DH_C:(g + 1) * DH_C] for g in range(G_C))], axis=0)
        st = jnp.where(visible, _dot_nt(kcc, qc.astype(BF16)), NEG_BIG)
        pt = jnp.exp(st - jnp.max(st, axis=0, keepdims=True))
        inv = jnp.where(any_visible, 1.0 / jnp.maximum(jnp.sum(pt, axis=0, keepdims=True), 1e-30), 0.0)
        pt = pt * inv
        o_cmps.append(_dot_tn(vcc, pt.astype(BF16))[h * DH_C:(h + 1) * DH_C, :].T)
        pgrp_t = pt[:, 0:tq]
        for g in range(1, G_C):
            pgrp_t = pgrp_t + pt[:, g * tq:(g + 1) * tq]
        imps_t.append(_dot_exact_lhs01(mimp_ref[...], pgrp_t))
    t_row2 = s0 + lax.broadcasted_iota(jnp.int32, (1, KVH_C * tq), 1) % tq
    sel_t = _select_blocks_t(jnp.concatenate(imps_t, axis=1), t_row2)

    half_of_lane = lax.broadcasted_iota(jnp.int32, (1, LANE), 1) // DH_C

    def into_half(chunk, src_half, dst_half):
        moved = chunk if src_half == dst_half else pltpu.roll(chunk, DH_C, 1)
        return jnp.where(half_of_lane == dst_half, moved, 0.0)

    for h in range(KVH_C):
        selb = jnp.where(sel_t[:, h * tq:(h + 1) * tq].T > 0.5, 0.0, NEG_BIG)
        q_part = jnp.concatenate(
            [into_half(q_rot[h][:, (g // 2) * LANE:(g // 2 + 1) * LANE], g % 2, h) for g in range(G_C)], axis=0)
        for ph in range(nsp // SEL_PHASE):
            sb = into_half(selb[:, (ph // 2) * LANE:(ph // 2 + 1) * LANE], ph % 2, 1 - h)
            qa_s[h, ph] = (q_part + jnp.concatenate([sb] * G_C, axis=0)).astype(BF16)
    m_s[...] = jnp.full(m_s.shape, NEG_BIG, F32)
    acc_s[...] = jnp.zeros(acc_s.shape, F32)

    def tile(k0, size, causal):
        k0 = pl.multiple_of(k0, size)
        ph = k0 // (SEL_PHASE * SLC_BLK)
        for h in range(KVH_C):
            st = _dot_nt(ka_ref[0, h, pl.ds(k0, size), :], qa_s[h, ph])
            if causal:
                kpos = k0 + lax.broadcasted_iota(jnp.int32, (size, 1), 0)
                t_row = s0 + lax.broadcasted_iota(jnp.int32, (1, rows), 1) % tq
                st = jnp.where(kpos <= t_row, st, NEG_BIG)
            m_old = m_s[h]
            m_new = jnp.maximum(m_old, jnp.max(st, axis=0, keepdims=True))
            pt = jnp.exp(st - m_new).astype(BF16)
            acc_s[h] = (jnp.exp(m_old - m_new) * acc_s[h]
                        + jnp.dot(va_ref[0, h, :, pl.ds(k0, size)], pt, preferred_element_type=F32))
            m_s[h] = m_new

    def full_tile(kt, carry):
        tile(kt * SEL_TK, SEL_TK, False)
        return carry

    half = SEL_TK // 2
    own = (s0 + tq - 1) // half
    lax.fori_loop(0, own // 2, full_tile, 0)

    @pl.when(own % 2 == 1)
    def _():
        tile((own - 1) * half, half, False)

    tile(own * half, half, True)

    def finish(acc, h):
        sums = acc[(1 - h) * DH_C:(1 - h) * DH_C + 1, :]
        return (acc[h * DH_C:(h + 1) * DH_C, :] / jnp.maximum(sums, 1e-30)).T

    nw = WIN_C + tq
    w0 = pl.multiple_of(jnp.maximum(s0 - WIN_C, 0), tq)
    dlt = (s0 + lax.broadcasted_iota(jnp.int32, (1, rows), 1) % tq) - (w0 + lax.broadcasted_iota(jnp.int32, (nw, 1), 0))
    in_win = (dlt >= 0) & (dlt < WIN_C)
    lane = lax.broadcasted_iota(jnp.int32, (1, LANE), 1) // DH_C
    kwc = kv_ref[0, pl.ds(w0, nw), 0:LANE]
    vwc = kv_ref[0, pl.ds(w0, nw), LANE:2 * LANE]
    for h in range(KVH_C):
        st = _dot_nt(jnp.where(lane == h, kwc, jnp.zeros_like(kwc)), qa_s[h, 0])
        st = jnp.where(in_win, st, NEG_BIG)
        pt = jnp.exp(st - jnp.max(st, axis=0, keepdims=True)).astype(BF16)
        o_win = finish(_dot_tn(jnp.where(lane == h, vwc, jnp.ones_like(vwc)), pt), h)
        o_ref[0, :, h * hw:(h + 1) * hw] = _gate_mix(gates, h, o_cmps[h], finish(acc_s[h], h), o_win, tq)


def _cols_t_kernel(x_ref, o_ref):
    o_ref[0] = x_ref[0].T.astype(BF16)


def cols_channel_major(z3, col0, *, ts=1024):
    b, s, _ = z3.shape
    ts = _pick_tile(s, ts)
    return pl.pallas_call(
        _cols_t_kernel,
        grid=(b, s // ts),
        in_specs=[pl.BlockSpec((1, ts, LANE), lambda bb, i: (bb, i, col0 // LANE))],
        out_specs=pl.BlockSpec((1, LANE, ts), lambda bb, i: (bb, 0, i)),
        out_shape=jax.ShapeDtypeStruct((b, LANE, s), BF16),
        compiler_params=_cparams(("parallel", "parallel")),
        name="cols_channel_major",
    )(z3)


def _cols_cast_kernel(x_ref, o_ref):
    o_ref[...] = x_ref[...].astype(BF16)


def cols_bf16(z3, col0, width, *, ts=1024):
    b, s, _ = z3.shape
    ts = _pick_tile(s, ts)
    return pl.pallas_call(
        _cols_cast_kernel,
        grid=(b, s // ts),
        in_specs=[pl.BlockSpec((1, ts, width), lambda bb, i: (bb, i, col0 // width))],
        out_specs=pl.BlockSpec((1, ts, width), lambda bb, i: (bb, i, 0)),
        out_shape=jax.ShapeDtypeStruct((b, s, width), BF16),
        compiler_params=_cparams(("parallel", "parallel")),
        name="cols_bf16",
    )(z3)


def _sel_operands(z3):
    s = z3.shape[1]
    lane = jnp.arange(LANE)
    hot = ((jnp.arange(s)[:, None] // SLC_BLK) % SEL_PHASE == lane[None, :] % SEL_PHASE).astype(BF16)
    ksel = cols_bf16(z3, C_SEL, HKV)
    vsel_t = cols_channel_major(z3, C_SEL + HKV)
    ka = jnp.stack([jnp.where((lane // DH_C == h)[None, None, :], ksel, hot[None]) for h in range(KVH_C)], axis=1)
    va = jnp.stack([jnp.where((lane // DH_C == h)[None, :, None], vsel_t, jnp.ones_like(vsel_t))
                    for h in range(KVH_C)], axis=1)
    return ka, va


def nsa_prompt(z3, kc, kvw, ka, va, cos_t, sin_t, mimp):
    b, s, _ = z3.shape
    nsp = mimp.shape[0]
    nq = s // QBLK_C
    ncmp = kc.shape[1]
    rows = G_C * QBLK_C
    vm = pltpu.VMEM
    scratch = [
        vm((KVH_C, nsp // SEL_PHASE, rows, LANE), BF16),
        vm((KVH_C, 1, rows), F32),
        vm((KVH_C, LANE, rows), F32),
    ]
    return pl.pallas_call(
        functools.partial(_nsa_prompt_kernel, nsp=nsp),
        grid=(b, nq),
        in_specs=[
            pl.BlockSpec((1, QBLK_C, C_Q), lambda bb, i: (bb, i, 0)),
            pl.BlockSpec((1, QBLK_C, LANE), lambda bb, i: (bb, i, C_GATE // LANE)),
            pl.BlockSpec((QBLK_C, LANE), lambda bb, i: (i, 0)),
            pl.BlockSpec((QBLK_C, LANE), lambda bb, i: (i, 0)),
            pl.BlockSpec((1, ncmp, KVW), lambda bb, i: (bb, 0, 0)),
            pl.BlockSpec((1, s, KVW), lambda bb, i: (bb, 0, 0)),
            pl.BlockSpec((1, KVH_C, s, LANE), lambda bb, i: (bb, 0, 0, 0)),
            pl.BlockSpec((1, KVH_C, LANE, s), lambda bb, i: (bb, 0, 0, 0)),
            pl.BlockSpec(mimp.shape, lambda bb, i: (0, 0)),
        ],
        out_specs=pl.BlockSpec((1, QBLK_C, C_Q), lambda bb, i: (bb, i, 0)),
        out_shape=jax.ShapeDtypeStruct((b, s, C_Q), F32),
        scratch_shapes=scratch,
        compiler_params=_cparams(("parallel", "arbitrary")),
        name="nsa_prompt",
    )(z3, z3, cos_t, sin_t, kc, kvw, ka, va, mimp)


def _nsa_sample_kernel(pt_ref, *refs, pg, nsp, past, t_new):
    del pt_ref
    page_refs = refs[:pg]
    (q_ref, gz_ref, cos_ref, sin_ref, kc_ref, mimp_ref, hot_ref, seln_ref, winb_ref, winn_ref, winnt_ref,
     o_ref, wo_ref, q2r_s, qa_s, sel_s, ocmp_s, m_s, l_s, acc_s) = refs[pg:]
    g = pl.program_id(1)
    ng = pl.num_programs(1)
    tq = T_PAD
    rows = G_C * tq
    t_pos = past + lax.broadcasted_iota(jnp.int32, (tq, 1), 0)
    scale = DH_C ** -0.5
    hw = G_C * DH_C
    nk = pg * PAGE_SIZE
    gblk = nk // SLC_BLK

    @pl.when(g == 0)
    def _():
        cos, sin = cos_ref[...], sin_ref[...]
        q2rs, imps = [], []
        for h in range(KVH_C):
            qh = q_ref[0, :, h * hw:(h + 1) * hw] * scale
            q2 = _stack_heads(qh, tq).astype(BF16)
            q2rs.append(_stack_heads(_rope_q(qh, cos, sin), tq))
            q2r_s[h] = q2rs[h].astype(BF16)
            kc = kc_ref[0, :, h * DH_C:(h + 1) * DH_C].astype(BF16)
            vc = kc_ref[0, :, KVH_C * DH_C + h * DH_C:KVH_C * DH_C + (h + 1) * DH_C].astype(BF16)
            o_cmp, pgrp = _cmp_branch(q2, kc, vc, t_pos, tq)
            ocmp_s[h] = o_cmp
            imps.append(_dot_exact_rhs01(pgrp, mimp_ref[...]))
        sel_all = _select_blocks(jnp.concatenate(imps, axis=0), jnp.concatenate([t_pos] * KVH_C, axis=0))
        zpad = [jnp.zeros((rows, DH_C - gblk), F32)] if gblk < DH_C else []
        for h in range(KVH_C):
            sel = sel_all[h * tq:(h + 1) * tq]
            sel_s[h] = sel
            selb = jnp.where(sel > 0.5, 0.0, NEG_BIG)
            for gg in range(qa_s.shape[1]):
                sb = jnp.concatenate([selb[:, gg * gblk:(gg + 1) * gblk]] * G_C, axis=0)
                qa_s[h, gg] = jnp.concatenate([q2rs[h], sb] + zpad, axis=1).astype(BF16)
        m_s[...] = jnp.full(m_s.shape, NEG_BIG, F32)
        l_s[...] = jnp.zeros(l_s.shape, F32)
        acc_s[...] = jnp.zeros(acc_s.shape, F32)

    kvt = jnp.concatenate([r[0] for r in page_refs], axis=1).astype(BF16)
    hot = hot_ref[...]
    for h in range(KVH_C):
        kst = jnp.concatenate([kvt[h * DH_C:(h + 1) * DH_C, :], hot], axis=0)
        vst = kvt[KVH_C * DH_C + h * DH_C:KVH_C * DH_C + (h + 1) * DH_C, :]
        s = jnp.dot(qa_s[h, g], kst, preferred_element_type=F32)
        m_old = m_s[h]
        m_new = jnp.maximum(m_old, jnp.max(s, axis=-1, keepdims=True))
        alpha = jnp.exp(m_old - m_new)
        p = jnp.exp(s - m_new)
        l_s[h] = alpha * l_s[h] + jnp.sum(p, axis=-1, keepdims=True)
        acc_s[h] = alpha * acc_s[h] + _dot_nt(p.astype(BF16), vst)
        m_s[h] = m_new

    @pl.when(g == ng - 1)
    def _():
        gates = jax.nn.sigmoid(gz_ref[0])
        n_buf = winb_ref.shape[2]
        lane_n = lax.broadcasted_iota(jnp.int32, (tq, nsp), 1)
        u = lax.broadcasted_iota(jnp.int32, (tq, tq), 1)
        seln = seln_ref[0]
        winn = winn_ref[0]
        for h in range(KVH_C):
            kcol = h * DH_C
            vcol = KVH_C * DH_C + h * DH_C
            q2r = q2r_s[h]
            new_sel = jnp.max(jnp.where(lane_n == past // SLC_BLK, sel_s[h], 0.0), axis=-1, keepdims=True) > 0.5
            mk = new_sel & (past + u <= t_pos) & (u < t_new)
            _flash_update(_dot_nt(q2r, seln[:, kcol:kcol + DH_C].astype(BF16)), mk,
                          seln[:, vcol:vcol + DH_C].astype(BF16), m_s.at[h], l_s.at[h], acc_s.at[h])
            o_sel = _flash_finish(m_s.at[h], l_s.at[h], acc_s.at[h])
            kbt = winb_ref[0, kcol:kcol + DH_C, :].astype(BF16)
            vbt = winb_ref[0, vcol:vcol + DH_C, :].astype(BF16)
            pos_b = past - n_buf + lax.broadcasted_iota(jnp.int32, (tq, n_buf), 1)
            d_b = t_pos - pos_b
            m_b = (pos_b >= 0) & (d_b >= 0) & (d_b < WIN_C)
            d_n = t_pos - (past + u)
            m_n = (d_n >= 0) & (d_n < WIN_C) & (u < t_new)
            s_b = _mask_rows(jnp.dot(q2r, kbt, preferred_element_type=F32), m_b, -jnp.inf)
            s_n = _mask_rows(_dot_nt(q2r, winn[:, kcol:kcol + DH_C].astype(BF16)), m_n, -jnp.inf)
            mx = jnp.maximum(jnp.max(s_b, axis=-1, keepdims=True), jnp.max(s_n, axis=-1, keepdims=True))
            mx = jnp.where(mx > -jnp.inf, mx, 0.0)
            p_b, p_n = jnp.exp(s_b - mx), jnp.exp(s_n - mx)
            den = jnp.maximum(jnp.sum(p_b, axis=-1, keepdims=True) + jnp.sum(p_n, axis=-1, keepdims=True), 1e-30)
            o_win = (_dot_nt((p_b / den).astype(BF16), vbt)
                     + jnp.dot((p_n / den).astype(BF16), winn[:, vcol:vcol + DH_C].astype(BF16),
                               preferred_element_type=F32))
            o_ref[0, :, h * hw:(h + 1) * hw] = _gate_mix(gates, h, ocmp_s[h], o_sel, o_win, tq)
        wo_ref[0, :, 0:n_buf - t_new] = winb_ref[0, :, t_new:n_buf]
        wo_ref[0, :, n_buf - t_new:n_buf] = winnt_ref[0, :, 0:t_new]


def nsa_sample(z3, kc, sel_pool_t, win_buf_t, win_new_t, page_table, cos_t, sin_t, mimp, *, t_new, pg=32):
    db, n_pages = page_table.shape
    past = n_pages * PAGE_SIZE
    ncmp = kc.shape[1]
    nsp = mimp.shape[1]
    n_buf = win_buf_t.shape[2]
    pg = min(pg, n_pages)
    nk = pg * PAGE_SIZE
    assert n_pages % pg == 0 and nk // SLC_BLK <= DH_C
    rows = G_C * T_PAD
    hot = jnp.asarray(np.arange(nk)[None, :] // SLC_BLK == np.arange(DH_C)[:, None], dtype=BF16)

    def page_spec(j):
        return pl.BlockSpec((1, KVW, PAGE_SIZE), lambda b, g, pt: (pt[b, g * pg + j], 0, 0))

    gs = pltpu.PrefetchScalarGridSpec(
        num_scalar_prefetch=1,
        grid=(db, n_pages // pg),
        in_specs=[page_spec(j) for j in range(pg)] + [
            pl.BlockSpec((1, T_PAD, C_Q), lambda b, g, pt: (b, 0, 0)),
            pl.BlockSpec((1, T_PAD, LANE), lambda b, g, pt: (b, 0, C_GATE // LANE)),
            pl.BlockSpec((T_PAD, LANE), lambda b, g, pt: (0, 0)),
            pl.BlockSpec((T_PAD, LANE), lambda b, g, pt: (0, 0)),
            pl.BlockSpec((1, ncmp, KVW), lambda b, g, pt: (b, 0, 0)),
            pl.BlockSpec(mimp.shape, lambda b, g, pt: (0, 0)),
            pl.BlockSpec((DH_C, nk), lambda b, g, pt: (0, 0)),
            pl.BlockSpec((1, T_PAD, KVW), lambda b, g, pt: (b, 0, C_SEL // KVW)),
            pl.BlockSpec((1, KVW, n_buf), lambda b, g, pt: (b, 0, 0)),
            pl.BlockSpec((1, T_PAD, KVW), lambda b, g, pt: (b, 0, C_WIN // KVW)),
            pl.BlockSpec((1, KVW, T_PAD), lambda b, g, pt: (b, 0, 0)),
        ],
        out_specs=[
            pl.BlockSpec((1, T_PAD, C_Q), lambda b, g, pt: (b, 0, 0)),
            pl.BlockSpec((1, KVW, n_buf), lambda b, g, pt: (b, 0, 0)),
        ],
        scratch_shapes=[
            pltpu.VMEM((KVH_C, rows, DH_C), BF16),
            pltpu.VMEM((KVH_C, n_pages // pg, rows, LANE), BF16),
            pltpu.VMEM((KVH_C, T_PAD, nsp), F32),
            pltpu.VMEM((KVH_C, rows, DH_C), F32),
            pltpu.VMEM((KVH_C, rows, 1), F32),
            pltpu.VMEM((KVH_C, rows, 1), F32),
            pltpu.VMEM((KVH_C, rows, DH_C), F32),
        ],
    )
    return pl.pallas_call(
        functools.partial(_nsa_sample_kernel, pg=pg, nsp=nsp, past=past, t_new=t_new),
        grid_spec=gs,
        out_shape=[jax.ShapeDtypeStruct((db, T_PAD, C_Q), F32), jax.ShapeDtypeStruct((db, KVW, n_buf), F32)],
        compiler_params=_cparams(("parallel", "arbitrary")),
        name="nsa_sample",
    )(page_table, *([sel_pool_t] * pg), z3, z3, cos_t, sin_t, kc, mimp, hot, z3, win_buf_t, z3, win_new_t)


def _rope_tables(pos):
    half = DH_C // 2
    inv = 1.0 / (ROPE_THETA ** (jnp.arange(half, dtype=F32) / half))
    ang = pos.astype(F32)[:, None] * inv[None, :]
    cos, sin = jnp.cos(ang), jnp.sin(ang)
    return jnp.tile(cos, (1, 4)), jnp.concatenate([-sin, sin, -sin, sin], axis=-1)


def _prep_w_in_ab(w):
    cuts = np.cumsum([A_QW, A_QW, A_QW, B_W, B_W, B_W, H_B, H_B])
    qa, ka, va, qb, kb, vb, ig, fg, og = jnp.split(w, cuts, axis=-1)
    padg = lambda t: jnp.pad(t, ((0, 0), (0, LANE - H_B)))
    return jnp.concatenate([qb, kb, vb, og, qa, ka, va, padg(ig), padg(fg)], axis=-1).astype(BF16)


def _ab_rope_flags():
    return tuple(range(AB_QA // LANE, AB_VA // LANE))


def _c_rope_flags():
    return (C_SEL // LANE, C_WIN // LANE)


def _prep_compress(w1, w2, pe):
    e2 = jnp.eye(KVH_C, dtype=F32)

    def half(w1h):
        return jnp.einsum('krde,hH->krhdHe', w1h, e2).reshape(2, CMP_STRIDE, HKV, HKV).astype(BF16)

    wab = jnp.concatenate([half(w1[:, :CMP_STRIDE]), half(w1[:, CMP_STRIDE:])], axis=-1)
    w2b = jnp.einsum('ked,hH->kheHd', w2, e2).reshape(2, HKV, HKV).astype(BF16)

    def pe_half(p):
        return jnp.broadcast_to(p[:, :, None, :], (2, CMP_STRIDE, KVH_C, DH_C)).reshape(2, 1, CMP_STRIDE * HKV)

    wab = wab.reshape(2, CMP_STRIDE * HKV, 2 * HKV)
    bias = compress_bias(pe_half(pe[:, :CMP_STRIDE]), pe_half(pe[:, CMP_STRIDE:]), wab)
    return (wab, w2b), bias


def _channel_major(x, lead):
    perm = tuple(range(lead)) + (lead + 1, lead + 2, lead + 3, lead)
    xt = jnp.transpose(x, perm)
    return xt.reshape(x.shape[:lead] + (x.shape[lead + 1] * x.shape[lead + 2] * x.shape[lead + 3], x.shape[lead]))


def _row_major(xt, c0, c1, c2):
    lead, _, rows = xt.shape
    return jnp.transpose(xt.reshape(lead, c0, c1, c2, rows), (0, 4, 1, 2, 3))


def _importance_matrix(n_rows, n_cmp, n_slc, n_cols):
    ratio = SLC_BLK // CMP_STRIDE
    m = np.zeros((n_rows, n_cols), np.float32)
    for jblk in range(n_slc):
        for off in range(1 - CMP_BLK // CMP_STRIDE, ratio):
            i = ratio * jblk + off
            if 0 <= i < n_cmp:
                m[i, jblk] = 1.0
    return jnp.asarray(m, dtype=BF16)


def _gates_t(z3, L):
    b, s, _ = z3.shape
    g = jnp.concatenate([z3[..., AB_IG:AB_IG + H_B], z3[..., AB_FG:AB_FG + H_B]], axis=-1)
    return g.reshape(b, s // L, L, 2 * H_B).transpose(0, 1, 3, 2)


def kernel(x_prompt, x_sample, cache_a0_kv, cache_a1_kv, cache_a2_kv, state_b_C, state_b_n, state_b_m,
           cache_c_cmp_kv, cache_c_sel_kv, cache_c_win_kv, page_table, norm_g, w_in_ab, b_if, g_mlstm,
           w_out_ab, w_in_c, cmp_w1, cmp_w2, cmp_pe, w_out_c, w_ffn_gate, w_ffn_up, w_ffn_down, norm_final):
    B, S, D = x_prompt.shape
    DB, T, _ = x_sample.shape
    depth = norm_g.shape[0]
    n_pages = page_table.shape[1]
    past = n_pages * PAGE_SIZE
    caches_a = (cache_a0_kv, cache_a1_kv, cache_a2_kv)
    assert T <= T_PAD and S % (DIL_CFG[-1][1] * BLK_A) == 0 and S >= WIN_C + QBLK_C

    hp = x_prompt.reshape(B * S, D)
    hs = jnp.pad(x_sample, ((0, 0), (0, T_PAD - T), (0, 0))).reshape(DB * T_PAD, D)

    pos_p = jnp.arange(S)
    pos_s = past + jnp.arange(T_PAD)
    cos_p1, sin_p1 = _rope_tables(pos_p)
    cos_s1, sin_s1 = _rope_tables(pos_s)
    cos_p, sin_p = jnp.tile(cos_p1, (B, 1)), jnp.tile(sin_p1, (B, 1))
    cos_s, sin_s = jnp.tile(cos_s1, (DB, 1)), jnp.tile(sin_s1, (DB, 1))

    a_p, a_s = [[], [], []], [[], [], []]
    bC_p, bC_s, bn_p, bn_s, bm_p, bm_s = [], [], [], [], [], []
    cc_p, cc_s, csl_p, csl_s, cw_p, cw_s = [], [], [], [], [], []

    for layer in range(depth):
        if layer % 2 == 0:
            e = layer // 2
            w_in = _prep_w_in_ab(w_in_ab[e])
            flags = _ab_rope_flags()
            w_out = w_out_ab[e].astype(BF16)
            z = norm_proj(hp, norm_g[layer, 0], w_in, flags, cos_p, sin_p)
            z3 = z.reshape(B, S, AB_N)
            os_, ls_ = [], []
            for gi, (win, dil) in enumerate(DIL_CFG):
                o, l = dil_prompt(z3, gi, win, dil)
                os_.append(o)
                ls_.append(l)
                nb = min(win, S)
                kk = z3[:, S - nb:, AB_KA + gi * AW:AB_KA + (gi + 1) * AW].reshape(B, nb, H_A, DH_A)
                vv = z3[:, S - nb:, AB_VA + gi * AW:AB_VA + (gi + 1) * AW].reshape(B, nb, H_A, DH_A)
                a_p[gi].append(jnp.stack([kk, vv], axis=2))
            zc = jnp.zeros
            hb, Cp, n_p, m_p = mlstm(z3, _gates_t(z3, MLSTM_CHUNK), b_if[e],
                                     zc((B, H_B, DH_B, DH_B), F32), zc((B, H_B, DH_B), F32), zc((B, H_B), F32),
                                     L=MLSTM_CHUNK, n_valid=MLSTM_CHUNK, bb=B if B <= 2 else 1)
            hp = ab_merge(os_, ls_, hb, z, g_mlstm[e], w_out, hp)
            bC_p.append(Cp); bn_p.append(n_p); bm_p.append(m_p)
            z = norm_proj(hs, norm_g[layer, 0], w_in, flags, cos_s, sin_s)
            z3 = z.reshape(DB, T_PAD, AB_N)
            os_, ls_ = [], []
            for gi, (win, dil) in enumerate(DIL_CFG):
                new_t = jnp.concatenate(
                    [jnp.swapaxes(z3[:, :, AB_KA + gi * AW:AB_KA + (gi + 1) * AW], 1, 2),
                     jnp.swapaxes(z3[:, :, AB_VA + gi * AW:AB_VA + (gi + 1) * AW], 1, 2)], axis=1)
                o, l, co = dil_sample(_channel_major(caches_a[gi][e], 1), z3, new_t, gi, win, dil, T)
                os_.append(o)
                ls_.append(l)
                a_s[gi].append(_row_major(co, 2, H_A, DH_A))
            bbs = 4 if DB % 4 == 0 else 1
            hb, Cs, n_s, m_s = mlstm(z3, _gates_t(z3, T_PAD), b_if[e], state_b_C[e], state_b_n[e], state_b_m[e],
                                     L=T_PAD, n_valid=T, bb=bbs)
            hs = ab_merge(os_, ls_, hb, z, g_mlstm[e], w_out, hs)
            bC_s.append(Cs); bn_s.append(n_s); bm_s.append(m_s)
        else:
            o_i = layer // 2
            w_in = jnp.pad(w_in_c[o_i], ((0, 0), (0, C_N - w_in_c.shape[-1]))).astype(BF16)
            flags = _c_rope_flags()
            w_out = w_out_c[o_i].astype(BF16)
            cw, cbias = _prep_compress(cmp_w1[o_i], cmp_w2[o_i], cmp_pe[o_i])
            z = norm_proj(hp, norm_g[layer, 0], w_in, flags, cos_p, sin_p)
            z3 = z.reshape(B, S, C_N)
            kv_cmp = z3[:, :, C_CMP:C_CMP + KVW]
            kv_sel = z3[:, :, C_SEL:C_SEL + KVW]
            kv_win = z3[:, :, C_WIN:C_WIN + KVW]
            kc = compress_prompt(z3, cw, cbias)
            n_cmp = (S - CMP_BLK) // CMP_STRIDE + 1
            n_slc = S // SLC_BLK
            mimp = _importance_matrix(S // CMP_STRIDE, n_cmp, n_slc, -(-n_slc // LANE) * LANE)
            ka, va = _sel_operands(z3)
            o = nsa_prompt(z3, kc, cols_bf16(z3, C_WIN, KVW), ka, va, cos_p1, sin_p1, mimp.T)
            hp = out_proj(o.reshape(B * S, C_Q), w_out, hp)
            nw = min(WIN_C, S)
            sh = lambda t: t.reshape(t.shape[0], t.shape[1], 2, KVH_C, DH_C)
            cc_p.append(sh(kv_cmp)); csl_p.append(sh(kv_sel)); cw_p.append(sh(kv_win[:, S - nw:]))
            z = norm_proj(hs, norm_g[layer, 0], w_in, flags, cos_s, sin_s)
            z3 = z.reshape(DB, T_PAD, C_N)
            kc = compress_paged(_channel_major(cache_c_cmp_kv[o_i], 1), page_table, cw, cbias)
            full_len = past + T
            n_cmp = (full_len - CMP_BLK) // CMP_STRIDE + 1
            assert (n_cmp + 1) * CMP_STRIDE <= past
            n_slc = past // SLC_BLK + -(-T // SLC_BLK)
            nsp = -(-n_slc // LANE) * LANE
            mimp = _importance_matrix(past // CMP_STRIDE, n_cmp, n_slc, nsp)
            o, wo = nsa_sample(z3, kc, _channel_major(cache_c_sel_kv[o_i], 1),
                               _channel_major(cache_c_win_kv[o_i], 1),
                               jnp.swapaxes(z3[:, :, C_WIN:C_WIN + KVW], 1, 2),
                               page_table, cos_s1, sin_s1, mimp, t_new=T)
            hs = out_proj(o.reshape(DB * T_PAD, C_Q), w_out, hs)
            cc_s.append(sh(z3[:, :T, C_CMP:C_CMP + KVW])); csl_s.append(sh(z3[:, :T, C_SEL:C_SEL + KVW]))
            cw_s.append(_row_major(wo, 2, KVH_C, DH_C))
        last = layer == depth - 1
        wg, wu, wd = (w_ffn_gate[layer].astype(BF16), w_ffn_up[layer].astype(BF16), w_ffn_down[layer].astype(BF16))
        hp = ffn(hp, norm_g[layer, 1], wg, wu, wd, norm_final, final_norm=last)
        hs = ffn(hs, norm_g[layer, 1], wg, wu, wd, norm_final, final_norm=last)

    y_prompt = hp.reshape(B, S, D)
    y_sample = hs.reshape(DB, T_PAD, D)[:, :T]
    st = lambda xs: jnp.stack(xs, axis=0)
    return (y_prompt, y_sample,
            st(a_p[0]), st(a_s[0]), st(a_p[1]), st(a_s[1]), st(a_p[2]), st(a_s[2]),
            st(bC_p), st(bC_s), st(bn_p), st(bn_s), st(bm_p), st(bm_s),
            st(cc_p), st(cc_s), st(csl_p), st(csl_s), st(cw_p), st(cw_s))
```
